```python
import jax, jax.numpy as jnp
from jax import lax
import numpy as np

D_MODEL = 1024
BATCH = 2
SEQ = 8192
DEPTH = 2

N_HEADS = 16
HEAD_DIM = D_MODEL // N_HEADS
D_FF = -(-8 * D_MODEL // (3 * 256)) * 256
BLOCK_Q = 128
N_A = DEPTH // 2
N_B = DEPTH - N_A
EPS = 1e-6

kernel_name = "yoco_stickbreak_fox_adaln"


def rms_norm(x, g):
    xf = x.astype(jnp.float32)
    y = xf * lax.rsqrt(jnp.mean(xf * xf, axis=-1, keepdims=True) + EPS)
    return (y * g.astype(jnp.float32)).astype(x.dtype)


def modulate(h, shift, scale):
    return h * (1.0 + scale[:, None, :]) + shift[:, None, :]


def split_heads(t):
    b, s, _ = t.shape
    return t.reshape(b, s, N_HEADS, HEAD_DIM).transpose(0, 2, 1, 3)


def merge_heads(t):
    b, h, s, d = t.shape
    return t.transpose(0, 2, 1, 3).reshape(b, s, h * d)


def to_blocks(t):
    b, h, s = t.shape[:3]
    t = t.reshape((b, h, s // BLOCK_Q, BLOCK_Q) + t.shape[3:])
    return jnp.moveaxis(t, 2, 0)


def from_blocks(t):
    nb, b, h, bq, d = t.shape
    return jnp.moveaxis(t, 0, 2).reshape(b, h, nb * bq, d)


def stick_breaking_attention(q, k, v):
    s_len = q.shape[2]
    scale = HEAD_DIM ** -0.5
    kpos = jnp.arange(s_len)

    def block(args):
        qb, start = args
        z = jnp.einsum('bhqd,bhkd->bhqk', qb, k).astype(jnp.float32) * scale
        qpos = start + jnp.arange(BLOCK_Q)
        mask = kpos[None, :] < qpos[:, None]
        log_1m = jnp.where(mask, jax.nn.log_sigmoid(-z), 0.0)
        log_rest = lax.cumsum(log_1m, axis=3, reverse=True) - log_1m
        a = jnp.where(mask, jnp.exp(jax.nn.log_sigmoid(z) + log_rest), 0.0)
        return jnp.einsum('bhqk,bhkd->bhqd', a.astype(v.dtype), v)

    starts = jnp.arange(s_len // BLOCK_Q) * BLOCK_Q
    return from_blocks(lax.map(block, (to_blocks(q), starts)))


def forgetting_attention(q, k, v, fcum):
    s_len = q.shape[2]
    scale = HEAD_DIM ** -0.5
    kpos = jnp.arange(s_len)

    def block(args):
        qb, fq, start = args
        logit = jnp.einsum('bhqd,bhkd->bhqk', qb, k).astype(jnp.float32) * scale
        logit = logit + fq[..., :, None] - fcum[:, :, None, :]
        qpos = start + jnp.arange(BLOCK_Q)
        mask = kpos[None, :] <= qpos[:, None]
        p = jax.nn.softmax(jnp.where(mask, logit, -jnp.inf), axis=-1)
        return jnp.einsum('bhqk,bhkd->bhqd', p.astype(v.dtype), v)

    starts = jnp.arange(s_len // BLOCK_Q) * BLOCK_Q
    return from_blocks(lax.map(block, (to_blocks(q), to_blocks(fcum), starts)))


def swiglu(h, w_in, w_down):
    g, u = jnp.split(h @ w_in, 2, axis=-1)
    return (jax.nn.silu(g) * u) @ w_down


def setup_inputs(seed: int = 0) -> dict:
    key = jax.random.key(seed)
    ks = jax.random.split(key, 20)
    D, H, F, dh = D_MODEL, N_HEADS, D_FF, HEAD_DIM
    f32 = jnp.float32

    def w(k, shape, fan_in, mult=1.0):
        return jax.random.normal(k, shape, f32) * (mult * fan_in ** -0.5)

    def gain(k, shape):
        return 1.0 + 0.02 * jax.random.normal(k, shape, f32)

    return {
        "x": jax.random.normal(ks[0], (BATCH, SEQ, D), f32),
        "c": jax.random.normal(ks[1], (BATCH, D), f32),
        "ada_w": w(ks[2], (DEPTH, D, 6 * D), D, 0.5),
        "ada_b": 0.02 * jax.random.normal(ks[3], (DEPTH, 6 * D), f32),
        "norm_attn_g": gain(ks[4], (DEPTH, D)),
        "norm_ffn_g": gain(ks[5], (DEPTH, D)),
        "w_ffn_in": w(ks[6], (DEPTH, D, 2 * F), D),
        "w_ffn_down": w(ks[7], (DEPTH, F, D), F),
        "sb_w_qkv": w(ks[8], (N_A, D, 3 * D), D),
        "sb_w_o": w(ks[9], (N_A, D, D), D),
        "kv_ada_w": w(ks[10], (D, 2 * D), D, 0.5),
        "kv_ada_b": 0.02 * jax.random.normal(ks[11], (2 * D,), f32),
        "kv_norm_g": gain(ks[12], (D,)),
        "w_kvf": w(ks[13], (D, 2 * D + H), D),
        "b_f": jax.random.uniform(ks[14], (H,), f32, 1.0, 4.0),
        "k_norm_g": gain(ks[15], (dh,)),
        "fox_w_q": w(ks[16], (N_B, D, D), D),
        "q_norm_g": gain(ks[17], (N_B, dh)),
        "fox_w_o": w(ks[18], (N_B, D, D), D),
    }


def reference(x, c, ada_w, ada_b, norm_attn_g, norm_ffn_g, w_ffn_in, w_ffn_down,
              sb_w_qkv, sb_w_o, kv_ada_w, kv_ada_b, kv_norm_g, w_kvf, b_f,
              k_norm_g, fox_w_q, q_norm_g, fox_w_o):
    D = D_MODEL
    c_act = jax.nn.silu(c.astype(jnp.float32)).astype(x.dtype)
    k_sh = v_sh = fcum = None
    for layer in range(DEPTH):
        mod = c_act @ ada_w[layer] + ada_b[layer]
        sh_a, sc_a, g_a, sh_f, sc_f, g_f = jnp.split(mod, 6, axis=-1)
        if layer < N_A:
            h = modulate(rms_norm(x, norm_attn_g[layer]), sh_a, sc_a)
            q, k, v = jnp.split(h @ sb_w_qkv[layer], 3, axis=-1)
            o = stick_breaking_attention(split_heads(q), split_heads(k), split_heads(v))
            mix = merge_heads(o) @ sb_w_o[layer]
        else:
            if layer == N_A:
                kv_shift, kv_scale = jnp.split(c_act @ kv_ada_w + kv_ada_b, 2, axis=-1)
                hk = modulate(rms_norm(x, kv_norm_g), kv_shift, kv_scale)
                proj = hk @ w_kvf
                k_sh = rms_norm(split_heads(proj[..., :D]), k_norm_g)
                v_sh = split_heads(proj[..., D:2 * D])
                log_f = jax.nn.log_sigmoid(proj[..., 2 * D:].astype(jnp.float32) + b_f)
                fcum = lax.cumsum(log_f, axis=1).transpose(0, 2, 1)
            i = layer - N_A
            h = modulate(rms_norm(x, norm_attn_g[layer]), sh_a, sc_a)
            q = rms_norm(split_heads(h @ fox_w_q[i]), q_norm_g[i])
            o = forgetting_attention(q, k_sh, v_sh, fcum)
            mix = merge_heads(o) @ fox_w_o[i]
        x = x + g_a[:, None, :] * mix
        h = modulate(rms_norm(x, norm_ffn_g[layer]), sh_f, sc_f)
        x = x + g_f[:, None, :] * swiglu(h, w_ffn_in[layer], w_ffn_down[layer])
    return x
```

```python
import functools

import jax
import jax.numpy as jnp
from jax import lax
from jax.experimental import pallas as pl
from jax.experimental.pallas import tpu as pltpu

D_MODEL = 1024
N_HEADS = 16
HEAD_DIM = D_MODEL // N_HEADS
EPS = 1e-6

LANES_V7X = 128
SUBLANES_V7X = 8
VMEM_BYTES_V7X = 64 * 1024 * 1024

HEADS_PER_LANE_TILE = LANES_V7X // HEAD_DIM
N_HEAD_PAIRS = N_HEADS // HEADS_PER_LANE_TILE

TM = 512
TQ = 256
TK = 256
T_CUM = TK

ZERO_EXP = 110.0
NEG_BIG = -1e30

F32 = jnp.float32
BF16 = jnp.bfloat16


def _vmem_limit(block_bytes):
    want = 2 * block_bytes + 16 * 1024 * 1024
    return int(min(want, VMEM_BYTES_V7X - 8 * 1024 * 1024))


def _split2(x):
    hi = x.astype(BF16)
    lo = (x - hi.astype(F32)).astype(BF16)
    return hi, lo


def _dot(a, b):
    return jnp.dot(a, b, preferred_element_type=F32)


def _dot_nt(a, b):
    return lax.dot_general(a, b, (((1,), (1,)), ((), ())), preferred_element_type=F32)


def _ada_kernel(c_ref, w_ref, b_ref, o_ref):
    c = c_ref[...]
    c_act = c * jax.nn.sigmoid(c)
    o_ref[...] = jnp.dot(c_act, w_ref[...], precision=lax.Precision.HIGHEST,
                         preferred_element_type=F32) + b_ref[...]


def _ada_mod(c_pad, w, b):
    n_layers, d, n = w.shape
    tn = 1024
    rows = c_pad.shape[0]
    block_bytes = 4 * (d * tn + rows * d + tn + rows * tn)
    return pl.pallas_call(
        _ada_kernel,
        out_shape=jax.ShapeDtypeStruct((n_layers, rows, n), F32),
        grid=(n_layers, n // tn),
        in_specs=[
            pl.BlockSpec((rows, d), lambda l, j: (0, 0)),
            pl.BlockSpec((None, d, tn), lambda l, j: (l, 0, j)),
            pl.BlockSpec((None, 1, tn), lambda l, j: (l, 0, j)),
        ],
        out_specs=pl.BlockSpec((None, rows, tn), lambda l, j: (l, 0, j)),
        compiler_params=pltpu.CompilerParams(
            dimension_semantics=("arbitrary", "arbitrary"),
            vmem_limit_bytes=_vmem_limit(block_bytes)),
        name="ada_mod",
    )(c_pad, w, b.reshape(n_layers, 1, n))


def _norm_modulate(x, g, shift, scale):
    ms = jnp.mean(x * x, axis=-1, keepdims=True)
    y = x * lax.rsqrt(ms + EPS)
    return (y * g) * (1.0 + scale) + shift


def _row_spec(d):
    return pl.BlockSpec((1, d), lambda *_: (0, 0))


def _batch_vec_spec(d, tiles_per_batch):
    return pl.BlockSpec((None, 1, d), lambda i, *_: (i // tiles_per_batch, 0, 0))


def _norm_matmul_kernel(x_ref, g_ref, sh_ref, sc_ref, w_ref, o_ref, h_scr):
    @pl.when(pl.program_id(1) == 0)
    def _():
        h_scr[...] = _norm_modulate(x_ref[...], g_ref[...], sh_ref[...], sc_ref[...]).astype(BF16)

    o_ref[...] = _dot(h_scr[...], w_ref[...]).astype(o_ref.dtype)


def _norm_matmul(x2d, g, shift, scale, w, seq):
    m, d = x2d.shape
    n = w.shape[1]
    tn = 1024
    block_bytes = 4 * TM * d + 2 * d * tn + 2 * TM * tn + 2 * TM * d
    return pl.pallas_call(
        _norm_matmul_kernel,
        out_shape=jax.ShapeDtypeStruct((m, n), BF16),
        grid=(m // TM, n // tn),
        in_specs=[
            pl.BlockSpec((TM, d), lambda i, j: (i, 0)),
            _row_spec(d),
            _batch_vec_spec(d, seq // TM),
            _batch_vec_spec(d, seq // TM),
            pl.BlockSpec((d, tn), lambda i, j: (0, j)),
        ],
        out_specs=pl.BlockSpec((TM, tn), lambda i, j: (i, j)),
        scratch_shapes=[pltpu.VMEM((TM, d), BF16)],
        compiler_params=pltpu.CompilerParams(
            dimension_semantics=("arbitrary", "arbitrary"),
            vmem_limit_bytes=_vmem_limit(block_bytes)),
        name="norm_matmul",
    )(x2d, g, shift, scale, w)


def _ffn_in_kernel(x_ref, g_ref, sh_ref, sc_ref, wg_ref, wu_ref, o_ref, h_scr):
    @pl.when(pl.program_id(1) == 0)
    def _():
        h_scr[...] = _norm_modulate(x_ref[...], g_ref[...], sh_ref[...], sc_ref[...]).astype(BF16)

    h = h_scr[...]
    gate = _dot(h, wg_ref[...])
    up = _dot(h, wu_ref[...])
    o_ref[...] = (gate * jax.nn.sigmoid(gate) * up).astype(o_ref.dtype)


def _ffn_in(x2d, g, shift, scale, w_in, seq):
    m, d = x2d.shape
    f = w_in.shape[1] // 2
    n_tiles = 2
    tn = f // n_tiles
    assert tn * n_tiles == f and tn % LANES_V7X == 0
    block_bytes = 4 * TM * d + 2 * 2 * d * tn + 2 * TM * tn + 2 * TM * d + 2 * 4 * TM * tn
    return pl.pallas_call(
        _ffn_in_kernel,
        out_shape=jax.ShapeDtypeStruct((m, f), BF16),
        grid=(m // TM, n_tiles),
        in_specs=[
            pl.BlockSpec((TM, d), lambda i, j: (i, 0)),
            _row_spec(d),
            _batch_vec_spec(d, seq // TM),
            _batch_vec_spec(d, seq // TM),
            pl.BlockSpec((d, tn), lambda i, j: (0, j)),
            pl.BlockSpec((d, tn), lambda i, j: (0, j + n_tiles)),
        ],
        out_specs=pl.BlockSpec((TM, tn), lambda i, j: (i, j)),
        scratch_shapes=[pltpu.VMEM((TM, d), BF16)],
        compiler_params=pltpu.CompilerParams(
            dimension_semantics=("arbitrary", "arbitrary"),
            vmem_limit_bytes=_vmem_limit(block_bytes)),
        name="ffn_in",
    )(x2d, g, shift, scale, w_in, w_in)


def _proj_residual_kernel(a_ref, w_ref, x_ref, gate_ref, o_ref):
    o_ref[...] = x_ref[...] + gate_ref[...] * _dot(a_ref[...], w_ref[...])


def _proj_residual(a, w, x2d, gate, seq):
    m, k = a.shape
    d = w.shape[1]
    block_bytes = 2 * TM * k + 2 * k * d + 2 * 4 * TM * d
    return pl.pallas_call(
        _proj_residual_kernel,
        out_shape=jax.ShapeDtypeStruct((m, d), F32),
        grid=(m // TM,),
        in_specs=[
            pl.BlockSpec((TM, k), lambda i: (i, 0)),
            pl.BlockSpec((k, d), lambda i: (0, 0)),
            pl.BlockSpec((TM, d), lambda i: (i, 0)),
            _batch_vec_spec(d, seq // TM),
        ],
        out_specs=pl.BlockSpec((TM, d), lambda i: (i, 0)),
        compiler_params=pltpu.CompilerParams(
            dimension_semantics=("arbitrary",),
            vmem_limit_bytes=_vmem_limit(block_bytes)),
        name="proj_residual",
    )(a, w, x2d, gate)


def _head_rms_norm(t, gain, group_mean):
    outs = []
    for c in range(t.shape[1] // LANES_V7X):
        tc = t[:, c * LANES_V7X:(c + 1) * LANES_V7X]
        hi, lo = _split2(tc * tc)
        ms = _dot(hi, group_mean) + _dot(lo, group_mean)
        outs.append(tc * lax.rsqrt(ms + EPS) * gain[:, c * LANES_V7X:(c + 1) * LANES_V7X])
    return jnp.concatenate(outs, axis=1)


def _l1_proj_kernel(x_ref, gkv_ref, shkv_ref, sckv_ref, gq_ref, shq_ref, scq_ref,
                    wk_ref, wv_ref, wf_ref, wq_ref, kg_ref, qg_ref, bf_ref, gm_ref,
                    k_ref, v_ref, q_ref, lf_ref):
    x = x_ref[...]
    ms = jnp.mean(x * x, axis=-1, keepdims=True)
    y = x * lax.rsqrt(ms + EPS)
    h_kv = ((y * gkv_ref[...]) * (1.0 + sckv_ref[...]) + shkv_ref[...]).astype(BF16)
    h_q = ((y * gq_ref[...]) * (1.0 + scq_ref[...]) + shq_ref[...]).astype(BF16)
    gm = gm_ref[...]
    k_ref[...] = _head_rms_norm(_dot(h_kv, wk_ref[...]), kg_ref[...], gm).astype(BF16)
    v_ref[...] = _dot(h_kv, wv_ref[...]).astype(BF16)
    q = _head_rms_norm(_dot(h_q, wq_ref[...]), qg_ref[...], gm)
    q_ref[...] = (q * (HEAD_DIM ** -0.5)).astype(BF16)
    lf_ref[...] = jax.nn.log_sigmoid(_dot(h_kv, wf_ref[...]) + bf_ref[...])


def _l1_proj(x2d, g_kv, sh_kv, sc_kv, g_q, sh_q, sc_q, wk, wv, wf, wq, k_gain, q_gain, b_f, seq):
    m, d = x2d.shape
    lane = jnp.arange(LANES_V7X)
    group_mean = jnp.where(lane[:, None] // HEAD_DIM == lane[None, :] // HEAD_DIM,
                           1.0 / HEAD_DIM, 0.0).astype(BF16)
    tpb = seq // TM
    block_bytes = 4 * TM * d + 3 * 2 * d * d + 2 * d * LANES_V7X + 3 * 2 * TM * d + 4 * TM * LANES_V7X
    full = lambda r, c: pl.BlockSpec((r, c), lambda i: (0, 0))
    return pl.pallas_call(
        _l1_proj_kernel,
        out_shape=(jax.ShapeDtypeStruct((m, d), BF16), jax.ShapeDtypeStruct((m, d), BF16),
                   jax.ShapeDtypeStruct((m, d), BF16), jax.ShapeDtypeStruct((m, LANES_V7X), F32)),
        grid=(m // TM,),
        in_specs=[
            pl.BlockSpec((TM, d), lambda i: (i, 0)),
            _row_spec(d), _batch_vec_spec(d, tpb), _batch_vec_spec(d, tpb),
            _row_spec(d), _batch_vec_spec(d, tpb), _batch_vec_spec(d, tpb),
            full(d, d), full(d, d), full(d, LANES_V7X), full(d, d),
            _row_spec(d), _row_spec(d), _row_spec(LANES_V7X), full(LANES_V7X, LANES_V7X),
        ],
        out_specs=(pl.BlockSpec((TM, d), lambda i: (i, 0)), pl.BlockSpec((TM, d), lambda i: (i, 0)),
                   pl.BlockSpec((TM, d), lambda i: (i, 0)), pl.BlockSpec((TM, LANES_V7X), lambda i: (i, 0))),
        compiler_params=pltpu.CompilerParams(
            dimension_semantics=("arbitrary",),
            vmem_limit_bytes=_vmem_limit(block_bytes)),
        name="l1_proj",
    )(x2d, g_kv, sh_kv, sc_kv, g_q, sh_q, sc_q, wk, wv, wf, wq, k_gain, q_gain, b_f, group_mean)


def _fcum_kernel(lf_ref, col_ref, row_ref, carry_scr):
    @pl.when(pl.program_id(1) == 0)
    def _():
        carry_scr[...] = jnp.zeros_like(carry_scr)

    x = lf_ref[...]
    r = lax.broadcasted_iota(jnp.int32, (T_CUM, T_CUM), 0)
    c = lax.broadcasted_iota(jnp.int32, (T_CUM, T_CUM), 1)
    tri = jnp.where(c <= r, 1.0, 0.0).astype(BF16)
    x0 = x.astype(BF16)
    r1 = x - x0.astype(F32)
    x1 = r1.astype(BF16)
    x2 = (r1 - x1.astype(F32)).astype(BF16)
    cs = _dot(tri, x0) + _dot(tri, x1) + _dot(tri, x2) + carry_scr[0:1, :]
    carry_scr[...] = jnp.broadcast_to(cs[T_CUM - 1:T_CUM, :], carry_scr.shape)
    col_ref[...] = cs
    row_ref[...] = cs.T[:N_HEADS, :]


def _fcum(log_f, batch, seq):
    lf = log_f.reshape(batch, seq, LANES_V7X)
    block_bytes = 4 * (2 * T_CUM * LANES_V7X + N_HEADS * T_CUM)
    return pl.pallas_call(
        _fcum_kernel,
        out_shape=(jax.ShapeDtypeStruct((batch, seq, LANES_V7X), F32),
                   jax.ShapeDtypeStruct((batch, seq // T_CUM, N_HEADS, T_CUM), F32)),
        grid=(batch, seq // T_CUM),
        in_specs=[pl.BlockSpec((None, T_CUM, LANES_V7X), lambda b, t: (b, t, 0))],
        out_specs=(pl.BlockSpec((None, T_CUM, LANES_V7X), lambda b, t: (b, t, 0)),
                   pl.BlockSpec((None, None, N_HEADS, T_CUM), lambda b, t: (b, t, 0, 0))),
        scratch_shapes=[pltpu.VMEM((SUBLANES_V7X, LANES_V7X), F32)],
        compiler_params=pltpu.CompilerParams(
            dimension_semantics=("arbitrary", "arbitrary"),
            vmem_limit_bytes=_vmem_limit(block_bytes)),
        name="fcum",
    )(lf)


def _head_select(head):
    lane = lax.broadcasted_iota(jnp.int32, (1, LANES_V7X), 1)
    return (lane // HEAD_DIM) == head


def _widen(col128, width):
    reps = width // LANES_V7X
    return col128 if reps == 1 else jnp.concatenate([col128] * reps, axis=1)


def _sb_kernel(q_ref, k_ref, v_ref, o_ref):
    i = pl.program_id(2)
    row = lax.broadcasted_iota(jnp.int32, (TQ, TK), 0)
    col = lax.broadcasted_iota(jnp.int32, (TQ, TK), 1)
    diag_mask = col < row
    r = lax.broadcasted_iota(jnp.int32, (TK, TK), 0)
    c = lax.broadcasted_iota(jnp.int32, (TK, TK), 1)
    after = jnp.where(r > c, 1.0, 0.0).astype(BF16)
    ones = jnp.ones((TK, LANES_V7X), BF16)
    q = q_ref[...] * (HEAD_DIM ** -0.5)

    def block(qh, j, carry, acc, mask):
        kj = k_ref[pl.ds(j * TK, TK), :]
        vj = v_ref[pl.ds(j * TK, TK), :]
        z = _dot_nt(qh, kj)
        softplus = jnp.maximum(z, 0.0) + jnp.log(1.0 + jnp.exp(-jnp.abs(z)))
        if mask is not None:
            softplus = jnp.where(mask, softplus, 0.0)
        hi, lo = _split2(softplus)
        within = _dot(hi, after) + _dot(lo, after)
        e = (z - softplus) - within + _widen(carry, TK)
        a = jnp.exp(e)
        if mask is not None:
            a = jnp.where(mask, a, 0.0)
        acc = acc + _dot(a.astype(BF16), vj)
        carry = carry - (_dot(hi, ones) + _dot(lo, ones))
        return carry, acc

    outs = []
    for head in range(HEADS_PER_LANE_TILE):
        qh = jnp.where(_head_select(head), q, 0.0).astype(BF16)
        carry0 = jnp.zeros((TQ, LANES_V7X), F32)
        acc0 = jnp.zeros((TQ, LANES_V7X), F32)
        carry, acc = block(qh, i, carry0, acc0, diag_mask)

        def cond(state):
            j, carry, _ = state
            return jnp.logical_and(j >= 0, jnp.max(carry) > -ZERO_EXP)

        def body(state, qh=qh):
            j, carry, acc = state
            carry, acc = block(qh, j, carry, acc, None)
            return j - 1, carry, acc

        _, _, acc = lax.while_loop(cond, body, (i - 1, carry, acc))
        outs.append(acc)
    o_ref[...] = jnp.where(_head_select(0), outs[0], outs[1]).astype(o_ref.dtype)


def _sb_attn(qkv, batch, seq):
    assert TQ == TK
    block_bytes = 2 * (2 * TQ * LANES_V7X + 2 * seq * LANES_V7X)
    return pl.pallas_call(
        _sb_kernel,
        out_shape=jax.ShapeDtypeStruct((batch, seq, D_MODEL), BF16),
        grid=(batch, N_HEAD_PAIRS, seq // TQ),
        in_specs=[
            pl.BlockSpec((None, TQ, LANES_V7X), lambda b, p, i: (b, i, p)),
            pl.BlockSpec((None, seq, LANES_V7X), lambda b, p, i: (b, 0, N_HEAD_PAIRS + p)),
            pl.BlockSpec((None, seq, LANES_V7X), lambda b, p, i: (b, 0, 2 * N_HEAD_PAIRS + p)),
        ],
        out_specs=pl.BlockSpec((None, TQ, LANES_V7X), lambda b, p, i: (b, i, p)),
        compiler_params=pltpu.CompilerParams(
            dimension_semantics=("arbitrary", "arbitrary", "arbitrary"),
            vmem_limit_bytes=_vmem_limit(block_bytes)),
        name="sb_attn",
    )(qkv, qkv, qkv)


def _fox_kernel(kb_ref, q_ref, k_ref, v_ref, fq_ref, fk_ref, o_ref):
    p_id = pl.program_id(1)
    i = pl.program_id(2)
    row = lax.broadcasted_iota(jnp.int32, (TQ, TK), 0)
    col = lax.broadcasted_iota(jnp.int32, (TQ, TK), 1)
    diag_mask = col <= row
    q = q_ref[...]
    fq_all = fq_ref[...]
    lane = lax.broadcasted_iota(jnp.int32, (1, LANES_V7X), 1)
    kb = kb_ref[0, 0]

    outs = []
    for head in range(HEADS_PER_LANE_TILE):
        h_idx = p_id * HEADS_PER_LANE_TILE + head
        qh = jnp.where(_head_select(head), q, 0.0).astype(BF16)
        qf = qh.astype(F32)
        q_reach = jnp.sqrt(jnp.sum(qf * qf, axis=-1, keepdims=True)) * kb
        fq = jnp.sum(jnp.where(lane == h_idx, fq_all, 0.0), axis=-1, keepdims=True)

        def block(j, m, l, acc, mask, qh=qh, fq=fq, h_idx=h_idx):
            kj = k_ref[pl.ds(j * TK, TK), :]
            vj = v_ref[pl.ds(j * TK, TK), :]
            fk = fk_ref[j, pl.ds(h_idx, 1), :]
            s = _dot_nt(qh, kj) + (fq - fk)
            if mask is not None:
                s = jnp.where(mask, s, NEG_BIG)
            m_new = jnp.maximum(m, jnp.max(s, axis=-1, keepdims=True))
            alpha = jnp.exp(m - m_new)
            p = jnp.exp(s - m_new)
            l = alpha * l + jnp.sum(p, axis=-1, keepdims=True)
            acc = alpha * acc + _dot(p.astype(BF16), vj)
            reach = jnp.max(q_reach + fq - fk[:, 0:1] - m_new)
            return m_new, l, acc, reach

        m0 = jnp.full((TQ, 1), NEG_BIG, F32)
        l0 = jnp.zeros((TQ, 1), F32)
        acc0 = jnp.zeros((TQ, LANES_V7X), F32)
        m, l, acc, reach = block(i, m0, l0, acc0, diag_mask)

        def cond(state):
            j, _, _, _, reach = state
            return jnp.logical_and(j >= 0, reach > -ZERO_EXP)

        def body(state, block=block):
            j, m, l, acc, _ = state
            m, l, acc, reach = block(j, m, l, acc, None)
            return j - 1, m, l, acc, reach

        _, _, l, acc, _ = lax.while_loop(cond, body, (i - 1, m, l, acc, reach))
        outs.append(acc / l)
    o_ref[...] = jnp.where(_head_select(0), outs[0], outs[1]).astype(o_ref.dtype)


def _fox_attn(q, k, v, fcum_col, fcum_row, key_norm_bound, batch, seq):
    assert TQ == TK
    block_bytes = (2 * (2 * TQ * LANES_V7X + 2 * seq * LANES_V7X)
                   + 4 * (TQ * LANES_V7X + seq * N_HEADS))
    return pl.pallas_call(
        _fox_kernel,
        out_shape=jax.ShapeDtypeStruct((batch, seq, D_MODEL), BF16),
        grid=(batch, N_HEAD_PAIRS, seq // TQ),
        in_specs=[
            pl.BlockSpec(memory_space=pltpu.SMEM),
            pl.BlockSpec((None, TQ, LANES_V7X), lambda b, p, i: (b, i, p)),
            pl.BlockSpec((None, seq, LANES_V7X), lambda b, p, i: (b, 0, p)),
            pl.BlockSpec((None, seq, LANES_V7X), lambda b, p, i: (b, 0, p)),
            pl.BlockSpec((None, TQ, LANES_V7X), lambda b, p, i: (b, i, 0)),
            pl.BlockSpec((None, seq // TK, N_HEADS, TK), lambda b, p, i: (b, 0, 0, 0)),
        ],
        out_specs=pl.BlockSpec((None, TQ, LANES_V7X), lambda b, p, i: (b, i, p)),
        compiler_params=pltpu.CompilerParams(
            dimension_semantics=("arbitrary", "arbitrary", "arbitrary"),
            vmem_limit_bytes=_vmem_limit(block_bytes)),
        name="fox_attn",
    )(key_norm_bound, q, k, v, fcum_col, fcum_row)


def kernel(x, c, ada_w, ada_b, norm_attn_g, norm_ffn_g, w_ffn_in, w_ffn_down, sb_w_qkv, sb_w_o,
           kv_ada_w, kv_ada_b, kv_norm_g, w_kvf, b_f, k_norm_g, fox_w_q, q_norm_g, fox_w_o):
    batch, seq, d = x.shape
    assert d == D_MODEL and seq % TM == 0 and seq % TQ == 0
    assert ada_w.shape[0] == 2 and sb_w_qkv.shape[0] == 1 and fox_w_q.shape[0] == 1
    m = batch * seq

    c_pad = jnp.zeros((SUBLANES_V7X, d), F32).at[:batch].set(c.astype(F32))
    mod = _ada_mod(c_pad, ada_w, ada_b)[:, :batch]
    kv_mod = _ada_mod(c_pad, kv_ada_w[None], kv_ada_b[None])[0, :batch]

    def vecs(t, n):
        return [t[:, None, j * d:(j + 1) * d] for j in range(n)]

    row = lambda t: t.reshape(1, -1).astype(F32)
    bf = lambda t: t.astype(BF16)

    x2d = x.reshape(m, d)

    sh_a, sc_a, g_a, sh_f, sc_f, g_f = vecs(mod[0], 6)
    qkv = _norm_matmul(x2d, row(norm_attn_g[0]), sh_a, sc_a, bf(sb_w_qkv[0]), seq)
    o = _sb_attn(qkv.reshape(batch, seq, 3 * d), batch, seq)
    x2d = _proj_residual(o.reshape(m, d), bf(sb_w_o[0]), x2d, g_a, seq)
    act = _ffn_in(x2d, row(norm_ffn_g[0]), sh_f, sc_f, bf(w_ffn_in[0]), seq)
    x2d = _proj_residual(act, bf(w_ffn_down[0]), x2d, g_f, seq)

    sh_a, sc_a, g_a, sh_f, sc_f, g_f = vecs(mod[1], 6)
    kv_shift, kv_scale = vecs(kv_mod, 2)
    w_f_pad = jnp.zeros((d, LANES_V7X), F32).at[:, :N_HEADS].set(w_kvf[:, 2 * d:])
    b_f_pad = jnp.zeros((1, LANES_V7X), F32).at[0, :N_HEADS].set(b_f)
    k, v, q, log_f = _l1_proj(
        x2d, row(kv_norm_g), kv_shift, kv_scale, row(norm_attn_g[1]), sh_a, sc_a,
        bf(w_kvf[:, :d]), bf(w_kvf[:, d:2 * d]), bf(w_f_pad), bf(fox_w_q[0]),
        row(jnp.tile(k_norm_g, N_HEADS)), row(jnp.tile(q_norm_g[0], N_HEADS)), b_f_pad, seq)
    fcum_col, fcum_row = _fcum(log_f, batch, seq)
    key_norm_bound = (1.01 * HEAD_DIM ** 0.5 * jnp.max(jnp.abs(k_norm_g))).reshape(1, 1).astype(F32)
    shp = (batch, seq, d)
    o = _fox_attn(q.reshape(shp), k.reshape(shp), v.reshape(shp), fcum_col, fcum_row,
                  key_norm_bound, batch, seq)
    x2d = _proj_residual(o.reshape(m, d), bf(fox_w_o[0]), x2d, g_a, seq)
    act = _ffn_in(x2d, row(norm_ffn_g[1]), sh_f, sc_f, bf(w_ffn_in[1]), seq)
    x2d = _proj_residual(act, bf(w_ffn_down[1]), x2d, g_f, seq)
    return x2d.reshape(batch, seq, d)
```

```python
import functools

import jax
import jax.numpy as jnp
from jax import lax
from jax.experimental import pallas as pl
from jax.experimental.pallas import tpu as pltpu

D_MODEL = 1024
N_HEADS = 16
HEAD_DIM = D_MODEL // N_HEADS
EPS = 1e-6

LANES_V7X = 128
SUBLANES_V7X = 8
VMEM_BYTES_V7X = 64 * 1024 * 1024

HEADS_PER_LANE_TILE = LANES_V7X // HEAD_DIM
N_HEAD_PAIRS = N_HEADS // HEADS_PER_LANE_TILE

TM = 512
TQ = 256
TK = 256

ZERO_EXP = 110.0
SAFE_EXP_RANGE = 80.0
NEG_BIG = -1e30

F32 = jnp.float32
BF16 = jnp.bfloat16


def _vmem_limit(block_bytes):
    want = 2 * block_bytes + 16 * 1024 * 1024
    return int(min(want, VMEM_BYTES_V7X - 8 * 1024 * 1024))


def _split2(x):
    hi = x.astype(BF16)
    lo = (x - hi.astype(F32)).astype(BF16)
    return hi, lo


def _dot(a, b):
    return jnp.dot(a, b, preferred_element_type=F32)


def _dot_nt(a, b):
    return lax.dot_general(a, b, (((1,), (1,)), ((), ())), preferred_element_type=F32)


def _ada_kernel(c_ref, w_ref, b_ref, o_ref):
    c = c_ref[...]
    c_act = c * jax.nn.sigmoid(c)
    o_ref[...] = jnp.dot(c_act, w_ref[...], precision=lax.Precision.HIGHEST,
                         preferred_element_type=F32) + b_ref[...]


def _ada_mod(c_pad, w, b):
    n_layers, d, n = w.shape
    tn = 1024
    rows = c_pad.shape[0]
    block_bytes = 4 * (d * tn + rows * d + tn + rows * tn)
    return pl.pallas_call(
        _ada_kernel,
        out_shape=jax.ShapeDtypeStruct((n_layers, rows, n), F32),
        grid=(n_layers, n // tn),
        in_specs=[
            pl.BlockSpec((rows, d), lambda l, j: (0, 0)),
            pl.BlockSpec((None, d, tn), lambda l, j: (l, 0, j)),
            pl.BlockSpec((None, 1, tn), lambda l, j: (l, 0, j)),
        ],
        out_specs=pl.BlockSpec((None, rows, tn), lambda l, j: (l, 0, j)),
        compiler_params=pltpu.CompilerParams(
            dimension_semantics=("arbitrary", "arbitrary"),
            vmem_limit_bytes=_vmem_limit(block_bytes)),
        name="ada_mod",
    )(c_pad, w, b.reshape(n_layers, 1, n))


def _norm_modulate(x, g, shift, scale):
    ms = jnp.mean(x * x, axis=-1, keepdims=True)
    y = x * lax.rsqrt(ms + EPS)
    return (y * g) * (1.0 + scale) + shift


def _row_spec(d):
    return pl.BlockSpec((1, d), lambda *_: (0, 0))


def _batch_vec_spec(d, tiles_per_batch):
    return pl.BlockSpec((None, 1, d), lambda i, *_: (i // tiles_per_batch, 0, 0))


def _norm_matmul_kernel(x_ref, g_ref, sh_ref, sc_ref, w_ref, o_ref, h_scr):
    @pl.when(pl.program_id(1) == 0)
    def _():
        h_scr[...] = _norm_modulate(x_ref[...], g_ref[...], sh_ref[...], sc_ref[...]).astype(BF16)

    o_ref[...] = _dot(h_scr[...], w_ref[...]).astype(o_ref.dtype)


def _norm_matmul(x2d, g, shift, scale, w, seq):
    m, d = x2d.shape
    n = w.shape[1]
    tn = 1024
    block_bytes = 4 * TM * d + 2 * d * tn + 2 * TM * tn + 2 * TM * d
    return pl.pallas_call(
        _norm_matmul_kernel,
        out_shape=jax.ShapeDtypeStruct((m, n), BF16),
        grid=(m // TM, n // tn),
        in_specs=[
            pl.BlockSpec((TM, d), lambda i, j: (i, 0)),
            _row_spec(d),
            _batch_vec_spec(d, seq // TM),
            _batch_vec_spec(d, seq // TM),
            pl.BlockSpec((d, tn), lambda i, j: (0, j)),
        ],
        out_specs=pl.BlockSpec((TM, tn), lambda i, j: (i, j)),
        scratch_shapes=[pltpu.VMEM((TM, d), BF16)],
        compiler_params=pltpu.CompilerParams(
            dimension_semantics=("arbitrary", "arbitrary"),
            vmem_limit_bytes=_vmem_limit(block_bytes)),
        name="norm_matmul",
    )(x2d, g, shift, scale, w)


def _ffn_in_kernel(x_ref, g_ref, sh_ref, sc_ref, wg_ref, wu_ref, o_ref, h_scr):
    @pl.when(pl.program_id(1) == 0)
    def _():
        h_scr[...] = _norm_modulate(x_ref[...], g_ref[...], sh_ref[...], sc_ref[...]).astype(BF16)

    h = h_scr[...]
    gate = _dot(h, wg_ref[...])
    up = _dot(h, wu_ref[...])
    o_ref[...] = (gate * jax.nn.sigmoid(gate) * up).astype(o_ref.dtype)


def _ffn_in(x2d, g, shift, scale, w_in, seq):
    m, d = x2d.shape
    f = w_in.shape[1] // 2
    n_tiles = 2
    tn = f // n_tiles
    assert tn * n_tiles == f and tn % LANES_V7X == 0
    block_bytes = 4 * TM * d + 2 * 2 * d * tn + 2 * TM * tn + 2 * TM * d + 2 * 4 * TM * tn
    return pl.pallas_call(
        _ffn_in_kernel,
        out_shape=jax.ShapeDtypeStruct((m, f), BF16),
        grid=(m // TM, n_tiles),
        in_specs=[
            pl.BlockSpec((TM, d), lambda i, j: (i, 0)),
            _row_spec(d),
            _batch_vec_spec(d, seq // TM),
            _batch_vec_spec(d, seq // TM),
            pl.BlockSpec((d, tn), lambda i, j: (0, j)),
            pl.BlockSpec((d, tn), lambda i, j: (0, j + n_tiles)),
        ],
        out_specs=pl.BlockSpec((TM, tn), lambda i, j: (i, j)),
        scratch_shapes=[pltpu.VMEM((TM, d), BF16)],
        compiler_params=pltpu.CompilerParams(
            dimension_semantics=("arbitrary", "arbitrary"),
            vmem_limit_bytes=_vmem_limit(block_bytes)),
        name="ffn_in",
    )(x2d, g, shift, scale, w_in, w_in)


def _proj_residual_kernel(a_ref, w_ref, x_ref, gate_ref, o_ref):
    o_ref[...] = x_ref[...] + gate_ref[...] * _dot(a_ref[...], w_ref[...])


def _proj_residual(a, w, x2d, gate, seq):
    m, k = a.shape
    d = w.shape[1]
    block_bytes = 2 * TM * k + 2 * k * d + 2 * 4 * TM * d
    return pl.pallas_call(
        _proj_residual_kernel,
        out_shape=jax.ShapeDtypeStruct((m, d), F32),
        grid=(m // TM,),
        in_specs=[
            pl.BlockSpec((TM, k), lambda i: (i, 0)),
            pl.BlockSpec((k, d), lambda i: (0, 0)),
            pl.BlockSpec((TM, d), lambda i: (i, 0)),
            _batch_vec_spec(d, seq // TM),
        ],
        out_specs=pl.BlockSpec((TM, d), lambda i: (i, 0)),
        compiler_params=pltpu.CompilerParams(
            dimension_semantics=("arbitrary",),
            vmem_limit_bytes=_vmem_limit(block_bytes)),
        name="proj_residual",
    )(a, w, x2d, gate)


def _head_rms_norm(t, gain, group_mean):
    outs = []
    for c in range(t.shape[1] // LANES_V7X):
        tc = t[:, c * LANES_V7X:(c + 1) * LANES_V7X]
        hi, lo = _split2(tc * tc)
        ms = _dot(hi, group_mean) + _dot(lo, group_mean)
        outs.append(tc * lax.rsqrt(ms + EPS) * gain[:, c * LANES_V7X:(c + 1) * LANES_V7X])
    return jnp.concatenate(outs, axis=1)


def _l1_proj_kernel(x_ref, gkv_ref, shkv_ref, sckv_ref, gq_ref, shq_ref, scq_ref,
                    wk_ref, wv_ref, wf_ref, wq_ref, kg_ref, qg_ref, bf_ref, gm_ref,
                    k_ref, v_ref, q_ref, lf_ref):
    x = x_ref[...]
    ms = jnp.mean(x * x, axis=-1, keepdims=True)
    y = x * lax.rsqrt(ms + EPS)
    h_kv = ((y * gkv_ref[...]) * (1.0 + sckv_ref[...]) + shkv_ref[...]).astype(BF16)
    h_q = ((y * gq_ref[...]) * (1.0 + scq_ref[...]) + shq_ref[...]).astype(BF16)
    gm = gm_ref[...]
    k_ref[...] = _head_rms_norm(_dot(h_kv, wk_ref[...]), kg_ref[...], gm).astype(BF16)
    v_ref[...] = _dot(h_kv, wv_ref[...]).astype(BF16)
    q = _head_rms_norm(_dot(h_q, wq_ref[...]), qg_ref[...], gm)
    q_ref[...] = (q * (HEAD_DIM ** -0.5)).astype(BF16)
    lf_ref[...] = jax.nn.log_sigmoid(_dot(h_kv, wf_ref[...]) + bf_ref[...])


def _l1_proj(x2d, g_kv, sh_kv, sc_kv, g_q, sh_q, sc_q, wk, wv, wf, wq, k_gain, q_gain, b_f, seq):
    m, d = x2d.shape
    lane = jnp.arange(LANES_V7X)
    group_mean = jnp.where(lane[:, None] // HEAD_DIM == lane[None, :] // HEAD_DIM,
                           1.0 / HEAD_DIM, 0.0).astype(BF16)
    tpb = seq // TM
    block_bytes = 4 * TM * d + 3 * 2 * d * d + 2 * d * LANES_V7X + 3 * 2 * TM * d + 4 * TM * LANES_V7X
    full = lambda r, c: pl.BlockSpec((r, c), lambda i: (0, 0))
    return pl.pallas_call(
        _l1_proj_kernel,
        out_shape=(jax.ShapeDtypeStruct((m, d), BF16), jax.ShapeDtypeStruct((m, d), BF16),
                   jax.ShapeDtypeStruct((m, d), BF16), jax.ShapeDtypeStruct((m, LANES_V7X), F32)),
        grid=(m // TM,),
        in_specs=[
            pl.BlockSpec((TM, d), lambda i: (i, 0)),
            _row_spec(d), _batch_vec_spec(d, tpb), _batch_vec_spec(d, tpb),
            _row_spec(d), _batch_vec_spec(d, tpb), _batch_vec_spec(d, tpb),
            full(d, d), full(d, d), full(d, LANES_V7X), full(d, d),
            _row_spec(d), _row_spec(d), _row_spec(LANES_V7X), full(LANES_V7X, LANES_V7X),
        ],
        out_specs=(pl.BlockSpec((TM, d), lambda i: (i, 0)), pl.BlockSpec((TM, d), lambda i: (i, 0)),
                   pl.BlockSpec((TM, d), lambda i: (i, 0)), pl.BlockSpec((TM, LANES_V7X), lambda i: (i, 0))),
        compiler_params=pltpu.CompilerParams(
            dimension_semantics=("arbitrary",),
            vmem_limit_bytes=_vmem_limit(block_bytes)),
        name="l1_proj",
    )(x2d, g_kv, sh_kv, sc_kv, g_q, sh_q, sc_q, wk, wv, wf, wq, k_gain, q_gain, b_f, group_mean)


N_SPLIT = 3


def _split3(x):
    x0 = x.astype(BF16)
    r1 = x - x0.astype(F32)
    x1 = r1.astype(BF16)
    x2 = (r1 - x1.astype(F32)).astype(BF16)
    return x0, x1, x2


def _fox_prep_kernel(kb_ref, lf_ref, q_ref, k_ref, qa_ref, ka_ref, stats_ref, carry_scr):
    @pl.when(pl.program_id(1) == 0)
    def _():
        carry_scr[...] = jnp.zeros_like(carry_scr)

    r = lax.broadcasted_iota(jnp.int32, (TK, TK), 0)
    c = lax.broadcasted_iota(jnp.int32, (TK, TK), 1)
    tri = jnp.where(c <= r, 1.0, 0.0).astype(BF16)
    x0, x1, x2 = _split3(lf_ref[...])
    cs = _dot(tri, x0) + _dot(tri, x1) + _dot(tri, x2) + carry_scr[0:1, :]
    carry_scr[...] = jnp.broadcast_to(cs[TK - 1:TK, :], carry_scr.shape)

    d_idx = lax.broadcasted_iota(jnp.int32, (D_MODEL, LANES_V7X), 0)
    h_idx = lax.broadcasted_iota(jnp.int32, (D_MODEL, LANES_V7X), 1)
    head_of = jnp.where(d_idx // HEAD_DIM == h_idx, 1.0, 0.0).astype(BF16)
    qf = q_ref[...].astype(F32)
    hi, lo = _split2(qf * qf)
    reach = jnp.sqrt(_dot(hi, head_of) + _dot(lo, head_of)) * kb_ref[0, 0]

    stats_ref[...] = jnp.concatenate([
        jnp.max(2.0 * reach + cs, axis=0, keepdims=True),
        cs[0:1, :],
        jnp.max(reach, axis=0, keepdims=True),
        jnp.zeros((SUBLANES_V7X - 3, LANES_V7X), F32)], axis=0)

    q_terms = jnp.concatenate(_split3(cs - reach), axis=1)
    k_terms = jnp.concatenate(_split3(-cs), axis=1)
    src = lax.broadcasted_iota(jnp.int32, (N_SPLIT * LANES_V7X, LANES_V7X), 0)
    dst = lax.broadcasted_iota(jnp.int32, (N_SPLIT * LANES_V7X, LANES_V7X), 1)
    lane = lax.broadcasted_iota(jnp.int32, (1, LANES_V7X), 1)
    for h in range(N_HEADS):
        pair, parity = divmod(h, HEADS_PER_LANE_TILE)
        spare = HEAD_DIM * (1 - parity)
        to_q = jnp.where((src % LANES_V7X == h) & (dst == spare + src // LANES_V7X), 1.0, 0.0)
        to_k = jnp.where((src % LANES_V7X == h) & (dst == spare + N_SPLIT + src // LANES_V7X), 1.0, 0.0)
        ones_q = jnp.where((lane >= spare + N_SPLIT) & (lane < spare + 2 * N_SPLIT), 1.0, 0.0)
        ones_k = jnp.where((lane >= spare) & (lane < spare + N_SPLIT), 1.0, 0.0)
        aug_q = (_dot(q_terms, to_q.astype(BF16)) + ones_q).astype(BF16)
        aug_k = (_dot(k_terms, to_k.astype(BF16)) + ones_k).astype(BF16)
        own = (lane // HEAD_DIM) == parity
        cols = slice(pair * LANES_V7X, (pair + 1) * LANES_V7X)
        qa_ref[h] = jnp.where(own, q_ref[:, cols], aug_q)
        ka_ref[h] = jnp.where(own, k_ref[:, cols], aug_k)


def _fox_prep(log_f, q, k, key_norm_bound, batch, seq):
    assert TQ == TK
    d = q.shape[1]
    block_bytes = (4 * TK * LANES_V7X + 2 * 2 * TK * d + 2 * 2 * N_HEADS * TK * LANES_V7X
                   + 4 * SUBLANES_V7X * LANES_V7X)
    aug_shape = jax.ShapeDtypeStruct((batch, N_HEADS, seq, LANES_V7X), BF16)
    aug_spec = pl.BlockSpec((None, N_HEADS, TK, LANES_V7X), lambda b, t: (b, 0, t, 0))
    return pl.pallas_call(
        _fox_prep_kernel,
        out_shape=(aug_shape, aug_shape,
                   jax.ShapeDtypeStruct((batch, seq // TK, SUBLANES_V7X, LANES_V7X), F32)),
        grid=(batch, seq // TK),
        in_specs=[
            pl.BlockSpec(memory_space=pltpu.SMEM),
            pl.BlockSpec((None, TK, LANES_V7X), lambda b, t: (b, t, 0)),
            pl.BlockSpec((None, TK, d), lambda b, t: (b, t, 0)),
            pl.BlockSpec((None, TK, d), lambda b, t: (b, t, 0)),
        ],
        out_specs=(aug_spec, aug_spec,
                   pl.BlockSpec((None, None, SUBLANES_V7X, LANES_V7X), lambda b, t: (b, t, 0, 0))),
        scratch_shapes=[pltpu.VMEM((SUBLANES_V7X, LANES_V7X), F32)],
        compiler_params=pltpu.CompilerParams(
            dimension_semantics=("arbitrary", "arbitrary"),
            vmem_limit_bytes=_vmem_limit(block_bytes)),
        name="fox_prep",
    )(key_norm_bound, log_f.reshape(batch, seq, LANES_V7X), q.reshape(batch, seq, d),
      k.reshape(batch, seq, d))


def _head_select(head):
    lane = lax.broadcasted_iota(jnp.int32, (1, LANES_V7X), 1)
    return (lane // HEAD_DIM) == head


def _widen(col128, width):
    reps = width // LANES_V7X
    return col128 if reps == 1 else jnp.concatenate([col128] * reps, axis=1)


def _sb_kernel(q_ref, k_ref, v_ref, o_ref):
    i = pl.program_id(2)
    row = lax.broadcasted_iota(jnp.int32, (TQ, TK), 0)
    col = lax.broadcasted_iota(jnp.int32, (TQ, TK), 1)
    diag_mask = col < row
    r = lax.broadcasted_iota(jnp.int32, (TK, TK), 0)
    c = lax.broadcasted_iota(jnp.int32, (TK, TK), 1)
    after = jnp.where(r > c, 1.0, 0.0).astype(BF16)
    ones = jnp.ones((TK, LANES_V7X), BF16)
    ones_wide = jnp.ones((TK, TK), BF16)
    q = q_ref[...] * (HEAD_DIM ** -0.5)

    def softplus(z):
        return jnp.maximum(z, 0.0) + jnp.log(1.0 + jnp.exp(-jnp.abs(z)))

    has_prev = i > 0
    j_prev = jnp.maximum(i - 1, 0)
    k_d = k_ref[pl.ds(i * TK, TK), :]
    v_d = v_ref[pl.ds(i * TK, TK), :]
    k_p = k_ref[pl.ds(j_prev * TK, TK), :]
    v_p = v_ref[pl.ds(j_prev * TK, TK), :]

    def band(qh):
        z_d = _dot_nt(qh, k_d)
        z_p = jnp.where(has_prev, _dot_nt(qh, k_p), NEG_BIG)
        sp_d = jnp.where(diag_mask, softplus(z_d), 0.0)
        sp_p = softplus(z_p)
        hi_d, lo_d = _split2(sp_d)
        hi_p, lo_p = _split2(sp_p)
        within_d = _dot(hi_d, after) + _dot(lo_d, after)
        within_p = (_dot(hi_p, after) + _dot(lo_p, after)
                    + _dot(hi_d, ones_wide) + _dot(lo_d, ones_wide))
        a_d = jnp.where(diag_mask, jnp.exp((z_d - sp_d) - within_d), 0.0)
        a_p = jnp.exp((z_p - sp_p) - within_p)
        acc = _dot(a_d.astype(BF16), v_d) + _dot(a_p.astype(BF16), v_p)
        total = (within_p + sp_p)[:, 0:1]
        return jnp.broadcast_to(-total, (TQ, LANES_V7X)), acc

    def block(qh, j, carry, acc):
        kj = k_ref[pl.ds(j * TK, TK), :]
        vj = v_ref[pl.ds(j * TK, TK), :]
        z = _dot_nt(qh, kj)
        sp = softplus(z)
        hi, lo = _split2(sp)
        within = _dot(hi, after) + _dot(lo, after)
        a = jnp.exp((z - sp) - within + _widen(carry, TK))
        acc = acc + _dot(a.astype(BF16), vj)
        carry = carry - (_dot(hi, ones) + _dot(lo, ones))
        return carry, acc

    q_heads = [jnp.where(_head_select(head), q, 0.0).astype(BF16)
               for head in range(HEADS_PER_LANE_TILE)]
    bands = [band(qh) for qh in q_heads]
    outs = []
    for qh, (carry, acc) in zip(q_heads, bands):
        def cond(state):
            j, carry, _ = state
            return jnp.logical_and(j >= 0, jnp.max(carry) > -ZERO_EXP)

        def body(state, qh=qh):
            j, carry, acc = state
            carry, acc = block(qh, j, carry, acc)
            return j - 1, carry, acc

        outs.append(lax.while_loop(cond, body, (i - 2, carry, acc))[2])
    o_ref[...] = jnp.where(_head_select(0), outs[0], outs[1]).astype(o_ref.dtype)


def _sb_attn(qkv, batch, seq):
    assert TQ == TK
    block_bytes = 2 * (2 * TQ * LANES_V7X + 2 * seq * LANES_V7X)
    return pl.pallas_call(
        _sb_kernel,
        out_shape=jax.ShapeDtypeStruct((batch, seq, D_MODEL), BF16),
        grid=(batch, N_HEAD_PAIRS, seq // TQ),
        in_specs=[
            pl.BlockSpec((None, TQ, LANES_V7X), lambda b, p, i: (b, i, p)),
            pl.BlockSpec((None, seq, LANES_V7X), lambda b, p, i: (b, 0, N_HEAD_PAIRS + p)),
            pl.BlockSpec((None, seq, LANES_V7X), lambda b, p, i: (b, 0, 2 * N_HEAD_PAIRS + p)),
        ],
        out_specs=pl.BlockSpec((None, TQ, LANES_V7X), lambda b, p, i: (b, i, p)),
        compiler_params=pltpu.CompilerParams(
            dimension_semantics=("arbitrary", "arbitrary", "arbitrary"),
            vmem_limit_bytes=_vmem_limit(block_bytes)),
        name="sb_attn",
    )(qkv, qkv, qkv)


def _fox_kernel(a_ref, fk0_ref, g_ref, qa_ref, ka_ref, v_ref, o_ref):
    b = pl.program_id(0)
    p_id = pl.program_id(1)
    i = pl.program_id(2)
    n_blocks = fk0_ref.shape[1]
    row = lax.broadcasted_iota(jnp.int32, (TQ, TK), 0)
    col = lax.broadcasted_iota(jnp.int32, (TQ, TK), 1)
    diag_mask = col <= row

    outs = []
    for head in range(HEADS_PER_LANE_TILE):
        bh = (b * N_HEAD_PAIRS + p_id) * HEADS_PER_LANE_TILE + head
        bound = a_ref[bh, i]
        qa = qa_ref[head]
        own = _head_select(head)

        def scores(j, head=head, qa=qa):
            return _dot_nt(qa, ka_ref[head, pl.ds(j * TK, TK), :])

        def absorb(s, j, acc, m, online, own=own):
            vj = v_ref[pl.ds(j * TK, TK), :]
            vj = jnp.where(own, vj, jnp.ones_like(vj))
            if online:
                m_new = jnp.maximum(m, jnp.max(s, axis=-1, keepdims=True))
                p = jnp.exp(s - m_new)
                return jnp.exp(m - m_new) * acc + _dot(p.astype(BF16), vj), m_new
            return acc + _dot(jnp.exp(s).astype(BF16), vj), m

        def walk(online, scores=scores, absorb=absorb, bound=bound, bh=bh):
            def more(n):
                j = i - 1 - n
                right = fk0_ref[bh, jnp.minimum(j + 1, n_blocks - 1)]
                return jnp.logical_and(j >= 0, bound - right > -ZERO_EXP)

            n_left = lax.while_loop(more, lambda n: n + 1, jnp.int32(0))

            def body(t, state):
                s, acc, m = state
                j = i - 1 - t
                s_next = scores(j)
                acc, m = absorb(s, j + 1, acc, m, online)
                return s_next, acc, m

            s0 = jnp.where(diag_mask, scores(i), NEG_BIG)
            acc0 = jnp.zeros((TQ, LANES_V7X), F32)
            m0 = jnp.full((TQ, 1), NEG_BIG, F32)
            s, acc, m = lax.fori_loop(0, n_left, body, (s0, acc0, m0))
            return absorb(s, i - n_left, acc, m, online)[0]

        acc = lax.cond(2.0 * g_ref[bh, i] < SAFE_EXP_RANGE,
                       functools.partial(walk, False), functools.partial(walk, True))
        denom = pltpu.roll(acc, HEAD_DIM, axis=1)
        outs.append(acc / denom)
    o_ref[...] = jnp.where(_head_select(0), outs[0], outs[1]).astype(o_ref.dtype)


def _fox_attn(q_aug, k_aug, v, walk_bound, fk_block_start, max_reach, batch, seq):
    assert TQ == TK
    block_bytes = 2 * (HEADS_PER_LANE_TILE * (TQ + seq) * LANES_V7X + seq * LANES_V7X + TQ * LANES_V7X)
    smem = pl.BlockSpec(memory_space=pltpu.SMEM)
    return pl.pallas_call(
        _fox_kernel,
        out_shape=jax.ShapeDtypeStruct((batch, seq, D_MODEL), BF16),
        grid=(batch, N_HEAD_PAIRS, seq // TQ),
        in_specs=[
            smem, smem, smem,
            pl.BlockSpec((None, HEADS_PER_LANE_TILE, TQ, LANES_V7X), lambda b, p, i: (b, p, i, 0)),
            pl.BlockSpec((None, HEADS_PER_LANE_TILE, seq, LANES_V7X), lambda b, p, i: (b, p, 0, 0)),
            pl.BlockSpec((None, seq, LANES_V7X), lambda b, p, i: (b, 0, p)),
        ],
        out_specs=pl.BlockSpec((None, TQ, LANES_V7X), lambda b, p, i: (b, i, p)),
        compiler_params=pltpu.CompilerParams(
            dimension_semantics=("arbitrary", "arbitrary", "arbitrary"),
            vmem_limit_bytes=_vmem_limit(block_bytes)),
        name="fox_attn",
    )(walk_bound, fk_block_start, max_reach, q_aug, k_aug, v)


def kernel(x, c, ada_w, ada_b, norm_attn_g, norm_ffn_g, w_ffn_in, w_ffn_down, sb_w_qkv, sb_w_o,
           kv_ada_w, kv_ada_b, kv_norm_g, w_kvf, b_f, k_norm_g, fox_w_q, q_norm_g, fox_w_o):
    batch, seq, d = x.shape
    assert d == D_MODEL and seq % TM == 0 and seq % TQ == 0
    assert ada_w.shape[0] == 2 and sb_w_qkv.shape[0] == 1 and fox_w_q.shape[0] == 1
    m = batch * seq

    c_pad = jnp.zeros((SUBLANES_V7X, d), F32).at[:batch].set(c.astype(F32))
    mod = _ada_mod(c_pad, ada_w, ada_b)[:, :batch]
    kv_mod = _ada_mod(c_pad, kv_ada_w[None], kv_ada_b[None])[0, :batch]

    def vecs(t, n):
        return [t[:, None, j * d:(j + 1) * d] for j in range(n)]

    row = lambda t: t.reshape(1, -1).astype(F32)
    bf = lambda t: t.astype(BF16)

    x2d = x.reshape(m, d)

    sh_a, sc_a, g_a, sh_f, sc_f, g_f = vecs(mod[0], 6)
    qkv = _norm_matmul(x2d, row(norm_attn_g[0]), sh_a, sc_a, bf(sb_w_qkv[0]), seq)
    o = _sb_attn(qkv.reshape(batch, seq, 3 * d), batch, seq)
    x2d = _proj_residual(o.reshape(m, d), bf(sb_w_o[0]), x2d, g_a, seq)
    act = _ffn_in(x2d, row(norm_ffn_g[0]), sh_f, sc_f, bf(w_ffn_in[0]), seq)
    x2d = _proj_residual(act, bf(w_ffn_down[0]), x2d, g_f, seq)

    sh_a, sc_a, g_a, sh_f, sc_f, g_f = vecs(mod[1], 6)
    kv_shift, kv_scale = vecs(kv_mod, 2)
    w_f_pad = jnp.zeros((d, LANES_V7X), F32).at[:, :N_HEADS].set(w_kvf[:, 2 * d:])
    b_f_pad = jnp.zeros((1, LANES_V7X), F32).at[0, :N_HEADS].set(b_f)
    k, v, q, log_f = _l1_proj(
        x2d, row(kv_norm_g), kv_shift, kv_scale, row(norm_attn_g[1]), sh_a, sc_a,
        bf(w_kvf[:, :d]), bf(w_kvf[:, d:2 * d]), bf(w_f_pad), bf(fox_w_q[0]),
        row(jnp.tile(k_norm_g, N_HEADS)), row(jnp.tile(q_norm_g[0], N_HEADS)), b_f_pad, seq)
    key_norm_bound = (1.01 * HEAD_DIM ** 0.5 * jnp.max(jnp.abs(k_norm_g))).reshape(1, 1).astype(F32)
    q_aug, k_aug, stats = _fox_prep(log_f, q, k, key_norm_bound, batch, seq)
    tables = stats[:, :, :3, :N_HEADS].transpose(2, 0, 3, 1).reshape(3, batch * N_HEADS, seq // TK)
    o = _fox_attn(q_aug, k_aug, v.reshape(batch, seq, d), tables[0], tables[1], tables[2], batch, seq)
    x2d = _proj_residual(o.reshape(m, d), bf(fox_w_o[0]), x2d, g_a, seq)
    act = _ffn_in(x2d, row(norm_ffn_g[1]), sh_f, sc_f, bf(w_ffn_in[1]), seq)
    x2d = _proj_residual(act, bf(w_ffn_down[1]), x2d, g_f, seq)
    return x2d.reshape(batch, seq, d)
```

```python
import functools

import jax
import jax.numpy as jnp
from jax import lax
from jax.experimental import pallas as pl
from jax.experimental.pallas import tpu as pltpu

D_MODEL = 1024
N_HEADS = 16
HEAD_DIM = D_MODEL // N_HEADS
EPS = 1e-6

LANES_V7X = 128
SUBLANES_V7X = 8
VMEM_BYTES_V7X = 64 * 1024 * 1024

HEADS_PER_LANE_TILE = LANES_V7X // HEAD_DIM
N_HEAD_PAIRS = N_HEADS // HEADS_PER_LANE_TILE

TM = 512
TQ = 512
TK = 512
SB_T = 256
SB_SUB = 2

ZERO_EXP = 110.0
SAFE_EXP_RANGE = 80.0
NEG_BIG = -1e30

F32 = jnp.float32
BF16 = jnp.bfloat16


def _vmem_limit(block_bytes):
    want = 2 * block_bytes + 16 * 1024 * 1024
    return int(min(want, VMEM_BYTES_V7X - 8 * 1024 * 1024))


def _split2(x):
    hi = x.astype(BF16)
    lo = (x - hi.astype(F32)).astype(BF16)
    return hi, lo


def _dot(a, b):
    return jnp.dot(a, b, preferred_element_type=F32)


def _dot_nt(a, b):
    return lax.dot_general(a, b, (((1,), (1,)), ((), ())), preferred_element_type=F32)


def _ada_kernel(c_ref, w_ref, b_ref, o_ref):
    c = c_ref[...]
    c_act = c * jax.nn.sigmoid(c)
    o_ref[...] = jnp.dot(c_act, w_ref[...], precision=lax.Precision.HIGHEST,
                         preferred_element_type=F32) + b_ref[...]


def _ada_mod(c_pad, w, b):
    n_layers, d, n = w.shape
    tn = 1024
    rows = c_pad.shape[0]
    block_bytes = 4 * (d * tn + rows * d + tn + rows * tn)
    return pl.pallas_call(
        _ada_kernel,
        out_shape=jax.ShapeDtypeStruct((n_layers, rows, n), F32),
        grid=(n_layers, n // tn),
        in_specs=[
            pl.BlockSpec((rows, d), lambda l, j: (0, 0)),
            pl.BlockSpec((None, d, tn), lambda l, j: (l, 0, j)),
            pl.BlockSpec((None, 1, tn), lambda l, j: (l, 0, j)),
        ],
        out_specs=pl.BlockSpec((None, rows, tn), lambda l, j: (l, 0, j)),
        compiler_params=pltpu.CompilerParams(
            dimension_semantics=("arbitrary", "arbitrary"),
            vmem_limit_bytes=_vmem_limit(block_bytes)),
        name="ada_mod",
    )(c_pad, w, b.reshape(n_layers, 1, n))


def _norm_modulate(x, g, shift, scale):
    ms = jnp.mean(x * x, axis=-1, keepdims=True)
    y = x * lax.rsqrt(ms + EPS)
    return (y * g) * (1.0 + scale) + shift


def _row_spec(d):
    return pl.BlockSpec((1, d), lambda *_: (0, 0))


def _batch_vec_spec(d, tiles_per_batch):
    return pl.BlockSpec((None, 1, d), lambda i, *_: (i // tiles_per_batch, 0, 0))


def _norm_matmul_kernel(x_ref, g_ref, sh_ref, sc_ref, w_ref, o_ref, h_scr):
    @pl.when(pl.program_id(1) == 0)
    def _():
        h_scr[...] = _norm_modulate(x_ref[...], g_ref[...], sh_ref[...], sc_ref[...]).astype(BF16)

    o_ref[...] = _dot(h_scr[...], w_ref[...]).astype(o_ref.dtype)


def _norm_matmul(x2d, g, shift, scale, w, seq):
    m, d = x2d.shape
    n = w.shape[1]
    tn = 1024
    block_bytes = 4 * TM * d + 2 * d * tn + 2 * TM * tn + 2 * TM * d
    return pl.pallas_call(
        _norm_matmul_kernel,
        out_shape=jax.ShapeDtypeStruct((m, n), BF16),
        grid=(m // TM, n // tn),
        in_specs=[
            pl.BlockSpec((TM, d), lambda i, j: (i, 0)),
            _row_spec(d),
            _batch_vec_spec(d, seq // TM),
            _batch_vec_spec(d, seq // TM),
            pl.BlockSpec((d, tn), lambda i, j: (0, j)),
        ],
        out_specs=pl.BlockSpec((TM, tn), lambda i, j: (i, j)),
        scratch_shapes=[pltpu.VMEM((TM, d), BF16)],
        compiler_params=pltpu.CompilerParams(
            dimension_semantics=("arbitrary", "arbitrary"),
            vmem_limit_bytes=_vmem_limit(block_bytes)),
        name="norm_matmul",
    )(x2d, g, shift, scale, w)


def _ffn_in_kernel(x_ref, g_ref, sh_ref, sc_ref, wg_ref, wu_ref, o_ref, h_scr):
    @pl.when(pl.program_id(1) == 0)
    def _():
        h_scr[...] = _norm_modulate(x_ref[...], g_ref[...], sh_ref[...], sc_ref[...]).astype(BF16)

    h = h_scr[...]
    gate = _dot(h, wg_ref[...])
    up = _dot(h, wu_ref[...])
    o_ref[...] = (gate * jax.nn.sigmoid(gate) * up).astype(o_ref.dtype)


def _ffn_in(x2d, g, shift, scale, w_in, seq):
    m, d = x2d.shape
    f = w_in.shape[1] // 2
    n_tiles = 2
    tn = f // n_tiles
    assert tn * n_tiles == f and tn % LANES_V7X == 0
    block_bytes = 4 * TM * d + 2 * 2 * d * tn + 2 * TM * tn + 2 * TM * d + 2 * 4 * TM * tn
    return pl.pallas_call(
        _ffn_in_kernel,
        out_shape=jax.ShapeDtypeStruct((m, f), BF16),
        grid=(m // TM, n_tiles),
        in_specs=[
            pl.BlockSpec((TM, d), lambda i, j: (i, 0)),
            _row_spec(d),
            _batch_vec_spec(d, seq // TM),
            _batch_vec_spec(d, seq // TM),
            pl.BlockSpec((d, tn), lambda i, j: (0, j)),
            pl.BlockSpec((d, tn), lambda i, j: (0, j + n_tiles)),
        ],
        out_specs=pl.BlockSpec((TM, tn), lambda i, j: (i, j)),
        scratch_shapes=[pltpu.VMEM((TM, d), BF16)],
        compiler_params=pltpu.CompilerParams(
            dimension_semantics=("arbitrary", "arbitrary"),
            vmem_limit_bytes=_vmem_limit(block_bytes)),
        name="ffn_in",
    )(x2d, g, shift, scale, w_in, w_in)


def _proj_residual_kernel(a_ref, w_ref, x_ref, gate_ref, o_ref):
    o_ref[...] = x_ref[...] + gate_ref[...] * _dot(a_ref[...], w_ref[...])


def _proj_residual(a, w, x2d, gate, seq):
    m, k = a.shape
    d = w.shape[1]
    block_bytes = 2 * TM * k + 2 * k * d + 2 * 4 * TM * d
    return pl.pallas_call(
        _proj_residual_kernel,
        out_shape=jax.ShapeDtypeStruct((m, d), F32),
        grid=(m // TM,),
        in_specs=[
            pl.BlockSpec((TM, k), lambda i: (i, 0)),
            pl.BlockSpec((k, d), lambda i: (0, 0)),
            pl.BlockSpec((TM, d), lambda i: (i, 0)),
            _batch_vec_spec(d, seq // TM),
        ],
        out_specs=pl.BlockSpec((TM, d), lambda i: (i, 0)),
        compiler_params=pltpu.CompilerParams(
            dimension_semantics=("arbitrary",),
            vmem_limit_bytes=_vmem_limit(block_bytes)),
        name="proj_residual",
    )(a, w, x2d, gate)


def _head_rms_norm(t, gain, group_mean):
    outs = []
    for c in range(t.shape[1] // LANES_V7X):
        tc = t[:, c * LANES_V7X:(c + 1) * LANES_V7X]
        hi, lo = _split2(tc * tc)
        ms = _dot(hi, group_mean) + _dot(lo, group_mean)
        outs.append(tc * lax.rsqrt(ms + EPS) * gain[:, c * LANES_V7X:(c + 1) * LANES_V7X])
    return jnp.concatenate(outs, axis=1)


def _l1_proj_kernel(x_ref, gkv_ref, shkv_ref, sckv_ref, gq_ref, shq_ref, scq_ref,
                    wk_ref, wv_ref, wf_ref, wq_ref, kg_ref, qg_ref, bf_ref, gm_ref,
                    k_ref, v_ref, q_ref, lf_ref):
    x = x_ref[...]
    ms = jnp.mean(x * x, axis=-1, keepdims=True)
    y = x * lax.rsqrt(ms + EPS)
    h_kv = ((y * gkv_ref[...]) * (1.0 + sckv_ref[...]) + shkv_ref[...]).astype(BF16)
    h_q = ((y * gq_ref[...]) * (1.0 + scq_ref[...]) + shq_ref[...]).astype(BF16)
    gm = gm_ref[...]
    k_ref[...] = _head_rms_norm(_dot(h_kv, wk_ref[...]), kg_ref[...], gm).astype(BF16)
    v_ref[...] = _dot(h_kv, wv_ref[...]).astype(BF16)
    q = _head_rms_norm(_dot(h_q, wq_ref[...]), qg_ref[...], gm)
    q_ref[...] = (q * (HEAD_DIM ** -0.5)).astype(BF16)
    lf_ref[...] = jax.nn.log_sigmoid(_dot(h_kv, wf_ref[...]) + bf_ref[...])


def _l1_proj(x2d, g_kv, sh_kv, sc_kv, g_q, sh_q, sc_q, wk, wv, wf, wq, k_gain, q_gain, b_f, seq):
    m, d = x2d.shape
    lane = jnp.arange(LANES_V7X)
    group_mean = jnp.where(lane[:, None] // HEAD_DIM == lane[None, :] // HEAD_DIM,
                           1.0 / HEAD_DIM, 0.0).astype(BF16)
    tpb = seq // TM
    block_bytes = 4 * TM * d + 3 * 2 * d * d + 2 * d * LANES_V7X + 3 * 2 * TM * d + 4 * TM * LANES_V7X
    full = lambda r, c: pl.BlockSpec((r, c), lambda i: (0, 0))
    return pl.pallas_call(
        _l1_proj_kernel,
        out_shape=(jax.ShapeDtypeStruct((m, d), BF16), jax.ShapeDtypeStruct((m, d), BF16),
                   jax.ShapeDtypeStruct((m, d), BF16), jax.ShapeDtypeStruct((m, LANES_V7X), F32)),
        grid=(m // TM,),
        in_specs=[
            pl.BlockSpec((TM, d), lambda i: (i, 0)),
            _row_spec(d), _batch_vec_spec(d, tpb), _batch_vec_spec(d, tpb),
            _row_spec(d), _batch_vec_spec(d, tpb), _batch_vec_spec(d, tpb),
            full(d, d), full(d, d), full(d, LANES_V7X), full(d, d),
            _row_spec(d), _row_spec(d), _row_spec(LANES_V7X), full(LANES_V7X, LANES_V7X),
        ],
        out_specs=(pl.BlockSpec((TM, d), lambda i: (i, 0)), pl.BlockSpec((TM, d), lambda i: (i, 0)),
                   pl.BlockSpec((TM, d), lambda i: (i, 0)), pl.BlockSpec((TM, LANES_V7X), lambda i: (i, 0))),
        compiler_params=pltpu.CompilerParams(
            dimension_semantics=("arbitrary",),
            vmem_limit_bytes=_vmem_limit(block_bytes)),
        name="l1_proj",
    )(x2d, g_kv, sh_kv, sc_kv, g_q, sh_q, sc_q, wk, wv, wf, wq, k_gain, q_gain, b_f, group_mean)


N_SPLIT = 3


def _split3(x):
    x0 = x.astype(BF16)
    r1 = x - x0.astype(F32)
    x1 = r1.astype(BF16)
    x2 = (r1 - x1.astype(F32)).astype(BF16)
    return x0, x1, x2


def _fox_prep_kernel(kb_ref, lf_ref, q_ref, k_ref, qa_ref, ka_ref, stats_ref, carry_scr):
    @pl.when(pl.program_id(1) == 0)
    def _():
        carry_scr[...] = jnp.zeros_like(carry_scr)

    r = lax.broadcasted_iota(jnp.int32, (TK, TK), 0)
    c = lax.broadcasted_iota(jnp.int32, (TK, TK), 1)
    tri = jnp.where(c <= r, 1.0, 0.0).astype(BF16)
    x0, x1, x2 = _split3(lf_ref[...])
    cs = _dot(tri, x0) + _dot(tri, x1) + _dot(tri, x2) + carry_scr[0:1, :]
    carry_scr[...] = jnp.broadcast_to(cs[TK - 1:TK, :], carry_scr.shape)

    d_idx = lax.broadcasted_iota(jnp.int32, (D_MODEL, LANES_V7X), 0)
    h_idx = lax.broadcasted_iota(jnp.int32, (D_MODEL, LANES_V7X), 1)
    head_of = jnp.where(d_idx // HEAD_DIM == h_idx, 1.0, 0.0).astype(BF16)
    qf = q_ref[...].astype(F32)
    hi, lo = _split2(qf * qf)
    reach = jnp.sqrt(_dot(hi, head_of) + _dot(lo, head_of)) * kb_ref[0, 0]

    stats_ref[...] = jnp.concatenate([
        jnp.max(2.0 * reach + cs, axis=0, keepdims=True),
        cs[0:1, :],
        jnp.max(reach, axis=0, keepdims=True),
        jnp.zeros((SUBLANES_V7X - 3, LANES_V7X), F32)], axis=0)

    q_terms = jnp.concatenate(_split3(cs - reach), axis=1)
    k_terms = jnp.concatenate(_split3(-cs), axis=1)
    src = lax.broadcasted_iota(jnp.int32, (N_SPLIT * LANES_V7X, LANES_V7X), 0)
    dst = lax.broadcasted_iota(jnp.int32, (N_SPLIT * LANES_V7X, LANES_V7X), 1)
    lane = lax.broadcasted_iota(jnp.int32, (1, LANES_V7X), 1)
    for h in range(N_HEADS):
        pair, parity = divmod(h, HEADS_PER_LANE_TILE)
        spare = HEAD_DIM * (1 - parity)
        to_q = jnp.where((src % LANES_V7X == h) & (dst == spare + src // LANES_V7X), 1.0, 0.0)
        to_k = jnp.where((src % LANES_V7X == h) & (dst == spare + N_SPLIT + src // LANES_V7X), 1.0, 0.0)
        ones_q = jnp.where((lane >= spare + N_SPLIT) & (lane < spare + 2 * N_SPLIT), 1.0, 0.0)
        ones_k = jnp.where((lane >= spare) & (lane < spare + N_SPLIT), 1.0, 0.0)
        aug_q = (_dot(q_terms, to_q.astype(BF16)) + ones_q).astype(BF16)
        aug_k = (_dot(k_terms, to_k.astype(BF16)) + ones_k).astype(BF16)
        own = (lane // HEAD_DIM) == parity
        cols = slice(pair * LANES_V7X, (pair + 1) * LANES_V7X)
        qa_ref[h] = jnp.where(own, q_ref[:, cols], aug_q)
        ka_ref[h] = jnp.where(own, k_ref[:, cols], aug_k)


def _fox_prep(log_f, q, k, key_norm_bound, batch, seq):
    assert TQ == TK
    d = q.shape[1]
    block_bytes = (4 * TK * LANES_V7X + 2 * 2 * TK * d + 2 * 2 * N_HEADS * TK * LANES_V7X
                   + 4 * SUBLANES_V7X * LANES_V7X)
    aug_shape = jax.ShapeDtypeStruct((batch, N_HEADS, seq, LANES_V7X), BF16)
    aug_spec = pl.BlockSpec((None, N_HEADS, TK, LANES_V7X), lambda b, t: (b, 0, t, 0))
    return pl.pallas_call(
        _fox_prep_kernel,
        out_shape=(aug_shape, aug_shape,
                   jax.ShapeDtypeStruct((batch, seq // TK, SUBLANES_V7X, LANES_V7X), F32)),
        grid=(batch, seq // TK),
        in_specs=[
            pl.BlockSpec(memory_space=pltpu.SMEM),
            pl.BlockSpec((None, TK, LANES_V7X), lambda b, t: (b, t, 0)),
            pl.BlockSpec((None, TK, d), lambda b, t: (b, t, 0)),
            pl.BlockSpec((None, TK, d), lambda b, t: (b, t, 0)),
        ],
        out_specs=(aug_spec, aug_spec,
                   pl.BlockSpec((None, None, SUBLANES_V7X, LANES_V7X), lambda b, t: (b, t, 0, 0))),
        scratch_shapes=[pltpu.VMEM((SUBLANES_V7X, LANES_V7X), F32)],
        compiler_params=pltpu.CompilerParams(
            dimension_semantics=("arbitrary", "arbitrary"),
            vmem_limit_bytes=_vmem_limit(block_bytes)),
        name="fox_prep",
    )(key_norm_bound, log_f.reshape(batch, seq, LANES_V7X), q.reshape(batch, seq, d),
      k.reshape(batch, seq, d))


def _head_select(head):
    lane = lax.broadcasted_iota(jnp.int32, (1, LANES_V7X), 1)
    return (lane // HEAD_DIM) == head


def _widen(col128, width):
    reps = width // LANES_V7X
    return col128 if reps == 1 else jnp.concatenate([col128] * reps, axis=1)


def _sb_kernel(q_ref, k_ref, v_ref, o_ref):
    i = pl.program_id(2)
    row = lax.broadcasted_iota(jnp.int32, (SB_T, SB_T), 0)
    col = lax.broadcasted_iota(jnp.int32, (SB_T, SB_T), 1)
    diag_mask = col < row
    after = jnp.where(row > col, 1.0, 0.0).astype(BF16)
    ones = jnp.ones((SB_T, LANES_V7X), BF16)
    ones_wide = jnp.ones((SB_T, SB_T), BF16)

    def softplus(z):
        return jnp.maximum(z, 0.0) + jnp.log(1.0 + jnp.exp(-jnp.abs(z)))

    def keys(j):
        return k_ref[pl.ds(j * SB_T, SB_T), :], v_ref[pl.ds(j * SB_T, SB_T), :]

    def band(qh, g):
        k_d, v_d = keys(g)
        k_p, v_p = keys(jnp.maximum(g - 1, 0))
        z_d = _dot_nt(qh, k_d)
        z_p = jnp.where(g > 0, _dot_nt(qh, k_p), NEG_BIG)
        sp_d = jnp.where(diag_mask, softplus(z_d), 0.0)
        sp_p = softplus(z_p)
        sp_d16 = sp_d.astype(BF16)
        within_d = _dot(sp_d16, after)
        within_p = _dot(sp_p.astype(BF16), after) + _dot(sp_d16, ones_wide)
        a_d = jnp.where(diag_mask, jnp.exp((z_d - sp_d) - within_d), 0.0)
        a_p = jnp.exp((z_p - sp_p) - within_p)
        acc = _dot(a_d.astype(BF16), v_d) + _dot(a_p.astype(BF16), v_p)
        total = (within_p + sp_p)[:, 0:1]
        return jnp.broadcast_to(-total, (SB_T, LANES_V7X)), acc

    def block(qh, j, carry, acc):
        kj, vj = keys(j)
        z = _dot_nt(qh, kj)
        sp = softplus(z)
        sp16 = sp.astype(BF16)
        within = _dot(sp16, after)
        a = jnp.exp((z - sp) - within + _widen(carry, SB_T))
        acc = acc + _dot(a.astype(BF16), vj)
        carry = carry - _dot(sp16, ones)
        return carry, acc

    chains = []
    for u in range(SB_SUB):
        g = i * SB_SUB + u
        q = q_ref[u * SB_T:(u + 1) * SB_T, :] * (HEAD_DIM ** -0.5)
        for head in range(HEADS_PER_LANE_TILE):
            qh = jnp.where(_head_select(head), q, 0.0).astype(BF16)
            chains.append((g, qh) + band(qh, g))

    outs = []
    for g, qh, carry, acc in chains:
        def cond(state):
            j, carry, _ = state
            return jnp.logical_and(j >= 0, jnp.max(carry) > -ZERO_EXP)

        def body(state, qh=qh):
            j, carry, acc = state
            carry, acc = block(qh, j, carry, acc)
            return j - 1, carry, acc

        outs.append(lax.while_loop(cond, body, (g - 2, carry, acc))[2])
    for u in range(SB_SUB):
        pair = outs[u * HEADS_PER_LANE_TILE:(u + 1) * HEADS_PER_LANE_TILE]
        o_ref[u * SB_T:(u + 1) * SB_T, :] = jnp.where(_head_select(0), pair[0], pair[1]).astype(o_ref.dtype)


def _sb_attn(qkv, batch, seq):
    rows = SB_SUB * SB_T
    block_bytes = 2 * (2 * rows * LANES_V7X + 2 * seq * LANES_V7X)
    return pl.pallas_call(
        _sb_kernel,
        out_shape=jax.ShapeDtypeStruct((batch, seq, D_MODEL), BF16),
        grid=(batch, N_HEAD_PAIRS, seq // rows),
        in_specs=[
            pl.BlockSpec((None, rows, LANES_V7X), lambda b, p, i: (b, i, p)),
            pl.BlockSpec((None, seq, LANES_V7X), lambda b, p, i: (b, 0, N_HEAD_PAIRS + p)),
            pl.BlockSpec((None, seq, LANES_V7X), lambda b, p, i: (b, 0, 2 * N_HEAD_PAIRS + p)),
        ],
        out_specs=pl.BlockSpec((None, rows, LANES_V7X), lambda b, p, i: (b, i, p)),
        compiler_params=pltpu.CompilerParams(
            dimension_semantics=("arbitrary", "arbitrary", "arbitrary"),
            vmem_limit_bytes=_vmem_limit(block_bytes)),
        name="sb_attn",
    )(qkv, qkv, qkv)


def _fox_kernel(a_ref, fk0_ref, g_ref, qa_ref, ka_ref, v_ref, o_ref):
    b = pl.program_id(0)
    p_id = pl.program_id(1)
    i = pl.program_id(2)
    n_blocks = fk0_ref.shape[1]
    row = lax.broadcasted_iota(jnp.int32, (TQ, TK), 0)
    col = lax.broadcasted_iota(jnp.int32, (TQ, TK), 1)
    diag_mask = col <= row

    outs = []
    for head in range(HEADS_PER_LANE_TILE):
        bh = (b * N_HEAD_PAIRS + p_id) * HEADS_PER_LANE_TILE + head
        bound = a_ref[bh, i]
        qa = qa_ref[head]
        own = _head_select(head)

        def scores(j, head=head, qa=qa):
            return _dot_nt(qa, ka_ref[head, pl.ds(j * TK, TK), :])

        def values(j, own=own):
            vj = v_ref[pl.ds(j * TK, TK), :]
            return jnp.where(own, vj, jnp.ones_like(vj))

        def count_left(bound=bound, bh=bh):
            def more(n):
                j = i - 1 - n
                right = fk0_ref[bh, jnp.minimum(j + 1, n_blocks - 1)]
                return jnp.logical_and(j >= 0, bound - right > -ZERO_EXP)

            return lax.while_loop(more, lambda n: n + 1, jnp.int32(0))

        acc0 = jnp.zeros((TQ, LANES_V7X), F32)

        def fast_walk(scores=scores, values=values, count_left=count_left):
            n_left = count_left()

            def step(j, p, acc):
                p_next = jnp.exp(scores(j)).astype(BF16)
                return p_next, acc + _dot(p, values(j + 1))

            def two_steps(t, state, first):
                p, acc = step(first - 2 * t, *state)
                return step(first - 2 * t - 1, p, acc)

            p0 = jnp.where(diag_mask, jnp.exp(scores(i)), 0.0).astype(BF16)
            odd = n_left % 2
            p, acc = lax.cond(odd == 1, lambda: step(i - 1, p0, acc0), lambda: (p0, acc0))
            p, acc = lax.fori_loop(0, n_left // 2,
                                   functools.partial(two_steps, first=i - 1 - odd), (p, acc))
            return acc + _dot(p, values(i - n_left))

        def online_walk(scores=scores, values=values, count_left=count_left):
            def body(t, state):
                acc, m = state
                s = jnp.where(jnp.logical_or(t > 0, diag_mask), scores(i - t), NEG_BIG)
                m_new = jnp.maximum(m, jnp.max(s, axis=-1, keepdims=True))
                p = jnp.exp(s - m_new).astype(BF16)
                return jnp.exp(m - m_new) * acc + _dot(p, values(i - t)), m_new

            m0 = jnp.full((TQ, 1), NEG_BIG, F32)
            return lax.fori_loop(0, count_left() + 1, body, (acc0, m0))[0]

        acc = lax.cond(2.0 * g_ref[bh, i] < SAFE_EXP_RANGE, fast_walk, online_walk)
        denom = pltpu.roll(acc, HEAD_DIM, axis=1)
        outs.append(acc / denom)
    o_ref[...] = jnp.where(_head_select(0), outs[0], outs[1]).astype(o_ref.dtype)


def _fox_attn(q_aug, k_aug, v, walk_bound, fk_block_start, max_reach, batch, seq):
    assert TQ == TK
    block_bytes = 2 * (HEADS_PER_LANE_TILE * (TQ + seq) * LANES_V7X + seq * LANES_V7X + TQ * LANES_V7X)
    smem = pl.BlockSpec(memory_space=pltpu.SMEM)
    return pl.pallas_call(
        _fox_kernel,
        out_shape=jax.ShapeDtypeStruct((batch, seq, D_MODEL), BF16),
        grid=(batch, N_HEAD_PAIRS, seq // TQ),
        in_specs=[
            smem, smem, smem,
            pl.BlockSpec((None, HEADS_PER_LANE_TILE, TQ, LANES_V7X), lambda b, p, i: (b, p, i, 0)),
            pl.BlockSpec((None, HEADS_PER_LANE_TILE, seq, LANES_V7X), lambda b, p, i: (b, p, 0, 0)),
            pl.BlockSpec((None, seq, LANES_V7X), lambda b, p, i: (b, 0, p)),
        ],
        out_specs=pl.BlockSpec((None, TQ, LANES_V7X), lambda b, p, i: (b, i, p)),
        compiler_params=pltpu.CompilerParams(
            dimension_semantics=("arbitrary", "arbitrary", "arbitrary"),
            vmem_limit_bytes=_vmem_limit(block_bytes)),
        name="fox_attn",
    )(walk_bound, fk_block_start, max_reach, q_aug, k_aug, v)


def kernel(x, c, ada_w, ada_b, norm_attn_g, norm_ffn_g, w_ffn_in, w_ffn_down, sb_w_qkv, sb_w_o,
           kv_ada_w, kv_ada_b, kv_norm_g, w_kvf, b_f, k_norm_g, fox_w_q, q_norm_g, fox_w_o):
    batch, seq, d = x.shape
    assert d == D_MODEL and seq % TM == 0 and seq % TQ == 0
    assert ada_w.shape[0] == 2 and sb_w_qkv.shape[0] == 1 and fox_w_q.shape[0] == 1
    m = batch * seq

    c_pad = jnp.zeros((SUBLANES_V7X, d), F32).at[:batch].set(c.astype(F32))
    mod = _ada_mod(c_pad, ada_w, ada_b)[:, :batch]
    kv_mod = _ada_mod(c_pad, kv_ada_w[None], kv_ada_b[None])[0, :batch]

    def vecs(t, n):
        return [t[:, None, j * d:(j + 1) * d] for j in range(n)]

    row = lambda t: t.reshape(1, -1).astype(F32)
    bf = lambda t: t.astype(BF16)

    x2d = x.reshape(m, d)

    sh_a, sc_a, g_a, sh_f, sc_f, g_f = vecs(mod[0], 6)
    qkv = _norm_matmul(x2d, row(norm_attn_g[0]), sh_a, sc_a, bf(sb_w_qkv[0]), seq)
    o = _sb_attn(qkv.reshape(batch, seq, 3 * d), batch, seq)
    x2d = _proj_residual(o.reshape(m, d), bf(sb_w_o[0]), x2d, g_a, seq)
    act = _ffn_in(x2d, row(norm_ffn_g[0]), sh_f, sc_f, bf(w_ffn_in[0]), seq)
    x2d = _proj_residual(act, bf(w_ffn_down[0]), x2d, g_f, seq)

    sh_a, sc_a, g_a, sh_f, sc_f, g_f = vecs(mod[1], 6)
    kv_shift, kv_scale = vecs(kv_mod, 2)
    w_f_pad = jnp.zeros((d, LANES_V7X), F32).at[:, :N_HEADS].set(w_kvf[:, 2 * d:])
    b_f_pad = jnp.zeros((1, LANES_V7X), F32).at[0, :N_HEADS].set(b_f)
    k, v, q, log_f = _l1_proj(
        x2d, row(kv_norm_g), kv_shift, kv_scale, row(norm_attn_g[1]), sh_a, sc_a,
        bf(w_kvf[:, :d]), bf(w_kvf[:, d:2 * d]), bf(w_f_pad), bf(fox_w_q[0]),
        row(jnp.tile(k_norm_g, N_HEADS)), row(jnp.tile(q_norm_g[0], N_HEADS)), b_f_pad, seq)
    key_norm_bound = (1.01 * HEAD_DIM ** 0.5 * jnp.max(jnp.abs(k_norm_g))).reshape(1, 1).astype(F32)
    q_aug, k_aug, stats = _fox_prep(log_f, q, k, key_norm_bound, batch, seq)
    tables = stats[:, :, :3, :N_HEADS].transpose(2, 0, 3, 1).reshape(3, batch * N_HEADS, seq // TK)
    o = _fox_attn(q_aug, k_aug, v.reshape(batch, seq, d), tables[0], tables[1], tables[2], batch, seq)
    x2d = _proj_residual(o.reshape(m, d), bf(fox_w_o[0]), x2d, g_a, seq)
    act = _ffn_in(x2d, row(norm_ffn_g[1]), sh_f, sc_f, bf(w_ffn_in[1]), seq)
    x2d = _proj_residual(act, bf(w_ffn_down[1]), x2d, g_f, seq)
    return x2d.reshape(batch, seq, d)
```

```python
import functools

import jax
import jax.numpy as jnp
from jax import lax
from jax.experimental import pallas as pl
from jax.experimental.pallas import tpu as pltpu

D_MODEL = 1024
N_HEADS = 16
HEAD_DIM = D_MODEL // N_HEADS
EPS = 1e-6

LANES_V7X = 128
SUBLANES_V7X = 8
VMEM_BYTES_V7X = 64 * 1024 * 1024

HEADS_PER_LANE_TILE = LANES_V7X // HEAD_DIM
N_HEAD_PAIRS = N_HEADS // HEADS_PER_LANE_TILE

TM = 512
TQ = 512
TK = 512
SB_T = 256
SB_SUB = 2

ZERO_EXP = 110.0
SAFE_EXP_RANGE = 80.0
NEG_BIG = -1e30
LOG2_E = 1.4426950408889634

F32 = jnp.float32
BF16 = jnp.bfloat16


def _vmem_limit(block_bytes):
    want = 2 * block_bytes + 16 * 1024 * 1024
    return int(min(want, VMEM_BYTES_V7X - 8 * 1024 * 1024))


def _split2(x):
    hi = x.astype(BF16)
    lo = (x - hi.astype(F32)).astype(BF16)
    return hi, lo


def _dot(a, b):
    return jnp.dot(a, b, preferred_element_type=F32)


def _dot_nt(a, b):
    return lax.dot_general(a, b, (((1,), (1,)), ((), ())), preferred_element_type=F32)


def _ada_kernel(c_ref, w_ref, b_ref, o_ref):
    c = c_ref[...]
    c_act = c * jax.nn.sigmoid(c)
    o_ref[...] = jnp.dot(c_act, w_ref[...], precision=lax.Precision.HIGHEST,
                         preferred_element_type=F32) + b_ref[...]


def _ada_mod(c_pad, w, b):
    n_layers, d, n = w.shape
    tn = 1024
    rows = c_pad.shape[0]
    block_bytes = 4 * (d * tn + rows * d + tn + rows * tn)
    return pl.pallas_call(
        _ada_kernel,
        out_shape=jax.ShapeDtypeStruct((n_layers, rows, n), F32),
        grid=(n_layers, n // tn),
        in_specs=[
            pl.BlockSpec((rows, d), lambda l, j: (0, 0)),
            pl.BlockSpec((None, d, tn), lambda l, j: (l, 0, j)),
            pl.BlockSpec((None, 1, tn), lambda l, j: (l, 0, j)),
        ],
        out_specs=pl.BlockSpec((None, rows, tn), lambda l, j: (l, 0, j)),
        compiler_params=pltpu.CompilerParams(
            dimension_semantics=("arbitrary", "arbitrary"),
            vmem_limit_bytes=_vmem_limit(block_bytes)),
        name="ada_mod",
    )(c_pad, w, b.reshape(n_layers, 1, n))


def _norm_modulate(x, g, shift, scale):
    ms = jnp.mean(x * x, axis=-1, keepdims=True)
    y = x * lax.rsqrt(ms + EPS)
    return (y * g) * (1.0 + scale) + shift


def _row_spec(d):
    return pl.BlockSpec((1, d), lambda *_: (0, 0))


def _batch_vec_spec(d, tiles_per_batch):
    return pl.BlockSpec((None, 1, d), lambda i, *_: (i // tiles_per_batch, 0, 0))


def _norm_matmul_kernel(x_ref, g_ref, sh_ref, sc_ref, w_ref, cs_ref, o_ref, h_scr):
    @pl.when(pl.program_id(1) == 0)
    def _():
        h_scr[...] = _norm_modulate(x_ref[...], g_ref[...], sh_ref[...], sc_ref[...]).astype(BF16)

    o_ref[...] = (_dot(h_scr[...], w_ref[...]) * cs_ref[...]).astype(o_ref.dtype)


def _norm_matmul(x2d, g, shift, scale, w, col_scale, seq):
    m, d = x2d.shape
    n = w.shape[1]
    tn = 1024
    block_bytes = 4 * TM * d + 2 * d * tn + 2 * TM * tn + 2 * TM * d
    return pl.pallas_call(
        _norm_matmul_kernel,
        out_shape=jax.ShapeDtypeStruct((m, n), BF16),
        grid=(m // TM, n // tn),
        in_specs=[
            pl.BlockSpec((TM, d), lambda i, j: (i, 0)),
            _row_spec(d),
            _batch_vec_spec(d, seq // TM),
            _batch_vec_spec(d, seq // TM),
            pl.BlockSpec((d, tn), lambda i, j: (0, j)),
            pl.BlockSpec((1, tn), lambda i, j: (0, j)),
        ],
        out_specs=pl.BlockSpec((TM, tn), lambda i, j: (i, j)),
        scratch_shapes=[pltpu.VMEM((TM, d), BF16)],
        compiler_params=pltpu.CompilerParams(
            dimension_semantics=("arbitrary", "arbitrary"),
            vmem_limit_bytes=_vmem_limit(block_bytes)),
        name="norm_matmul",
    )(x2d, g, shift, scale, w, col_scale)


def _ffn_in_kernel(x_ref, g_ref, sh_ref, sc_ref, wg_ref, wu_ref, o_ref, h_scr):
    @pl.when(pl.program_id(1) == 0)
    def _():
        h_scr[...] = _norm_modulate(x_ref[...], g_ref[...], sh_ref[...], sc_ref[...]).astype(BF16)

    h = h_scr[...]
    gate = _dot(h, wg_ref[...])
    up = _dot(h, wu_ref[...])
    o_ref[...] = (gate * jax.nn.sigmoid(gate) * up).astype(o_ref.dtype)


def _ffn_in(x2d, g, shift, scale, w_in, seq):
    m, d = x2d.shape
    f = w_in.shape[1] // 2
    n_tiles = 2
    tn = f // n_tiles
    assert tn * n_tiles == f and tn % LANES_V7X == 0
    block_bytes = 4 * TM * d + 2 * 2 * d * tn + 2 * TM * tn + 2 * TM * d + 2 * 4 * TM * tn
    return pl.pallas_call(
        _ffn_in_kernel,
        out_shape=jax.ShapeDtypeStruct((m, f), BF16),
        grid=(m // TM, n_tiles),
        in_specs=[
            pl.BlockSpec((TM, d), lambda i, j: (i, 0)),
            _row_spec(d),
            _batch_vec_spec(d, seq // TM),
            _batch_vec_spec(d, seq // TM),
            pl.BlockSpec((d, tn), lambda i, j: (0, j)),
            pl.BlockSpec((d, tn), lambda i, j: (0, j + n_tiles)),
        ],
        out_specs=pl.BlockSpec((TM, tn), lambda i, j: (i, j)),
        scratch_shapes=[pltpu.VMEM((TM, d), BF16)],
        compiler_params=pltpu.CompilerParams(
            dimension_semantics=("arbitrary", "arbitrary"),
            vmem_limit_bytes=_vmem_limit(block_bytes)),
        name="ffn_in",
    )(x2d, g, shift, scale, w_in, w_in)


def _proj_residual_kernel(a_ref, w_ref, x_ref, gate_ref, o_ref):
    o_ref[...] = x_ref[...] + gate_ref[...] * _dot(a_ref[...], w_ref[...])


def _proj_residual(a, w, x2d, gate, seq):
    m, k = a.shape
    d = w.shape[1]
    block_bytes = 2 * TM * k + 2 * k * d + 2 * 4 * TM * d
    return pl.pallas_call(
        _proj_residual_kernel,
        out_shape=jax.ShapeDtypeStruct((m, d), F32),
        grid=(m // TM,),
        in_specs=[
            pl.BlockSpec((TM, k), lambda i: (i, 0)),
            pl.BlockSpec((k, d), lambda i: (0, 0)),
            pl.BlockSpec((TM, d), lambda i: (i, 0)),
            _batch_vec_spec(d, seq // TM),
        ],
        out_specs=pl.BlockSpec((TM, d), lambda i: (i, 0)),
        compiler_params=pltpu.CompilerParams(
            dimension_semantics=("arbitrary",),
            vmem_limit_bytes=_vmem_limit(block_bytes)),
        name="proj_residual",
    )(a, w, x2d, gate)


def _head_rms_norm(t, gain, group_mean):
    outs = []
    for c in range(t.shape[1] // LANES_V7X):
        tc = t[:, c * LANES_V7X:(c + 1) * LANES_V7X]
        hi, lo = _split2(tc * tc)
        ms = _dot(hi, group_mean) + _dot(lo, group_mean)
        outs.append(tc * lax.rsqrt(ms + EPS) * gain[:, c * LANES_V7X:(c + 1) * LANES_V7X])
    return jnp.concatenate(outs, axis=1)


def _l1_proj_kernel(x_ref, gkv_ref, shkv_ref, sckv_ref, gq_ref, shq_ref, scq_ref,
                    wk_ref, wv_ref, wf_ref, wq_ref, kg_ref, qg_ref, bf_ref, gm_ref,
                    k_ref, v_ref, q_ref, lf_ref):
    x = x_ref[...]
    ms = jnp.mean(x * x, axis=-1, keepdims=True)
    y = x * lax.rsqrt(ms + EPS)
    h_kv = ((y * gkv_ref[...]) * (1.0 + sckv_ref[...]) + shkv_ref[...]).astype(BF16)
    h_q = ((y * gq_ref[...]) * (1.0 + scq_ref[...]) + shq_ref[...]).astype(BF16)
    gm = gm_ref[...]
    k_ref[...] = _head_rms_norm(_dot(h_kv, wk_ref[...]), kg_ref[...], gm).astype(BF16)
    v_ref[...] = _dot(h_kv, wv_ref[...]).astype(BF16)
    q = _head_rms_norm(_dot(h_q, wq_ref[...]), qg_ref[...], gm)
    q_ref[...] = (q * (HEAD_DIM ** -0.5)).astype(BF16)
    lf_ref[...] = jax.nn.log_sigmoid(_dot(h_kv, wf_ref[...]) + bf_ref[...])


def _l1_proj(x2d, g_kv, sh_kv, sc_kv, g_q, sh_q, sc_q, wk, wv, wf, wq, k_gain, q_gain, b_f, seq):
    m, d = x2d.shape
    lane = jnp.arange(LANES_V7X)
    group_mean = jnp.where(lane[:, None] // HEAD_DIM == lane[None, :] // HEAD_DIM,
                           1.0 / HEAD_DIM, 0.0).astype(BF16)
    tpb = seq // TM
    block_bytes = 4 * TM * d + 3 * 2 * d * d + 2 * d * LANES_V7X + 3 * 2 * TM * d + 4 * TM * LANES_V7X
    full = lambda r, c: pl.BlockSpec((r, c), lambda i: (0, 0))
    return pl.pallas_call(
        _l1_proj_kernel,
        out_shape=(jax.ShapeDtypeStruct((m, d), BF16), jax.ShapeDtypeStruct((m, d), BF16),
                   jax.ShapeDtypeStruct((m, d), BF16), jax.ShapeDtypeStruct((m, LANES_V7X), F32)),
        grid=(m // TM,),
        in_specs=[
            pl.BlockSpec((TM, d), lambda i: (i, 0)),
            _row_spec(d), _batch_vec_spec(d, tpb), _batch_vec_spec(d, tpb),
            _row_spec(d), _batch_vec_spec(d, tpb), _batch_vec_spec(d, tpb),
            full(d, d), full(d, d), full(d, LANES_V7X), full(d, d),
            _row_spec(d), _row_spec(d), _row_spec(LANES_V7X), full(LANES_V7X, LANES_V7X),
        ],
        out_specs=(pl.BlockSpec((TM, d), lambda i: (i, 0)), pl.BlockSpec((TM, d), lambda i: (i, 0)),
                   pl.BlockSpec((TM, d), lambda i: (i, 0)), pl.BlockSpec((TM, LANES_V7X), lambda i: (i, 0))),
        compiler_params=pltpu.CompilerParams(
            dimension_semantics=("arbitrary",),
            vmem_limit_bytes=_vmem_limit(block_bytes)),
        name="l1_proj",
    )(x2d, g_kv, sh_kv, sc_kv, g_q, sh_q, sc_q, wk, wv, wf, wq, k_gain, q_gain, b_f, group_mean)


N_SPLIT = 3


def _split3(x):
    x0 = x.astype(BF16)
    r1 = x - x0.astype(F32)
    x1 = r1.astype(BF16)
    x2 = (r1 - x1.astype(F32)).astype(BF16)
    return x0, x1, x2


def _fox_prep_kernel(kb_ref, lf_ref, q_ref, k_ref, qa_ref, ka_ref, stats_ref, carry_scr):
    @pl.when(pl.program_id(1) == 0)
    def _():
        carry_scr[...] = jnp.zeros_like(carry_scr)

    r = lax.broadcasted_iota(jnp.int32, (TK, TK), 0)
    c = lax.broadcasted_iota(jnp.int32, (TK, TK), 1)
    tri = jnp.where(c <= r, 1.0, 0.0).astype(BF16)
    x0, x1, x2 = _split3(lf_ref[...])
    cs = _dot(tri, x0) + _dot(tri, x1) + _dot(tri, x2) + carry_scr[0:1, :]
    carry_scr[...] = jnp.broadcast_to(cs[TK - 1:TK, :], carry_scr.shape)

    d_idx = lax.broadcasted_iota(jnp.int32, (D_MODEL, LANES_V7X), 0)
    h_idx = lax.broadcasted_iota(jnp.int32, (D_MODEL, LANES_V7X), 1)
    head_of = jnp.where(d_idx // HEAD_DIM == h_idx, 1.0, 0.0).astype(BF16)
    qf = q_ref[...].astype(F32)
    hi, lo = _split2(qf * qf)
    reach = jnp.sqrt(_dot(hi, head_of) + _dot(lo, head_of)) * kb_ref[0, 0]

    stats_ref[...] = jnp.concatenate([
        jnp.max(2.0 * reach + cs, axis=0, keepdims=True),
        cs[0:1, :],
        jnp.max(reach, axis=0, keepdims=True),
        jnp.zeros((SUBLANES_V7X - 3, LANES_V7X), F32)], axis=0)

    lane = lax.broadcasted_iota(jnp.int32, (1, LANES_V7X), 1)
    packed = jnp.zeros((TK, LANES_V7X), F32)
    for i, term in enumerate(_split3(cs - reach) + _split3(-cs)):
        term = jnp.where(lane < N_HEADS, term.astype(F32), 0.0)
        packed = packed + (term if i == 0 else pltpu.roll(term, N_HEADS * i, axis=1))
    packed = packed.astype(BF16)

    src = lax.broadcasted_iota(jnp.int32, (LANES_V7X, 2 * LANES_V7X), 0)
    dst = lax.broadcasted_iota(jnp.int32, (LANES_V7X, 2 * LANES_V7X), 1)
    for h in range(N_HEADS):
        pair, parity = divmod(h, HEADS_PER_LANE_TILE)
        spare = HEAD_DIM * (1 - parity)
        place = jnp.where((src % N_HEADS == h) & (src < 2 * N_SPLIT * N_HEADS)
                          & (dst % LANES_V7X == spare + src // N_HEADS)
                          & ((dst >= LANES_V7X) == (src >= N_SPLIT * N_HEADS)), 1.0, 0.0)
        placed = _dot(packed, place.astype(BF16))
        ones_q = jnp.where((lane >= spare + N_SPLIT) & (lane < spare + 2 * N_SPLIT), 1.0, 0.0)
        ones_k = jnp.where((lane >= spare) & (lane < spare + N_SPLIT), 1.0, 0.0)
        aug_q = (placed[:, :LANES_V7X] + ones_q).astype(BF16)
        aug_k = (placed[:, LANES_V7X:] + ones_k).astype(BF16)
        own = (lane // HEAD_DIM) == parity
        cols = slice(pair * LANES_V7X, (pair + 1) * LANES_V7X)
        qa_ref[h] = jnp.where(own, q_ref[:, cols], aug_q)
        ka_ref[h] = jnp.where(own, k_ref[:, cols], aug_k)


def _fox_prep(log_f, q, k, key_norm_bound, batch, seq):
    assert TQ == TK
    d = q.shape[1]
    block_bytes = (4 * TK * LANES_V7X + 2 * 2 * TK * d + 2 * 2 * N_HEADS * TK * LANES_V7X
                   + 4 * SUBLANES_V7X * LANES_V7X)
    aug_shape = jax.ShapeDtypeStruct((batch, N_HEADS, seq, LANES_V7X), BF16)
    aug_spec = pl.BlockSpec((None, N_HEADS, TK, LANES_V7X), lambda b, t: (b, 0, t, 0))
    return pl.pallas_call(
        _fox_prep_kernel,
        out_shape=(aug_shape, aug_shape,
                   jax.ShapeDtypeStruct((batch, seq // TK, SUBLANES_V7X, LANES_V7X), F32)),
        grid=(batch, seq // TK),
        in_specs=[
            pl.BlockSpec(memory_space=pltpu.SMEM),
            pl.BlockSpec((None, TK, LANES_V7X), lambda b, t: (b, t, 0)),
            pl.BlockSpec((None, TK, d), lambda b, t: (b, t, 0)),
            pl.BlockSpec((None, TK, d), lambda b, t: (b, t, 0)),
        ],
        out_specs=(aug_spec, aug_spec,
                   pl.BlockSpec((None, None, SUBLANES_V7X, LANES_V7X), lambda b, t: (b, t, 0, 0))),
        scratch_shapes=[pltpu.VMEM((SUBLANES_V7X, LANES_V7X), F32)],
        compiler_params=pltpu.CompilerParams(
            dimension_semantics=("arbitrary", "arbitrary"),
            vmem_limit_bytes=_vmem_limit(block_bytes)),
        name="fox_prep",
    )(key_norm_bound, log_f.reshape(batch, seq, LANES_V7X), q.reshape(batch, seq, d),
      k.reshape(batch, seq, d))


def _head_select(head):
    lane = lax.broadcasted_iota(jnp.int32, (1, LANES_V7X), 1)
    return (lane // HEAD_DIM) == head


def _widen(col128, width):
    reps = width // LANES_V7X
    return col128 if reps == 1 else jnp.concatenate([col128] * reps, axis=1)


def _sb_kernel(q_ref, k_ref, v_ref, o_ref):
    i = pl.program_id(2)
    row = lax.broadcasted_iota(jnp.int32, (SB_T, SB_T), 0)
    col = lax.broadcasted_iota(jnp.int32, (SB_T, SB_T), 1)
    diag_mask = col < row
    after = jnp.where(row > col, 1.0, 0.0).astype(BF16)
    ones = jnp.ones((SB_T, LANES_V7X), BF16)
    ones_wide = jnp.ones((SB_T, SB_T), BF16)

    def softplus(z):
        sign_bit = jnp.uint32(1 << 31)
        neg_abs = lax.bitcast_convert_type(lax.bitcast_convert_type(z, jnp.uint32) | sign_bit, F32)
        return jnp.maximum(z, 0.0) + jnp.log2(1.0 + jnp.exp2(neg_abs))

    def keys(j):
        return k_ref[pl.ds(j * SB_T, SB_T), :], v_ref[pl.ds(j * SB_T, SB_T), :]

    def band(qh, g):
        k_d, v_d = keys(g)
        k_p, v_p = keys(jnp.maximum(g - 1, 0))
        z_d = _dot_nt(qh, k_d)
        z_p = jnp.where(g > 0, _dot_nt(qh, k_p), NEG_BIG)
        sp_d = jnp.where(diag_mask, softplus(z_d), 0.0)
        sp_p = softplus(z_p)
        sp_d16 = sp_d.astype(BF16)
        within_d = _dot(sp_d16, after)
        within_p = _dot(sp_p.astype(BF16), after) + _dot(sp_d16, ones_wide)
        a_d = jnp.where(diag_mask, jnp.exp2((z_d - sp_d) - within_d), 0.0)
        a_p = jnp.exp2((z_p - sp_p) - within_p)
        acc = _dot(a_d.astype(BF16), v_d) + _dot(a_p.astype(BF16), v_p)
        total = (within_p + sp_p)[:, 0:1]
        return jnp.broadcast_to(-total, (SB_T, LANES_V7X)), acc

    def block(qh, j, carry, acc):
        kj, vj = keys(j)
        z = _dot_nt(qh, kj)
        sp = softplus(z)
        sp16 = sp.astype(BF16)
        within = _dot(sp16, after)
        a = jnp.exp2((z - sp) - within + _widen(carry, SB_T))
        acc = acc + _dot(a.astype(BF16), vj)
        carry = carry - _dot(sp16, ones)
        return carry, acc

    chains = []
    for u in range(SB_SUB):
        g = i * SB_SUB + u
        q = q_ref[u * SB_T:(u + 1) * SB_T, :]
        for head in range(HEADS_PER_LANE_TILE):
            qh = jnp.where(_head_select(head), q, 0.0).astype(BF16)
            chains.append((g, qh) + band(qh, g))

    def walk_on():
        outs = []
        for g, qh, carry, acc in chains:
            def cond(state):
                j, carry, _ = state
                return jnp.logical_and(j >= 0, jnp.max(carry) > -ZERO_EXP * LOG2_E)

            def body(state, qh=qh):
                j, carry, acc = state
                carry, acc = block(qh, j, carry, acc)
                return j - 1, carry, acc

            outs.append(lax.while_loop(cond, body, (g - 2, carry, acc))[2])
        return outs

    open_rows = functools.reduce(jnp.maximum, [carry for _, _, carry, _ in chains])
    outs = lax.cond(jnp.max(open_rows) > -ZERO_EXP * LOG2_E, walk_on,
                    lambda: [acc for _, _, _, acc in chains])
    for u in range(SB_SUB):
        pair = outs[u * HEADS_PER_LANE_TILE:(u + 1) * HEADS_PER_LANE_TILE]
        o_ref[u * SB_T:(u + 1) * SB_T, :] = jnp.where(_head_select(0), pair[0], pair[1]).astype(o_ref.dtype)


def _sb_attn(qkv, batch, seq):
    rows = SB_SUB * SB_T
    block_bytes = 2 * (2 * rows * LANES_V7X + 2 * seq * LANES_V7X)
    return pl.pallas_call(
        _sb_kernel,
        out_shape=jax.ShapeDtypeStruct((batch, seq, D_MODEL), BF16),
        grid=(batch, N_HEAD_PAIRS, seq // rows),
        in_specs=[
            pl.BlockSpec((None, rows, LANES_V7X), lambda b, p, i: (b, i, p)),
            pl.BlockSpec((None, seq, LANES_V7X), lambda b, p, i: (b, 0, N_HEAD_PAIRS + p)),
            pl.BlockSpec((None, seq, LANES_V7X), lambda b, p, i: (b, 0, 2 * N_HEAD_PAIRS + p)),
        ],
        out_specs=pl.BlockSpec((None, rows, LANES_V7X), lambda b, p, i: (b, i, p)),
        compiler_params=pltpu.CompilerParams(
            dimension_semantics=("arbitrary", "arbitrary", "arbitrary"),
            vmem_limit_bytes=_vmem_limit(block_bytes)),
        name="sb_attn",
    )(qkv, qkv, qkv)


def _fox_kernel(a_ref, fk0_ref, g_ref, qa_ref, ka_ref, v_ref, o_ref):
    b = pl.program_id(0)
    p_id = pl.program_id(1)
    i = pl.program_id(2)
    n_blocks = fk0_ref.shape[1]
    row = lax.broadcasted_iota(jnp.int32, (TQ, TK), 0)
    col = lax.broadcasted_iota(jnp.int32, (TQ, TK), 1)
    diag_mask = col <= row

    outs = []
    for head in range(HEADS_PER_LANE_TILE):
        bh = (b * N_HEAD_PAIRS + p_id) * HEADS_PER_LANE_TILE + head
        bound = a_ref[bh, i]
        qa = qa_ref[head]
        own = _head_select(head)

        def scores(j, head=head, qa=qa):
            return _dot_nt(qa, ka_ref[head, pl.ds(j * TK, TK), :])

        def values(j, own=own):
            vj = v_ref[pl.ds(j * TK, TK), :]
            return jnp.where(own, vj, jnp.ones_like(vj))

        def count_left(bound=bound, bh=bh):
            def more(n):
                j = i - 1 - n
                right = fk0_ref[bh, jnp.minimum(j + 1, n_blocks - 1)]
                return jnp.logical_and(j >= 0, bound - right > -ZERO_EXP)

            return lax.while_loop(more, lambda n: n + 1, jnp.int32(0))

        acc0 = jnp.zeros((TQ, LANES_V7X), F32)

        def fast_walk(scores=scores, values=values, count_left=count_left):
            n_left = count_left()
            n_left = n_left + jnp.where(jnp.logical_and(n_left % 2 == 1, n_left < i), 1, 0)

            def step(j, p, acc):
                p_next = jnp.exp(scores(j)).astype(BF16)
                return p_next, acc + _dot(p, values(j + 1))

            def two_steps(t, state, first):
                p, acc = step(first - 2 * t, *state)
                return step(first - 2 * t - 1, p, acc)

            p0 = jnp.where(diag_mask, jnp.exp(scores(i)), 0.0).astype(BF16)
            odd = n_left % 2
            p, acc = lax.cond(odd == 1, lambda: step(i - 1, p0, acc0), lambda: (p0, acc0))
            p, acc = lax.fori_loop(0, n_left // 2,
                                   functools.partial(two_steps, first=i - 1 - odd), (p, acc))
            return acc + _dot(p, values(i - n_left))

        def online_walk(scores=scores, values=values, count_left=count_left):
            def body(t, state):
                acc, m = state
                s = jnp.where(jnp.logical_or(t > 0, diag_mask), scores(i - t), NEG_BIG)
                m_new = jnp.maximum(m, jnp.max(s, axis=-1, keepdims=True))
                p = jnp.exp(s - m_new).astype(BF16)
                return jnp.exp(m - m_new) * acc + _dot(p, values(i - t)), m_new

            m0 = jnp.full((TQ, 1), NEG_BIG, F32)
            return lax.fori_loop(0, count_left() + 1, body, (acc0, m0))[0]

        acc = lax.cond(2.0 * g_ref[bh, i] < SAFE_EXP_RANGE, fast_walk, online_walk)
        denom = pltpu.roll(acc, HEAD_DIM, axis=1)
        outs.append(acc / denom)
    o_ref[...] = jnp.where(_head_select(0), outs[0], outs[1]).astype(o_ref.dtype)


def _fox_attn(q_aug, k_aug, v, walk_bound, fk_block_start, max_reach, batch, seq):
    assert TQ == TK
    block_bytes = 2 * (HEADS_PER_LANE_TILE * (TQ + seq) * LANES_V7X + seq * LANES_V7X + TQ * LANES_V7X)
    smem = pl.BlockSpec(memory_space=pltpu.SMEM)
    return pl.pallas_call(
        _fox_kernel,
        out_shape=jax.ShapeDtypeStruct((batch, seq, D_MODEL), BF16),
        grid=(batch, N_HEAD_PAIRS, seq // TQ),
        in_specs=[
            smem, smem, smem,
            pl.BlockSpec((None, HEADS_PER_LANE_TILE, TQ, LANES_V7X), lambda b, p, i: (b, p, i, 0)),
            pl.BlockSpec((None, HEADS_PER_LANE_TILE, seq, LANES_V7X), lambda b, p, i: (b, p, 0, 0)),
            pl.BlockSpec((None, seq, LANES_V7X), lambda b, p, i: (b, 0, p)),
        ],
        out_specs=pl.BlockSpec((None, TQ, LANES_V7X), lambda b, p, i: (b, i, p)),
        compiler_params=pltpu.CompilerParams(
            dimension_semantics=("arbitrary", "arbitrary", "arbitrary"),
            vmem_limit_bytes=_vmem_limit(block_bytes)),
        name="fox_attn",
    )(walk_bound, fk_block_start, max_reach, q_aug, k_aug, v)


def kernel(x, c, ada_w, ada_b, norm_attn_g, norm_ffn_g, w_ffn_in, w_ffn_down, sb_w_qkv, sb_w_o,
           kv_ada_w, kv_ada_b, kv_norm_g, w_kvf, b_f, k_norm_g, fox_w_q, q_norm_g, fox_w_o):
    batch, seq, d = x.shape
    assert d == D_MODEL and seq % TM == 0 and seq % TQ == 0
    assert ada_w.shape[0] == 2 and sb_w_qkv.shape[0] == 1 and fox_w_q.shape[0] == 1
    m = batch * seq

    c_pad = jnp.zeros((SUBLANES_V7X, d), F32).at[:batch].set(c.astype(F32))
    mod = _ada_mod(c_pad, ada_w, ada_b)[:, :batch]
    kv_mod = _ada_mod(c_pad, kv_ada_w[None], kv_ada_b[None])[0, :batch]

    def vecs(t, n):
        return [t[:, None, j * d:(j + 1) * d] for j in range(n)]

    row = lambda t: t.reshape(1, -1).astype(F32)
    bf = lambda t: t.astype(BF16)

    x2d = x.reshape(m, d)

    sh_a, sc_a, g_a, sh_f, sc_f, g_f = vecs(mod[0], 6)
    qkv_scale = jnp.concatenate([jnp.full((1, d), LOG2_E * HEAD_DIM ** -0.5, F32),
                                 jnp.ones((1, 2 * d), F32)], axis=1)
    qkv = _norm_matmul(x2d, row(norm_attn_g[0]), sh_a, sc_a, bf(sb_w_qkv[0]), qkv_scale, seq)
    o = _sb_attn(qkv.reshape(batch, seq, 3 * d), batch, seq)
    x2d = _proj_residual(o.reshape(m, d), bf(sb_w_o[0]), x2d, g_a, seq)
    act = _ffn_in(x2d, row(norm_ffn_g[0]), sh_f, sc_f, bf(w_ffn_in[0]), seq)
    x2d = _proj_residual(act, bf(w_ffn_down[0]), x2d, g_f, seq)

    sh_a, sc_a, g_a, sh_f, sc_f, g_f = vecs(mod[1], 6)
    kv_shift, kv_scale = vecs(kv_mod, 2)
    w_f_pad = jnp.zeros((d, LANES_V7X), F32).at[:, :N_HEADS].set(w_kvf[:, 2 * d:])
    b_f_pad = jnp.zeros((1, LANES_V7X), F32).at[0, :N_HEADS].set(b_f)
    k, v, q, log_f = _l1_proj(
        x2d, row(kv_norm_g), kv_shift, kv_scale, row(norm_attn_g[1]), sh_a, sc_a,
        bf(w_kvf[:, :d]), bf(w_kvf[:, d:2 * d]), bf(w_f_pad), bf(fox_w_q[0]),
        row(jnp.tile(k_norm_g, N_HEADS)), row(jnp.tile(q_norm_g[0], N_HEADS)), b_f_pad, seq)
    key_norm_bound = (1.01 * HEAD_DIM ** 0.5 * jnp.max(jnp.abs(k_norm_g))).reshape(1, 1).astype(F32)
    q_aug, k_aug, stats = _fox_prep(log_f, q, k, key_norm_bound, batch, seq)
    tables = stats[:, :, :3, :N_HEADS].transpose(2, 0, 3, 1).reshape(3, batch * N_HEADS, seq // TK)
    o = _fox_attn(q_aug, k_aug, v.reshape(batch, seq, d), tables[0], tables[1], tables[2], batch, seq)
    x2d = _proj_residual(o.reshape(m, d), bf(fox_w_o[0]), x2d, g_a, seq)
    act = _ffn_in(x2d, row(norm_ffn_g[1]), sh_f, sc_f, bf(w_ffn_in[1]), seq)
    x2d = _proj_residual(act, bf(w_ffn_down[1]), x2d, g_f, seq)
    return x2d.reshape(batch, seq, d)
```

```python
import functools

import jax
import jax.numpy as jnp
from jax import lax
from jax.experimental import pallas as pl
from jax.experimental.pallas import tpu as pltpu

D_MODEL = 1024
N_HEADS = 16
HEAD_DIM = D_MODEL // N_HEADS
EPS = 1e-6

LANES_V7X = 128
SUBLANES_V7X = 8
VMEM_BYTES_V7X = 64 * 1024 * 1024

HEADS_PER_LANE_TILE = LANES_V7X // HEAD_DIM
N_HEAD_PAIRS = N_HEADS // HEADS_PER_LANE_TILE

MXU_WIDTH_V7X = 256

TM = 512
COL_CHUNK = 2 * MXU_WIDTH_V7X
TQ = 512
TK = 512
SB_T = 256
SB_SUB = 4

ZERO_EXP = 110.0
SAFE_EXP_RANGE = 80.0
NEG_BIG = -1e30
LOG2_E = 1.4426950408889634

F32 = jnp.float32
BF16 = jnp.bfloat16


def _vmem_limit(block_bytes):
    want = 2 * block_bytes + 16 * 1024 * 1024
    return int(min(want, VMEM_BYTES_V7X - 8 * 1024 * 1024))


def _split2(x):
    hi = x.astype(BF16)
    lo = (x - hi.astype(F32)).astype(BF16)
    return hi, lo


def _dot(a, b):
    return jnp.dot(a, b, preferred_element_type=F32)


def _dot_nt(a, b):
    return lax.dot_general(a, b, (((1,), (1,)), ((), ())), preferred_element_type=F32)


def _ada_kernel(c_ref, w_ref, b_ref, o_ref):
    c = c_ref[...]
    c_act = c * jax.nn.sigmoid(c)
    o_ref[...] = jnp.dot(c_act, w_ref[...], precision=lax.Precision.HIGHEST,
                         preferred_element_type=F32) + b_ref[...]


def _ada_mod(c_pad, w, b):
    n_layers, d, n = w.shape
    tn = 1024
    rows = c_pad.shape[0]
    block_bytes = 4 * (d * tn + rows * d + tn + rows * tn)
    return pl.pallas_call(
        _ada_kernel,
        out_shape=jax.ShapeDtypeStruct((n_layers, rows, n), F32),
        grid=(n_layers, n // tn),
        in_specs=[
            pl.BlockSpec((rows, d), lambda l, j: (0, 0)),
            pl.BlockSpec((None, d, tn), lambda l, j: (l, 0, j)),
            pl.BlockSpec((None, 1, tn), lambda l, j: (l, 0, j)),
        ],
        out_specs=pl.BlockSpec((None, rows, tn), lambda l, j: (l, 0, j)),
        compiler_params=pltpu.CompilerParams(
            dimension_semantics=("arbitrary", "arbitrary"),
            vmem_limit_bytes=_vmem_limit(block_bytes)),
        name="ada_mod",
    )(c_pad, w, b.reshape(n_layers, 1, n))


def _norm_modulate(x, g, shift, scale):
    ms = jnp.mean(x * x, axis=-1, keepdims=True)
    y = x * lax.rsqrt(ms + EPS)
    return (y * g) * (1.0 + scale) + shift


def _row_spec(d):
    return pl.BlockSpec((1, d), lambda *_: (0, 0))


def _batch_vec_spec(d, tiles_per_batch):
    return pl.BlockSpec((None, 1, d), lambda i, *_: (i // tiles_per_batch, 0, 0))


def _column_chunks(n):
    return [(c, min(c + COL_CHUNK, n)) for c in range(0, n, COL_CHUNK)]


def _norm_matmul_kernel(x_ref, g_ref, sh_ref, sc_ref, w_ref, cs_ref, o_ref):
    h = _norm_modulate(x_ref[...], g_ref[...], sh_ref[...], sc_ref[...]).astype(BF16)
    for c0, c1 in _column_chunks(o_ref.shape[1]):
        o_ref[:, c0:c1] = (_dot(h, w_ref[:, c0:c1]) * cs_ref[:, c0:c1]).astype(o_ref.dtype)


def _norm_matmul(x2d, g, shift, scale, w, col_scale, seq):
    m, d = x2d.shape
    n = w.shape[1]
    block_bytes = 4 * TM * d + 2 * d * n + 2 * TM * n + 4 * n
    return pl.pallas_call(
        _norm_matmul_kernel,
        out_shape=jax.ShapeDtypeStruct((m, n), BF16),
        grid=(m // TM,),
        in_specs=[
            pl.BlockSpec((TM, d), lambda i: (i, 0)),
            _row_spec(d),
            _batch_vec_spec(d, seq // TM),
            _batch_vec_spec(d, seq // TM),
            pl.BlockSpec((d, n), lambda i: (0, 0)),
            pl.BlockSpec((1, n), lambda i: (0, 0)),
        ],
        out_specs=pl.BlockSpec((TM, n), lambda i: (i, 0)),
        compiler_params=pltpu.CompilerParams(
            dimension_semantics=("arbitrary",),
            vmem_limit_bytes=_vmem_limit(block_bytes)),
        name="norm_matmul",
    )(x2d, g, shift, scale, w, col_scale)


def _ffn_in_kernel(x_ref, g_ref, sh_ref, sc_ref, wg_ref, wu_ref, o_ref):
    h = _norm_modulate(x_ref[...], g_ref[...], sh_ref[...], sc_ref[...]).astype(BF16)
    for c0, c1 in _column_chunks(o_ref.shape[1]):
        gate = _dot(h, wg_ref[:, c0:c1])
        up = _dot(h, wu_ref[:, c0:c1])
        o_ref[:, c0:c1] = (gate * jax.nn.sigmoid(gate) * up).astype(o_ref.dtype)


def _ffn_in(x2d, g, shift, scale, w_in, seq):
    m, d = x2d.shape
    f = w_in.shape[1] // 2
    block_bytes = 4 * TM * d + 2 * 2 * d * f + 2 * TM * f
    return pl.pallas_call(
        _ffn_in_kernel,
        out_shape=jax.ShapeDtypeStruct((m, f), BF16),
        grid=(m // TM,),
        in_specs=[
            pl.BlockSpec((TM, d), lambda i: (i, 0)),
            _row_spec(d),
            _batch_vec_spec(d, seq // TM),
            _batch_vec_spec(d, seq // TM),
            pl.BlockSpec((d, f), lambda i: (0, 0)),
            pl.BlockSpec((d, f), lambda i: (0, 1)),
        ],
        out_specs=pl.BlockSpec((TM, f), lambda i: (i, 0)),
        compiler_params=pltpu.CompilerParams(
            dimension_semantics=("arbitrary",),
            vmem_limit_bytes=_vmem_limit(block_bytes)),
        name="ffn_in",
    )(x2d, g, shift, scale, w_in, w_in)


def _proj_residual_kernel(a_ref, w_ref, x_ref, gate_ref, o_ref):
    o_ref[...] = x_ref[...] + gate_ref[...] * _dot(a_ref[...], w_ref[...])


def _proj_residual(a, w, x2d, gate, seq):
    m, k = a.shape
    d = w.shape[1]
    block_bytes = 2 * TM * k + 2 * k * d + 2 * 4 * TM * d
    return pl.pallas_call(
        _proj_residual_kernel,
        out_shape=jax.ShapeDtypeStruct((m, d), F32),
        grid=(m // TM,),
        in_specs=[
            pl.BlockSpec((TM, k), lambda i: (i, 0)),
            pl.BlockSpec((k, d), lambda i: (0, 0)),
            pl.BlockSpec((TM, d), lambda i: (i, 0)),
            _batch_vec_spec(d, seq // TM),
        ],
        out_specs=pl.BlockSpec((TM, d), lambda i: (i, 0)),
        compiler_params=pltpu.CompilerParams(
            dimension_semantics=("arbitrary",),
            vmem_limit_bytes=_vmem_limit(block_bytes)),
        name="proj_residual",
    )(a, w, x2d, gate)


def _head_rms_norm(t, gain, group_mean):
    outs = []
    for c in range(t.shape[1] // LANES_V7X):
        tc = t[:, c * LANES_V7X:(c + 1) * LANES_V7X]
        hi, lo = _split2(tc * tc)
        ms = _dot(hi, group_mean) + _dot(lo, group_mean)
        outs.append(tc * lax.rsqrt(ms + EPS) * gain[:, c * LANES_V7X:(c + 1) * LANES_V7X])
    return jnp.concatenate(outs, axis=1)


def _l1_proj_kernel(x_ref, gkv_ref, shkv_ref, sckv_ref, gq_ref, shq_ref, scq_ref,
                    wk_ref, wv_ref, wf_ref, wq_ref, kg_ref, qg_ref, bf_ref, gm_ref,
                    k_ref, v_ref, q_ref, lf_ref):
    x = x_ref[...]
    ms = jnp.mean(x * x, axis=-1, keepdims=True)
    y = x * lax.rsqrt(ms + EPS)
    h_kv = ((y * gkv_ref[...]) * (1.0 + sckv_ref[...]) + shkv_ref[...]).astype(BF16)
    h_q = ((y * gq_ref[...]) * (1.0 + scq_ref[...]) + shq_ref[...]).astype(BF16)
    gm = gm_ref[...]
    k_ref[...] = _head_rms_norm(_dot(h_kv, wk_ref[...]), kg_ref[...], gm).astype(BF16)
    v_ref[...] = _dot(h_kv, wv_ref[...]).astype(BF16)
    q = _head_rms_norm(_dot(h_q, wq_ref[...]), qg_ref[...], gm)
    q_ref[...] = (q * (HEAD_DIM ** -0.5)).astype(BF16)
    lf_ref[...] = jax.nn.log_sigmoid(_dot(h_kv, wf_ref[...]) + bf_ref[...])


def _l1_proj(x2d, g_kv, sh_kv, sc_kv, g_q, sh_q, sc_q, wk, wv, wf, wq, k_gain, q_gain, b_f, seq):
    m, d = x2d.shape
    lane = jnp.arange(LANES_V7X)
    group_mean = jnp.where(lane[:, None] // HEAD_DIM == lane[None, :] // HEAD_DIM,
                           1.0 / HEAD_DIM, 0.0).astype(BF16)
    tpb = seq // TM
    block_bytes = 4 * TM * d + 3 * 2 * d * d + 2 * d * LANES_V7X + 3 * 2 * TM * d + 4 * TM * LANES_V7X
    full = lambda r, c: pl.BlockSpec((r, c), lambda i: (0, 0))
    return pl.pallas_call(
        _l1_proj_kernel,
        out_shape=(jax.ShapeDtypeStruct((m, d), BF16), jax.ShapeDtypeStruct((m, d), BF16),
                   jax.ShapeDtypeStruct((m, d), BF16), jax.ShapeDtypeStruct((m, LANES_V7X), F32)),
        grid=(m // TM,),
        in_specs=[
            pl.BlockSpec((TM, d), lambda i: (i, 0)),
            _row_spec(d), _batch_vec_spec(d, tpb), _batch_vec_spec(d, tpb),
            _row_spec(d), _batch_vec_spec(d, tpb), _batch_vec_spec(d, tpb),
            full(d, d), full(d, d), full(d, LANES_V7X), full(d, d),
            _row_spec(d), _row_spec(d), _row_spec(LANES_V7X), full(LANES_V7X, LANES_V7X),
        ],
        out_specs=(pl.BlockSpec((TM, d), lambda i: (i, 0)), pl.BlockSpec((TM, d), lambda i: (i, 0)),
                   pl.BlockSpec((TM, d), lambda i: (i, 0)), pl.BlockSpec((TM, LANES_V7X), lambda i: (i, 0))),
        compiler_params=pltpu.CompilerParams(
            dimension_semantics=("arbitrary",),
            vmem_limit_bytes=_vmem_limit(block_bytes)),
        name="l1_proj",
    )(x2d, g_kv, sh_kv, sc_kv, g_q, sh_q, sc_q, wk, wv, wf, wq, k_gain, q_gain, b_f, group_mean)


N_SPLIT = 3


def _split3(x):
    x0 = x.astype(BF16)
    r1 = x - x0.astype(F32)
    x1 = r1.astype(BF16)
    x2 = (r1 - x1.astype(F32)).astype(BF16)
    return x0, x1, x2


def _fox_prep_kernel(kb_ref, lf_ref, q_ref, k_ref, qa_ref, ka_ref, stats_ref, carry_scr):
    @pl.when(pl.program_id(1) == 0)
    def _():
        carry_scr[...] = jnp.zeros_like(carry_scr)

    r = lax.broadcasted_iota(jnp.int32, (TK, TK), 0)
    c = lax.broadcasted_iota(jnp.int32, (TK, TK), 1)
    tri = jnp.where(c <= r, 1.0, 0.0).astype(BF16)
    x0, x1, x2 = _split3(lf_ref[...])
    cs = _dot(tri, x0) + _dot(tri, x1) + _dot(tri, x2) + carry_scr[0:1, :]
    carry_scr[...] = jnp.broadcast_to(cs[TK - 1:TK, :], carry_scr.shape)

    d_idx = lax.broadcasted_iota(jnp.int32, (D_MODEL, LANES_V7X), 0)
    h_idx = lax.broadcasted_iota(jnp.int32, (D_MODEL, LANES_V7X), 1)
    head_of = jnp.where(d_idx // HEAD_DIM == h_idx, 1.0, 0.0).astype(BF16)
    qf = q_ref[...].astype(F32)
    hi, lo = _split2(qf * qf)
    reach = jnp.sqrt(_dot(hi, head_of) + _dot(lo, head_of)) * kb_ref[0, 0]

    stats_ref[...] = jnp.concatenate([
        jnp.max(2.0 * reach + cs, axis=0, keepdims=True),
        cs[0:1, :],
        jnp.max(reach, axis=0, keepdims=True),
        jnp.zeros((SUBLANES_V7X - 3, LANES_V7X), F32)], axis=0)

    lane = lax.broadcasted_iota(jnp.int32, (1, LANES_V7X), 1)
    packed = jnp.zeros((TK, LANES_V7X), F32)
    for i, term in enumerate(_split3(cs - reach) + _split3(-cs)):
        term = jnp.where(lane < N_HEADS, term.astype(F32), 0.0)
        packed = packed + (term if i == 0 else pltpu.roll(term, N_HEADS * i, axis=1))
    packed = packed.astype(BF16)

    src = lax.broadcasted_iota(jnp.int32, (LANES_V7X, 2 * LANES_V7X), 0)
    dst = lax.broadcasted_iota(jnp.int32, (LANES_V7X, 2 * LANES_V7X), 1)
    for h in range(N_HEADS):
        pair, parity = divmod(h, HEADS_PER_LANE_TILE)
        spare = HEAD_DIM * (1 - parity)
        place = jnp.where((src % N_HEADS == h) & (src < 2 * N_SPLIT * N_HEADS)
                          & (dst % LANES_V7X == spare + src // N_HEADS)
                          & ((dst >= LANES_V7X) == (src >= N_SPLIT * N_HEADS)), 1.0, 0.0)
        placed = _dot(packed, place.astype(BF16))
        ones_q = jnp.where((lane >= spare + N_SPLIT) & (lane < spare + 2 * N_SPLIT), 1.0, 0.0)
        ones_k = jnp.where((lane >= spare) & (lane < spare + N_SPLIT), 1.0, 0.0)
        aug_q = (placed[:, :LANES_V7X] + ones_q).astype(BF16)
        aug_k = (placed[:, LANES_V7X:] + ones_k).astype(BF16)
        own = (lane // HEAD_DIM) == parity
        cols = slice(pair * LANES_V7X, (pair + 1) * LANES_V7X)
        qa_ref[h] = jnp.where(own, q_ref[:, cols], aug_q)
        ka_ref[h] = jnp.where(own, k_ref[:, cols], aug_k)


def _fox_prep(log_f, q, k, key_norm_bound, batch, seq):
    assert TQ == TK
    d = q.shape[1]
    block_bytes = (4 * TK * LANES_V7X + 2 * 2 * TK * d + 2 * 2 * N_HEADS * TK * LANES_V7X
                   + 4 * SUBLANES_V7X * LANES_V7X)
    aug_shape = jax.ShapeDtypeStruct((batch, N_HEADS, seq, LANES_V7X), BF16)
    aug_spec = pl.BlockSpec((None, N_HEADS, TK, LANES_V7X), lambda b, t: (b, 0, t, 0))
    return pl.pallas_call(
        _fox_prep_kernel,
        out_shape=(aug_shape, aug_shape,
                   jax.ShapeDtypeStruct((batch, seq // TK, SUBLANES_V7X, LANES_V7X), F32)),
        grid=(batch, seq // TK),
        in_specs=[
            pl.BlockSpec(memory_space=pltpu.SMEM),
            pl.BlockSpec((None, TK, LANES_V7X), lambda b, t: (b, t, 0)),
            pl.BlockSpec((None, TK, d), lambda b, t: (b, t, 0)),
            pl.BlockSpec((None, TK, d), lambda b, t: (b, t, 0)),
        ],
        out_specs=(aug_spec, aug_spec,
                   pl.BlockSpec((None, None, SUBLANES_V7X, LANES_V7X), lambda b, t: (b, t, 0, 0))),
        scratch_shapes=[pltpu.VMEM((SUBLANES_V7X, LANES_V7X), F32)],
        compiler_params=pltpu.CompilerParams(
            dimension_semantics=("arbitrary", "arbitrary"),
            vmem_limit_bytes=_vmem_limit(block_bytes)),
        name="fox_prep",
    )(key_norm_bound, log_f.reshape(batch, seq, LANES_V7X), q.reshape(batch, seq, d),
      k.reshape(batch, seq, d))


def _head_select(head):
    lane = lax.broadcasted_iota(jnp.int32, (1, LANES_V7X), 1)
    return (lane // HEAD_DIM) == head


def _widen(col128, width):
    reps = width // LANES_V7X
    return col128 if reps == 1 else jnp.concatenate([col128] * reps, axis=1)


def _sb_kernel(q_ref, k_ref, v_ref, o_ref):
    i = pl.program_id(2)
    row = lax.broadcasted_iota(jnp.int32, (SB_T, SB_T), 0)
    col = lax.broadcasted_iota(jnp.int32, (SB_T, SB_T), 1)
    diag_mask = col < row
    after = jnp.where(row > col, 1.0, 0.0).astype(BF16)
    ones = jnp.ones((SB_T, LANES_V7X), BF16)
    ones_wide = jnp.ones((SB_T, SB_T), BF16)

    def softplus(z):
        return jnp.maximum(z, 0.0) + jnp.log2(1.0 + jnp.exp2(-jnp.abs(z)))

    def keys(j):
        return k_ref[pl.ds(j * SB_T, SB_T), :], v_ref[pl.ds(j * SB_T, SB_T), :]

    def band(qh, g):
        k_d, v_d = keys(g)
        k_p, v_p = keys(jnp.maximum(g - 1, 0))
        z_d = _dot_nt(qh, k_d)
        z_p = jnp.where(g > 0, _dot_nt(qh, k_p), NEG_BIG)
        sp_d = jnp.where(diag_mask, softplus(z_d), 0.0)
        sp_p = softplus(z_p)
        sp_d16 = sp_d.astype(BF16)
        within_d = _dot(sp_d16, after)
        within_p = _dot(sp_p.astype(BF16), after) + _dot(sp_d16, ones_wide)
        a_d = jnp.where(diag_mask, jnp.exp2((z_d - sp_d) - within_d), 0.0)
        a_p = jnp.exp2((z_p - sp_p) - within_p)
        acc = _dot(a_d.astype(BF16), v_d) + _dot(a_p.astype(BF16), v_p)
        total = (within_p + sp_p)[:, 0:1]
        return jnp.broadcast_to(-total, (SB_T, LANES_V7X)), acc

    def block(qh, j, carry, acc):
        kj, vj = keys(j)
        z = _dot_nt(qh, kj)
        sp = softplus(z)
        sp16 = sp.astype(BF16)
        within = _dot(sp16, after)
        a = jnp.exp2((z - sp) - within + _widen(carry, SB_T))
        acc = acc + _dot(a.astype(BF16), vj)
        carry = carry - _dot(sp16, ones)
        return carry, acc

    chains = []
    for u in range(SB_SUB):
        g = i * SB_SUB + u
        q = q_ref[u * SB_T:(u + 1) * SB_T, :]
        for head in range(HEADS_PER_LANE_TILE):
            qh = jnp.where(_head_select(head), q, 0.0).astype(BF16)
            chains.append((g, qh) + band(qh, g))

    def walk_on():
        outs = []
        for g, qh, carry, acc in chains:
            def cond(state):
                j, carry, _ = state
                return jnp.logical_and(j >= 0, jnp.max(carry) > -ZERO_EXP * LOG2_E)

            def body(state, qh=qh):
                j, carry, acc = state
                carry, acc = block(qh, j, carry, acc)
                return j - 1, carry, acc

            outs.append(lax.while_loop(cond, body, (g - 2, carry, acc))[2])
        return outs

    open_rows = functools.reduce(jnp.maximum, [carry for _, _, carry, _ in chains])
    outs = lax.cond(jnp.max(open_rows) > -ZERO_EXP * LOG2_E, walk_on,
                    lambda: [acc for _, _, _, acc in chains])
    for u in range(SB_SUB):
        pair = outs[u * HEADS_PER_LANE_TILE:(u + 1) * HEADS_PER_LANE_TILE]
        o_ref[u * SB_T:(u + 1) * SB_T, :] = jnp.where(_head_select(0), pair[0], pair[1]).astype(o_ref.dtype)


def _sb_attn(qkv, batch, seq):
    rows = SB_SUB * SB_T
    block_bytes = 2 * (2 * rows * LANES_V7X + 2 * seq * LANES_V7X)
    return pl.pallas_call(
        _sb_kernel,
        out_shape=jax.ShapeDtypeStruct((batch, seq, D_MODEL), BF16),
        grid=(batch, N_HEAD_PAIRS, seq // rows),
        in_specs=[
            pl.BlockSpec((None, rows, LANES_V7X), lambda b, p, i: (b, i, p)),
            pl.BlockSpec((None, seq, LANES_V7X), lambda b, p, i: (b, 0, N_HEAD_PAIRS + p)),
            pl.BlockSpec((None, seq, LANES_V7X), lambda b, p, i: (b, 0, 2 * N_HEAD_PAIRS + p)),
        ],
        out_specs=pl.BlockSpec((None, rows, LANES_V7X), lambda b, p, i: (b, i, p)),
        compiler_params=pltpu.CompilerParams(
            dimension_semantics=("arbitrary", "arbitrary", "arbitrary"),
            vmem_limit_bytes=_vmem_limit(block_bytes)),
        name="sb_attn",
    )(qkv, qkv, qkv)


def _fox_kernel(a_ref, fk0_ref, g_ref, qa_ref, ka_ref, v_ref, o_ref):
    b = pl.program_id(0)
    p_id = pl.program_id(1)
    i = pl.program_id(2)
    n_blocks = fk0_ref.shape[1]
    row = lax.broadcasted_iota(jnp.int32, (TQ, TK), 0)
    col = lax.broadcasted_iota(jnp.int32, (TQ, TK), 1)
    diag_mask = col <= row

    acc0 = jnp.zeros((TQ, LANES_V7X), F32)
    heads = range(HEADS_PER_LANE_TILE)
    bhs = [(b * N_HEAD_PAIRS + p_id) * HEADS_PER_LANE_TILE + head for head in heads]

    def scores(head, j):
        return _dot_nt(qa_ref[head], ka_ref[head, pl.ds(j * TK, TK), :])

    def values(head, j):
        vj = v_ref[pl.ds(j * TK, TK), :]
        return jnp.where(_head_select(head), vj, jnp.ones_like(vj))

    def count_left(head):
        bound = a_ref[bhs[head], i]

        def more(n):
            j = i - 1 - n
            right = fk0_ref[bhs[head], jnp.minimum(j + 1, n_blocks - 1)]
            return jnp.logical_and(j >= 0, bound - right > -ZERO_EXP)

        return lax.while_loop(more, lambda n: n + 1, jnp.int32(0))

    def fast_walks():
        n_left = [count_left(head) for head in heads]
        state = [(jnp.where(diag_mask, jnp.exp(scores(head, i)), 0.0).astype(BF16), acc0)
                 for head in heads]
        for head in heads:
            def step(j, p, acc, head=head):
                p_next = jnp.exp(scores(head, j)).astype(BF16)
                return p_next, acc + _dot(p, values(head, j + 1))

            def two_steps(t, carried, first, step=step):
                p, acc = step(first - 2 * t, *carried)
                return step(first - 2 * t - 1, p, acc)

            odd = n_left[head] % 2
            p0, a0 = state[head]
            carried = lax.cond(odd == 1, lambda: step(i - 1, p0, a0), lambda: (p0, a0))
            state[head] = lax.fori_loop(0, n_left[head] // 2,
                                        functools.partial(two_steps, first=i - 1 - odd), carried)
        return [acc + _dot(p, values(head, i - n_left[head]))
                for head, (p, acc) in zip(heads, state)]

    def online_walks():
        accs = []
        for head in heads:
            def body(t, carried, head=head):
                acc, m = carried
                s = jnp.where(jnp.logical_or(t > 0, diag_mask), scores(head, i - t), NEG_BIG)
                m_new = jnp.maximum(m, jnp.max(s, axis=-1, keepdims=True))
                p = jnp.exp(s - m_new).astype(BF16)
                return jnp.exp(m - m_new) * acc + _dot(p, values(head, i - t)), m_new

            m0 = jnp.full((TQ, 1), NEG_BIG, F32)
            accs.append(lax.fori_loop(0, count_left(head) + 1, body, (acc0, m0))[0])
        return accs

    largest_reach = jnp.maximum(g_ref[bhs[0], i], g_ref[bhs[1], i])
    accs = lax.cond(2.0 * largest_reach < SAFE_EXP_RANGE, fast_walks, online_walks)
    outs = [acc / pltpu.roll(acc, HEAD_DIM, axis=1) for acc in accs]
    o_ref[...] = jnp.where(_head_select(0), outs[0], outs[1]).astype(o_ref.dtype)


def _fox_attn(q_aug, k_aug, v, walk_bound, fk_block_start, max_reach, batch, seq):
    assert TQ == TK
    block_bytes = 2 * (HEADS_PER_LANE_TILE * (TQ + seq) * LANES_V7X + seq * LANES_V7X + TQ * LANES_V7X)
    smem = pl.BlockSpec(memory_space=pltpu.SMEM)
    return pl.pallas_call(
        _fox_kernel,
        out_shape=jax.ShapeDtypeStruct((batch, seq, D_MODEL), BF16),
        grid=(batch, N_HEAD_PAIRS, seq // TQ),
        in_specs=[
            smem, smem, smem,
            pl.BlockSpec((None, HEADS_PER_LANE_TILE, TQ, LANES_V7X), lambda b, p, i: (b, p, i, 0)),
            pl.BlockSpec((None, HEADS_PER_LANE_TILE, seq, LANES_V7X), lambda b, p, i: (b, p, 0, 0)),
            pl.BlockSpec((None, seq, LANES_V7X), lambda b, p, i: (b, 0, p)),
        ],
        out_specs=pl.BlockSpec((None, TQ, LANES_V7X), lambda b, p, i: (b, i, p)),
        compiler_params=pltpu.CompilerParams(
            dimension_semantics=("arbitrary", "arbitrary", "arbitrary"),
            vmem_limit_bytes=_vmem_limit(block_bytes)),
        name="fox_attn",
    )(walk_bound, fk_block_start, max_reach, q_aug, k_aug, v)


def kernel(x, c, ada_w, ada_b, norm_attn_g, norm_ffn_g, w_ffn_in, w_ffn_down, sb_w_qkv, sb_w_o,
           kv_ada_w, kv_ada_b, kv_norm_g, w_kvf, b_f, k_norm_g, fox_w_q, q_norm_g, fox_w_o):
    batch, seq, d = x.shape
    assert d == D_MODEL and seq % TM == 0 and seq % TQ == 0
    assert ada_w.shape[0] == 2 and sb_w_qkv.shape[0] == 1 and fox_w_q.shape[0] == 1
    m = batch * seq

    c_pad = jnp.zeros((SUBLANES_V7X, d), F32).at[:batch].set(c.astype(F32))
    mod = _ada_mod(c_pad, ada_w, ada_b)[:, :batch]
    kv_mod = _ada_mod(c_pad, kv_ada_w[None], kv_ada_b[None])[0, :batch]

    def vecs(t, n):
        return [t[:, None, j * d:(j + 1) * d] for j in range(n)]

    row = lambda t: t.reshape(1, -1).astype(F32)
    bf = lambda t: t.astype(BF16)

    x2d = x.reshape(m, d)

    sh_a, sc_a, g_a, sh_f, sc_f, g_f = vecs(mod[0], 6)
    qkv_scale = jnp.concatenate([jnp.full((1, d), LOG2_E * HEAD_DIM ** -0.5, F32),
                                 jnp.ones((1, 2 * d), F32)], axis=1)
    qkv = _norm_matmul(x2d, row(norm_attn_g[0]), sh_a, sc_a, bf(sb_w_qkv[0]), qkv_scale, seq)
    o = _sb_attn(qkv.reshape(batch, seq, 3 * d), batch, seq)
    x2d = _proj_residual(o.reshape(m, d), bf(sb_w_o[0]), x2d, g_a, seq)
    act = _ffn_in(x2d, row(norm_ffn_g[0]), sh_f, sc_f, bf(w_ffn_in[0]), seq)
    x2d = _proj_residual(act, bf(w_ffn_down[0]), x2d, g_f, seq)

    sh_a, sc_a, g_a, sh_f, sc_f, g_f = vecs(mod[1], 6)
    kv_shift, kv_scale = vecs(kv_mod, 2)
    w_f_pad = jnp.zeros((d, LANES_V7X), F32).at[:, :N_HEADS].set(w_kvf[:, 2 * d:])
    b_f_pad = jnp.zeros((1, LANES_V7X), F32).at[0, :N_HEADS].set(b_f)
    k, v, q, log_f = _l1_proj(
        x2d, row(kv_norm_g), kv_shift, kv_scale, row(norm_attn_g[1]), sh_a, sc_a,
        bf(w_kvf[:, :d]), bf(w_kvf[:, d:2 * d]), bf(w_f_pad), bf(fox_w_q[0]),
        row(jnp.tile(k_norm_g, N_HEADS)), row(jnp.tile(q_norm_g[0], N_HEADS)), b_f_pad, seq)
    key_norm_bound = (1.01 * HEAD_DIM ** 0.5 * jnp.max(jnp.abs(k_norm_g))).reshape(1, 1).astype(F32)
    q_aug, k_aug, stats = _fox_prep(log_f, q, k, key_norm_bound, batch, seq)
    tables = stats[:, :, :3, :N_HEADS].transpose(2, 0, 3, 1).reshape(3, batch * N_HEADS, seq // TK)
    o = _fox_attn(q_aug, k_aug, v.reshape(batch, seq, d), tables[0], tables[1], tables[2], batch, seq)
    x2d = _proj_residual(o.reshape(m, d), bf(fox_w_o[0]), x2d, g_a, seq)
    act = _ffn_in(x2d, row(norm_ffn_g[1]), sh_f, sc_f, bf(w_ffn_in[1]), seq)
    x2d = _proj_residual(act, bf(w_ffn_down[1]), x2d, g_f, seq)
    return x2d.reshape(batch, seq, d)
```

```python
import functools

import jax
import jax.numpy as jnp
from jax import lax
from jax.experimental import pallas as pl
from jax.experimental.pallas import tpu as pltpu

D_MODEL = 1024
N_HEADS = 16
HEAD_DIM = D_MODEL // N_HEADS
EPS = 1e-6

LANES_V7X = 128
SUBLANES_V7X = 8
VMEM_BYTES_V7X = 64 * 1024 * 1024

HEADS_PER_LANE_TILE = LANES_V7X // HEAD_DIM
N_HEAD_PAIRS = N_HEADS // HEADS_PER_LANE_TILE

MXU_WIDTH_V7X = 256

TM = 512
COL_CHUNK = 2 * MXU_WIDTH_V7X
TQ = 512
TK = 512
SB_T = 256
SB_SUB = 4

ZERO_EXP = 104.5
SAFE_EXP_RANGE = 80.0
NEG_BIG = -1e30
LOG2_E = 1.4426950408889634

F32 = jnp.float32
BF16 = jnp.bfloat16


def _vmem_limit(block_bytes):
    want = 2 * block_bytes + 16 * 1024 * 1024
    return int(min(want, VMEM_BYTES_V7X - 8 * 1024 * 1024))


def _split2(x):
    hi = x.astype(BF16)
    lo = (x - hi.astype(F32)).astype(BF16)
    return hi, lo


def _dot(a, b):
    return jnp.dot(a, b, preferred_element_type=F32)


def _dot_nt(a, b):
    return lax.dot_general(a, b, (((1,), (1,)), ((), ())), preferred_element_type=F32)


def _ada_kernel(c_ref, w_ref, b_ref, o_ref):
    c = c_ref[...]
    c_act = c * jax.nn.sigmoid(c)
    o_ref[...] = jnp.dot(c_act, w_ref[...], precision=lax.Precision.HIGHEST,
                         preferred_element_type=F32) + b_ref[...]


def _ada_mod(c_pad, w, b):
    n_layers, d, n = w.shape
    tn = 1024
    rows = c_pad.shape[0]
    block_bytes = 4 * (d * tn + rows * d + tn + rows * tn)
    return pl.pallas_call(
        _ada_kernel,
        out_shape=jax.ShapeDtypeStruct((n_layers, rows, n), F32),
        grid=(n_layers, n // tn),
        in_specs=[
            pl.BlockSpec((rows, d), lambda l, j: (0, 0)),
            pl.BlockSpec((None, d, tn), lambda l, j: (l, 0, j)),
            pl.BlockSpec((None, 1, tn), lambda l, j: (l, 0, j)),
        ],
        out_specs=pl.BlockSpec((None, rows, tn), lambda l, j: (l, 0, j)),
        compiler_params=pltpu.CompilerParams(
            dimension_semantics=("arbitrary", "arbitrary"),
            vmem_limit_bytes=_vmem_limit(block_bytes)),
        name="ada_mod",
    )(c_pad, w, b.reshape(n_layers, 1, n))


def _norm_modulate(x, g, shift, scale):
    ms = jnp.mean(x * x, axis=-1, keepdims=True)
    y = x * lax.rsqrt(ms + EPS)
    return (y * g) * (1.0 + scale) + shift


def _row_spec(d):
    return pl.BlockSpec((1, d), lambda *_: (0, 0))


def _batch_vec_spec(d, tiles_per_batch):
    return pl.BlockSpec((None, 1, d), lambda i, *_: (i // tiles_per_batch, 0, 0))


def _column_chunks(n):
    return [(c, min(c + COL_CHUNK, n)) for c in range(0, n, COL_CHUNK)]


def _norm_matmul_kernel(x_ref, g_ref, sh_ref, sc_ref, w_ref, cs_ref, o_ref):
    h = _norm_modulate(x_ref[...], g_ref[...], sh_ref[...], sc_ref[...]).astype(BF16)
    for c0, c1 in _column_chunks(o_ref.shape[1]):
        o_ref[:, c0:c1] = (_dot(h, w_ref[:, c0:c1]) * cs_ref[:, c0:c1]).astype(o_ref.dtype)


def _norm_matmul(x2d, g, shift, scale, w, col_scale, seq):
    m, d = x2d.shape
    n = w.shape[1]
    block_bytes = 4 * TM * d + 2 * d * n + 2 * TM * n + 4 * n
    return pl.pallas_call(
        _norm_matmul_kernel,
        out_shape=jax.ShapeDtypeStruct((m, n), BF16),
        grid=(m // TM,),
        in_specs=[
            pl.BlockSpec((TM, d), lambda i: (i, 0)),
            _row_spec(d),
            _batch_vec_spec(d, seq // TM),
            _batch_vec_spec(d, seq // TM),
            pl.BlockSpec((d, n), lambda i: (0, 0)),
            pl.BlockSpec((1, n), lambda i: (0, 0)),
        ],
        out_specs=pl.BlockSpec((TM, n), lambda i: (i, 0)),
        compiler_params=pltpu.CompilerParams(
            dimension_semantics=("arbitrary",),
            vmem_limit_bytes=_vmem_limit(block_bytes)),
        name="norm_matmul",
    )(x2d, g, shift, scale, w, col_scale)


def _out_ffn_in_kernel(a_ref, wo_ref, x_ref, gate_ref, g_ref, sh_ref, sc_ref, wg_ref, wu_ref,
                       x1_ref, act_ref):
    x1 = x_ref[...] + gate_ref[...] * _dot(a_ref[...], wo_ref[...])
    x1_ref[...] = x1
    h = _norm_modulate(x1, g_ref[...], sh_ref[...], sc_ref[...]).astype(BF16)
    for c0, c1 in _column_chunks(act_ref.shape[1]):
        gate = _dot(h, wg_ref[:, c0:c1])
        up = _dot(h, wu_ref[:, c0:c1])
        act_ref[:, c0:c1] = (gate * jax.nn.sigmoid(gate) * up).astype(act_ref.dtype)


def _out_ffn_in(a, w_o, x2d, gate, g, shift, scale, w_in, seq):
    m, d = x2d.shape
    f = w_in.shape[1] // 2
    tpb = seq // TM
    block_bytes = 2 * TM * d + 2 * d * d + 2 * 4 * TM * d + 2 * 2 * d * f + 2 * TM * f
    return pl.pallas_call(
        _out_ffn_in_kernel,
        out_shape=(jax.ShapeDtypeStruct((m, d), F32), jax.ShapeDtypeStruct((m, f), BF16)),
        grid=(m // TM,),
        in_specs=[
            pl.BlockSpec((TM, d), lambda i: (i, 0)),
            pl.BlockSpec((d, d), lambda i: (0, 0)),
            pl.BlockSpec((TM, d), lambda i: (i, 0)),
            _batch_vec_spec(d, tpb),
            _row_spec(d),
            _batch_vec_spec(d, tpb),
            _batch_vec_spec(d, tpb),
            pl.BlockSpec((d, f), lambda i: (0, 0)),
            pl.BlockSpec((d, f), lambda i: (0, 1)),
        ],
        out_specs=(pl.BlockSpec((TM, d), lambda i: (i, 0)), pl.BlockSpec((TM, f), lambda i: (i, 0))),
        compiler_params=pltpu.CompilerParams(
            dimension_semantics=("arbitrary",),
            vmem_limit_bytes=_vmem_limit(block_bytes)),
        name="out_ffn_in",
    )(a, w_o, x2d, gate, g, shift, scale, w_in, w_in)


def _proj_residual_kernel(a_ref, w_ref, x_ref, gate_ref, o_ref):
    o_ref[...] = x_ref[...] + gate_ref[...] * _dot(a_ref[...], w_ref[...])


def _proj_residual(a, w, x2d, gate, seq):
    m, k = a.shape
    d = w.shape[1]
    block_bytes = 2 * TM * k + 2 * k * d + 2 * 4 * TM * d
    return pl.pallas_call(
        _proj_residual_kernel,
        out_shape=jax.ShapeDtypeStruct((m, d), F32),
        grid=(m // TM,),
        in_specs=[
            pl.BlockSpec((TM, k), lambda i: (i, 0)),
            pl.BlockSpec((k, d), lambda i: (0, 0)),
            pl.BlockSpec((TM, d), lambda i: (i, 0)),
            _batch_vec_spec(d, seq // TM),
        ],
        out_specs=pl.BlockSpec((TM, d), lambda i: (i, 0)),
        compiler_params=pltpu.CompilerParams(
            dimension_semantics=("arbitrary",),
            vmem_limit_bytes=_vmem_limit(block_bytes)),
        name="proj_residual",
    )(a, w, x2d, gate)


def _head_rms_norm(t, gain, group_mean):
    outs = []
    for c in range(t.shape[1] // LANES_V7X):
        tc = t[:, c * LANES_V7X:(c + 1) * LANES_V7X]
        hi, lo = _split2(tc * tc)
        ms = _dot(hi, group_mean) + _dot(lo, group_mean)
        outs.append(tc * lax.rsqrt(ms + EPS) * gain[:, c * LANES_V7X:(c + 1) * LANES_V7X])
    return jnp.concatenate(outs, axis=1)


def _l1_proj_kernel(x_ref, gkv_ref, shkv_ref, sckv_ref, gq_ref, shq_ref, scq_ref,
                    wk_ref, wv_ref, wf_ref, wq_ref, kg_ref, qg_ref, bf_ref, gm_ref,
                    k_ref, v_ref, q_ref, lf_ref):
    x = x_ref[...]
    ms = jnp.mean(x * x, axis=-1, keepdims=True)
    y = x * lax.rsqrt(ms + EPS)
    h_kv = ((y * gkv_ref[...]) * (1.0 + sckv_ref[...]) + shkv_ref[...]).astype(BF16)
    h_q = ((y * gq_ref[...]) * (1.0 + scq_ref[...]) + shq_ref[...]).astype(BF16)
    gm = gm_ref[...]
    k_ref[...] = _head_rms_norm(_dot(h_kv, wk_ref[...]), kg_ref[...], gm).astype(BF16)
    v_ref[...] = _dot(h_kv, wv_ref[...]).astype(BF16)
    q = _head_rms_norm(_dot(h_q, wq_ref[...]), qg_ref[...], gm)
    q_ref[...] = (q * (HEAD_DIM ** -0.5)).astype(BF16)
    lf_ref[...] = jax.nn.log_sigmoid(_dot(h_kv, wf_ref[...]) + bf_ref[...])


def _l1_proj(x2d, g_kv, sh_kv, sc_kv, g_q, sh_q, sc_q, wk, wv, wf, wq, k_gain, q_gain, b_f, seq):
    m, d = x2d.shape
    lane = jnp.arange(LANES_V7X)
    group_mean = jnp.where(lane[:, None] // HEAD_DIM == lane[None, :] // HEAD_DIM,
                           1.0 / HEAD_DIM, 0.0).astype(BF16)
    tpb = seq // TM
    block_bytes = 4 * TM * d + 3 * 2 * d * d + 2 * d * LANES_V7X + 3 * 2 * TM * d + 4 * TM * LANES_V7X
    full = lambda r, c: pl.BlockSpec((r, c), lambda i: (0, 0))
    return pl.pallas_call(
        _l1_proj_kernel,
        out_shape=(jax.ShapeDtypeStruct((m, d), BF16), jax.ShapeDtypeStruct((m, d), BF16),
                   jax.ShapeDtypeStruct((m, d), BF16), jax.ShapeDtypeStruct((m, LANES_V7X), F32)),
        grid=(m // TM,),
        in_specs=[
            pl.BlockSpec((TM, d), lambda i: (i, 0)),
            _row_spec(d), _batch_vec_spec(d, tpb), _batch_vec_spec(d, tpb),
            _row_spec(d), _batch_vec_spec(d, tpb), _batch_vec_spec(d, tpb),
            full(d, d), full(d, d), full(d, LANES_V7X), full(d, d),
            _row_spec(d), _row_spec(d), _row_spec(LANES_V7X), full(LANES_V7X, LANES_V7X),
        ],
        out_specs=(pl.BlockSpec((TM, d), lambda i: (i, 0)), pl.BlockSpec((TM, d), lambda i: (i, 0)),
                   pl.BlockSpec((TM, d), lambda i: (i, 0)), pl.BlockSpec((TM, LANES_V7X), lambda i: (i, 0))),
        compiler_params=pltpu.CompilerParams(
            dimension_semantics=("arbitrary",),
            vmem_limit_bytes=_vmem_limit(block_bytes)),
        name="l1_proj",
    )(x2d, g_kv, sh_kv, sc_kv, g_q, sh_q, sc_q, wk, wv, wf, wq, k_gain, q_gain, b_f, group_mean)


N_SPLIT = 3


def _split3(x):
    x0 = x.astype(BF16)
    r1 = x - x0.astype(F32)
    x1 = r1.astype(BF16)
    x2 = (r1 - x1.astype(F32)).astype(BF16)
    return x0, x1, x2


def _fox_prep_kernel(kb_ref, lf_ref, q_ref, k_ref, qa_ref, ka_ref, stats_ref, carry_scr):
    @pl.when(pl.program_id(1) == 0)
    def _():
        carry_scr[...] = jnp.zeros_like(carry_scr)

    r = lax.broadcasted_iota(jnp.int32, (TK, TK), 0)
    c = lax.broadcasted_iota(jnp.int32, (TK, TK), 1)
    tri = jnp.where(c <= r, 1.0, 0.0).astype(BF16)
    x0, x1, x2 = _split3(lf_ref[...])
    cs = _dot(tri, x0) + _dot(tri, x1) + _dot(tri, x2) + carry_scr[0:1, :]
    carry_scr[...] = jnp.broadcast_to(cs[TK - 1:TK, :], carry_scr.shape)

    d_idx = lax.broadcasted_iota(jnp.int32, (D_MODEL, LANES_V7X), 0)
    h_idx = lax.broadcasted_iota(jnp.int32, (D_MODEL, LANES_V7X), 1)
    head_of = jnp.where(d_idx // HEAD_DIM == h_idx, 1.0, 0.0).astype(BF16)
    qf = q_ref[...].astype(F32)
    hi, lo = _split2(qf * qf)
    reach = jnp.sqrt(_dot(hi, head_of) + _dot(lo, head_of)) * kb_ref[0, 0]
    hi, lo = _split2(qf * k_ref[...].astype(F32))
    own_logit = _dot(hi, head_of) + _dot(lo, head_of)

    stats_ref[...] = jnp.concatenate([
        jnp.max(reach - own_logit + cs, axis=0, keepdims=True),
        cs[0:1, :],
        jnp.max(reach, axis=0, keepdims=True),
        jnp.zeros((SUBLANES_V7X - 3, LANES_V7X), F32)], axis=0)

    lane = lax.broadcasted_iota(jnp.int32, (1, LANES_V7X), 1)
    packed = jnp.zeros((TK, LANES_V7X), F32)
    for i, term in enumerate(_split3(cs - reach) + _split3(-cs)):
        term = jnp.where(lane < N_HEADS, term.astype(F32), 0.0)
        packed = packed + (term if i == 0 else pltpu.roll(term, N_HEADS * i, axis=1))
    packed = packed.astype(BF16)

    src = lax.broadcasted_iota(jnp.int32, (LANES_V7X, 2 * LANES_V7X), 0)
    dst = lax.broadcasted_iota(jnp.int32, (LANES_V7X, 2 * LANES_V7X), 1)
    for h in range(N_HEADS):
        pair, parity = divmod(h, HEADS_PER_LANE_TILE)
        spare = HEAD_DIM * (1 - parity)
        place = jnp.where((src % N_HEADS == h) & (src < 2 * N_SPLIT * N_HEADS)
                          & (dst % LANES_V7X == spare + src // N_HEADS)
                          & ((dst >= LANES_V7X) == (src >= N_SPLIT * N_HEADS)), 1.0, 0.0)
        placed = _dot(packed, place.astype(BF16))
        ones_q = jnp.where((lane >= spare + N_SPLIT) & (lane < spare + 2 * N_SPLIT), 1.0, 0.0)
        ones_k = jnp.where((lane >= spare) & (lane < spare + N_SPLIT), 1.0, 0.0)
        aug_q = (placed[:, :LANES_V7X] + ones_q).astype(BF16)
        aug_k = (placed[:, LANES_V7X:] + ones_k).astype(BF16)
        own = (lane // HEAD_DIM) == parity
        cols = slice(pair * LANES_V7X, (pair + 1) * LANES_V7X)
        qa_ref[h] = jnp.where(own, q_ref[:, cols], aug_q)
        ka_ref[h] = jnp.where(own, k_ref[:, cols], aug_k)


def _fox_prep(log_f, q, k, key_norm_bound, batch, seq):
    assert TQ == TK
    d = q.shape[1]
    block_bytes = (4 * TK * LANES_V7X + 2 * 2 * TK * d + 2 * 2 * N_HEADS * TK * LANES_V7X
                   + 4 * SUBLANES_V7X * LANES_V7X)
    aug_shape = jax.ShapeDtypeStruct((batch, N_HEADS, seq, LANES_V7X), BF16)
    aug_spec = pl.BlockSpec((None, N_HEADS, TK, LANES_V7X), lambda b, t: (b, 0, t, 0))
    return pl.pallas_call(
        _fox_prep_kernel,
        out_shape=(aug_shape, aug_shape,
                   jax.ShapeDtypeStruct((batch, seq // TK, SUBLANES_V7X, LANES_V7X), F32)),
        grid=(batch, seq // TK),
        in_specs=[
            pl.BlockSpec(memory_space=pltpu.SMEM),
            pl.BlockSpec((None, TK, LANES_V7X), lambda b, t: (b, t, 0)),
            pl.BlockSpec((None, TK, d), lambda b, t: (b, t, 0)),
            pl.BlockSpec((None, TK, d), lambda b, t: (b, t, 0)),
        ],
        out_specs=(aug_spec, aug_spec,
                   pl.BlockSpec((None, None, SUBLANES_V7X, LANES_V7X), lambda b, t: (b, t, 0, 0))),
        scratch_shapes=[pltpu.VMEM((SUBLANES_V7X, LANES_V7X), F32)],
        compiler_params=pltpu.CompilerParams(
            dimension_semantics=("arbitrary", "arbitrary"),
            vmem_limit_bytes=_vmem_limit(block_bytes)),
        name="fox_prep",
    )(key_norm_bound, log_f.reshape(batch, seq, LANES_V7X), q.reshape(batch, seq, d),
      k.reshape(batch, seq, d))


def _head_select(head):
    lane = lax.broadcasted_iota(jnp.int32, (1, LANES_V7X), 1)
    return (lane // HEAD_DIM) == head


def _widen(col128, width):
    reps = width // LANES_V7X
    return col128 if reps == 1 else jnp.concatenate([col128] * reps, axis=1)


def _sb_kernel(q_ref, k_ref, v_ref, o_ref):
    i = pl.program_id(2)
    row = lax.broadcasted_iota(jnp.int32, (SB_T, SB_T), 0)
    col = lax.broadcasted_iota(jnp.int32, (SB_T, SB_T), 1)
    diag_mask = col < row
    after = jnp.where(row > col, 1.0, 0.0).astype(BF16)
    ones = jnp.ones((SB_T, LANES_V7X), BF16)
    ones_wide = jnp.ones((SB_T, SB_T), BF16)

    def softplus(z):
        return jnp.maximum(z, 0.0) + jnp.log2(1.0 + jnp.exp2(-jnp.abs(z)))

    def keys(j):
        return k_ref[pl.ds(j * SB_T, SB_T), :], v_ref[pl.ds(j * SB_T, SB_T), :]

    def band(qh, g):
        k_d, v_d = keys(g)
        k_p, v_p = keys(jnp.maximum(g - 1, 0))
        z_d = _dot_nt(qh, k_d)
        z_p = jnp.where(g > 0, _dot_nt(qh, k_p), NEG_BIG)
        sp_d = jnp.where(diag_mask, softplus(z_d), 0.0)
        sp_p = softplus(z_p)
        sp_d16 = sp_d.astype(BF16)
        within_d = _dot(sp_d16, after)
        within_p = _dot(sp_p.astype(BF16), after) + _dot(sp_d16, ones_wide)
        a_d = jnp.where(diag_mask, jnp.exp2((z_d - sp_d) - within_d), 0.0)
        a_p = jnp.exp2((z_p - sp_p) - within_p)
        acc = _dot(a_d.astype(BF16), v_d) + _dot(a_p.astype(BF16), v_p)
        total = (within_p + sp_p)[:, 0:1]
        return jnp.broadcast_to(-total, (SB_T, LANES_V7X)), acc

    def block(qh, j, carry, acc):
        kj, vj = keys(j)
        z = _dot_nt(qh, kj)
        sp = softplus(z)
        sp16 = sp.astype(BF16)
        within = _dot(sp16, after)
        a = jnp.exp2((z - sp) - within + _widen(carry, SB_T))
        acc = acc + _dot(a.astype(BF16), vj)
        carry = carry - _dot(sp16, ones)
        return carry, acc

    chains = []
    for u in range(SB_SUB):
        g = i * SB_SUB + u
        q = q_ref[u * SB_T:(u + 1) * SB_T, :]
        for head in range(HEADS_PER_LANE_TILE):
            qh = jnp.where(_head_select(head), q, 0.0).astype(BF16)
            chains.append((g, qh) + band(qh, g))

    def walk_on():
        outs = []
        for g, qh, carry, acc in chains:
            def cond(state):
                j, carry, _ = state
                return jnp.logical_and(j >= 0, jnp.max(carry) > -ZERO_EXP * LOG2_E)

            def body(state, qh=qh):
                j, carry, acc = state
                carry, acc = block(qh, j, carry, acc)
                return j - 1, carry, acc

            outs.append(lax.while_loop(cond, body, (g - 2, carry, acc))[2])
        return outs

    open_rows = functools.reduce(jnp.maximum, [carry for _, _, carry, _ in chains])
    outs = lax.cond(jnp.max(open_rows) > -ZERO_EXP * LOG2_E, walk_on,
                    lambda: [acc for _, _, _, acc in chains])
    for u in range(SB_SUB):
        pair = outs[u * HEADS_PER_LANE_TILE:(u + 1) * HEADS_PER_LANE_TILE]
        o_ref[u * SB_T:(u + 1) * SB_T, :] = jnp.where(_head_select(0), pair[0], pair[1]).astype(o_ref.dtype)


def _sb_attn(qkv, batch, seq):
    rows = SB_SUB * SB_T
    block_bytes = 2 * (2 * rows * LANES_V7X + 2 * seq * LANES_V7X)
    return pl.pallas_call(
        _sb_kernel,
        out_shape=jax.ShapeDtypeStruct((batch, seq, D_MODEL), BF16),
        grid=(batch, N_HEAD_PAIRS, seq // rows),
        in_specs=[
            pl.BlockSpec((None, rows, LANES_V7X), lambda b, p, i: (b, i, p)),
            pl.BlockSpec((None, seq, LANES_V7X), lambda b, p, i: (b, 0, N_HEAD_PAIRS + p)),
            pl.BlockSpec((None, seq, LANES_V7X), lambda b, p, i: (b, 0, 2 * N_HEAD_PAIRS + p)),
        ],
        out_specs=pl.BlockSpec((None, rows, LANES_V7X), lambda b, p, i: (b, i, p)),
        compiler_params=pltpu.CompilerParams(
            dimension_semantics=("arbitrary", "arbitrary", "arbitrary"),
            vmem_limit_bytes=_vmem_limit(block_bytes)),
        name="sb_attn",
    )(qkv, qkv, qkv)


def _fox_kernel(a_ref, fk0_ref, g_ref, qa_ref, ka_ref, v_ref, o_ref):
    b = pl.program_id(0)
    p_id = pl.program_id(1)
    i = pl.program_id(2)
    n_blocks = fk0_ref.shape[1]
    row = lax.broadcasted_iota(jnp.int32, (TQ, TK), 0)
    col = lax.broadcasted_iota(jnp.int32, (TQ, TK), 1)
    diag_mask = col <= row

    acc0 = jnp.zeros((TQ, LANES_V7X), F32)
    heads = range(HEADS_PER_LANE_TILE)
    bhs = [(b * N_HEAD_PAIRS + p_id) * HEADS_PER_LANE_TILE + head for head in heads]

    def scores(head, j):
        return _dot_nt(qa_ref[head], ka_ref[head, pl.ds(j * TK, TK), :])

    def values(head, j):
        vj = v_ref[pl.ds(j * TK, TK), :]
        return jnp.where(_head_select(head), vj, jnp.ones_like(vj))

    def count_left(head):
        bound = a_ref[bhs[head], i]

        def more(n):
            j = i - 1 - n
            right = fk0_ref[bhs[head], jnp.minimum(j + 1, n_blocks - 1)]
            return jnp.logical_and(j >= 0, bound - right > -ZERO_EXP)

        return lax.while_loop(more, lambda n: n + 1, jnp.int32(0))

    def fast_walks():
        n_left = [count_left(head) for head in heads]
        state = [(jnp.where(diag_mask, jnp.exp(scores(head, i)), 0.0).astype(BF16), acc0)
                 for head in heads]
        for head in heads:
            def step(j, p, acc, head=head):
                p_next = jnp.exp(scores(head, j)).astype(BF16)
                return p_next, acc + _dot(p, values(head, j + 1))

            def two_steps(t, carried, first, step=step):
                p, acc = step(first - 2 * t, *carried)
                return step(first - 2 * t - 1, p, acc)

            odd = n_left[head] % 2
            p0, a0 = state[head]
            carried = lax.cond(odd == 1, lambda: step(i - 1, p0, a0), lambda: (p0, a0))
            state[head] = lax.fori_loop(0, n_left[head] // 2,
                                        functools.partial(two_steps, first=i - 1 - odd), carried)
        return [acc + _dot(p, values(head, i - n_left[head]))
                for head, (p, acc) in zip(heads, state)]

    def online_walks():
        accs = []
        for head in heads:
            def body(t, carried, head=head):
                acc, m = carried
                s = jnp.where(jnp.logical_or(t > 0, diag_mask), scores(head, i - t), NEG_BIG)
                m_new = jnp.maximum(m, jnp.max(s, axis=-1, keepdims=True))
                p = jnp.exp(s - m_new).astype(BF16)
                return jnp.exp(m - m_new) * acc + _dot(p, values(head, i - t)), m_new

            m0 = jnp.full((TQ, 1), NEG_BIG, F32)
            accs.append(lax.fori_loop(0, count_left(head) + 1, body, (acc0, m0))[0])
        return accs

    largest_reach = jnp.maximum(g_ref[bhs[0], i], g_ref[bhs[1], i])
    accs = lax.cond(2.0 * largest_reach < SAFE_EXP_RANGE, fast_walks, online_walks)
    outs = [acc / pltpu.roll(acc, HEAD_DIM, axis=1) for acc in accs]
    o_ref[...] = jnp.where(_head_select(0), outs[0], outs[1]).astype(o_ref.dtype)


def _fox_attn(q_aug, k_aug, v, walk_bound, fk_block_start, max_reach, batch, seq):
    assert TQ == TK
    block_bytes = 2 * (HEADS_PER_LANE_TILE * (TQ + seq) * LANES_V7X + seq * LANES_V7X + TQ * LANES_V7X)
    smem = pl.BlockSpec(memory_space=pltpu.SMEM)
    return pl.pallas_call(
        _fox_kernel,
        out_shape=jax.ShapeDtypeStruct((batch, seq, D_MODEL), BF16),
        grid=(batch, N_HEAD_PAIRS, seq // TQ),
        in_specs=[
            smem, smem, smem,
            pl.BlockSpec((None, HEADS_PER_LANE_TILE, TQ, LANES_V7X), lambda b, p, i: (b, p, i, 0)),
            pl.BlockSpec((None, HEADS_PER_LANE_TILE, seq, LANES_V7X), lambda b, p, i: (b, p, 0, 0)),
            pl.BlockSpec((None, seq, LANES_V7X), lambda b, p, i: (b, 0, p)),
        ],
        out_specs=pl.BlockSpec((None, TQ, LANES_V7X), lambda b, p, i: (b, i, p)),
        compiler_params=pltpu.CompilerParams(
            dimension_semantics=("arbitrary", "arbitrary", "arbitrary"),
            vmem_limit_bytes=_vmem_limit(block_bytes)),
        name="fox_attn",
    )(walk_bound, fk_block_start, max_reach, q_aug, k_aug, v)


def kernel(x, c, ada_w, ada_b, norm_attn_g, norm_ffn_g, w_ffn_in, w_ffn_down, sb_w_qkv, sb_w_o,
           kv_ada_w, kv_ada_b, kv_norm_g, w_kvf, b_f, k_norm_g, fox_w_q, q_norm_g, fox_w_o):
    batch, seq, d = x.shape
    assert d == D_MODEL and seq % TM == 0 and seq % TQ == 0
    assert ada_w.shape[0] == 2 and sb_w_qkv.shape[0] == 1 and fox_w_q.shape[0] == 1
    m = batch * seq

    c_pad = jnp.zeros((SUBLANES_V7X, d), F32).at[:batch].set(c.astype(F32))
    mod = _ada_mod(c_pad, ada_w, ada_b)[:, :batch]
    kv_mod = _ada_mod(c_pad, kv_ada_w[None], kv_ada_b[None])[0, :batch]

    def vecs(t, n):
        return [t[:, None, j * d:(j + 1) * d] for j in range(n)]

    row = lambda t: t.reshape(1, -1).astype(F32)
    bf = lambda t: t.astype(BF16)

    x2d = x.reshape(m, d)

    sh_a, sc_a, g_a, sh_f, sc_f, g_f = vecs(mod[0], 6)
    qkv_scale = jnp.concatenate([jnp.full((1, d), LOG2_E * HEAD_DIM ** -0.5, F32),
                                 jnp.ones((1, 2 * d), F32)], axis=1)
    qkv = _norm_matmul(x2d, row(norm_attn_g[0]), sh_a, sc_a, bf(sb_w_qkv[0]), qkv_scale, seq)
    o = _sb_attn(qkv.reshape(batch, seq, 3 * d), batch, seq)
    x2d, act = _out_ffn_in(o.reshape(m, d), bf(sb_w_o[0]), x2d, g_a,
                           row(norm_ffn_g[0]), sh_f, sc_f, bf(w_ffn_in[0]), seq)
    x2d = _proj_residual(act, bf(w_ffn_down[0]), x2d, g_f, seq)

    sh_a, sc_a, g_a, sh_f, sc_f, g_f = vecs(mod[1], 6)
    kv_shift, kv_scale = vecs(kv_mod, 2)
    w_f_pad = jnp.zeros((d, LANES_V7X), F32).at[:, :N_HEADS].set(w_kvf[:, 2 * d:])
    b_f_pad = jnp.zeros((1, LANES_V7X), F32).at[0, :N_HEADS].set(b_f)
    k, v, q, log_f = _l1_proj(
        x2d, row(kv_norm_g), kv_shift, kv_scale, row(norm_attn_g[1]), sh_a, sc_a,
        bf(w_kvf[:, :d]), bf(w_kvf[:, d:2 * d]), bf(w_f_pad), bf(fox_w_q[0]),
        row(jnp.tile(k_norm_g, N_HEADS)), row(jnp.tile(q_norm_g[0], N_HEADS)), b_f_pad, seq)
    key_norm_bound = (1.01 * HEAD_DIM ** 0.5 * jnp.max(jnp.abs(k_norm_g))).reshape(1, 1).astype(F32)
    q_aug, k_aug, stats = _fox_prep(log_f, q, k, key_norm_bound, batch, seq)
    tables = stats[:, :, :3, :N_HEADS].transpose(2, 0, 3, 1).reshape(3, batch * N_HEADS, seq // TK)
    o = _fox_attn(q_aug, k_aug, v.reshape(batch, seq, d), tables[0], tables[1], tables[2], batch, seq)
    x2d, act = _out_ffn_in(o.reshape(m, d), bf(fox_w_o[0]), x2d, g_a,
                           row(norm_ffn_g[1]), sh_f, sc_f, bf(w_ffn_in[1]), seq)
    x2d = _proj_residual(act, bf(w_ffn_down[1]), x2d, g_f, seq)
    return x2d.reshape(batch, seq, d)
```

```python
import functools

import jax
import jax.numpy as jnp
from jax import lax
from jax.experimental import pallas as pl
from jax.experimental.pallas import tpu as pltpu

D_MODEL = 1024
N_HEADS = 16
HEAD_DIM = D_MODEL // N_HEADS
EPS = 1e-6

LANES_V7X = 128
SUBLANES_V7X = 8
VMEM_BYTES_V7X = 64 * 1024 * 1024

HEADS_PER_LANE_TILE = LANES_V7X // HEAD_DIM
N_HEAD_PAIRS = N_HEADS // HEADS_PER_LANE_TILE

MXU_WIDTH_V7X = 256

TM = 512
COL_CHUNK = 2 * MXU_WIDTH_V7X
TQ = 512
TK = 512
SB_T = 256
SB_SUB = 4

ZERO_EXP = 104.5
SAFE_EXP_RANGE = 80.0
NEG_BIG = -1e30
LOG2_E = 1.4426950408889634

F32 = jnp.float32
BF16 = jnp.bfloat16


def _vmem_limit(block_bytes):
    want = 2 * block_bytes + 16 * 1024 * 1024
    return int(min(want, VMEM_BYTES_V7X - 8 * 1024 * 1024))


def _split2(x):
    hi = x.astype(BF16)
    lo = (x - hi.astype(F32)).astype(BF16)
    return hi, lo


def _dot(a, b):
    return jnp.dot(a, b, preferred_element_type=F32)


def _dot_nt(a, b):
    return lax.dot_general(a, b, (((1,), (1,)), ((), ())), preferred_element_type=F32)


def _ada_kernel(c_ref, w_ref, b_ref, o_ref):
    c = c_ref[...]
    c_act = c * jax.nn.sigmoid(c)
    o_ref[...] = jnp.dot(c_act, w_ref[...], precision=lax.Precision.HIGHEST,
                         preferred_element_type=F32) + b_ref[...]


def _ada_mod(c_pad, w, b):
    n_layers, d, n = w.shape
    tn = 1024
    rows = c_pad.shape[0]
    block_bytes = 4 * (d * tn + rows * d + tn + rows * tn)
    return pl.pallas_call(
        _ada_kernel,
        out_shape=jax.ShapeDtypeStruct((n_layers, rows, n), F32),
        grid=(n_layers, n // tn),
        in_specs=[
            pl.BlockSpec((rows, d), lambda l, j: (0, 0)),
            pl.BlockSpec((None, d, tn), lambda l, j: (l, 0, j)),
            pl.BlockSpec((None, 1, tn), lambda l, j: (l, 0, j)),
        ],
        out_specs=pl.BlockSpec((None, rows, tn), lambda l, j: (l, 0, j)),
        compiler_params=pltpu.CompilerParams(
            dimension_semantics=("arbitrary", "arbitrary"),
            vmem_limit_bytes=_vmem_limit(block_bytes)),
        name="ada_mod",
    )(c_pad, w, b.reshape(n_layers, 1, n))


def _norm_modulate(x, g, shift, scale):
    ms = jnp.mean(x * x, axis=-1, keepdims=True)
    y = x * lax.rsqrt(ms + EPS)
    return (y * g) * (1.0 + scale) + shift


def _row_spec(d):
    return pl.BlockSpec((1, d), lambda *_: (0, 0))


def _batch_vec_spec(d, tiles_per_batch):
    return pl.BlockSpec((None, 1, d), lambda i, *_: (i // tiles_per_batch, 0, 0))


def _column_chunks(n):
    return [(c, min(c + COL_CHUNK, n)) for c in range(0, n, COL_CHUNK)]


def _norm_matmul_kernel(x_ref, g_ref, sh_ref, sc_ref, w_ref, cs_ref, o_ref):
    h = _norm_modulate(x_ref[...], g_ref[...], sh_ref[...], sc_ref[...]).astype(BF16)
    for c0, c1 in _column_chunks(o_ref.shape[1]):
        o_ref[:, c0:c1] = (_dot(h, w_ref[:, c0:c1]) * cs_ref[:, c0:c1]).astype(o_ref.dtype)


def _norm_matmul(x2d, g, shift, scale, w, col_scale, seq):
    m, d = x2d.shape
    n = w.shape[1]
    block_bytes = 4 * TM * d + 2 * d * n + 2 * TM * n + 4 * n
    return pl.pallas_call(
        _norm_matmul_kernel,
        out_shape=jax.ShapeDtypeStruct((m, n), BF16),
        grid=(m // TM,),
        in_specs=[
            pl.BlockSpec((TM, d), lambda i: (i, 0)),
            _row_spec(d),
            _batch_vec_spec(d, seq // TM),
            _batch_vec_spec(d, seq // TM),
            pl.BlockSpec((d, n), lambda i: (0, 0)),
            pl.BlockSpec((1, n), lambda i: (0, 0)),
        ],
        out_specs=pl.BlockSpec((TM, n), lambda i: (i, 0)),
        compiler_params=pltpu.CompilerParams(
            dimension_semantics=("arbitrary",),
            vmem_limit_bytes=_vmem_limit(block_bytes)),
        name="norm_matmul",
    )(x2d, g, shift, scale, w, col_scale)


def _out_ffn_in_kernel(a_ref, wo_ref, x_ref, gate_ref, g_ref, sh_ref, sc_ref, wg_ref, wu_ref,
                       x1_ref, act_ref):
    x1 = x_ref[...] + gate_ref[...] * _dot(a_ref[...], wo_ref[...])
    x1_ref[...] = x1
    h = _norm_modulate(x1, g_ref[...], sh_ref[...], sc_ref[...]).astype(BF16)
    for c0, c1 in _column_chunks(act_ref.shape[1]):
        gate = _dot(h, wg_ref[:, c0:c1])
        up = _dot(h, wu_ref[:, c0:c1])
        act_ref[:, c0:c1] = (gate * jax.nn.sigmoid(gate) * up).astype(act_ref.dtype)


def _out_ffn_in(a, w_o, x2d, gate, g, shift, scale, w_in, seq):
    m, d = x2d.shape
    f = w_in.shape[1] // 2
    tpb = seq // TM
    block_bytes = 2 * TM * d + 2 * d * d + 2 * 4 * TM * d + 2 * 2 * d * f + 2 * TM * f
    return pl.pallas_call(
        _out_ffn_in_kernel,
        out_shape=(jax.ShapeDtypeStruct((m, d), F32), jax.ShapeDtypeStruct((m, f), BF16)),
        grid=(m // TM,),
        in_specs=[
            pl.BlockSpec((TM, d), lambda i: (i, 0)),
            pl.BlockSpec((d, d), lambda i: (0, 0)),
            pl.BlockSpec((TM, d), lambda i: (i, 0)),
            _batch_vec_spec(d, tpb),
            _row_spec(d),
            _batch_vec_spec(d, tpb),
            _batch_vec_spec(d, tpb),
            pl.BlockSpec((d, f), lambda i: (0, 0)),
            pl.BlockSpec((d, f), lambda i: (0, 1)),
        ],
        out_specs=(pl.BlockSpec((TM, d), lambda i: (i, 0)), pl.BlockSpec((TM, f), lambda i: (i, 0))),
        compiler_params=pltpu.CompilerParams(
            dimension_semantics=("arbitrary",),
            vmem_limit_bytes=_vmem_limit(block_bytes)),
        name="out_ffn_in",
    )(a, w_o, x2d, gate, g, shift, scale, w_in, w_in)


def _proj_residual_kernel(a_ref, w_ref, x_ref, gate_ref, o_ref):
    o_ref[...] = x_ref[...] + gate_ref[...] * _dot(a_ref[...], w_ref[...])


def _proj_residual(a, w, x2d, gate, seq):
    m, k = a.shape
    d = w.shape[1]
    block_bytes = 2 * TM * k + 2 * k * d + 2 * 4 * TM * d
    return pl.pallas_call(
        _proj_residual_kernel,
        out_shape=jax.ShapeDtypeStruct((m, d), F32),
        grid=(m // TM,),
        in_specs=[
            pl.BlockSpec((TM, k), lambda i: (i, 0)),
            pl.BlockSpec((k, d), lambda i: (0, 0)),
            pl.BlockSpec((TM, d), lambda i: (i, 0)),
            _batch_vec_spec(d, seq // TM),
        ],
        out_specs=pl.BlockSpec((TM, d), lambda i: (i, 0)),
        compiler_params=pltpu.CompilerParams(
            dimension_semantics=("arbitrary",),
            vmem_limit_bytes=_vmem_limit(block_bytes)),
        name="proj_residual",
    )(a, w, x2d, gate)


def _head_rms_norm(t, gain, group_mean):
    ms = _dot((t * t).astype(BF16), group_mean)
    return t * lax.rsqrt(ms + EPS) * gain


def _l1_proj_kernel(x_ref, gkv_ref, shkv_ref, sckv_ref, gq_ref, shq_ref, scq_ref,
                    wk_ref, wv_ref, wf_ref, wq_ref, kg_ref, qg_ref, bf_ref, gm_ref,
                    k_ref, v_ref, q_ref, lf_ref):
    x = x_ref[...]
    ms = jnp.mean(x * x, axis=-1, keepdims=True)
    y = x * lax.rsqrt(ms + EPS)
    h_kv = ((y * gkv_ref[...]) * (1.0 + sckv_ref[...]) + shkv_ref[...]).astype(BF16)
    h_q = ((y * gq_ref[...]) * (1.0 + scq_ref[...]) + shq_ref[...]).astype(BF16)
    gm = gm_ref[...]
    k_all = _dot(h_kv, wk_ref[...])
    q_all = _dot(h_q, wq_ref[...])
    v_ref[...] = _dot(h_kv, wv_ref[...]).astype(BF16)
    lf_ref[...] = jax.nn.log_sigmoid(_dot(h_kv, wf_ref[...]) + bf_ref[...])
    for c0 in range(0, k_ref.shape[1], MXU_WIDTH_V7X):
        cols = slice(c0, c0 + MXU_WIDTH_V7X)
        k_ref[:, cols] = _head_rms_norm(k_all[:, cols], kg_ref[:, cols], gm).astype(BF16)
        q = _head_rms_norm(q_all[:, cols], qg_ref[:, cols], gm)
        q_ref[:, cols] = (q * (HEAD_DIM ** -0.5)).astype(BF16)


def _l1_proj(x2d, g_kv, sh_kv, sc_kv, g_q, sh_q, sc_q, wk, wv, wf, wq, k_gain, q_gain, b_f, seq):
    m, d = x2d.shape
    lane = jnp.arange(MXU_WIDTH_V7X)
    group_mean = jnp.where(lane[:, None] // HEAD_DIM == lane[None, :] // HEAD_DIM,
                           1.0 / HEAD_DIM, 0.0).astype(BF16)
    tpb = seq // TM
    block_bytes = 4 * TM * d + 3 * 2 * d * d + 2 * d * LANES_V7X + 3 * 2 * TM * d + 4 * TM * LANES_V7X
    full = lambda r, c: pl.BlockSpec((r, c), lambda i: (0, 0))
    return pl.pallas_call(
        _l1_proj_kernel,
        out_shape=(jax.ShapeDtypeStruct((m, d), BF16), jax.ShapeDtypeStruct((m, d), BF16),
                   jax.ShapeDtypeStruct((m, d), BF16), jax.ShapeDtypeStruct((m, LANES_V7X), F32)),
        grid=(m // TM,),
        in_specs=[
            pl.BlockSpec((TM, d), lambda i: (i, 0)),
            _row_spec(d), _batch_vec_spec(d, tpb), _batch_vec_spec(d, tpb),
            _row_spec(d), _batch_vec_spec(d, tpb), _batch_vec_spec(d, tpb),
            full(d, d), full(d, d), full(d, LANES_V7X), full(d, d),
            _row_spec(d), _row_spec(d), _row_spec(LANES_V7X), full(MXU_WIDTH_V7X, MXU_WIDTH_V7X),
        ],
        out_specs=(pl.BlockSpec((TM, d), lambda i: (i, 0)), pl.BlockSpec((TM, d), lambda i: (i, 0)),
                   pl.BlockSpec((TM, d), lambda i: (i, 0)), pl.BlockSpec((TM, LANES_V7X), lambda i: (i, 0))),
        compiler_params=pltpu.CompilerParams(
            dimension_semantics=("arbitrary",),
            vmem_limit_bytes=_vmem_limit(block_bytes)),
        name="l1_proj",
    )(x2d, g_kv, sh_kv, sc_kv, g_q, sh_q, sc_q, wk, wv, wf, wq, k_gain, q_gain, b_f, group_mean)


N_SPLIT = 3


def _split3(x):
    x0 = x.astype(BF16)
    r1 = x - x0.astype(F32)
    x1 = r1.astype(BF16)
    x2 = (r1 - x1.astype(F32)).astype(BF16)
    return x0, x1, x2


def _fox_prep_kernel(kb_ref, lf_ref, q_ref, k_ref, qa_ref, ka_ref, stats_ref, carry_scr):
    @pl.when(pl.program_id(1) == 0)
    def _():
        carry_scr[...] = jnp.zeros_like(carry_scr)

    r = lax.broadcasted_iota(jnp.int32, (TK, TK), 0)
    c = lax.broadcasted_iota(jnp.int32, (TK, TK), 1)
    tri = jnp.where(c <= r, 1.0, 0.0).astype(BF16)
    x0, x1, x2 = _split3(lf_ref[...])
    cs = _dot(tri, x0) + _dot(tri, x1) + _dot(tri, x2) + carry_scr[0:1, :]
    carry_scr[...] = jnp.broadcast_to(cs[TK - 1:TK, :], carry_scr.shape)

    d_idx = lax.broadcasted_iota(jnp.int32, (D_MODEL, LANES_V7X), 0)
    h_idx = lax.broadcasted_iota(jnp.int32, (D_MODEL, LANES_V7X), 1)
    head_of = jnp.where(d_idx // HEAD_DIM == h_idx, 1.0, 0.0).astype(BF16)
    qf = q_ref[...].astype(F32)
    reach = jnp.sqrt(_dot((qf * qf).astype(BF16), head_of)) * kb_ref[0, 0]
    own_logit = _dot((qf * k_ref[...].astype(F32)).astype(BF16), head_of)

    stats_ref[...] = jnp.concatenate([
        jnp.max(reach * (1.0 + 2.0 ** -8) - own_logit + cs, axis=0, keepdims=True),
        cs[0:1, :],
        jnp.max(reach, axis=0, keepdims=True),
        jnp.zeros((SUBLANES_V7X - 3, LANES_V7X), F32)], axis=0)

    lane = lax.broadcasted_iota(jnp.int32, (1, LANES_V7X), 1)
    packed = jnp.zeros((TK, LANES_V7X), F32)
    for i, term in enumerate(_split3(cs - reach) + _split3(-cs)):
        term = jnp.where(lane < N_HEADS, term.astype(F32), 0.0)
        packed = packed + (term if i == 0 else pltpu.roll(term, N_HEADS * i, axis=1))
    packed = packed.astype(BF16)

    src = lax.broadcasted_iota(jnp.int32, (LANES_V7X, 2 * LANES_V7X), 0)
    dst = lax.broadcasted_iota(jnp.int32, (LANES_V7X, 2 * LANES_V7X), 1)
    for h in range(N_HEADS):
        pair, parity = divmod(h, HEADS_PER_LANE_TILE)
        spare = HEAD_DIM * (1 - parity)
        place = jnp.where((src % N_HEADS == h) & (src < 2 * N_SPLIT * N_HEADS)
                          & (dst % LANES_V7X == spare + src // N_HEADS)
                          & ((dst >= LANES_V7X) == (src >= N_SPLIT * N_HEADS)), 1.0, 0.0)
        placed = _dot(packed, place.astype(BF16))
        ones_q = jnp.where((lane >= spare + N_SPLIT) & (lane < spare + 2 * N_SPLIT), 1.0, 0.0)
        ones_k = jnp.where((lane >= spare) & (lane < spare + N_SPLIT), 1.0, 0.0)
        aug_q = (placed[:, :LANES_V7X] + ones_q).astype(BF16)
        aug_k = (placed[:, LANES_V7X:] + ones_k).astype(BF16)
        own = (lane // HEAD_DIM) == parity
        cols = slice(pair * LANES_V7X, (pair + 1) * LANES_V7X)
        qa_ref[h] = jnp.where(own, q_ref[:, cols], aug_q)
        ka_ref[h] = jnp.where(own, k_ref[:, cols], aug_k)


def _fox_prep(log_f, q, k, key_norm_bound, batch, seq):
    assert TQ == TK
    d = q.shape[1]
    block_bytes = (4 * TK * LANES_V7X + 2 * 2 * TK * d + 2 * 2 * N_HEADS * TK * LANES_V7X
                   + 4 * SUBLANES_V7X * LANES_V7X)
    aug_shape = jax.ShapeDtypeStruct((batch, N_HEADS, seq, LANES_V7X), BF16)
    aug_spec = pl.BlockSpec((None, N_HEADS, TK, LANES_V7X), lambda b, t: (b, 0, t, 0))
    return pl.pallas_call(
        _fox_prep_kernel,
        out_shape=(aug_shape, aug_shape,
                   jax.ShapeDtypeStruct((batch, seq // TK, SUBLANES_V7X, LANES_V7X), F32)),
        grid=(batch, seq // TK),
        in_specs=[
            pl.BlockSpec(memory_space=pltpu.SMEM),
            pl.BlockSpec((None, TK, LANES_V7X), lambda b, t: (b, t, 0)),
            pl.BlockSpec((None, TK, d), lambda b, t: (b, t, 0)),
            pl.BlockSpec((None, TK, d), lambda b, t: (b, t, 0)),
        ],
        out_specs=(aug_spec, aug_spec,
                   pl.BlockSpec((None, None, SUBLANES_V7X, LANES_V7X), lambda b, t: (b, t, 0, 0))),
        scratch_shapes=[pltpu.VMEM((SUBLANES_V7X, LANES_V7X), F32)],
        compiler_params=pltpu.CompilerParams(
            dimension_semantics=("arbitrary", "arbitrary"),
            vmem_limit_bytes=_vmem_limit(block_bytes)),
        name="fox_prep",
    )(key_norm_bound, log_f.reshape(batch, seq, LANES_V7X), q.reshape(batch, seq, d),
      k.reshape(batch, seq, d))


def _head_select(head):
    lane = lax.broadcasted_iota(jnp.int32, (1, LANES_V7X), 1)
    return (lane // HEAD_DIM) == head


def _widen(col128, width):
    reps = width // LANES_V7X
    return col128 if reps == 1 else jnp.concatenate([col128] * reps, axis=1)


def _sb_kernel(q_ref, k_ref, v_ref, o_ref):
    i = pl.program_id(2)
    row = lax.broadcasted_iota(jnp.int32, (SB_T, SB_T), 0)
    col = lax.broadcasted_iota(jnp.int32, (SB_T, SB_T), 1)
    diag_mask = col < row
    after = jnp.where(row > col, 1.0, 0.0).astype(BF16)
    ones = jnp.ones((SB_T, LANES_V7X), BF16)
    ones_wide = jnp.ones((SB_T, SB_T), BF16)

    def softplus(z):
        return jnp.maximum(z, 0.0) + jnp.log2(1.0 + jnp.exp2(-jnp.abs(z)))

    def keys(j):
        return k_ref[pl.ds(j * SB_T, SB_T), :], v_ref[pl.ds(j * SB_T, SB_T), :]

    def band(qh, g):
        k_d, v_d = keys(g)
        k_p, v_p = keys(jnp.maximum(g - 1, 0))
        z_d = _dot_nt(qh, k_d)
        z_p = jnp.where(g > 0, _dot_nt(qh, k_p), NEG_BIG)
        sp_d = jnp.where(diag_mask, softplus(z_d), 0.0)
        sp_p = softplus(z_p)
        sp_d16 = sp_d.astype(BF16)
        within_d = _dot(sp_d16, after)
        within_p = _dot(sp_p.astype(BF16), after) + _dot(sp_d16, ones_wide)
        a_d = jnp.where(diag_mask, jnp.exp2((z_d - sp_d) - within_d), 0.0)
        a_p = jnp.exp2((z_p - sp_p) - within_p)
        acc = _dot(a_d.astype(BF16), v_d) + _dot(a_p.astype(BF16), v_p)
        total = (within_p + sp_p)[:, 0:1]
        return jnp.broadcast_to(-total, (SB_T, LANES_V7X)), acc

    def block(qh, j, carry, acc):
        kj, vj = keys(j)
        z = _dot_nt(qh, kj)
        sp = softplus(z)
        sp16 = sp.astype(BF16)
        within = _dot(sp16, after)
        a = jnp.exp2((z - sp) - within + _widen(carry, SB_T))
        acc = acc + _dot(a.astype(BF16), vj)
        carry = carry - _dot(sp16, ones)
        return carry, acc

    chains = []
    for u in range(SB_SUB):
        g = i * SB_SUB + u
        q = q_ref[u * SB_T:(u + 1) * SB_T, :]
        for head in range(HEADS_PER_LANE_TILE):
            qh = jnp.where(_head_select(head), q, 0.0).astype(BF16)
            chains.append((g, qh) + band(qh, g))

    def is_open(carry):
        return jnp.max(carry) > -ZERO_EXP * LOG2_E

    chain_open = [is_open(carry) for _, _, carry, _ in chains]

    def walk_on():
        outs = []
        for (g, qh, carry, acc), this_open in zip(chains, chain_open):
            def cond(state):
                j, carry, _ = state
                return jnp.logical_and(j >= 0, is_open(carry))

            def body(state, qh=qh):
                j, carry, acc = state
                carry, acc = block(qh, j, carry, acc)
                return j - 1, carry, acc

            def walk(cond=cond, body=body, start=(g - 2, carry, acc)):
                return lax.while_loop(cond, body, start)[2]

            outs.append(lax.cond(this_open, walk, lambda acc=acc: acc))
        return outs

    outs = lax.cond(functools.reduce(jnp.logical_or, chain_open), walk_on,
                    lambda: [acc for _, _, _, acc in chains])
    for u in range(SB_SUB):
        pair = outs[u * HEADS_PER_LANE_TILE:(u + 1) * HEADS_PER_LANE_TILE]
        o_ref[u * SB_T:(u + 1) * SB_T, :] = jnp.where(_head_select(0), pair[0], pair[1]).astype(o_ref.dtype)


def _sb_attn(qkv, batch, seq):
    rows = SB_SUB * SB_T
    block_bytes = 2 * (2 * rows * LANES_V7X + 2 * seq * LANES_V7X)
    return pl.pallas_call(
        _sb_kernel,
        out_shape=jax.ShapeDtypeStruct((batch, seq, D_MODEL), BF16),
        grid=(batch, N_HEAD_PAIRS, seq // rows),
        in_specs=[
            pl.BlockSpec((None, rows, LANES_V7X), lambda b, p, i: (b, i, p)),
            pl.BlockSpec((None, seq, LANES_V7X), lambda b, p, i: (b, 0, N_HEAD_PAIRS + p)),
            pl.BlockSpec((None, seq, LANES_V7X), lambda b, p, i: (b, 0, 2 * N_HEAD_PAIRS + p)),
        ],
        out_specs=pl.BlockSpec((None, rows, LANES_V7X), lambda b, p, i: (b, i, p)),
        compiler_params=pltpu.CompilerParams(
            dimension_semantics=("arbitrary", "arbitrary", "arbitrary"),
            vmem_limit_bytes=_vmem_limit(block_bytes)),
        name="sb_attn",
    )(qkv, qkv, qkv)


def _fox_kernel(a_ref, fk0_ref, g_ref, qa_ref, ka_ref, v_ref, o_ref):
    b = pl.program_id(0)
    p_id = pl.program_id(1)
    i = pl.program_id(2)
    n_blocks = fk0_ref.shape[1]
    row = lax.broadcasted_iota(jnp.int32, (TQ, TK), 0)
    col = lax.broadcasted_iota(jnp.int32, (TQ, TK), 1)
    diag_mask = col <= row

    acc0 = jnp.zeros((TQ, LANES_V7X), F32)
    heads = range(HEADS_PER_LANE_TILE)
    bhs = [(b * N_HEAD_PAIRS + p_id) * HEADS_PER_LANE_TILE + head for head in heads]

    def scores(head, j):
        return _dot_nt(qa_ref[head], ka_ref[head, pl.ds(j * TK, TK), :])

    def values(head, j):
        vj = v_ref[pl.ds(j * TK, TK), :]
        return jnp.where(_head_select(head), vj, jnp.ones_like(vj))

    def count_left(head):
        bound = a_ref[bhs[head], i]

        def more(n):
            j = i - 1 - n
            right = fk0_ref[bhs[head], jnp.minimum(j + 1, n_blocks - 1)]
            return jnp.logical_and(j >= 0, bound - right > -ZERO_EXP)

        return lax.while_loop(more, lambda n: n + 1, jnp.int32(0))

    def fast_walks():
        n_left = [count_left(head) for head in heads]
        state = [(jnp.where(diag_mask, jnp.exp(scores(head, i)), 0.0).astype(BF16), acc0)
                 for head in heads]
        for head in heads:
            def step(j, p, acc, head=head):
                p_next = jnp.exp(scores(head, j)).astype(BF16)
                return p_next, acc + _dot(p, values(head, j + 1))

            def two_steps(t, carried, first, step=step):
                p, acc = step(first - 2 * t, *carried)
                return step(first - 2 * t - 1, p, acc)

            odd = n_left[head] % 2
            p0, a0 = state[head]
            carried = lax.cond(odd == 1, lambda: step(i - 1, p0, a0), lambda: (p0, a0))
            state[head] = lax.fori_loop(0, n_left[head] // 2,
                                        functools.partial(two_steps, first=i - 1 - odd), carried)
        return [acc + _dot(p, values(head, i - n_left[head]))
                for head, (p, acc) in zip(heads, state)]

    def online_walks():
        accs = []
        for head in heads:
            def body(t, carried, head=head):
                acc, m = carried
                s = jnp.where(jnp.logical_or(t > 0, diag_mask), scores(head, i - t), NEG_BIG)
                m_new = jnp.maximum(m, jnp.max(s, axis=-1, keepdims=True))
                p = jnp.exp(s - m_new).astype(BF16)
                return jnp.exp(m - m_new) * acc + _dot(p, values(head, i - t)), m_new

            m0 = jnp.full((TQ, 1), NEG_BIG, F32)
            accs.append(lax.fori_loop(0, count_left(head) + 1, body, (acc0, m0))[0])
        return accs

    largest_reach = jnp.maximum(g_ref[bhs[0], i], g_ref[bhs[1], i])
    accs = lax.cond(2.0 * largest_reach < SAFE_EXP_RANGE, fast_walks, online_walks)
    outs = [acc / pltpu.roll(acc, HEAD_DIM, axis=1) for acc in accs]
    o_ref[...] = jnp.where(_head_select(0), outs[0], outs[1]).astype(o_ref.dtype)


def _fox_attn(q_aug, k_aug, v, walk_bound, fk_block_start, max_reach, batch, seq):
    assert TQ == TK
    block_bytes = 2 * (HEADS_PER_LANE_TILE * (TQ + seq) * LANES_V7X + seq * LANES_V7X + TQ * LANES_V7X)
    smem = pl.BlockSpec(memory_space=pltpu.SMEM)
    return pl.pallas_call(
        _fox_kernel,
        out_shape=jax.ShapeDtypeStruct((batch, seq, D_MODEL), BF16),
        grid=(batch, N_HEAD_PAIRS, seq // TQ),
        in_specs=[
            smem, smem, smem,
            pl.BlockSpec((None, HEADS_PER_LANE_TILE, TQ, LANES_V7X), lambda b, p, i: (b, p, i, 0)),
            pl.BlockSpec((None, HEADS_PER_LANE_TILE, seq, LANES_V7X), lambda b, p, i: (b, p, 0, 0)),
            pl.BlockSpec((None, seq, LANES_V7X), lambda b, p, i: (b, 0, p)),
        ],
        out_specs=pl.BlockSpec((None, TQ, LANES_V7X), lambda b, p, i: (b, i, p)),
        compiler_params=pltpu.CompilerParams(
            dimension_semantics=("arbitrary", "arbitrary", "arbitrary"),
            vmem_limit_bytes=_vmem_limit(block_bytes)),
        name="fox_attn",
    )(walk_bound, fk_block_start, max_reach, q_aug, k_aug, v)


def kernel(x, c, ada_w, ada_b, norm_attn_g, norm_ffn_g, w_ffn_in, w_ffn_down, sb_w_qkv, sb_w_o,
           kv_ada_w, kv_ada_b, kv_norm_g, w_kvf, b_f, k_norm_g, fox_w_q, q_norm_g, fox_w_o):
    batch, seq, d = x.shape
    assert d == D_MODEL and seq % TM == 0 and seq % TQ == 0
    assert ada_w.shape[0] == 2 and sb_w_qkv.shape[0] == 1 and fox_w_q.shape[0] == 1
    m = batch * seq

    c_pad = jnp.zeros((SUBLANES_V7X, d), F32).at[:batch].set(c.astype(F32))
    mod = _ada_mod(c_pad, ada_w, ada_b)[:, :batch]
    kv_mod = _ada_mod(c_pad, kv_ada_w[None], kv_ada_b[None])[0, :batch]

    def vecs(t, n):
        return [t[:, None, j * d:(j + 1) * d] for j in range(n)]

    row = lambda t: t.reshape(1, -1).astype(F32)
    bf = lambda t: t.astype(BF16)

    x2d = x.reshape(m, d)

    sh_a, sc_a, g_a, sh_f, sc_f, g_f = vecs(mod[0], 6)
    qkv_scale = jnp.concatenate([jnp.full((1, d), LOG2_E * HEAD_DIM ** -0.5, F32),
                                 jnp.ones((1, 2 * d), F32)], axis=1)
    qkv = _norm_matmul(x2d, row(norm_attn_g[0]), sh_a, sc_a, bf(sb_w_qkv[0]), qkv_scale, seq)
    o = _sb_attn(qkv.reshape(batch, seq, 3 * d), batch, seq)
    x2d, act = _out_ffn_in(o.reshape(m, d), bf(sb_w_o[0]), x2d, g_a,
                           row(norm_ffn_g[0]), sh_f, sc_f, bf(w_ffn_in[0]), seq)
    x2d = _proj_residual(act, bf(w_ffn_down[0]), x2d, g_f, seq)

    sh_a, sc_a, g_a, sh_f, sc_f, g_f = vecs(mod[1], 6)
    kv_shift, kv_scale = vecs(kv_mod, 2)
    w_f_pad = jnp.zeros((d, LANES_V7X), F32).at[:, :N_HEADS].set(w_kvf[:, 2 * d:])
    b_f_pad = jnp.zeros((1, LANES_V7X), F32).at[0, :N_HEADS].set(b_f)
    k, v, q, log_f = _l1_proj(
        x2d, row(kv_norm_g), kv_shift, kv_scale, row(norm_attn_g[1]), sh_a, sc_a,
        bf(w_kvf[:, :d]), bf(w_kvf[:, d:2 * d]), bf(w_f_pad), bf(fox_w_q[0]),
        row(jnp.tile(k_norm_g, N_HEADS)), row(jnp.tile(q_norm_g[0], N_HEADS)), b_f_pad, seq)
    key_norm_bound = (1.01 * HEAD_DIM ** 0.5 * jnp.max(jnp.abs(k_norm_g))).reshape(1, 1).astype(F32)
    q_aug, k_aug, stats = _fox_prep(log_f, q, k, key_norm_bound, batch, seq)
    tables = stats[:, :, :3, :N_HEADS].transpose(2, 0, 3, 1).reshape(3, batch * N_HEADS, seq // TK)
    o = _fox_attn(q_aug, k_aug, v.reshape(batch, seq, d), tables[0], tables[1], tables[2], batch, seq)
    x2d, act = _out_ffn_in(o.reshape(m, d), bf(fox_w_o[0]), x2d, g_a,
                           row(norm_ffn_g[1]), sh_f, sc_f, bf(w_ffn_in[1]), seq)
    x2d = _proj_residual(act, bf(w_ffn_down[1]), x2d, g_f, seq)
    return x2d.reshape(batch, seq, d)
```

```python
import functools

import jax
import jax.numpy as jnp
from jax import lax
from jax.experimental import pallas as pl
from jax.experimental.pallas import tpu as pltpu

D_MODEL = 1024
N_HEADS = 16
HEAD_DIM = D_MODEL // N_HEADS
EPS = 1e-6

LANES_V7X = 128
SUBLANES_V7X = 8
VMEM_BYTES_V7X = 64 * 1024 * 1024

HEADS_PER_LANE_TILE = LANES_V7X // HEAD_DIM
N_HEAD_PAIRS = N_HEADS // HEADS_PER_LANE_TILE

MXU_WIDTH_V7X = 256

TM = 512
COL_CHUNK = 2 * MXU_WIDTH_V7X
TQ = 512
TK = 512
SB_T = 256
SB_SUB = 4

ZERO_EXP = 104.5
SAFE_EXP_RANGE = 80.0
NEG_BIG = -1e30
LOG2_E = 1.4426950408889634

F32 = jnp.float32
BF16 = jnp.bfloat16


def _vmem_limit(block_bytes):
    want = 2 * block_bytes + 16 * 1024 * 1024
    return int(min(want, VMEM_BYTES_V7X - 8 * 1024 * 1024))


def _split2(x):
    hi = x.astype(BF16)
    lo = (x - hi.astype(F32)).astype(BF16)
    return hi, lo


def _dot(a, b):
    return jnp.dot(a, b, preferred_element_type=F32)


def _dot_nt(a, b):
    return lax.dot_general(a, b, (((1,), (1,)), ((), ())), preferred_element_type=F32)


def _ada_kernel(c_ref, w_ref, b_ref, o_ref):
    c = c_ref[...]
    c_act = c * jax.nn.sigmoid(c)
    o_ref[...] = jnp.dot(c_act, w_ref[...], precision=lax.Precision.HIGHEST,
                         preferred_element_type=F32) + b_ref[...]


def _ada_mod(c_pad, w, b):
    n_layers, d, n = w.shape
    tn = 1024
    rows = c_pad.shape[0]
    block_bytes = 4 * (d * tn + rows * d + tn + rows * tn)
    return pl.pallas_call(
        _ada_kernel,
        out_shape=jax.ShapeDtypeStruct((n_layers, rows, n), F32),
        grid=(n_layers, n // tn),
        in_specs=[
            pl.BlockSpec((rows, d), lambda l, j: (0, 0)),
            pl.BlockSpec((None, d, tn), lambda l, j: (l, 0, j)),
            pl.BlockSpec((None, 1, tn), lambda l, j: (l, 0, j)),
        ],
        out_specs=pl.BlockSpec((None, rows, tn), lambda l, j: (l, 0, j)),
        compiler_params=pltpu.CompilerParams(
            dimension_semantics=("arbitrary", "arbitrary"),
            vmem_limit_bytes=_vmem_limit(block_bytes)),
        name="ada_mod",
    )(c_pad, w, b.reshape(n_layers, 1, n))


def _norm_modulate(x, g, shift, scale):
    ms = jnp.mean(x * x, axis=-1, keepdims=True)
    y = x * lax.rsqrt(ms + EPS)
    return (y * g) * (1.0 + scale) + shift


def _row_spec(d):
    return pl.BlockSpec((1, d), lambda *_: (0, 0))


def _batch_vec_spec(d, tiles_per_batch):
    return pl.BlockSpec((None, 1, d), lambda i, *_: (i // tiles_per_batch, 0, 0))


def _column_chunks(n):
    return [(c, min(c + COL_CHUNK, n)) for c in range(0, n, COL_CHUNK)]


def _norm_matmul_kernel(x_ref, g_ref, sh_ref, sc_ref, w_ref, cs_ref, o_ref):
    h = _norm_modulate(x_ref[...], g_ref[...], sh_ref[...], sc_ref[...]).astype(BF16)
    for c0, c1 in _column_chunks(o_ref.shape[1]):
        o_ref[:, c0:c1] = (_dot(h, w_ref[:, c0:c1]) * cs_ref[:, c0:c1]).astype(o_ref.dtype)


def _norm_matmul(x2d, g, shift, scale, w, col_scale, seq):
    m, d = x2d.shape
    n = w.shape[1]
    block_bytes = 4 * TM * d + 2 * d * n + 2 * TM * n + 4 * n
    return pl.pallas_call(
        _norm_matmul_kernel,
        out_shape=jax.ShapeDtypeStruct((m, n), BF16),
        grid=(m // TM,),
        in_specs=[
            pl.BlockSpec((TM, d), lambda i: (i, 0)),
            _row_spec(d),
            _batch_vec_spec(d, seq // TM),
            _batch_vec_spec(d, seq // TM),
            pl.BlockSpec((d, n), lambda i: (0, 0)),
            pl.BlockSpec((1, n), lambda i: (0, 0)),
        ],
        out_specs=pl.BlockSpec((TM, n), lambda i: (i, 0)),
        compiler_params=pltpu.CompilerParams(
            dimension_semantics=("arbitrary",),
            vmem_limit_bytes=_vmem_limit(block_bytes)),
        name="norm_matmul",
    )(x2d, g, shift, scale, w, col_scale)


def _out_ffn_in_kernel(a_ref, wo_ref, x_ref, gate_ref, g_ref, sh_ref, sc_ref, wg_ref, wu_ref,
                       x1_ref, act_ref):
    x1 = x_ref[...] + gate_ref[...] * _dot(a_ref[...], wo_ref[...])
    x1_ref[...] = x1
    h = _norm_modulate(x1, g_ref[...], sh_ref[...], sc_ref[...]).astype(BF16)
    for c0, c1 in _column_chunks(act_ref.shape[1]):
        gate = _dot(h, wg_ref[:, c0:c1])
        up = _dot(h, wu_ref[:, c0:c1])
        act_ref[:, c0:c1] = (gate * jax.nn.sigmoid(gate) * up).astype(act_ref.dtype)


def _out_ffn_in(a, w_o, x2d, gate, g, shift, scale, w_in, seq):
    m, d = x2d.shape
    f = w_in.shape[1] // 2
    tpb = seq // TM
    block_bytes = 2 * TM * d + 2 * d * d + 2 * 4 * TM * d + 2 * 2 * d * f + 2 * TM * f
    return pl.pallas_call(
        _out_ffn_in_kernel,
        out_shape=(jax.ShapeDtypeStruct((m, d), F32), jax.ShapeDtypeStruct((m, f), BF16)),
        grid=(m // TM,),
        in_specs=[
            pl.BlockSpec((TM, d), lambda i: (i, 0)),
            pl.BlockSpec((d, d), lambda i: (0, 0)),
            pl.BlockSpec((TM, d), lambda i: (i, 0)),
            _batch_vec_spec(d, tpb),
            _row_spec(d),
            _batch_vec_spec(d, tpb),
            _batch_vec_spec(d, tpb),
            pl.BlockSpec((d, f), lambda i: (0, 0)),
            pl.BlockSpec((d, f), lambda i: (0, 1)),
        ],
        out_specs=(pl.BlockSpec((TM, d), lambda i: (i, 0)), pl.BlockSpec((TM, f), lambda i: (i, 0))),
        compiler_params=pltpu.CompilerParams(
            dimension_semantics=("arbitrary",),
            vmem_limit_bytes=_vmem_limit(block_bytes)),
        name="out_ffn_in",
    )(a, w_o, x2d, gate, g, shift, scale, w_in, w_in)


def _proj_residual_kernel(a_ref, w_ref, x_ref, gate_ref, o_ref):
    o_ref[...] = x_ref[...] + gate_ref[...] * _dot(a_ref[...], w_ref[...])


def _proj_residual(a, w, x2d, gate, seq):
    m, k = a.shape
    d = w.shape[1]
    block_bytes = 2 * TM * k + 2 * k * d + 2 * 4 * TM * d
    return pl.pallas_call(
        _proj_residual_kernel,
        out_shape=jax.ShapeDtypeStruct((m, d), F32),
        grid=(m // TM,),
        in_specs=[
            pl.BlockSpec((TM, k), lambda i: (i, 0)),
            pl.BlockSpec((k, d), lambda i: (0, 0)),
            pl.BlockSpec((TM, d), lambda i: (i, 0)),
            _batch_vec_spec(d, seq // TM),
        ],
        out_specs=pl.BlockSpec((TM, d), lambda i: (i, 0)),
        compiler_params=pltpu.CompilerParams(
            dimension_semantics=("arbitrary",),
            vmem_limit_bytes=_vmem_limit(block_bytes)),
        name="proj_residual",
    )(a, w, x2d, gate)


def _head_rms_norm(t, gain, group_mean):
    ms = _dot((t * t).astype(BF16), group_mean)
    return t * lax.rsqrt(ms + EPS) * gain


def _l1_proj_kernel(x_ref, gkv_ref, shkv_ref, sckv_ref, gq_ref, shq_ref, scq_ref,
                    wk_ref, wv_ref, wf_ref, wq_ref, kg_ref, qg_ref, bf_ref, gm_ref,
                    k_ref, v_ref, q_ref, lf_ref):
    x = x_ref[...]
    ms = jnp.mean(x * x, axis=-1, keepdims=True)
    y = x * lax.rsqrt(ms + EPS)
    h_kv = ((y * gkv_ref[...]) * (1.0 + sckv_ref[...]) + shkv_ref[...]).astype(BF16)
    h_q = ((y * gq_ref[...]) * (1.0 + scq_ref[...]) + shq_ref[...]).astype(BF16)
    gm = gm_ref[...]
    k_all = _dot(h_kv, wk_ref[...])
    q_all = _dot(h_q, wq_ref[...])
    v_ref[...] = _dot(h_kv, wv_ref[...]).astype(BF16)
    lf_ref[...] = jax.nn.log_sigmoid(_dot(h_kv, wf_ref[...]) + bf_ref[...])
    for c0 in range(0, k_ref.shape[1], MXU_WIDTH_V7X):
        cols = slice(c0, c0 + MXU_WIDTH_V7X)
        k_ref[:, cols] = _head_rms_norm(k_all[:, cols], kg_ref[:, cols], gm).astype(BF16)
        q = _head_rms_norm(q_all[:, cols], qg_ref[:, cols], gm)
        q_ref[:, cols] = (q * (HEAD_DIM ** -0.5)).astype(BF16)


def _l1_proj(x2d, g_kv, sh_kv, sc_kv, g_q, sh_q, sc_q, wk, wv, wf, wq, k_gain, q_gain, b_f, seq):
    m, d = x2d.shape
    lane = jnp.arange(MXU_WIDTH_V7X)
    group_mean = jnp.where(lane[:, None] // HEAD_DIM == lane[None, :] // HEAD_DIM,
                           1.0 / HEAD_DIM, 0.0).astype(BF16)
    tpb = seq // TM
    block_bytes = 4 * TM * d + 3 * 2 * d * d + 2 * d * LANES_V7X + 3 * 2 * TM * d + 4 * TM * LANES_V7X
    full = lambda r, c: pl.BlockSpec((r, c), lambda i: (0, 0))
    return pl.pallas_call(
        _l1_proj_kernel,
        out_shape=(jax.ShapeDtypeStruct((m, d), BF16), jax.ShapeDtypeStruct((m, d), BF16),
                   jax.ShapeDtypeStruct((m, d), BF16), jax.ShapeDtypeStruct((m, LANES_V7X), F32)),
        grid=(m // TM,),
        in_specs=[
            pl.BlockSpec((TM, d), lambda i: (i, 0)),
            _row_spec(d), _batch_vec_spec(d, tpb), _batch_vec_spec(d, tpb),
            _row_spec(d), _batch_vec_spec(d, tpb), _batch_vec_spec(d, tpb),
            full(d, d), full(d, d), full(d, LANES_V7X), full(d, d),
            _row_spec(d), _row_spec(d), _row_spec(LANES_V7X), full(MXU_WIDTH_V7X, MXU_WIDTH_V7X),
        ],
        out_specs=(pl.BlockSpec((TM, d), lambda i: (i, 0)), pl.BlockSpec((TM, d), lambda i: (i, 0)),
                   pl.BlockSpec((TM, d), lambda i: (i, 0)), pl.BlockSpec((TM, LANES_V7X), lambda i: (i, 0))),
        compiler_params=pltpu.CompilerParams(
            dimension_semantics=("arbitrary",),
            vmem_limit_bytes=_vmem_limit(block_bytes)),
        name="l1_proj",
    )(x2d, g_kv, sh_kv, sc_kv, g_q, sh_q, sc_q, wk, wv, wf, wq, k_gain, q_gain, b_f, group_mean)


N_SPLIT = 3


def _split3(x):
    x0 = x.astype(BF16)
    r1 = x - x0.astype(F32)
    x1 = r1.astype(BF16)
    x2 = (r1 - x1.astype(F32)).astype(BF16)
    return x0, x1, x2


def _fox_prep_kernel(kb_ref, lf_ref, q_ref, k_ref, qa_ref, ka_ref, stats_ref, carry_scr):
    @pl.when(pl.program_id(1) == 0)
    def _():
        carry_scr[...] = jnp.zeros_like(carry_scr)

    r = lax.broadcasted_iota(jnp.int32, (TK, TK), 0)
    c = lax.broadcasted_iota(jnp.int32, (TK, TK), 1)
    tri = jnp.where(c <= r, 1.0, 0.0).astype(BF16)
    x0, x1, x2 = _split3(lf_ref[...])
    cs = _dot(tri, x0) + _dot(tri, x1) + _dot(tri, x2) + carry_scr[0:1, :]
    carry_scr[...] = jnp.broadcast_to(cs[TK - 1:TK, :], carry_scr.shape)

    d_idx = lax.broadcasted_iota(jnp.int32, (D_MODEL, LANES_V7X), 0)
    h_idx = lax.broadcasted_iota(jnp.int32, (D_MODEL, LANES_V7X), 1)
    head_of = jnp.where(d_idx // HEAD_DIM == h_idx, 1.0, 0.0).astype(BF16)
    qf = q_ref[...].astype(F32)
    reach = jnp.sqrt(_dot((qf * qf).astype(BF16), head_of)) * kb_ref[0, 0]
    own_logit = _dot((qf * k_ref[...].astype(F32)).astype(BF16), head_of)

    stats_ref[...] = jnp.concatenate([
        jnp.max(reach * (1.0 + 2.0 ** -8) - own_logit + cs, axis=0, keepdims=True),
        cs[0:1, :],
        jnp.max(reach, axis=0, keepdims=True),
        jnp.zeros((SUBLANES_V7X - 3, LANES_V7X), F32)], axis=0)

    lane = lax.broadcasted_iota(jnp.int32, (1, LANES_V7X), 1)
    packed = jnp.zeros((TK, LANES_V7X), F32)
    for i, term in enumerate(_split3(cs - reach) + _split3(-cs)):
        term = jnp.where(lane < N_HEADS, term.astype(F32), 0.0)
        packed = packed + (term if i == 0 else pltpu.roll(term, N_HEADS * i, axis=1))
    packed = packed.astype(BF16)

    src = lax.broadcasted_iota(jnp.int32, (LANES_V7X, 2 * LANES_V7X), 0)
    dst = lax.broadcasted_iota(jnp.int32, (LANES_V7X, 2 * LANES_V7X), 1)
    for h in range(N_HEADS):
        pair, parity = divmod(h, HEADS_PER_LANE_TILE)
        spare = HEAD_DIM * (1 - parity)
        place = jnp.where((src % N_HEADS == h) & (src < 2 * N_SPLIT * N_HEADS)
                          & (dst % LANES_V7X == spare + src // N_HEADS)
                          & ((dst >= LANES_V7X) == (src >= N_SPLIT * N_HEADS)), 1.0, 0.0)
        placed = _dot(packed, place.astype(BF16))
        ones_q = jnp.where((lane >= spare + N_SPLIT) & (lane < spare + 2 * N_SPLIT), 1.0, 0.0)
        ones_k = jnp.where((lane >= spare) & (lane < spare + N_SPLIT), 1.0, 0.0)
        aug_q = (placed[:, :LANES_V7X] + ones_q).astype(BF16)
        aug_k = (placed[:, LANES_V7X:] + ones_k).astype(BF16)
        own = (lane // HEAD_DIM) == parity
        cols = slice(pair * LANES_V7X, (pair + 1) * LANES_V7X)
        qa_ref[h] = jnp.where(own, q_ref[:, cols], aug_q)
        ka_ref[h] = jnp.where(own, k_ref[:, cols], aug_k)


def _fox_prep(log_f, q, k, key_norm_bound, batch, seq):
    assert TQ == TK
    d = q.shape[1]
    block_bytes = (4 * TK * LANES_V7X + 2 * 2 * TK * d + 2 * 2 * N_HEADS * TK * LANES_V7X
                   + 4 * SUBLANES_V7X * LANES_V7X)
    aug_shape = jax.ShapeDtypeStruct((batch, N_HEADS, seq, LANES_V7X), BF16)
    aug_spec = pl.BlockSpec((None, N_HEADS, TK, LANES_V7X), lambda b, t: (b, 0, t, 0))
    return pl.pallas_call(
        _fox_prep_kernel,
        out_shape=(aug_shape, aug_shape,
                   jax.ShapeDtypeStruct((batch, seq // TK, SUBLANES_V7X, LANES_V7X), F32)),
        grid=(batch, seq // TK),
        in_specs=[
            pl.BlockSpec(memory_space=pltpu.SMEM),
            pl.BlockSpec((None, TK, LANES_V7X), lambda b, t: (b, t, 0)),
            pl.BlockSpec((None, TK, d), lambda b, t: (b, t, 0)),
            pl.BlockSpec((None, TK, d), lambda b, t: (b, t, 0)),
        ],
        out_specs=(aug_spec, aug_spec,
                   pl.BlockSpec((None, None, SUBLANES_V7X, LANES_V7X), lambda b, t: (b, t, 0, 0))),
        scratch_shapes=[pltpu.VMEM((SUBLANES_V7X, LANES_V7X), F32)],
        compiler_params=pltpu.CompilerParams(
            dimension_semantics=("arbitrary", "arbitrary"),
            vmem_limit_bytes=_vmem_limit(block_bytes)),
        name="fox_prep",
    )(key_norm_bound, log_f.reshape(batch, seq, LANES_V7X), q.reshape(batch, seq, d),
      k.reshape(batch, seq, d))


def _head_select(head):
    lane = lax.broadcasted_iota(jnp.int32, (1, LANES_V7X), 1)
    return (lane // HEAD_DIM) == head


def _widen(col128, width):
    reps = width // LANES_V7X
    return col128 if reps == 1 else jnp.concatenate([col128] * reps, axis=1)


def _sb_kernel(q_ref, k_ref, v_ref, o_ref):
    i = pl.program_id(2)
    row = lax.broadcasted_iota(jnp.int32, (SB_T, SB_T), 0)
    col = lax.broadcasted_iota(jnp.int32, (SB_T, SB_T), 1)
    diag_mask = col < row
    after = jnp.where(row > col, 1.0, 0.0).astype(BF16)
    ones = jnp.ones((SB_T, LANES_V7X), BF16)
    ones_wide = jnp.ones((SB_T, SB_T), BF16)

    def softplus(z):
        return jnp.maximum(z, 0.0) + jnp.log2(1.0 + jnp.exp2(-jnp.abs(z)))

    def keys(j):
        return k_ref[pl.ds(j * SB_T, SB_T), :], v_ref[pl.ds(j * SB_T, SB_T), :]

    def band(qh, g):
        k_d, v_d = keys(g)
        k_p, v_p = keys(jnp.maximum(g - 1, 0))
        z_d = _dot_nt(qh, k_d)
        z_p = jnp.where(g > 0, _dot_nt(qh, k_p), NEG_BIG)
        sp_d = jnp.where(diag_mask, softplus(z_d), 0.0)
        sp_p = softplus(z_p)
        sp_d16 = sp_d.astype(BF16)
        within_d = _dot(sp_d16, after)
        within_p = _dot(sp_p.astype(BF16), after) + _dot(sp_d16, ones_wide)
        a_d = jnp.where(diag_mask, jnp.exp2((z_d - sp_d) - within_d), 0.0)
        a_p = jnp.exp2((z_p - sp_p) - within_p)
        acc = _dot(a_d.astype(BF16), v_d) + _dot(a_p.astype(BF16), v_p)
        total = (within_p + sp_p)[:, 0:1]
        return jnp.broadcast_to(-total, (SB_T, LANES_V7X)), acc

    def block(qh, j, carry, acc):
        kj, vj = keys(j)
        z = _dot_nt(qh, kj)
        sp = softplus(z)
        sp16 = sp.astype(BF16)
        within = _dot(sp16, after)
        a = jnp.exp2((z - sp) - within + _widen(carry, SB_T))
        acc = acc + _dot(a.astype(BF16), vj)
        carry = carry - _dot(sp16, ones)
        return carry, acc

    chains = []
    for u in range(SB_SUB):
        g = i * SB_SUB + u
        q = q_ref[u * SB_T:(u + 1) * SB_T, :]
        for head in range(HEADS_PER_LANE_TILE):
            qh = jnp.where(_head_select(head), q, 0.0).astype(BF16)
            chains.append((g, qh) + band(qh, g))

    def is_open(carry):
        return jnp.max(carry) > -ZERO_EXP * LOG2_E

    chain_open = [is_open(carry) for _, _, carry, _ in chains]

    def walk_on():
        outs = []
        for (g, qh, carry, acc), this_open in zip(chains, chain_open):
            def cond(state):
                j, carry, _ = state
                return jnp.logical_and(j >= 0, is_open(carry))

            def body(state, qh=qh):
                j, carry, acc = state
                carry, acc = block(qh, j, carry, acc)
                return j - 1, carry, acc

            def walk(cond=cond, body=body, start=(g - 2, carry, acc)):
                return lax.while_loop(cond, body, start)[2]

            outs.append(lax.cond(this_open, walk, lambda acc=acc: acc))
        return outs

    outs = lax.cond(functools.reduce(jnp.logical_or, chain_open), walk_on,
                    lambda: [acc for _, _, _, acc in chains])
    for u in range(SB_SUB):
        pair = outs[u * HEADS_PER_LANE_TILE:(u + 1) * HEADS_PER_LANE_TILE]
        o_ref[u * SB_T:(u + 1) * SB_T, :] = jnp.where(_head_select(0), pair[0], pair[1]).astype(o_ref.dtype)


def _sb_attn(qkv, batch, seq):
    rows = SB_SUB * SB_T
    block_bytes = 2 * (2 * rows * LANES_V7X + 2 * seq * LANES_V7X)
    return pl.pallas_call(
        _sb_kernel,
        out_shape=jax.ShapeDtypeStruct((batch, seq, D_MODEL), BF16),
        grid=(batch, N_HEAD_PAIRS, seq // rows),
        in_specs=[
            pl.BlockSpec((None, rows, LANES_V7X), lambda b, p, i: (b, i, p)),
            pl.BlockSpec((None, seq, LANES_V7X), lambda b, p, i: (b, 0, N_HEAD_PAIRS + p)),
            pl.BlockSpec((None, seq, LANES_V7X), lambda b, p, i: (b, 0, 2 * N_HEAD_PAIRS + p)),
        ],
        out_specs=pl.BlockSpec((None, rows, LANES_V7X), lambda b, p, i: (b, i, p)),
        compiler_params=pltpu.CompilerParams(
            dimension_semantics=("arbitrary", "arbitrary", "arbitrary"),
            vmem_limit_bytes=_vmem_limit(block_bytes)),
        name="sb_attn",
    )(qkv, qkv, qkv)


def _fox_kernel(a_ref, fk0_ref, g_ref, qa_ref, ka_ref, v_ref, o_ref):
    b = pl.program_id(0)
    p_id = pl.program_id(1)
    i = pl.program_id(2)
    n_blocks = fk0_ref.shape[1]
    row = lax.broadcasted_iota(jnp.int32, (TQ, TK), 0)
    col = lax.broadcasted_iota(jnp.int32, (TQ, TK), 1)
    diag_mask = col <= row

    acc0 = jnp.zeros((TQ, LANES_V7X), F32)
    heads = range(HEADS_PER_LANE_TILE)
    bhs = [(b * N_HEAD_PAIRS + p_id) * HEADS_PER_LANE_TILE + head for head in heads]

    def scores(head, j):
        return _dot_nt(qa_ref[head], ka_ref[head, pl.ds(j * TK, TK), :])

    def values(head, j):
        vj = v_ref[pl.ds(j * TK, TK), :]
        return jnp.where(_head_select(head), vj, jnp.ones_like(vj))

    def count_left(head):
        bound = a_ref[bhs[head], i]

        def more(n):
            j = i - 1 - n
            right = fk0_ref[bhs[head], jnp.minimum(j + 1, n_blocks - 1)]
            return jnp.logical_and(j >= 0, bound - right > -ZERO_EXP)

        return lax.while_loop(more, lambda n: n + 1, jnp.int32(0))

    def fast_walks():
        n0, n1 = count_left(0), count_left(1)
        n_items = n0 + n1 + 2
        diag_bias = jnp.where(diag_mask, 0.0, NEG_BIG)

        def step(t, p, p_head, p_block, accs):
            on_head1 = t > n0
            head = jnp.where(on_head1, 1, 0)
            back = jnp.where(on_head1, t - n0 - 1, t)
            block = jnp.maximum(i - back, 0)
            s = scores(head, block) + jnp.where(back == 0, diag_bias,
                                                jnp.where(t >= n_items, NEG_BIG, 0.0))
            met = _dot(p, values(p_head, p_block))
            accs = tuple(acc + jnp.where(p_head == h, met, 0.0) for h, acc in zip(heads, accs))
            return jnp.exp(s).astype(BF16), head, block, accs

        def steps(first, count, carried):
            for t in range(count):
                carried = step(first + t, *carried)
            return carried

        pairs = (n_items + 2) // 2
        start = (jnp.zeros((TQ, TK), BF16), jnp.int32(0), i, (acc0, acc0))
        carried = lax.fori_loop(0, pairs // 2, lambda u, c: steps(4 * u, 4, c), start)
        carried = lax.fori_loop(0, pairs % 2, lambda _, c: steps(4 * (pairs // 2), 2, c), carried)
        return list(carried[3])

    def online_walks():
        accs = []
        for head in heads:
            def body(t, carried, head=head):
                acc, m = carried
                s = jnp.where(jnp.logical_or(t > 0, diag_mask), scores(head, i - t), NEG_BIG)
                m_new = jnp.maximum(m, jnp.max(s, axis=-1, keepdims=True))
                p = jnp.exp(s - m_new).astype(BF16)
                return jnp.exp(m - m_new) * acc + _dot(p, values(head, i - t)), m_new

            m0 = jnp.full((TQ, 1), NEG_BIG, F32)
            accs.append(lax.fori_loop(0, count_left(head) + 1, body, (acc0, m0))[0])
        return accs

    largest_reach = jnp.maximum(g_ref[bhs[0], i], g_ref[bhs[1], i])
    accs = lax.cond(2.0 * largest_reach < SAFE_EXP_RANGE, fast_walks, online_walks)
    outs = [acc / pltpu.roll(acc, HEAD_DIM, axis=1) for acc in accs]
    o_ref[...] = jnp.where(_head_select(0), outs[0], outs[1]).astype(o_ref.dtype)


def _fox_attn(q_aug, k_aug, v, walk_bound, fk_block_start, max_reach, batch, seq):
    assert TQ == TK
    block_bytes = 2 * (HEADS_PER_LANE_TILE * (TQ + seq) * LANES_V7X + seq * LANES_V7X + TQ * LANES_V7X)
    smem = pl.BlockSpec(memory_space=pltpu.SMEM)
    return pl.pallas_call(
        _fox_kernel,
        out_shape=jax.ShapeDtypeStruct((batch, seq, D_MODEL), BF16),
        grid=(batch, N_HEAD_PAIRS, seq // TQ),
        in_specs=[
            smem, smem, smem,
            pl.BlockSpec((None, HEADS_PER_LANE_TILE, TQ, LANES_V7X), lambda b, p, i: (b, p, i, 0)),
            pl.BlockSpec((None, HEADS_PER_LANE_TILE, seq, LANES_V7X), lambda b, p, i: (b, p, 0, 0)),
            pl.BlockSpec((None, seq, LANES_V7X), lambda b, p, i: (b, 0, p)),
        ],
        out_specs=pl.BlockSpec((None, TQ, LANES_V7X), lambda b, p, i: (b, i, p)),
        compiler_params=pltpu.CompilerParams(
            dimension_semantics=("arbitrary", "arbitrary", "arbitrary"),
            vmem_limit_bytes=_vmem_limit(block_bytes)),
        name="fox_attn",
    )(walk_bound, fk_block_start, max_reach, q_aug, k_aug, v)


def kernel(x, c, ada_w, ada_b, norm_attn_g, norm_ffn_g, w_ffn_in, w_ffn_down, sb_w_qkv, sb_w_o,
           kv_ada_w, kv_ada_b, kv_norm_g, w_kvf, b_f, k_norm_g, fox_w_q, q_norm_g, fox_w_o):
    batch, seq, d = x.shape
    assert d == D_MODEL and seq % TM == 0 and seq % TQ == 0
    assert ada_w.shape[0] == 2 and sb_w_qkv.shape[0] == 1 and fox_w_q.shape[0] == 1
    m = batch * seq

    c_pad = jnp.zeros((SUBLANES_V7X, d), F32).at[:batch].set(c.astype(F32))
    mod = _ada_mod(c_pad, ada_w, ada_b)[:, :batch]
    kv_mod = _ada_mod(c_pad, kv_ada_w[None], kv_ada_b[None])[0, :batch]

    def vecs(t, n):
        return [t[:, None, j * d:(j + 1) * d] for j in range(n)]

    row = lambda t: t.reshape(1, -1).astype(F32)
    bf = lambda t: t.astype(BF16)

    x2d = x.reshape(m, d)

    sh_a, sc_a, g_a, sh_f, sc_f, g_f = vecs(mod[0], 6)
    qkv_scale = jnp.concatenate([jnp.full((1, d), LOG2_E * HEAD_DIM ** -0.5, F32),
                                 jnp.ones((1, 2 * d), F32)], axis=1)
    qkv = _norm_matmul(x2d, row(norm_attn_g[0]), sh_a, sc_a, bf(sb_w_qkv[0]), qkv_scale, seq)
    o = _sb_attn(qkv.reshape(batch, seq, 3 * d), batch, seq)
    x2d, act = _out_ffn_in(o.reshape(m, d), bf(sb_w_o[0]), x2d, g_a,
                           row(norm_ffn_g[0]), sh_f, sc_f, bf(w_ffn_in[0]), seq)
    x2d = _proj_residual(act, bf(w_ffn_down[0]), x2d, g_f, seq)

    sh_a, sc_a, g_a, sh_f, sc_f, g_f = vecs(mod[1], 6)
    kv_shift, kv_scale = vecs(kv_mod, 2)
    w_f_pad = jnp.zeros((d, LANES_V7X), F32).at[:, :N_HEADS].set(w_kvf[:, 2 * d:])
    b_f_pad = jnp.zeros((1, LANES_V7X), F32).at[0, :N_HEADS].set(b_f)
    k, v, q, log_f = _l1_proj(
        x2d, row(kv_norm_g), kv_shift, kv_scale, row(norm_attn_g[1]), sh_a, sc_a,
        bf(w_kvf[:, :d]), bf(w_kvf[:, d:2 * d]), bf(w_f_pad), bf(fox_w_q[0]),
        row(jnp.tile(k_norm_g, N_HEADS)), row(jnp.tile(q_norm_g[0], N_HEADS)), b_f_pad, seq)
    key_norm_bound = (1.01 * HEAD_DIM ** 0.5 * jnp.max(jnp.abs(k_norm_g))).reshape(1, 1).astype(F32)
    q_aug, k_aug, stats = _fox_prep(log_f, q, k, key_norm_bound, batch, seq)
    tables = stats[:, :, :3, :N_HEADS].transpose(2, 0, 3, 1).reshape(3, batch * N_HEADS, seq // TK)
    o = _fox_attn(q_aug, k_aug, v.reshape(batch, seq, d), tables[0], tables[1], tables[2], batch, seq)
    x2d, act = _out_ffn_in(o.reshape(m, d), bf(fox_w_o[0]), x2d, g_a,
                           row(norm_ffn_g[1]), sh_f, sc_f, bf(w_ffn_in[1]), seq)
    x2d = _proj_residual(act, bf(w_ffn_down[1]), x2d, g_f, seq)
    return x2d.reshape(batch, seq, d)
```

```python
import functools

import jax
import jax.numpy as jnp
from jax import lax
from jax.experimental import pallas as pl
from jax.experimental.pallas import tpu as pltpu

D_MODEL = 1024
N_HEADS = 16
HEAD_DIM = D_MODEL // N_HEADS
EPS = 1e-6

LANES_V7X = 128
SUBLANES_V7X = 8
VMEM_BYTES_V7X = 64 * 1024 * 1024

HEADS_PER_LANE_TILE = LANES_V7X // HEAD_DIM
N_HEAD_PAIRS = N_HEADS // HEADS_PER_LANE_TILE

MXU_WIDTH_V7X = 256

TM = 512
COL_CHUNK = 2 * MXU_WIDTH_V7X
TQ = 512
TK = 512
FOX_SUB = 2
SB_T = 256
SB_SUB = 4

ZERO_EXP = 104.5
SAFE_EXP_RANGE = 80.0
NEG_BIG = -1e30
LOG2_E = 1.4426950408889634

F32 = jnp.float32
BF16 = jnp.bfloat16


def _vmem_limit(block_bytes):
    want = 2 * block_bytes + 16 * 1024 * 1024
    return int(min(want, VMEM_BYTES_V7X - 8 * 1024 * 1024))


def _split2(x):
    hi = x.astype(BF16)
    lo = (x - hi.astype(F32)).astype(BF16)
    return hi, lo


def _dot(a, b):
    return jnp.dot(a, b, preferred_element_type=F32)


def _dot_nt(a, b):
    return lax.dot_general(a, b, (((1,), (1,)), ((), ())), preferred_element_type=F32)


def _ada_kernel(c_ref, w_ref, b_ref, o_ref):
    c = c_ref[...]
    c_act = c * jax.nn.sigmoid(c)
    o_ref[...] = jnp.dot(c_act, w_ref[...], precision=lax.Precision.HIGHEST,
                         preferred_element_type=F32) + b_ref[...]


def _ada_mod(c_pad, w, b):
    n_layers, d, n = w.shape
    tn = 1024
    rows = c_pad.shape[0]
    block_bytes = 4 * (d * tn + rows * d + tn + rows * tn)
    return pl.pallas_call(
        _ada_kernel,
        out_shape=jax.ShapeDtypeStruct((n_layers, rows, n), F32),
        grid=(n_layers, n // tn),
        in_specs=[
            pl.BlockSpec((rows, d), lambda l, j: (0, 0)),
            pl.BlockSpec((None, d, tn), lambda l, j: (l, 0, j)),
            pl.BlockSpec((None, 1, tn), lambda l, j: (l, 0, j)),
        ],
        out_specs=pl.BlockSpec((None, rows, tn), lambda l, j: (l, 0, j)),
        compiler_params=pltpu.CompilerParams(
            dimension_semantics=("arbitrary", "arbitrary"),
            vmem_limit_bytes=_vmem_limit(block_bytes)),
        name="ada_mod",
    )(c_pad, w, b.reshape(n_layers, 1, n))


def _norm_modulate(x, g, shift, scale):
    ms = jnp.mean(x * x, axis=-1, keepdims=True)
    y = x * lax.rsqrt(ms + EPS)
    return (y * g) * (1.0 + scale) + shift


def _row_spec(d):
    return pl.BlockSpec((1, d), lambda *_: (0, 0))


def _batch_vec_spec(d, tiles_per_batch):
    return pl.BlockSpec((None, 1, d), lambda i, *_: (i // tiles_per_batch, 0, 0))


def _column_chunks(n):
    return [(c, min(c + COL_CHUNK, n)) for c in range(0, n, COL_CHUNK)]


def _norm_matmul_kernel(x_ref, g_ref, sh_ref, sc_ref, w_ref, cs_ref, o_ref):
    h = _norm_modulate(x_ref[...], g_ref[...], sh_ref[...], sc_ref[...]).astype(BF16)
    for c0, c1 in _column_chunks(o_ref.shape[1]):
        o_ref[:, c0:c1] = (_dot(h, w_ref[:, c0:c1]) * cs_ref[:, c0:c1]).astype(o_ref.dtype)


def _norm_matmul(x2d, g, shift, scale, w, col_scale, seq):
    m, d = x2d.shape
    n = w.shape[1]
    block_bytes = 4 * TM * d + 2 * d * n + 2 * TM * n + 4 * n
    return pl.pallas_call(
        _norm_matmul_kernel,
        out_shape=jax.ShapeDtypeStruct((m, n), BF16),
        grid=(m // TM,),
        in_specs=[
            pl.BlockSpec((TM, d), lambda i: (i, 0)),
            _row_spec(d),
            _batch_vec_spec(d, seq // TM),
            _batch_vec_spec(d, seq // TM),
            pl.BlockSpec((d, n), lambda i: (0, 0)),
            pl.BlockSpec((1, n), lambda i: (0, 0)),
        ],
        out_specs=pl.BlockSpec((TM, n), lambda i: (i, 0)),
        compiler_params=pltpu.CompilerParams(
            dimension_semantics=("arbitrary",),
            vmem_limit_bytes=_vmem_limit(block_bytes)),
        name="norm_matmul",
    )(x2d, g, shift, scale, w, col_scale)


def _out_ffn_in_kernel(a_ref, wo_ref, x_ref, gate_ref, g_ref, sh_ref, sc_ref, wg_ref, wu_ref,
                       x1_ref, act_ref):
    x1 = x_ref[...] + gate_ref[...] * _dot(a_ref[...], wo_ref[...])
    x1_ref[...] = x1
    h = _norm_modulate(x1, g_ref[...], sh_ref[...], sc_ref[...]).astype(BF16)
    for c0, c1 in _column_chunks(act_ref.shape[1]):
        gate = _dot(h, wg_ref[:, c0:c1])
        up = _dot(h, wu_ref[:, c0:c1])
        act_ref[:, c0:c1] = (gate * jax.nn.sigmoid(gate) * up).astype(act_ref.dtype)


def _out_ffn_in(a, w_o, x2d, gate, g, shift, scale, w_in, seq):
    m, d = x2d.shape
    f = w_in.shape[1] // 2
    tpb = seq // TM
    block_bytes = 2 * TM * d + 2 * d * d + 2 * 4 * TM * d + 2 * 2 * d * f + 2 * TM * f
    return pl.pallas_call(
        _out_ffn_in_kernel,
        out_shape=(jax.ShapeDtypeStruct((m, d), F32), jax.ShapeDtypeStruct((m, f), BF16)),
        grid=(m // TM,),
        in_specs=[
            pl.BlockSpec((TM, d), lambda i: (i, 0)),
            pl.BlockSpec((d, d), lambda i: (0, 0)),
            pl.BlockSpec((TM, d), lambda i: (i, 0)),
            _batch_vec_spec(d, tpb),
            _row_spec(d),
            _batch_vec_spec(d, tpb),
            _batch_vec_spec(d, tpb),
            pl.BlockSpec((d, f), lambda i: (0, 0)),
            pl.BlockSpec((d, f), lambda i: (0, 1)),
        ],
        out_specs=(pl.BlockSpec((TM, d), lambda i: (i, 0)), pl.BlockSpec((TM, f), lambda i: (i, 0))),
        compiler_params=pltpu.CompilerParams(
            dimension_semantics=("arbitrary",),
            vmem_limit_bytes=_vmem_limit(block_bytes)),
        name="out_ffn_in",
    )(a, w_o, x2d, gate, g, shift, scale, w_in, w_in)


def _proj_residual_kernel(a_ref, w_ref, x_ref, gate_ref, o_ref):
    o_ref[...] = x_ref[...] + gate_ref[...] * _dot(a_ref[...], w_ref[...])


def _proj_residual(a, w, x2d, gate, seq):
    m, k = a.shape
    d = w.shape[1]
    block_bytes = 2 * TM * k + 2 * k * d + 2 * 4 * TM * d
    return pl.pallas_call(
        _proj_residual_kernel,
        out_shape=jax.ShapeDtypeStruct((m, d), F32),
        grid=(m // TM,),
        in_specs=[
            pl.BlockSpec((TM, k), lambda i: (i, 0)),
            pl.BlockSpec((k, d), lambda i: (0, 0)),
            pl.BlockSpec((TM, d), lambda i: (i, 0)),
            _batch_vec_spec(d, seq // TM),
        ],
        out_specs=pl.BlockSpec((TM, d), lambda i: (i, 0)),
        compiler_params=pltpu.CompilerParams(
            dimension_semantics=("arbitrary",),
            vmem_limit_bytes=_vmem_limit(block_bytes)),
        name="proj_residual",
    )(a, w, x2d, gate)


def _head_rms_norm(t, gain, group_mean):
    ms = _dot((t * t).astype(BF16), group_mean)
    return t * lax.rsqrt(ms + EPS) * gain


def _l1_proj_kernel(x_ref, gkv_ref, shkv_ref, sckv_ref, gq_ref, shq_ref, scq_ref,
                    wk_ref, wv_ref, wf_ref, wq_ref, kg_ref, qg_ref, bf_ref, gm_ref,
                    k_ref, v_ref, q_ref, lf_ref):
    x = x_ref[...]
    ms = jnp.mean(x * x, axis=-1, keepdims=True)
    y = x * lax.rsqrt(ms + EPS)
    h_kv = ((y * gkv_ref[...]) * (1.0 + sckv_ref[...]) + shkv_ref[...]).astype(BF16)
    h_q = ((y * gq_ref[...]) * (1.0 + scq_ref[...]) + shq_ref[...]).astype(BF16)
    gm = gm_ref[...]
    k_all = _dot(h_kv, wk_ref[...])
    q_all = _dot(h_q, wq_ref[...])
    v_ref[...] = _dot(h_kv, wv_ref[...]).astype(BF16)
    lf_ref[...] = jax.nn.log_sigmoid(_dot(h_kv, wf_ref[...]) + bf_ref[...])
    for c0 in range(0, k_ref.shape[1], MXU_WIDTH_V7X):
        cols = slice(c0, c0 + MXU_WIDTH_V7X)
        k_ref[:, cols] = _head_rms_norm(k_all[:, cols], kg_ref[:, cols], gm).astype(BF16)
        q = _head_rms_norm(q_all[:, cols], qg_ref[:, cols], gm)
        q_ref[:, cols] = (q * (HEAD_DIM ** -0.5)).astype(BF16)


def _l1_proj(x2d, g_kv, sh_kv, sc_kv, g_q, sh_q, sc_q, wk, wv, wf, wq, k_gain, q_gain, b_f, seq):
    m, d = x2d.shape
    lane = jnp.arange(MXU_WIDTH_V7X)
    group_mean = jnp.where(lane[:, None] // HEAD_DIM == lane[None, :] // HEAD_DIM,
                           1.0 / HEAD_DIM, 0.0).astype(BF16)
    tpb = seq // TM
    block_bytes = 4 * TM * d + 3 * 2 * d * d + 2 * d * LANES_V7X + 3 * 2 * TM * d + 4 * TM * LANES_V7X
    full = lambda r, c: pl.BlockSpec((r, c), lambda i: (0, 0))
    return pl.pallas_call(
        _l1_proj_kernel,
        out_shape=(jax.ShapeDtypeStruct((m, d), BF16), jax.ShapeDtypeStruct((m, d), BF16),
                   jax.ShapeDtypeStruct((m, d), BF16), jax.ShapeDtypeStruct((m, LANES_V7X), F32)),
        grid=(m // TM,),
        in_specs=[
            pl.BlockSpec((TM, d), lambda i: (i, 0)),
            _row_spec(d), _batch_vec_spec(d, tpb), _batch_vec_spec(d, tpb),
            _row_spec(d), _batch_vec_spec(d, tpb), _batch_vec_spec(d, tpb),
            full(d, d), full(d, d), full(d, LANES_V7X), full(d, d),
            _row_spec(d), _row_spec(d), _row_spec(LANES_V7X), full(MXU_WIDTH_V7X, MXU_WIDTH_V7X),
        ],
        out_specs=(pl.BlockSpec((TM, d), lambda i: (i, 0)), pl.BlockSpec((TM, d), lambda i: (i, 0)),
                   pl.BlockSpec((TM, d), lambda i: (i, 0)), pl.BlockSpec((TM, LANES_V7X), lambda i: (i, 0))),
        compiler_params=pltpu.CompilerParams(
            dimension_semantics=("arbitrary",),
            vmem_limit_bytes=_vmem_limit(block_bytes)),
        name="l1_proj",
    )(x2d, g_kv, sh_kv, sc_kv, g_q, sh_q, sc_q, wk, wv, wf, wq, k_gain, q_gain, b_f, group_mean)


N_SPLIT = 3


def _split3(x):
    x0 = x.astype(BF16)
    r1 = x - x0.astype(F32)
    x1 = r1.astype(BF16)
    x2 = (r1 - x1.astype(F32)).astype(BF16)
    return x0, x1, x2


def _fox_prep_kernel(kb_ref, lf_ref, q_ref, k_ref, qa_ref, ka_ref, stats_ref, carry_scr):
    @pl.when(pl.program_id(1) == 0)
    def _():
        carry_scr[...] = jnp.zeros_like(carry_scr)

    r = lax.broadcasted_iota(jnp.int32, (TK, TK), 0)
    c = lax.broadcasted_iota(jnp.int32, (TK, TK), 1)
    tri = jnp.where(c <= r, 1.0, 0.0).astype(BF16)
    x0, x1, x2 = _split3(lf_ref[...])
    cs = _dot(tri, x0) + _dot(tri, x1) + _dot(tri, x2) + carry_scr[0:1, :]
    carry_scr[...] = jnp.broadcast_to(cs[TK - 1:TK, :], carry_scr.shape)

    d_idx = lax.broadcasted_iota(jnp.int32, (D_MODEL, LANES_V7X), 0)
    h_idx = lax.broadcasted_iota(jnp.int32, (D_MODEL, LANES_V7X), 1)
    head_of = jnp.where(d_idx // HEAD_DIM == h_idx, 1.0, 0.0).astype(BF16)
    qf = q_ref[...].astype(F32)
    reach = jnp.sqrt(_dot((qf * qf).astype(BF16), head_of)) * kb_ref[0, 0]
    own_logit = _dot((qf * k_ref[...].astype(F32)).astype(BF16), head_of)

    stats_ref[...] = jnp.concatenate([
        jnp.max(reach * (1.0 + 2.0 ** -8) - own_logit + cs, axis=0, keepdims=True),
        cs[0:1, :],
        jnp.max(reach, axis=0, keepdims=True),
        jnp.zeros((SUBLANES_V7X - 3, LANES_V7X), F32)], axis=0)

    lane = lax.broadcasted_iota(jnp.int32, (1, LANES_V7X), 1)
    packed = jnp.zeros((TK, LANES_V7X), F32)
    for i, term in enumerate(_split3(cs - reach) + _split3(-cs)):
        term = jnp.where(lane < N_HEADS, term.astype(F32), 0.0)
        packed = packed + (term if i == 0 else pltpu.roll(term, N_HEADS * i, axis=1))
    packed = packed.astype(BF16)

    src = lax.broadcasted_iota(jnp.int32, (LANES_V7X, 2 * LANES_V7X), 0)
    dst = lax.broadcasted_iota(jnp.int32, (LANES_V7X, 2 * LANES_V7X), 1)
    for h in range(N_HEADS):
        pair, parity = divmod(h, HEADS_PER_LANE_TILE)
        spare = HEAD_DIM * (1 - parity)
        place = jnp.where((src % N_HEADS == h) & (src < 2 * N_SPLIT * N_HEADS)
                          & (dst % LANES_V7X == spare + src // N_HEADS)
                          & ((dst >= LANES_V7X) == (src >= N_SPLIT * N_HEADS)), 1.0, 0.0)
        placed = _dot(packed, place.astype(BF16))
        ones_q = jnp.where((lane >= spare + N_SPLIT) & (lane < spare + 2 * N_SPLIT), 1.0, 0.0)
        ones_k = jnp.where((lane >= spare) & (lane < spare + N_SPLIT), 1.0, 0.0)
        aug_q = (placed[:, :LANES_V7X] + ones_q).astype(BF16)
        aug_k = (placed[:, LANES_V7X:] + ones_k).astype(BF16)
        own = (lane // HEAD_DIM) == parity
        cols = slice(pair * LANES_V7X, (pair + 1) * LANES_V7X)
        qa_ref[h] = jnp.where(own, q_ref[:, cols], aug_q)
        ka_ref[h] = jnp.where(own, k_ref[:, cols], aug_k)


def _fox_prep(log_f, q, k, key_norm_bound, batch, seq):
    assert TQ == TK
    d = q.shape[1]
    block_bytes = (4 * TK * LANES_V7X + 2 * 2 * TK * d + 2 * 2 * N_HEADS * TK * LANES_V7X
                   + 4 * SUBLANES_V7X * LANES_V7X)
    aug_shape = jax.ShapeDtypeStruct((batch, N_HEADS, seq, LANES_V7X), BF16)
    aug_spec = pl.BlockSpec((None, N_HEADS, TK, LANES_V7X), lambda b, t: (b, 0, t, 0))
    return pl.pallas_call(
        _fox_prep_kernel,
        out_shape=(aug_shape, aug_shape,
                   jax.ShapeDtypeStruct((batch, seq // TK, SUBLANES_V7X, LANES_V7X), F32)),
        grid=(batch, seq // TK),
        in_specs=[
            pl.BlockSpec(memory_space=pltpu.SMEM),
            pl.BlockSpec((None, TK, LANES_V7X), lambda b, t: (b, t, 0)),
            pl.BlockSpec((None, TK, d), lambda b, t: (b, t, 0)),
            pl.BlockSpec((None, TK, d), lambda b, t: (b, t, 0)),
        ],
        out_specs=(aug_spec, aug_spec,
                   pl.BlockSpec((None, None, SUBLANES_V7X, LANES_V7X), lambda b, t: (b, t, 0, 0))),
        scratch_shapes=[pltpu.VMEM((SUBLANES_V7X, LANES_V7X), F32)],
        compiler_params=pltpu.CompilerParams(
            dimension_semantics=("arbitrary", "arbitrary"),
            vmem_limit_bytes=_vmem_limit(block_bytes)),
        name="fox_prep",
    )(key_norm_bound, log_f.reshape(batch, seq, LANES_V7X), q.reshape(batch, seq, d),
      k.reshape(batch, seq, d))


def _head_select(head):
    lane = lax.broadcasted_iota(jnp.int32, (1, LANES_V7X), 1)
    return (lane // HEAD_DIM) == head


def _widen(col128, width):
    reps = width // LANES_V7X
    return col128 if reps == 1 else jnp.concatenate([col128] * reps, axis=1)


def _sb_kernel(q_ref, k_ref, v_ref, o_ref):
    i = pl.program_id(2)
    row = lax.broadcasted_iota(jnp.int32, (SB_T, SB_T), 0)
    col = lax.broadcasted_iota(jnp.int32, (SB_T, SB_T), 1)
    diag_mask = col < row
    after = jnp.where(row > col, 1.0, 0.0).astype(BF16)
    ones = jnp.ones((SB_T, LANES_V7X), BF16)
    ones_wide = jnp.ones((SB_T, SB_T), BF16)

    def softplus(z):
        return jnp.maximum(z, 0.0) + jnp.log2(1.0 + jnp.exp2(-jnp.abs(z)))

    def keys(j):
        return k_ref[pl.ds(j * SB_T, SB_T), :], v_ref[pl.ds(j * SB_T, SB_T), :]

    def band(qh, g):
        k_d, v_d = keys(g)
        k_p, v_p = keys(jnp.maximum(g - 1, 0))
        z_d = _dot_nt(qh, k_d)
        z_p = jnp.where(g > 0, _dot_nt(qh, k_p), NEG_BIG)
        sp_d = jnp.where(diag_mask, softplus(z_d), 0.0)
        sp_p = softplus(z_p)
        sp_d16 = sp_d.astype(BF16)
        within_d = _dot(sp_d16, after)
        within_p = _dot(sp_p.astype(BF16), after) + _dot(sp_d16, ones_wide)
        a_d = jnp.where(diag_mask, jnp.exp2((z_d - sp_d) - within_d), 0.0)
        a_p = jnp.exp2((z_p - sp_p) - within_p)
        acc = _dot(a_d.astype(BF16), v_d) + _dot(a_p.astype(BF16), v_p)
        total = (within_p + sp_p)[:, 0:1]
        return jnp.broadcast_to(-total, (SB_T, LANES_V7X)), acc

    def block(qh, j, carry, acc):
        kj, vj = keys(j)
        z = _dot_nt(qh, kj)
        sp = softplus(z)
        sp16 = sp.astype(BF16)
        within = _dot(sp16, after)
        a = jnp.exp2((z - sp) - within + _widen(carry, SB_T))
        acc = acc + _dot(a.astype(BF16), vj)
        carry = carry - _dot(sp16, ones)
        return carry, acc

    chains = []
    for u in range(SB_SUB):
        g = i * SB_SUB + u
        q = q_ref[u * SB_T:(u + 1) * SB_T, :]
        for head in range(HEADS_PER_LANE_TILE):
            qh = jnp.where(_head_select(head), q, 0.0).astype(BF16)
            chains.append((g, qh) + band(qh, g))

    def is_open(carry):
        return jnp.max(carry) > -ZERO_EXP * LOG2_E

    chain_open = [is_open(carry) for _, _, carry, _ in chains]

    def walk_on():
        outs = []
        for (g, qh, carry, acc), this_open in zip(chains, chain_open):
            def cond(state):
                j, carry, _ = state
                return jnp.logical_and(j >= 0, is_open(carry))

            def body(state, qh=qh):
                j, carry, acc = state
                carry, acc = block(qh, j, carry, acc)
                return j - 1, carry, acc

            def walk(cond=cond, body=body, start=(g - 2, carry, acc)):
                return lax.while_loop(cond, body, start)[2]

            outs.append(lax.cond(this_open, walk, lambda acc=acc: acc))
        return outs

    outs = lax.cond(functools.reduce(jnp.logical_or, chain_open), walk_on,
                    lambda: [acc for _, _, _, acc in chains])
    for u in range(SB_SUB):
        pair = outs[u * HEADS_PER_LANE_TILE:(u + 1) * HEADS_PER_LANE_TILE]
        o_ref[u * SB_T:(u + 1) * SB_T, :] = jnp.where(_head_select(0), pair[0], pair[1]).astype(o_ref.dtype)


def _sb_attn(qkv, batch, seq):
    rows = SB_SUB * SB_T
    block_bytes = 2 * (2 * rows * LANES_V7X + 2 * seq * LANES_V7X)
    return pl.pallas_call(
        _sb_kernel,
        out_shape=jax.ShapeDtypeStruct((batch, seq, D_MODEL), BF16),
        grid=(batch, N_HEAD_PAIRS, seq // rows),
        in_specs=[
            pl.BlockSpec((None, rows, LANES_V7X), lambda b, p, i: (b, i, p)),
            pl.BlockSpec((None, seq, LANES_V7X), lambda b, p, i: (b, 0, N_HEAD_PAIRS + p)),
            pl.BlockSpec((None, seq, LANES_V7X), lambda b, p, i: (b, 0, 2 * N_HEAD_PAIRS + p)),
        ],
        out_specs=pl.BlockSpec((None, rows, LANES_V7X), lambda b, p, i: (b, i, p)),
        compiler_params=pltpu.CompilerParams(
            dimension_semantics=("arbitrary", "arbitrary", "arbitrary"),
            vmem_limit_bytes=_vmem_limit(block_bytes)),
        name="sb_attn",
    )(qkv, qkv, qkv)


def _fox_kernel(a_ref, fk0_ref, g_ref, qa_ref, ka_ref, v_ref, o_ref, acc_scr):
    b = pl.program_id(0)
    p_id = pl.program_id(1)
    i = pl.program_id(2)
    n_blocks = fk0_ref.shape[1]
    row = lax.broadcasted_iota(jnp.int32, (TQ, TK), 0)
    col = lax.broadcasted_iota(jnp.int32, (TQ, TK), 1)
    diag_mask = col <= row

    acc0 = jnp.zeros((TQ, LANES_V7X), F32)
    chains = range(FOX_SUB * HEADS_PER_LANE_TILE)
    bh0 = (b * N_HEAD_PAIRS + p_id) * HEADS_PER_LANE_TILE

    def scores(chain, j):
        sub, head = chain // HEADS_PER_LANE_TILE, chain % HEADS_PER_LANE_TILE
        q = qa_ref[head, pl.ds(pl.multiple_of(sub * TQ, TQ), TQ), :]
        return _dot_nt(q, ka_ref[head, pl.ds(pl.multiple_of(j * TK, TK), TK), :])

    def values(chain, j):
        vj = v_ref[pl.ds(pl.multiple_of(j * TK, TK), TK), :]
        own = _head_select(chain % HEADS_PER_LANE_TILE)
        return jnp.where(own, vj, jnp.ones_like(vj))

    def diag_block(chain):
        return i * FOX_SUB + chain // HEADS_PER_LANE_TILE

    def count_left(chain):
        bh, qi = bh0 + chain % HEADS_PER_LANE_TILE, diag_block(chain)
        bound = a_ref[bh, qi]

        def more(n):
            j = qi - 1 - n
            right = fk0_ref[bh, jnp.minimum(j + 1, n_blocks - 1)]
            return jnp.logical_and(j >= 0, bound - right > -ZERO_EXP)

        return lax.while_loop(more, lambda n: n + 1, jnp.int32(0))

    def fast_walks():
        counts = [count_left(chain) for chain in chains]
        firsts = [sum(n + 1 for n in counts[:chain]) for chain in chains]
        n_items = firsts[-1] + counts[-1] + 1
        diag_bias = jnp.where(diag_mask, 0.0, NEG_BIG)
        acc_scr[...] = jnp.zeros_like(acc_scr)

        def step(t, p, p_chain, p_block):
            chain = sum(jnp.where(t >= first, 1, 0) for first in firsts[1:])
            first = functools.reduce(lambda f, nxt: jnp.where(t >= nxt, nxt, f), firsts[1:], 0)
            back = t - first
            block = jnp.maximum(diag_block(chain) - back, 0)
            s = scores(chain, block) + jnp.where(back == 0, diag_bias,
                                                 jnp.where(t >= n_items, NEG_BIG, 0.0))
            acc_scr[p_chain] += _dot(p, values(p_chain, p_block))
            return jnp.exp(s).astype(BF16), chain, block

        def steps(first, count, carried):
            for t in range(count):
                carried = step(first + t, *carried)
            return carried

        pairs = (n_items + 2) // 2
        start = (jnp.zeros((TQ, TK), BF16), jnp.int32(0), jnp.int32(0))
        carried = lax.fori_loop(0, pairs // 2, lambda u, c: steps(4 * u, 4, c), start)
        lax.fori_loop(0, pairs % 2, lambda _, c: steps(4 * (pairs // 2), 2, c), carried)
        return [acc_scr[chain] for chain in chains]

    def online_walks():
        accs = []
        for chain in chains:
            def body(t, carried, chain=chain):
                acc, m = carried
                block = diag_block(chain) - t
                s = jnp.where(jnp.logical_or(t > 0, diag_mask), scores(chain, block), NEG_BIG)
                m_new = jnp.maximum(m, jnp.max(s, axis=-1, keepdims=True))
                p = jnp.exp(s - m_new).astype(BF16)
                return jnp.exp(m - m_new) * acc + _dot(p, values(chain, block)), m_new

            m0 = jnp.full((TQ, 1), NEG_BIG, F32)
            accs.append(lax.fori_loop(0, count_left(chain) + 1, body, (acc0, m0))[0])
        return accs

    largest_reach = functools.reduce(jnp.maximum, [
        g_ref[bh0 + chain % HEADS_PER_LANE_TILE, diag_block(chain)] for chain in chains])
    accs = lax.cond(2.0 * largest_reach < SAFE_EXP_RANGE, fast_walks, online_walks)
    outs = [acc / pltpu.roll(acc, HEAD_DIM, axis=1) for acc in accs]
    for sub in range(FOX_SUB):
        pair = outs[sub * HEADS_PER_LANE_TILE:(sub + 1) * HEADS_PER_LANE_TILE]
        o_ref[sub * TQ:(sub + 1) * TQ, :] = jnp.where(_head_select(0), pair[0], pair[1]).astype(o_ref.dtype)


def _fox_attn(q_aug, k_aug, v, walk_bound, fk_block_start, max_reach, batch, seq):
    assert TQ == TK
    rows = FOX_SUB * TQ
    n_chains = FOX_SUB * HEADS_PER_LANE_TILE
    block_bytes = (2 * (HEADS_PER_LANE_TILE * (rows + seq) * LANES_V7X + seq * LANES_V7X + rows * LANES_V7X)
                   + 4 * n_chains * TQ * LANES_V7X)
    smem = pl.BlockSpec(memory_space=pltpu.SMEM)
    return pl.pallas_call(
        _fox_kernel,
        out_shape=jax.ShapeDtypeStruct((batch, seq, D_MODEL), BF16),
        grid=(batch, N_HEAD_PAIRS, seq // rows),
        in_specs=[
            smem, smem, smem,
            pl.BlockSpec((None, HEADS_PER_LANE_TILE, rows, LANES_V7X), lambda b, p, i: (b, p, i, 0)),
            pl.BlockSpec((None, HEADS_PER_LANE_TILE, seq, LANES_V7X), lambda b, p, i: (b, p, 0, 0)),
            pl.BlockSpec((None, seq, LANES_V7X), lambda b, p, i: (b, 0, p)),
        ],
        out_specs=pl.BlockSpec((None, rows, LANES_V7X), lambda b, p, i: (b, i, p)),
        scratch_shapes=[pltpu.VMEM((n_chains, TQ, LANES_V7X), F32)],
        compiler_params=pltpu.CompilerParams(
            dimension_semantics=("arbitrary", "arbitrary", "arbitrary"),
            vmem_limit_bytes=_vmem_limit(block_bytes)),
        name="fox_attn",
    )(walk_bound, fk_block_start, max_reach, q_aug, k_aug, v)


def kernel(x, c, ada_w, ada_b, norm_attn_g, norm_ffn_g, w_ffn_in, w_ffn_down, sb_w_qkv, sb_w_o,
           kv_ada_w, kv_ada_b, kv_norm_g, w_kvf, b_f, k_norm_g, fox_w_q, q_norm_g, fox_w_o):
    batch, seq, d = x.shape
    assert d == D_MODEL and seq % TM == 0
    assert seq % (FOX_SUB * TQ) == 0 and seq % (SB_SUB * SB_T) == 0
    assert ada_w.shape[0] == 2 and sb_w_qkv.shape[0] == 1 and fox_w_q.shape[0] == 1
    m = batch * seq

    c_pad = jnp.zeros((SUBLANES_V7X, d), F32).at[:batch].set(c.astype(F32))
    mod = _ada_mod(c_pad, ada_w, ada_b)[:, :batch]
    kv_mod = _ada_mod(c_pad, kv_ada_w[None], kv_ada_b[None])[0, :batch]

    def vecs(t, n):
        return [t[:, None, j * d:(j + 1) * d] for j in range(n)]

    row = lambda t: t.reshape(1, -1).astype(F32)
    bf = lambda t: t.astype(BF16)

    x2d = x.reshape(m, d)

    sh_a, sc_a, g_a, sh_f, sc_f, g_f = vecs(mod[0], 6)
    qkv_scale = jnp.concatenate([jnp.full((1, d), LOG2_E * HEAD_DIM ** -0.5, F32),
                                 jnp.ones((1, 2 * d), F32)], axis=1)
    qkv = _norm_matmul(x2d, row(norm_attn_g[0]), sh_a, sc_a, bf(sb_w_qkv[0]), qkv_scale, seq)
    o = _sb_attn(qkv.reshape(batch, seq, 3 * d), batch, seq)
    x2d, act = _out_ffn_in(o.reshape(m, d), bf(sb_w_o[0]), x2d, g_a,
                           row(norm_ffn_g[0]), sh_f, sc_f, bf(w_ffn_in[0]), seq)
    x2d = _proj_residual(act, bf(w_ffn_down[0]), x2d, g_f, seq)

    sh_a, sc_a, g_a, sh_f, sc_f, g_f = vecs(mod[1], 6)
    kv_shift, kv_scale = vecs(kv_mod, 2)
    w_f_pad = jnp.zeros((d, LANES_V7X), F32).at[:, :N_HEADS].set(w_kvf[:, 2 * d:])
    b_f_pad = jnp.zeros((1, LANES_V7X), F32).at[0, :N_HEADS].set(b_f)
    k, v, q, log_f = _l1_proj(
        x2d, row(kv_norm_g), kv_shift, kv_scale, row(norm_attn_g[1]), sh_a, sc_a,
        bf(w_kvf[:, :d]), bf(w_kvf[:, d:2 * d]), bf(w_f_pad), bf(fox_w_q[0]),
        row(jnp.tile(k_norm_g, N_HEADS)), row(jnp.tile(q_norm_g[0], N_HEADS)), b_f_pad, seq)
    key_norm_bound = (1.01 * HEAD_DIM ** 0.5 * jnp.max(jnp.abs(k_norm_g))).reshape(1, 1).astype(F32)
    q_aug, k_aug, stats = _fox_prep(log_f, q, k, key_norm_bound, batch, seq)
    tables = stats[:, :, :3, :N_HEADS].transpose(2, 0, 3, 1).reshape(3, batch * N_HEADS, seq // TK)
    o = _fox_attn(q_aug, k_aug, v.reshape(batch, seq, d), tables[0], tables[1], tables[2], batch, seq)
    x2d, act = _out_ffn_in(o.reshape(m, d), bf(fox_w_o[0]), x2d, g_a,
                           row(norm_ffn_g[1]), sh_f, sc_f, bf(w_ffn_in[1]), seq)
    x2d = _proj_residual(act, bf(w_ffn_down[1]), x2d, g_f, seq)
    return x2d.reshape(batch, seq, d)
```

```python
import functools

import jax
import jax.numpy as jnp
from jax import lax
from jax.experimental import pallas as pl
from jax.experimental.pallas import tpu as pltpu

D_MODEL = 1024
N_HEADS = 16
HEAD_DIM = D_MODEL // N_HEADS
EPS = 1e-6

LANES_V7X = 128
SUBLANES_V7X = 8
VMEM_BYTES_V7X = 64 * 1024 * 1024

HEADS_PER_LANE_TILE = LANES_V7X // HEAD_DIM
N_HEAD_PAIRS = N_HEADS // HEADS_PER_LANE_TILE

MXU_WIDTH_V7X = 256

TM = 512
COL_CHUNK = 2 * MXU_WIDTH_V7X
TQ = 512
TK = 512
FOX_SUB = 4
SB_T = 256
SB_SUB = 4

ZERO_EXP = 104.5
SAFE_EXP_RANGE = 80.0
NEG_BIG = -1e30
LOG2_E = 1.4426950408889634

F32 = jnp.float32
BF16 = jnp.bfloat16


def _vmem_limit(block_bytes):
    want = 2 * block_bytes + 16 * 1024 * 1024
    return int(min(want, VMEM_BYTES_V7X - 8 * 1024 * 1024))


def _split2(x):
    hi = x.astype(BF16)
    lo = (x - hi.astype(F32)).astype(BF16)
    return hi, lo


def _dot(a, b):
    return jnp.dot(a, b, preferred_element_type=F32)


def _dot_nt(a, b):
    return lax.dot_general(a, b, (((1,), (1,)), ((), ())), preferred_element_type=F32)


def _ada_kernel(c_ref, w_ref, b_ref, o_ref):
    c = c_ref[...]
    c_act = c * jax.nn.sigmoid(c)
    o_ref[...] = jnp.dot(c_act, w_ref[...], precision=lax.Precision.HIGHEST,
                         preferred_element_type=F32) + b_ref[...]


def _ada_mod(c_pad, w, b):
    n_layers, d, n = w.shape
    tn = 1024
    rows = c_pad.shape[0]
    block_bytes = 4 * (d * tn + rows * d + tn + rows * tn)
    return pl.pallas_call(
        _ada_kernel,
        out_shape=jax.ShapeDtypeStruct((n_layers, rows, n), F32),
        grid=(n_layers, n // tn),
        in_specs=[
            pl.BlockSpec((rows, d), lambda l, j: (0, 0)),
            pl.BlockSpec((None, d, tn), lambda l, j: (l, 0, j)),
            pl.BlockSpec((None, 1, tn), lambda l, j: (l, 0, j)),
        ],
        out_specs=pl.BlockSpec((None, rows, tn), lambda l, j: (l, 0, j)),
        compiler_params=pltpu.CompilerParams(
            dimension_semantics=("arbitrary", "arbitrary"),
            vmem_limit_bytes=_vmem_limit(block_bytes)),
        name="ada_mod",
    )(c_pad, w, b.reshape(n_layers, 1, n))


def _norm_modulate(x, g, shift, scale):
    ms = jnp.mean(x * x, axis=-1, keepdims=True)
    y = x * lax.rsqrt(ms + EPS)
    return (y * g) * (1.0 + scale) + shift


def _row_spec(d):
    return pl.BlockSpec((1, d), lambda *_: (0, 0))


def _batch_vec_spec(d, tiles_per_batch):
    return pl.BlockSpec((None, 1, d), lambda i, *_: (i // tiles_per_batch, 0, 0))


def _column_chunks(n):
    return [(c, min(c + COL_CHUNK, n)) for c in range(0, n, COL_CHUNK)]


def _norm_matmul_kernel(x_ref, g_ref, sh_ref, sc_ref, w_ref, cs_ref, o_ref):
    h = _norm_modulate(x_ref[...], g_ref[...], sh_ref[...], sc_ref[...]).astype(BF16)
    for c0, c1 in _column_chunks(o_ref.shape[1]):
        o_ref[:, c0:c1] = (_dot(h, w_ref[:, c0:c1]) * cs_ref[:, c0:c1]).astype(o_ref.dtype)


def _norm_matmul(x2d, g, shift, scale, w, col_scale, seq):
    m, d = x2d.shape
    n = w.shape[1]
    block_bytes = 4 * TM * d + 2 * d * n + 2 * TM * n + 4 * n
    return pl.pallas_call(
        _norm_matmul_kernel,
        out_shape=jax.ShapeDtypeStruct((m, n), BF16),
        grid=(m // TM,),
        in_specs=[
            pl.BlockSpec((TM, d), lambda i: (i, 0)),
            _row_spec(d),
            _batch_vec_spec(d, seq // TM),
            _batch_vec_spec(d, seq // TM),
            pl.BlockSpec((d, n), lambda i: (0, 0)),
            pl.BlockSpec((1, n), lambda i: (0, 0)),
        ],
        out_specs=pl.BlockSpec((TM, n), lambda i: (i, 0)),
        compiler_params=pltpu.CompilerParams(
            dimension_semantics=("arbitrary",),
            vmem_limit_bytes=_vmem_limit(block_bytes)),
        name="norm_matmul",
    )(x2d, g, shift, scale, w, col_scale)


def _out_ffn_in_kernel(a_ref, wo_ref, x_ref, gate_ref, g_ref, sh_ref, sc_ref, wg_ref, wu_ref,
                       x1_ref, act_ref):
    x1 = x_ref[...] + gate_ref[...] * _dot(a_ref[...], wo_ref[...])
    x1_ref[...] = x1
    h = _norm_modulate(x1, g_ref[...], sh_ref[...], sc_ref[...]).astype(BF16)
    for c0, c1 in _column_chunks(act_ref.shape[1]):
        gate = _dot(h, wg_ref[:, c0:c1])
        up = _dot(h, wu_ref[:, c0:c1])
        act_ref[:, c0:c1] = (gate * jax.nn.sigmoid(gate) * up).astype(act_ref.dtype)


def _out_ffn_in(a, w_o, x2d, gate, g, shift, scale, w_in, seq):
    m, d = x2d.shape
    f = w_in.shape[1] // 2
    tpb = seq // TM
    block_bytes = 2 * TM * d + 2 * d * d + 2 * 4 * TM * d + 2 * 2 * d * f + 2 * TM * f
    return pl.pallas_call(
        _out_ffn_in_kernel,
        out_shape=(jax.ShapeDtypeStruct((m, d), F32), jax.ShapeDtypeStruct((m, f), BF16)),
        grid=(m // TM,),
        in_specs=[
            pl.BlockSpec((TM, d), lambda i: (i, 0)),
            pl.BlockSpec((d, d), lambda i: (0, 0)),
            pl.BlockSpec((TM, d), lambda i: (i, 0)),
            _batch_vec_spec(d, tpb),
            _row_spec(d),
            _batch_vec_spec(d, tpb),
            _batch_vec_spec(d, tpb),
            pl.BlockSpec((d, f), lambda i: (0, 0)),
            pl.BlockSpec((d, f), lambda i: (0, 1)),
        ],
        out_specs=(pl.BlockSpec((TM, d), lambda i: (i, 0)), pl.BlockSpec((TM, f), lambda i: (i, 0))),
        compiler_params=pltpu.CompilerParams(
            dimension_semantics=("arbitrary",),
            vmem_limit_bytes=_vmem_limit(block_bytes)),
        name="out_ffn_in",
    )(a, w_o, x2d, gate, g, shift, scale, w_in, w_in)


def _proj_residual_kernel(a_ref, w_ref, x_ref, gate_ref, o_ref):
    o_ref[...] = x_ref[...] + gate_ref[...] * _dot(a_ref[...], w_ref[...])


def _proj_residual(a, w, x2d, gate, seq):
    m, k = a.shape
    d = w.shape[1]
    block_bytes = 2 * TM * k + 2 * k * d + 2 * 4 * TM * d
    return pl.pallas_call(
        _proj_residual_kernel,
        out_shape=jax.ShapeDtypeStruct((m, d), F32),
        grid=(m // TM,),
        in_specs=[
            pl.BlockSpec((TM, k), lambda i: (i, 0)),
            pl.BlockSpec((k, d), lambda i: (0, 0)),
            pl.BlockSpec((TM, d), lambda i: (i, 0)),
            _batch_vec_spec(d, seq // TM),
        ],
        out_specs=pl.BlockSpec((TM, d), lambda i: (i, 0)),
        compiler_params=pltpu.CompilerParams(
            dimension_semantics=("arbitrary",),
            vmem_limit_bytes=_vmem_limit(block_bytes)),
        name="proj_residual",
    )(a, w, x2d, gate)


def _head_rms_norm(t, gain, group_mean):
    ms = _dot((t * t).astype(BF16), group_mean)
    return t * lax.rsqrt(ms + EPS) * gain


def _l1_proj_kernel(act_ref, wd_ref, x_ref, gate_ref,
                    gkv_ref, shkv_ref, sckv_ref, gq_ref, shq_ref, scq_ref,
                    wk_ref, wv_ref, wf_ref, wq_ref, kg_ref, qg_ref, bf_ref, gm_ref,
                    x2_ref, k_ref, v_ref, q_ref, lf_ref):
    x = x_ref[...] + gate_ref[...] * _dot(act_ref[...], wd_ref[...])
    x2_ref[...] = x
    ms = jnp.mean(x * x, axis=-1, keepdims=True)
    y = x * lax.rsqrt(ms + EPS)
    h_kv = ((y * gkv_ref[...]) * (1.0 + sckv_ref[...]) + shkv_ref[...]).astype(BF16)
    h_q = ((y * gq_ref[...]) * (1.0 + scq_ref[...]) + shq_ref[...]).astype(BF16)
    gm = gm_ref[...]
    k_all = _dot(h_kv, wk_ref[...])
    q_all = _dot(h_q, wq_ref[...])
    v_ref[...] = _dot(h_kv, wv_ref[...]).astype(BF16)
    lf_ref[...] = jax.nn.log_sigmoid(_dot(h_kv, wf_ref[...]) + bf_ref[...])
    for c0 in range(0, k_ref.shape[1], MXU_WIDTH_V7X):
        cols = slice(c0, c0 + MXU_WIDTH_V7X)
        k_ref[:, cols] = _head_rms_norm(k_all[:, cols], kg_ref[:, cols], gm).astype(BF16)
        q = _head_rms_norm(q_all[:, cols], qg_ref[:, cols], gm)
        q_ref[:, cols] = (q * (HEAD_DIM ** -0.5)).astype(BF16)


def _l1_proj(act, w_down, x2d, gate, g_kv, sh_kv, sc_kv, g_q, sh_q, sc_q, wk, wv, wf, wq,
             k_gain, q_gain, b_f, seq):
    m, d = x2d.shape
    f = act.shape[1]
    lane = jnp.arange(MXU_WIDTH_V7X)
    group_mean = jnp.where(lane[:, None] // HEAD_DIM == lane[None, :] // HEAD_DIM,
                           1.0 / HEAD_DIM, 0.0).astype(BF16)
    tpb = seq // TM
    block_bytes = (2 * TM * f + 2 * f * d + 2 * 4 * TM * d + 3 * 2 * d * d + 2 * d * LANES_V7X
                   + 3 * 2 * TM * d + 4 * TM * LANES_V7X)
    full = lambda r, c: pl.BlockSpec((r, c), lambda i: (0, 0))
    rows = lambda c: pl.BlockSpec((TM, c), lambda i: (i, 0))
    return pl.pallas_call(
        _l1_proj_kernel,
        out_shape=(jax.ShapeDtypeStruct((m, d), F32), jax.ShapeDtypeStruct((m, d), BF16),
                   jax.ShapeDtypeStruct((m, d), BF16), jax.ShapeDtypeStruct((m, d), BF16),
                   jax.ShapeDtypeStruct((m, LANES_V7X), F32)),
        grid=(m // TM,),
        in_specs=[
            rows(f), full(f, d), rows(d), _batch_vec_spec(d, tpb),
            _row_spec(d), _batch_vec_spec(d, tpb), _batch_vec_spec(d, tpb),
            _row_spec(d), _batch_vec_spec(d, tpb), _batch_vec_spec(d, tpb),
            full(d, d), full(d, d), full(d, LANES_V7X), full(d, d),
            _row_spec(d), _row_spec(d), _row_spec(LANES_V7X), full(MXU_WIDTH_V7X, MXU_WIDTH_V7X),
        ],
        out_specs=(rows(d), rows(d), rows(d), rows(d), rows(LANES_V7X)),
        compiler_params=pltpu.CompilerParams(
            dimension_semantics=("arbitrary",),
            vmem_limit_bytes=_vmem_limit(block_bytes)),
        name="l1_proj",
    )(act, w_down, x2d, gate, g_kv, sh_kv, sc_kv, g_q, sh_q, sc_q, wk, wv, wf, wq,
      k_gain, q_gain, b_f, group_mean)


N_SPLIT = 3


def _split3(x):
    x0 = x.astype(BF16)
    r1 = x - x0.astype(F32)
    x1 = r1.astype(BF16)
    x2 = (r1 - x1.astype(F32)).astype(BF16)
    return x0, x1, x2


def _fox_prep_kernel(kb_ref, lf_ref, q_ref, k_ref, qa_ref, ka_ref, stats_ref, carry_scr):
    @pl.when(pl.program_id(1) == 0)
    def _():
        carry_scr[...] = jnp.zeros_like(carry_scr)

    r = lax.broadcasted_iota(jnp.int32, (TK, TK), 0)
    c = lax.broadcasted_iota(jnp.int32, (TK, TK), 1)
    tri = jnp.where(c <= r, 1.0, 0.0).astype(BF16)
    x0, x1, x2 = _split3(lf_ref[...])
    cs = _dot(tri, x0) + _dot(tri, x1) + _dot(tri, x2) + carry_scr[0:1, :]
    carry_scr[...] = jnp.broadcast_to(cs[TK - 1:TK, :], carry_scr.shape)

    d_idx = lax.broadcasted_iota(jnp.int32, (D_MODEL, LANES_V7X), 0)
    h_idx = lax.broadcasted_iota(jnp.int32, (D_MODEL, LANES_V7X), 1)
    head_of = jnp.where(d_idx // HEAD_DIM == h_idx, 1.0, 0.0).astype(BF16)
    qf = q_ref[...].astype(F32)
    reach = jnp.sqrt(_dot((qf * qf).astype(BF16), head_of)) * kb_ref[0, 0]
    own_logit = _dot((qf * k_ref[...].astype(F32)).astype(BF16), head_of)

    stats_ref[...] = jnp.concatenate([
        jnp.max(reach * (1.0 + 2.0 ** -8) - own_logit + cs, axis=0, keepdims=True),
        cs[0:1, :],
        jnp.max(reach, axis=0, keepdims=True),
        jnp.zeros((SUBLANES_V7X - 3, LANES_V7X), F32)], axis=0)

    lane = lax.broadcasted_iota(jnp.int32, (1, LANES_V7X), 1)
    packed = jnp.zeros((TK, LANES_V7X), F32)
    for i, term in enumerate(_split3(cs - reach) + _split3(-cs)):
        term = jnp.where(lane < N_HEADS, term.astype(F32), 0.0)
        packed = packed + (term if i == 0 else pltpu.roll(term, N_HEADS * i, axis=1))
    packed = packed.astype(BF16)

    src = lax.broadcasted_iota(jnp.int32, (LANES_V7X, 2 * LANES_V7X), 0)
    dst = lax.broadcasted_iota(jnp.int32, (LANES_V7X, 2 * LANES_V7X), 1)
    for h in range(N_HEADS):
        pair, parity = divmod(h, HEADS_PER_LANE_TILE)
        spare = HEAD_DIM * (1 - parity)
        place = jnp.where((src % N_HEADS == h) & (src < 2 * N_SPLIT * N_HEADS)
                          & (dst % LANES_V7X == spare + src // N_HEADS)
                          & ((dst >= LANES_V7X) == (src >= N_SPLIT * N_HEADS)), 1.0, 0.0)
        placed = _dot(packed, place.astype(BF16))
        ones_q = jnp.where((lane >= spare + N_SPLIT) & (lane < spare + 2 * N_SPLIT), 1.0, 0.0)
        ones_k = jnp.where((lane >= spare) & (lane < spare + N_SPLIT), 1.0, 0.0)
        aug_q = (placed[:, :LANES_V7X] + ones_q).astype(BF16)
        aug_k = (placed[:, LANES_V7X:] + ones_k).astype(BF16)
        own = (lane // HEAD_DIM) == parity
        cols = slice(pair * LANES_V7X, (pair + 1) * LANES_V7X)
        qa_ref[h] = jnp.where(own, q_ref[:, cols], aug_q)
        ka_ref[h] = jnp.where(own, k_ref[:, cols], aug_k)


def _fox_prep(log_f, q, k, key_norm_bound, batch, seq):
    assert TQ == TK
    d = q.shape[1]
    block_bytes = (4 * TK * LANES_V7X + 2 * 2 * TK * d + 2 * 2 * N_HEADS * TK * LANES_V7X
                   + 4 * SUBLANES_V7X * LANES_V7X)
    aug_shape = jax.ShapeDtypeStruct((batch, N_HEADS, seq, LANES_V7X), BF16)
    aug_spec = pl.BlockSpec((None, N_HEADS, TK, LANES_V7X), lambda b, t: (b, 0, t, 0))
    return pl.pallas_call(
        _fox_prep_kernel,
        out_shape=(aug_shape, aug_shape,
                   jax.ShapeDtypeStruct((batch, seq // TK, SUBLANES_V7X, LANES_V7X), F32)),
        grid=(batch, seq // TK),
        in_specs=[
            pl.BlockSpec(memory_space=pltpu.SMEM),
            pl.BlockSpec((None, TK, LANES_V7X), lambda b, t: (b, t, 0)),
            pl.BlockSpec((None, TK, d), lambda b, t: (b, t, 0)),
            pl.BlockSpec((None, TK, d), lambda b, t: (b, t, 0)),
        ],
        out_specs=(aug_spec, aug_spec,
                   pl.BlockSpec((None, None, SUBLANES_V7X, LANES_V7X), lambda b, t: (b, t, 0, 0))),
        scratch_shapes=[pltpu.VMEM((SUBLANES_V7X, LANES_V7X), F32)],
        compiler_params=pltpu.CompilerParams(
            dimension_semantics=("arbitrary", "arbitrary"),
            vmem_limit_bytes=_vmem_limit(block_bytes)),
        name="fox_prep",
    )(key_norm_bound, log_f.reshape(batch, seq, LANES_V7X), q.reshape(batch, seq, d),
      k.reshape(batch, seq, d))


def _head_select(head):
    lane = lax.broadcasted_iota(jnp.int32, (1, LANES_V7X), 1)
    return (lane // HEAD_DIM) == head


def _widen(col128, width):
    reps = width // LANES_V7X
    return col128 if reps == 1 else jnp.concatenate([col128] * reps, axis=1)


def _sb_kernel(q_ref, k_ref, v_ref, o_ref):
    i = pl.program_id(2)
    row = lax.broadcasted_iota(jnp.int32, (SB_T, SB_T), 0)
    col = lax.broadcasted_iota(jnp.int32, (SB_T, SB_T), 1)
    diag_mask = col < row
    after = jnp.where(row > col, 1.0, 0.0).astype(BF16)
    ones = jnp.ones((SB_T, LANES_V7X), BF16)
    ones_wide = jnp.ones((SB_T, SB_T), BF16)

    def softplus(z):
        return jnp.maximum(z, 0.0) + jnp.log2(1.0 + jnp.exp2(-jnp.abs(z)))

    def keys(j):
        return k_ref[pl.ds(j * SB_T, SB_T), :], v_ref[pl.ds(j * SB_T, SB_T), :]

    def band(qh, g):
        k_d, v_d = keys(g)
        k_p, v_p = keys(jnp.maximum(g - 1, 0))
        z_d = _dot_nt(qh, k_d)
        z_p = jnp.where(g > 0, _dot_nt(qh, k_p), NEG_BIG)
        sp_d = jnp.where(diag_mask, softplus(z_d), 0.0)
        sp_p = softplus(z_p)
        sp_d16 = sp_d.astype(BF16)
        within_d = _dot(sp_d16, after)
        within_p = _dot(sp_p.astype(BF16), after) + _dot(sp_d16, ones_wide)
        a_d = jnp.where(diag_mask, jnp.exp2((z_d - sp_d) - within_d), 0.0)
        a_p = jnp.exp2((z_p - sp_p) - within_p)
        acc = _dot(a_d.astype(BF16), v_d) + _dot(a_p.astype(BF16), v_p)
        total = (within_p + sp_p)[:, 0:1]
        return jnp.broadcast_to(-total, (SB_T, LANES_V7X)), acc

    def block(qh, j, carry, acc):
        kj, vj = keys(j)
        z = _dot_nt(qh, kj)
        sp = softplus(z)
        sp16 = sp.astype(BF16)
        within = _dot(sp16, after)
        a = jnp.exp2((z - sp) - within + _widen(carry, SB_T))
        acc = acc + _dot(a.astype(BF16), vj)
        carry = carry - _dot(sp16, ones)
        return carry, acc

    chains = []
    for u in range(SB_SUB):
        g = i * SB_SUB + u
        q = q_ref[u * SB_T:(u + 1) * SB_T, :]
        for head in range(HEADS_PER_LANE_TILE):
            qh = jnp.where(_head_select(head), q, 0.0).astype(BF16)
            chains.append((g, qh) + band(qh, g))

    def is_open(carry):
        return jnp.max(carry) > -ZERO_EXP * LOG2_E

    chain_open = [is_open(carry) for _, _, carry, _ in chains]

    def walk_on():
        outs = []
        for (g, qh, carry, acc), this_open in zip(chains, chain_open):
            def cond(state):
                j, carry, _ = state
                return jnp.logical_and(j >= 0, is_open(carry))

            def body(state, qh=qh):
                j, carry, acc = state
                carry, acc = block(qh, j, carry, acc)
                return j - 1, carry, acc

            def walk(cond=cond, body=body, start=(g - 2, carry, acc)):
                return lax.while_loop(cond, body, start)[2]

            outs.append(lax.cond(this_open, walk, lambda acc=acc: acc))
        return outs

    outs = lax.cond(functools.reduce(jnp.logical_or, chain_open), walk_on,
                    lambda: [acc for _, _, _, acc in chains])
    for u in range(SB_SUB):
        pair = outs[u * HEADS_PER_LANE_TILE:(u + 1) * HEADS_PER_LANE_TILE]
        o_ref[u * SB_T:(u + 1) * SB_T, :] = jnp.where(_head_select(0), pair[0], pair[1]).astype(o_ref.dtype)


def _sb_attn(qkv, batch, seq):
    rows = SB_SUB * SB_T
    block_bytes = 2 * (2 * rows * LANES_V7X + 2 * seq * LANES_V7X)
    return pl.pallas_call(
        _sb_kernel,
        out_shape=jax.ShapeDtypeStruct((batch, seq, D_MODEL), BF16),
        grid=(batch, N_HEAD_PAIRS, seq // rows),
        in_specs=[
            pl.BlockSpec((None, rows, LANES_V7X), lambda b, p, i: (b, i, p)),
            pl.BlockSpec((None, seq, LANES_V7X), lambda b, p, i: (b, 0, N_HEAD_PAIRS + p)),
            pl.BlockSpec((None, seq, LANES_V7X), lambda b, p, i: (b, 0, 2 * N_HEAD_PAIRS + p)),
        ],
        out_specs=pl.BlockSpec((None, rows, LANES_V7X), lambda b, p, i: (b, i, p)),
        compiler_params=pltpu.CompilerParams(
            dimension_semantics=("arbitrary", "arbitrary", "arbitrary"),
            vmem_limit_bytes=_vmem_limit(block_bytes)),
        name="sb_attn",
    )(qkv, qkv, qkv)


def _fox_kernel(a_ref, fk0_ref, g_ref, qa_ref, ka_ref, v_ref, o_ref, acc_scr):
    b = pl.program_id(0)
    p_id = pl.program_id(1)
    i = pl.program_id(2)
    n_blocks = fk0_ref.shape[1]
    row = lax.broadcasted_iota(jnp.int32, (TQ, TK), 0)
    col = lax.broadcasted_iota(jnp.int32, (TQ, TK), 1)
    diag_mask = col <= row

    acc0 = jnp.zeros((TQ, LANES_V7X), F32)
    chains = range(FOX_SUB * HEADS_PER_LANE_TILE)
    bh0 = (b * N_HEAD_PAIRS + p_id) * HEADS_PER_LANE_TILE

    def scores(chain, j):
        sub, head = chain // HEADS_PER_LANE_TILE, chain % HEADS_PER_LANE_TILE
        q = qa_ref[head, pl.ds(pl.multiple_of(sub * TQ, TQ), TQ), :]
        return _dot_nt(q, ka_ref[head, pl.ds(pl.multiple_of(j * TK, TK), TK), :])

    def values(chain, j):
        vj = v_ref[pl.ds(pl.multiple_of(j * TK, TK), TK), :]
        own = _head_select(chain % HEADS_PER_LANE_TILE)
        return jnp.where(own, vj, jnp.ones_like(vj))

    def diag_block(chain):
        return i * FOX_SUB + chain // HEADS_PER_LANE_TILE

    def count_left(chain):
        bh, qi = bh0 + chain % HEADS_PER_LANE_TILE, diag_block(chain)
        bound = a_ref[bh, qi]

        def more(n):
            j = qi - 1 - n
            right = fk0_ref[bh, jnp.minimum(j + 1, n_blocks - 1)]
            return jnp.logical_and(j >= 0, bound - right > -ZERO_EXP)

        return lax.while_loop(more, lambda n: n + 1, jnp.int32(0))

    def fast_walks():
        counts = [count_left(chain) for chain in chains]
        firsts = [sum(n + 1 for n in counts[:chain]) for chain in chains]
        n_items = firsts[-1] + counts[-1] + 1
        diag_bias = jnp.where(diag_mask, 0.0, NEG_BIG)
        acc_scr[...] = jnp.zeros_like(acc_scr)

        def step(t, p, p_chain, p_block):
            chain = sum(jnp.where(t >= first, 1, 0) for first in firsts[1:])
            first = functools.reduce(lambda f, nxt: jnp.where(t >= nxt, nxt, f), firsts[1:], 0)
            back = t - first
            block = jnp.maximum(diag_block(chain) - back, 0)
            s = scores(chain, block) + jnp.where(back == 0, diag_bias,
                                                 jnp.where(t >= n_items, NEG_BIG, 0.0))
            acc_scr[p_chain] += _dot(p, values(p_chain, p_block))
            return jnp.exp(s).astype(BF16), chain, block

        def steps(first, count, carried):
            for t in range(count):
                carried = step(first + t, *carried)
            return carried

        pairs = (n_items + 2) // 2
        start = (jnp.zeros((TQ, TK), BF16), jnp.int32(0), jnp.int32(0))
        carried = lax.fori_loop(0, pairs // 2, lambda u, c: steps(4 * u, 4, c), start)
        lax.fori_loop(0, pairs % 2, lambda _, c: steps(4 * (pairs // 2), 2, c), carried)
        return [acc_scr[chain] for chain in chains]

    def online_walks():
        accs = []
        for chain in chains:
            def body(t, carried, chain=chain):
                acc, m = carried
                block = diag_block(chain) - t
                s = jnp.where(jnp.logical_or(t > 0, diag_mask), scores(chain, block), NEG_BIG)
                m_new = jnp.maximum(m, jnp.max(s, axis=-1, keepdims=True))
                p = jnp.exp(s - m_new).astype(BF16)
                return jnp.exp(m - m_new) * acc + _dot(p, values(chain, block)), m_new

            m0 = jnp.full((TQ, 1), NEG_BIG, F32)
            accs.append(lax.fori_loop(0, count_left(chain) + 1, body, (acc0, m0))[0])
        return accs

    largest_reach = functools.reduce(jnp.maximum, [
        g_ref[bh0 + chain % HEADS_PER_LANE_TILE, diag_block(chain)] for chain in chains])
    accs = lax.cond(2.0 * largest_reach < SAFE_EXP_RANGE, fast_walks, online_walks)
    outs = [acc / pltpu.roll(acc, HEAD_DIM, axis=1) for acc in accs]
    for sub in range(FOX_SUB):
        pair = outs[sub * HEADS_PER_LANE_TILE:(sub + 1) * HEADS_PER_LANE_TILE]
        o_ref[sub * TQ:(sub + 1) * TQ, :] = jnp.where(_head_select(0), pair[0], pair[1]).astype(o_ref.dtype)


def _fox_attn(q_aug, k_aug, v, walk_bound, fk_block_start, max_reach, batch, seq):
    assert TQ == TK
    rows = FOX_SUB * TQ
    n_chains = FOX_SUB * HEADS_PER_LANE_TILE
    block_bytes = (2 * (HEADS_PER_LANE_TILE * (rows + seq) * LANES_V7X + seq * LANES_V7X + rows * LANES_V7X)
                   + 4 * n_chains * TQ * LANES_V7X)
    smem = pl.BlockSpec(memory_space=pltpu.SMEM)
    return pl.pallas_call(
        _fox_kernel,
        out_shape=jax.ShapeDtypeStruct((batch, seq, D_MODEL), BF16),
        grid=(batch, N_HEAD_PAIRS, seq // rows),
        in_specs=[
            smem, smem, smem,
            pl.BlockSpec((None, HEADS_PER_LANE_TILE, rows, LANES_V7X), lambda b, p, i: (b, p, i, 0)),
            pl.BlockSpec((None, HEADS_PER_LANE_TILE, seq, LANES_V7X), lambda b, p, i: (b, p, 0, 0)),
            pl.BlockSpec((None, seq, LANES_V7X), lambda b, p, i: (b, 0, p)),
        ],
        out_specs=pl.BlockSpec((None, rows, LANES_V7X), lambda b, p, i: (b, i, p)),
        scratch_shapes=[pltpu.VMEM((n_chains, TQ, LANES_V7X), F32)],
        compiler_params=pltpu.CompilerParams(
            dimension_semantics=("arbitrary", "arbitrary", "arbitrary"),
            vmem_limit_bytes=_vmem_limit(block_bytes)),
        name="fox_attn",
    )(walk_bound, fk_block_start, max_reach, q_aug, k_aug, v)


def kernel(x, c, ada_w, ada_b, norm_attn_g, norm_ffn_g, w_ffn_in, w_ffn_down, sb_w_qkv, sb_w_o,
           kv_ada_w, kv_ada_b, kv_norm_g, w_kvf, b_f, k_norm_g, fox_w_q, q_norm_g, fox_w_o):
    batch, seq, d = x.shape
    assert d == D_MODEL and seq % TM == 0
    assert seq % (FOX_SUB * TQ) == 0 and seq % (SB_SUB * SB_T) == 0
    assert ada_w.shape[0] == 2 and sb_w_qkv.shape[0] == 1 and fox_w_q.shape[0] == 1
    m = batch * seq

    c_pad = jnp.zeros((SUBLANES_V7X, d), F32).at[:batch].set(c.astype(F32))
    mod = _ada_mod(c_pad, ada_w, ada_b)[:, :batch]
    kv_mod = _ada_mod(c_pad, kv_ada_w[None], kv_ada_b[None])[0, :batch]

    def vecs(t, n):
        return [t[:, None, j * d:(j + 1) * d] for j in range(n)]

    row = lambda t: t.reshape(1, -1).astype(F32)
    bf = lambda t: t.astype(BF16)

    x2d = x.reshape(m, d)

    sh_a, sc_a, g_a, sh_f, sc_f, g_f = vecs(mod[0], 6)
    qkv_scale = jnp.concatenate([jnp.full((1, d), LOG2_E * HEAD_DIM ** -0.5, F32),
                                 jnp.ones((1, 2 * d), F32)], axis=1)
    qkv = _norm_matmul(x2d, row(norm_attn_g[0]), sh_a, sc_a, bf(sb_w_qkv[0]), qkv_scale, seq)
    o = _sb_attn(qkv.reshape(batch, seq, 3 * d), batch, seq)
    x2d, act = _out_ffn_in(o.reshape(m, d), bf(sb_w_o[0]), x2d, g_a,
                           row(norm_ffn_g[0]), sh_f, sc_f, bf(w_ffn_in[0]), seq)
    g_f0 = g_f

    sh_a, sc_a, g_a, sh_f, sc_f, g_f = vecs(mod[1], 6)
    kv_shift, kv_scale = vecs(kv_mod, 2)
    w_f_pad = jnp.zeros((d, LANES_V7X), F32).at[:, :N_HEADS].set(w_kvf[:, 2 * d:])
    b_f_pad = jnp.zeros((1, LANES_V7X), F32).at[0, :N_HEADS].set(b_f)
    x2d, k, v, q, log_f = _l1_proj(
        act, bf(w_ffn_down[0]), x2d, g_f0,
        row(kv_norm_g), kv_shift, kv_scale, row(norm_attn_g[1]), sh_a, sc_a,
        bf(w_kvf[:, :d]), bf(w_kvf[:, d:2 * d]), bf(w_f_pad), bf(fox_w_q[0]),
        row(jnp.tile(k_norm_g, N_HEADS)), row(jnp.tile(q_norm_g[0], N_HEADS)), b_f_pad, seq)
    key_norm_bound = (1.01 * HEAD_DIM ** 0.5 * jnp.max(jnp.abs(k_norm_g))).reshape(1, 1).astype(F32)
    q_aug, k_aug, stats = _fox_prep(log_f, q, k, key_norm_bound, batch, seq)
    tables = stats[:, :, :3, :N_HEADS].transpose(2, 0, 3, 1).reshape(3, batch * N_HEADS, seq // TK)
    o = _fox_attn(q_aug, k_aug, v.reshape(batch, seq, d), tables[0], tables[1], tables[2], batch, seq)
    x2d, act = _out_ffn_in(o.reshape(m, d), bf(fox_w_o[0]), x2d, g_a,
                           row(norm_ffn_g[1]), sh_f, sc_f, bf(w_ffn_in[1]), seq)
    x2d = _proj_residual(act, bf(w_ffn_down[1]), x2d, g_f, seq)
    return x2d.reshape(batch, seq, d)
```

```python
import functools

import jax
import jax.numpy as jnp
from jax import lax
from jax.experimental import pallas as pl
from jax.experimental.pallas import tpu as pltpu

D_MODEL = 1024
N_HEADS = 16
HEAD_DIM = D_MODEL // N_HEADS
EPS = 1e-6

LANES_V7X = 128
SUBLANES_V7X = 8
VMEM_BYTES_V7X = 64 * 1024 * 1024

HEADS_PER_LANE_TILE = LANES_V7X // HEAD_DIM
N_HEAD_PAIRS = N_HEADS // HEADS_PER_LANE_TILE

MXU_WIDTH_V7X = 256

TM = 512
COL_CHUNK = 2 * MXU_WIDTH_V7X
TQ = 512
TK = 512
FOX_SUB = 4
SB_T = 256
SB_SUB = 8

ZERO_EXP = 104.5
SAFE_EXP_RANGE = 80.0
NEG_BIG = -1e30
LOG2_E = 1.4426950408889634
SOFTPLUS_LINEAR = 64.0

F32 = jnp.float32
BF16 = jnp.bfloat16


def _vmem_limit(block_bytes):
    want = 2 * block_bytes + 16 * 1024 * 1024
    return int(min(want, VMEM_BYTES_V7X - 8 * 1024 * 1024))


def _split2(x):
    hi = x.astype(BF16)
    lo = (x - hi.astype(F32)).astype(BF16)
    return hi, lo


def _dot(a, b):
    return jnp.dot(a, b, preferred_element_type=F32)


def _dot_nt(a, b):
    return lax.dot_general(a, b, (((1,), (1,)), ((), ())), preferred_element_type=F32)


def _ada_kernel(c_ref, w_ref, b_ref, o_ref):
    c = c_ref[...]
    c_act = c * jax.nn.sigmoid(c)
    o_ref[...] = jnp.dot(c_act, w_ref[...], precision=lax.Precision.HIGHEST,
                         preferred_element_type=F32) + b_ref[...]


def _ada_mod(c_pad, w, b):
    n_layers, d, n = w.shape
    tn = 1024
    rows = c_pad.shape[0]
    block_bytes = 4 * (d * tn + rows * d + tn + rows * tn)
    return pl.pallas_call(
        _ada_kernel,
        out_shape=jax.ShapeDtypeStruct((n_layers, rows, n), F32),
        grid=(n_layers, n // tn),
        in_specs=[
            pl.BlockSpec((rows, d), lambda l, j: (0, 0)),
            pl.BlockSpec((None, d, tn), lambda l, j: (l, 0, j)),
            pl.BlockSpec((None, 1, tn), lambda l, j: (l, 0, j)),
        ],
        out_specs=pl.BlockSpec((None, rows, tn), lambda l, j: (l, 0, j)),
        compiler_params=pltpu.CompilerParams(
            dimension_semantics=("arbitrary", "arbitrary"),
            vmem_limit_bytes=_vmem_limit(block_bytes)),
        name="ada_mod",
    )(c_pad, w, b.reshape(n_layers, 1, n))


def _norm_modulate(x, g, shift, scale):
    ms = jnp.mean(x * x, axis=-1, keepdims=True)
    y = x * lax.rsqrt(ms + EPS)
    return (y * g) * (1.0 + scale) + shift


def _row_spec(d):
    return pl.BlockSpec((1, d), lambda *_: (0, 0))


def _batch_vec_spec(d, tiles_per_batch):
    return pl.BlockSpec((None, 1, d), lambda i, *_: (i // tiles_per_batch, 0, 0))


def _column_chunks(n):
    return [(c, min(c + COL_CHUNK, n)) for c in range(0, n, COL_CHUNK)]


def _norm_matmul_kernel(x_ref, g_ref, sh_ref, sc_ref, w_ref, cs_ref, o_ref):
    h = _norm_modulate(x_ref[...], g_ref[...], sh_ref[...], sc_ref[...]).astype(BF16)
    for c0, c1 in _column_chunks(o_ref.shape[1]):
        o_ref[:, c0:c1] = (_dot(h, w_ref[:, c0:c1]) * cs_ref[:, c0:c1]).astype(o_ref.dtype)


def _norm_matmul(x2d, g, shift, scale, w, col_scale, seq):
    m, d = x2d.shape
    n = w.shape[1]
    block_bytes = 4 * TM * d + 2 * d * n + 2 * TM * n + 4 * n
    return pl.pallas_call(
        _norm_matmul_kernel,
        out_shape=jax.ShapeDtypeStruct((m, n), BF16),
        grid=(m // TM,),
        in_specs=[
            pl.BlockSpec((TM, d), lambda i: (i, 0)),
            _row_spec(d),
            _batch_vec_spec(d, seq // TM),
            _batch_vec_spec(d, seq // TM),
            pl.BlockSpec((d, n), lambda i: (0, 0)),
            pl.BlockSpec((1, n), lambda i: (0, 0)),
        ],
        out_specs=pl.BlockSpec((TM, n), lambda i: (i, 0)),
        compiler_params=pltpu.CompilerParams(
            dimension_semantics=("arbitrary",),
            vmem_limit_bytes=_vmem_limit(block_bytes)),
        name="norm_matmul",
    )(x2d, g, shift, scale, w, col_scale)


def _out_ffn_in_kernel(a_ref, wo_ref, x_ref, gate_ref, g_ref, sh_ref, sc_ref, wg_ref, wu_ref,
                       x1_ref, act_ref):
    x1 = x_ref[...] + gate_ref[...] * _dot(a_ref[...], wo_ref[...])
    x1_ref[...] = x1
    h = _norm_modulate(x1, g_ref[...], sh_ref[...], sc_ref[...]).astype(BF16)
    for c0, c1 in _column_chunks(act_ref.shape[1]):
        gate = _dot(h, wg_ref[:, c0:c1])
        up = _dot(h, wu_ref[:, c0:c1])
        act_ref[:, c0:c1] = (gate * jax.nn.sigmoid(gate) * up).astype(act_ref.dtype)


def _out_ffn_in(a, w_o, x2d, gate, g, shift, scale, w_in, seq):
    m, d = x2d.shape
    f = w_in.shape[1] // 2
    tpb = seq // TM
    block_bytes = 2 * TM * d + 2 * d * d + 2 * 4 * TM * d + 2 * 2 * d * f + 2 * TM * f
    return pl.pallas_call(
        _out_ffn_in_kernel,
        out_shape=(jax.ShapeDtypeStruct((m, d), F32), jax.ShapeDtypeStruct((m, f), BF16)),
        grid=(m // TM,),
        in_specs=[
            pl.BlockSpec((TM, d), lambda i: (i, 0)),
            pl.BlockSpec((d, d), lambda i: (0, 0)),
            pl.BlockSpec((TM, d), lambda i: (i, 0)),
            _batch_vec_spec(d, tpb),
            _row_spec(d),
            _batch_vec_spec(d, tpb),
            _batch_vec_spec(d, tpb),
            pl.BlockSpec((d, f), lambda i: (0, 0)),
            pl.BlockSpec((d, f), lambda i: (0, 1)),
        ],
        out_specs=(pl.BlockSpec((TM, d), lambda i: (i, 0)), pl.BlockSpec((TM, f), lambda i: (i, 0))),
        compiler_params=pltpu.CompilerParams(
            dimension_semantics=("arbitrary",),
            vmem_limit_bytes=_vmem_limit(block_bytes)),
        name="out_ffn_in",
    )(a, w_o, x2d, gate, g, shift, scale, w_in, w_in)


def _proj_residual_kernel(a_ref, w_ref, x_ref, gate_ref, o_ref):
    o_ref[...] = x_ref[...] + gate_ref[...] * _dot(a_ref[...], w_ref[...])


def _proj_residual(a, w, x2d, gate, seq):
    m, k = a.shape
    d = w.shape[1]
    block_bytes = 2 * TM * k + 2 * k * d + 2 * 4 * TM * d
    return pl.pallas_call(
        _proj_residual_kernel,
        out_shape=jax.ShapeDtypeStruct((m, d), F32),
        grid=(m // TM,),
        in_specs=[
            pl.BlockSpec((TM, k), lambda i: (i, 0)),
            pl.BlockSpec((k, d), lambda i: (0, 0)),
            pl.BlockSpec((TM, d), lambda i: (i, 0)),
            _batch_vec_spec(d, seq // TM),
        ],
        out_specs=pl.BlockSpec((TM, d), lambda i: (i, 0)),
        compiler_params=pltpu.CompilerParams(
            dimension_semantics=("arbitrary",),
            vmem_limit_bytes=_vmem_limit(block_bytes)),
        name="proj_residual",
    )(a, w, x2d, gate)


def _head_rms_norm(t, gain, group_mean):
    ms = _dot((t * t).astype(BF16), group_mean)
    return t * lax.rsqrt(ms + EPS) * gain


def _l1_proj_kernel(act_ref, wd_ref, x_ref, gate_ref,
                    gkv_ref, shkv_ref, sckv_ref, gq_ref, shq_ref, scq_ref,
                    wk_ref, wv_ref, wf_ref, wq_ref, kg_ref, qg_ref, bf_ref, gm_ref,
                    x2_ref, k_ref, v_ref, q_ref, lf_ref):
    x = x_ref[...] + gate_ref[...] * _dot(act_ref[...], wd_ref[...])
    x2_ref[...] = x
    ms = jnp.mean(x * x, axis=-1, keepdims=True)
    y = x * lax.rsqrt(ms + EPS)
    h_kv = ((y * gkv_ref[...]) * (1.0 + sckv_ref[...]) + shkv_ref[...]).astype(BF16)
    h_q = ((y * gq_ref[...]) * (1.0 + scq_ref[...]) + shq_ref[...]).astype(BF16)
    gm = gm_ref[...]
    k_all = _dot(h_kv, wk_ref[...])
    q_all = _dot(h_q, wq_ref[...])
    v_ref[...] = _dot(h_kv, wv_ref[...]).astype(BF16)
    lf_ref[...] = jax.nn.log_sigmoid(_dot(h_kv, wf_ref[...]) + bf_ref[...])
    for c0 in range(0, k_ref.shape[1], MXU_WIDTH_V7X):
        cols = slice(c0, c0 + MXU_WIDTH_V7X)
        k_ref[:, cols] = _head_rms_norm(k_all[:, cols], kg_ref[:, cols], gm).astype(BF16)
        q = _head_rms_norm(q_all[:, cols], qg_ref[:, cols], gm)
        q_ref[:, cols] = (q * (HEAD_DIM ** -0.5)).astype(BF16)


def _l1_proj(act, w_down, x2d, gate, g_kv, sh_kv, sc_kv, g_q, sh_q, sc_q, wk, wv, wf, wq,
             k_gain, q_gain, b_f, seq):
    m, d = x2d.shape
    f = act.shape[1]
    lane = jnp.arange(MXU_WIDTH_V7X)
    group_mean = jnp.where(lane[:, None] // HEAD_DIM == lane[None, :] // HEAD_DIM,
                           1.0 / HEAD_DIM, 0.0).astype(BF16)
    tpb = seq // TM
    block_bytes = (2 * TM * f + 2 * f * d + 2 * 4 * TM * d + 3 * 2 * d * d + 2 * d * LANES_V7X
                   + 3 * 2 * TM * d + 4 * TM * LANES_V7X)
    full = lambda r, c: pl.BlockSpec((r, c), lambda i: (0, 0))
    rows = lambda c: pl.BlockSpec((TM, c), lambda i: (i, 0))
    return pl.pallas_call(
        _l1_proj_kernel,
        out_shape=(jax.ShapeDtypeStruct((m, d), F32), jax.ShapeDtypeStruct((m, d), BF16),
                   jax.ShapeDtypeStruct((m, d), BF16), jax.ShapeDtypeStruct((m, d), BF16),
                   jax.ShapeDtypeStruct((m, LANES_V7X), F32)),
        grid=(m // TM,),
        in_specs=[
            rows(f), full(f, d), rows(d), _batch_vec_spec(d, tpb),
            _row_spec(d), _batch_vec_spec(d, tpb), _batch_vec_spec(d, tpb),
            _row_spec(d), _batch_vec_spec(d, tpb), _batch_vec_spec(d, tpb),
            full(d, d), full(d, d), full(d, LANES_V7X), full(d, d),
            _row_spec(d), _row_spec(d), _row_spec(LANES_V7X), full(MXU_WIDTH_V7X, MXU_WIDTH_V7X),
        ],
        out_specs=(rows(d), rows(d), rows(d), rows(d), rows(LANES_V7X)),
        compiler_params=pltpu.CompilerParams(
            dimension_semantics=("arbitrary",),
            vmem_limit_bytes=_vmem_limit(block_bytes)),
        name="l1_proj",
    )(act, w_down, x2d, gate, g_kv, sh_kv, sc_kv, g_q, sh_q, sc_q, wk, wv, wf, wq,
      k_gain, q_gain, b_f, group_mean)


N_SPLIT = 3


def _split3(x):
    x0 = x.astype(BF16)
    r1 = x - x0.astype(F32)
    x1 = r1.astype(BF16)
    x2 = (r1 - x1.astype(F32)).astype(BF16)
    return x0, x1, x2


def _fox_prep_kernel(kb_ref, lf_ref, q_ref, k_ref, qa_ref, ka_ref, stats_ref, carry_scr):
    @pl.when(pl.program_id(1) == 0)
    def _():
        carry_scr[...] = jnp.zeros_like(carry_scr)

    r = lax.broadcasted_iota(jnp.int32, (TK, TK), 0)
    c = lax.broadcasted_iota(jnp.int32, (TK, TK), 1)
    tri = jnp.where(c <= r, 1.0, 0.0).astype(BF16)
    x0, x1, x2 = _split3(lf_ref[...])
    cs = _dot(tri, x0) + _dot(tri, x1) + _dot(tri, x2) + carry_scr[0:1, :]
    carry_scr[...] = jnp.broadcast_to(cs[TK - 1:TK, :], carry_scr.shape)

    d_idx = lax.broadcasted_iota(jnp.int32, (D_MODEL, LANES_V7X), 0)
    h_idx = lax.broadcasted_iota(jnp.int32, (D_MODEL, LANES_V7X), 1)
    head_of = jnp.where(d_idx // HEAD_DIM == h_idx, 1.0, 0.0).astype(BF16)
    qf = q_ref[...].astype(F32)
    reach = jnp.sqrt(_dot((qf * qf).astype(BF16), head_of)) * kb_ref[0, 0]
    own_logit = _dot((qf * k_ref[...].astype(F32)).astype(BF16), head_of)

    stats_ref[...] = jnp.concatenate([
        jnp.max(reach * (1.0 + 2.0 ** -8) - own_logit + cs, axis=0, keepdims=True),
        cs[0:1, :],
        jnp.max(reach, axis=0, keepdims=True),
        jnp.zeros((SUBLANES_V7X - 3, LANES_V7X), F32)], axis=0)

    lane = lax.broadcasted_iota(jnp.int32, (1, LANES_V7X), 1)
    packed = jnp.zeros((TK, LANES_V7X), F32)
    for i, term in enumerate(_split3(cs - reach) + _split3(-cs)):
        term = jnp.where(lane < N_HEADS, term.astype(F32), 0.0)
        packed = packed + (term if i == 0 else pltpu.roll(term, N_HEADS * i, axis=1))
    packed = packed.astype(BF16)

    src = lax.broadcasted_iota(jnp.int32, (LANES_V7X, 2 * LANES_V7X), 0)
    dst = lax.broadcasted_iota(jnp.int32, (LANES_V7X, 2 * LANES_V7X), 1)
    for h in range(N_HEADS):
        pair, parity = divmod(h, HEADS_PER_LANE_TILE)
        spare = HEAD_DIM * (1 - parity)
        place = jnp.where((src % N_HEADS == h) & (src < 2 * N_SPLIT * N_HEADS)
                          & (dst % LANES_V7X == spare + src // N_HEADS)
                          & ((dst >= LANES_V7X) == (src >= N_SPLIT * N_HEADS)), 1.0, 0.0)
        placed = _dot(packed, place.astype(BF16))
        ones_q = jnp.where((lane >= spare + N_SPLIT) & (lane < spare + 2 * N_SPLIT), 1.0, 0.0)
        ones_k = jnp.where((lane >= spare) & (lane < spare + N_SPLIT), 1.0, 0.0)
        aug_q = (placed[:, :LANES_V7X] + ones_q).astype(BF16)
        aug_k = (placed[:, LANES_V7X:] + ones_k).astype(BF16)
        own = (lane // HEAD_DIM) == parity
        cols = slice(pair * LANES_V7X, (pair + 1) * LANES_V7X)
        qa_ref[h] = jnp.where(own, q_ref[:, cols], aug_q)
        ka_ref[h] = jnp.where(own, k_ref[:, cols], aug_k)


def _fox_prep(log_f, q, k, key_norm_bound, batch, seq):
    assert TQ == TK
    d = q.shape[1]
    block_bytes = (4 * TK * LANES_V7X + 2 * 2 * TK * d + 2 * 2 * N_HEADS * TK * LANES_V7X
                   + 4 * SUBLANES_V7X * LANES_V7X)
    aug_shape = jax.ShapeDtypeStruct((batch, N_HEADS, seq, LANES_V7X), BF16)
    aug_spec = pl.BlockSpec((None, N_HEADS, TK, LANES_V7X), lambda b, t: (b, 0, t, 0))
    return pl.pallas_call(
        _fox_prep_kernel,
        out_shape=(aug_shape, aug_shape,
                   jax.ShapeDtypeStruct((batch, seq // TK, SUBLANES_V7X, LANES_V7X), F32)),
        grid=(batch, seq // TK),
        in_specs=[
            pl.BlockSpec(memory_space=pltpu.SMEM),
            pl.BlockSpec((None, TK, LANES_V7X), lambda b, t: (b, t, 0)),
            pl.BlockSpec((None, TK, d), lambda b, t: (b, t, 0)),
            pl.BlockSpec((None, TK, d), lambda b, t: (b, t, 0)),
        ],
        out_specs=(aug_spec, aug_spec,
                   pl.BlockSpec((None, None, SUBLANES_V7X, LANES_V7X), lambda b, t: (b, t, 0, 0))),
        scratch_shapes=[pltpu.VMEM((SUBLANES_V7X, LANES_V7X), F32)],
        compiler_params=pltpu.CompilerParams(
            dimension_semantics=("arbitrary", "arbitrary"),
            vmem_limit_bytes=_vmem_limit(block_bytes)),
        name="fox_prep",
    )(key_norm_bound, log_f.reshape(batch, seq, LANES_V7X), q.reshape(batch, seq, d),
      k.reshape(batch, seq, d))


def _head_select(head):
    lane = lax.broadcasted_iota(jnp.int32, (1, LANES_V7X), 1)
    return (lane // HEAD_DIM) == head


def _widen(col128, width):
    reps = width // LANES_V7X
    return col128 if reps == 1 else jnp.concatenate([col128] * reps, axis=1)


def _sb_kernel(q_ref, k_ref, v_ref, o_ref):
    i = pl.program_id(2)
    row = lax.broadcasted_iota(jnp.int32, (SB_T, SB_T), 0)
    col = lax.broadcasted_iota(jnp.int32, (SB_T, SB_T), 1)
    diag_bias = jnp.where(col < row, 0.0, NEG_BIG)
    after = jnp.where(row > col, 1.0, 0.0).astype(BF16)
    ones = jnp.ones((SB_T, LANES_V7X), BF16)
    ones_wide = jnp.ones((SB_T, SB_T), BF16)

    def softplus(z):
        return jnp.maximum(jnp.log2(1.0 + jnp.exp2(jnp.minimum(z, SOFTPLUS_LINEAR))), z)

    def keys(j):
        return k_ref[pl.ds(j * SB_T, SB_T), :], v_ref[pl.ds(j * SB_T, SB_T), :]

    def band_logits(qh, g):
        k_d, _ = keys(g)
        k_p, _ = keys(jnp.maximum(g - 1, 0))
        return _dot_nt(qh, k_d) + diag_bias, jnp.where(g > 0, _dot_nt(qh, k_p), NEG_BIG)

    def band_softplus(z_d, z_p):
        return z_d, z_p, softplus(z_d), softplus(z_p)

    def band_sums(z_d, z_p, sp_d, sp_p):
        sp_d16 = sp_d.astype(BF16)
        within_d = _dot(sp_d16, after)
        within_p = _dot(sp_p.astype(BF16), after) + _dot(sp_d16, ones_wide)
        total = (within_p + sp_p)[:, 0:1]
        return (z_d - sp_d) - within_d, (z_p - sp_p) - within_p, total

    def band_weights(e_d, e_p, total):
        return jnp.exp2(e_d).astype(BF16), jnp.exp2(e_p).astype(BF16), total

    def band_out(g, a_d, a_p, total):
        _, v_d = keys(g)
        _, v_p = keys(jnp.maximum(g - 1, 0))
        return jnp.broadcast_to(-total, (SB_T, LANES_V7X)), _dot(a_d, v_d) + _dot(a_p, v_p)

    def block(qh, j, carry, acc):
        kj, vj = keys(j)
        z = _dot_nt(qh, kj)
        sp = softplus(z)
        sp16 = sp.astype(BF16)
        within = _dot(sp16, after)
        a = jnp.exp2((z - sp) - within + _widen(carry, SB_T))
        acc = acc + _dot(a.astype(BF16), vj)
        carry = carry - _dot(sp16, ones)
        return carry, acc

    heads = []
    for u in range(SB_SUB):
        g = i * SB_SUB + u
        q = q_ref[u * SB_T:(u + 1) * SB_T, :]
        for head in range(HEADS_PER_LANE_TILE):
            heads.append((g, jnp.where(_head_select(head), q, 0.0).astype(BF16)))
    logits = [band_logits(qh, g) for g, qh in heads]
    softplus_terms = [band_softplus(*z) for z in logits]
    sums = [band_sums(*t) for t in softplus_terms]
    weights = [band_weights(*s) for s in sums]
    chains = [(g, qh) + band_out(g, *w) for (g, qh), w in zip(heads, weights)]

    def is_open(carry):
        return jnp.max(carry) > -ZERO_EXP * LOG2_E

    chain_open = [is_open(carry) for _, _, carry, _ in chains]

    def walk_on():
        outs = []
        for (g, qh, carry, acc), this_open in zip(chains, chain_open):
            def cond(state):
                j, carry, _ = state
                return jnp.logical_and(j >= 0, is_open(carry))

            def body(state, qh=qh):
                j, carry, acc = state
                carry, acc = block(qh, j, carry, acc)
                return j - 1, carry, acc

            def walk(cond=cond, body=body, start=(g - 2, carry, acc)):
                return lax.while_loop(cond, body, start)[2]

            outs.append(lax.cond(this_open, walk, lambda acc=acc: acc))
        return outs

    outs = lax.cond(functools.reduce(jnp.logical_or, chain_open), walk_on,
                    lambda: [acc for _, _, _, acc in chains])
    for u in range(SB_SUB):
        pair = outs[u * HEADS_PER_LANE_TILE:(u + 1) * HEADS_PER_LANE_TILE]
        o_ref[u * SB_T:(u + 1) * SB_T, :] = jnp.where(_head_select(0), pair[0], pair[1]).astype(o_ref.dtype)


def _sb_attn(qkv, batch, seq):
    rows = SB_SUB * SB_T
    stage_bytes = SB_SUB * HEADS_PER_LANE_TILE * 2 * 2 * 4 * SB_T * SB_T
    block_bytes = 2 * (2 * rows * LANES_V7X + 2 * seq * LANES_V7X) + stage_bytes // 2
    return pl.pallas_call(
        _sb_kernel,
        out_shape=jax.ShapeDtypeStruct((batch, seq, D_MODEL), BF16),
        grid=(batch, N_HEAD_PAIRS, seq // rows),
        in_specs=[
            pl.BlockSpec((None, rows, LANES_V7X), lambda b, p, i: (b, i, p)),
            pl.BlockSpec((None, seq, LANES_V7X), lambda b, p, i: (b, 0, N_HEAD_PAIRS + p)),
            pl.BlockSpec((None, seq, LANES_V7X), lambda b, p, i: (b, 0, 2 * N_HEAD_PAIRS + p)),
        ],
        out_specs=pl.BlockSpec((None, rows, LANES_V7X), lambda b, p, i: (b, i, p)),
        compiler_params=pltpu.CompilerParams(
            dimension_semantics=("arbitrary", "arbitrary", "arbitrary"),
            vmem_limit_bytes=_vmem_limit(block_bytes)),
        name="sb_attn",
    )(qkv, qkv, qkv)


def _fox_kernel(a_ref, fk0_ref, g_ref, qa_ref, ka_ref, v_ref, o_ref, acc_scr):
    b = pl.program_id(0)
    p_id = pl.program_id(1)
    i = pl.program_id(2)
    n_blocks = fk0_ref.shape[1]
    row = lax.broadcasted_iota(jnp.int32, (TQ, TK), 0)
    col = lax.broadcasted_iota(jnp.int32, (TQ, TK), 1)
    diag_mask = col <= row

    acc0 = jnp.zeros((TQ, LANES_V7X), F32)
    chains = range(FOX_SUB * HEADS_PER_LANE_TILE)
    bh0 = (b * N_HEAD_PAIRS + p_id) * HEADS_PER_LANE_TILE

    def scores(chain, j):
        sub, head = chain // HEADS_PER_LANE_TILE, chain % HEADS_PER_LANE_TILE
        q = qa_ref[head, pl.ds(pl.multiple_of(sub * TQ, TQ), TQ), :]
        return _dot_nt(q, ka_ref[head, pl.ds(pl.multiple_of(j * TK, TK), TK), :])

    def values(chain, j):
        vj = v_ref[pl.ds(pl.multiple_of(j * TK, TK), TK), :]
        own = _head_select(chain % HEADS_PER_LANE_TILE)
        return jnp.where(own, vj, jnp.ones_like(vj))

    def diag_block(chain):
        return i * FOX_SUB + chain // HEADS_PER_LANE_TILE

    def count_left(chain):
        bh, qi = bh0 + chain % HEADS_PER_LANE_TILE, diag_block(chain)
        bound = a_ref[bh, qi]

        def more(n):
            j = qi - 1 - n
            right = fk0_ref[bh, jnp.minimum(j + 1, n_blocks - 1)]
            return jnp.logical_and(j >= 0, bound - right > -ZERO_EXP)

        return lax.while_loop(more, lambda n: n + 1, jnp.int32(0))

    def fast_walks():
        counts = [count_left(chain) for chain in chains]
        firsts = [sum(n + 1 for n in counts[:chain]) for chain in chains]
        n_items = firsts[-1] + counts[-1] + 1
        diag_bias = jnp.where(diag_mask, 0.0, NEG_BIG)
        acc_scr[...] = jnp.zeros_like(acc_scr)

        def step(t, p, p_chain, p_block):
            chain = sum(jnp.where(t >= first, 1, 0) for first in firsts[1:])
            first = functools.reduce(lambda f, nxt: jnp.where(t >= nxt, nxt, f), firsts[1:], 0)
            back = t - first
            block = jnp.maximum(diag_block(chain) - back, 0)
            s = scores(chain, block) + jnp.where(back == 0, diag_bias,
                                                 jnp.where(t >= n_items, NEG_BIG, 0.0))
            acc_scr[p_chain] += _dot(p, values(p_chain, p_block))
            return jnp.exp(s).astype(BF16), chain, block

        def steps(first, count, carried):
            for t in range(count):
                carried = step(first + t, *carried)
            return carried

        pairs = (n_items + 2) // 2
        start = (jnp.zeros((TQ, TK), BF16), jnp.int32(0), jnp.int32(0))
        carried = lax.fori_loop(0, pairs // 2, lambda u, c: steps(4 * u, 4, c), start)
        lax.fori_loop(0, pairs % 2, lambda _, c: steps(4 * (pairs // 2), 2, c), carried)
        return [acc_scr[chain] for chain in chains]

    def online_walks():
        accs = []
        for chain in chains:
            def body(t, carried, chain=chain):
                acc, m = carried
                block = diag_block(chain) - t
                s = jnp.where(jnp.logical_or(t > 0, diag_mask), scores(chain, block), NEG_BIG)
                m_new = jnp.maximum(m, jnp.max(s, axis=-1, keepdims=True))
                p = jnp.exp(s - m_new).astype(BF16)
                return jnp.exp(m - m_new) * acc + _dot(p, values(chain, block)), m_new

            m0 = jnp.full((TQ, 1), NEG_BIG, F32)
            accs.append(lax.fori_loop(0, count_left(chain) + 1, body, (acc0, m0))[0])
        return accs

    largest_reach = functools.reduce(jnp.maximum, [
        g_ref[bh0 + chain % HEADS_PER_LANE_TILE, diag_block(chain)] for chain in chains])
    accs = lax.cond(2.0 * largest_reach < SAFE_EXP_RANGE, fast_walks, online_walks)
    outs = [acc / pltpu.roll(acc, HEAD_DIM, axis=1) for acc in accs]
    for sub in range(FOX_SUB):
        pair = outs[sub * HEADS_PER_LANE_TILE:(sub + 1) * HEADS_PER_LANE_TILE]
        o_ref[sub * TQ:(sub + 1) * TQ, :] = jnp.where(_head_select(0), pair[0], pair[1]).astype(o_ref.dtype)


def _fox_attn(q_aug, k_aug, v, walk_bound, fk_block_start, max_reach, batch, seq):
    assert TQ == TK
    rows = FOX_SUB * TQ
    n_chains = FOX_SUB * HEADS_PER_LANE_TILE
    block_bytes = (2 * (HEADS_PER_LANE_TILE * (rows + seq) * LANES_V7X + seq * LANES_V7X + rows * LANES_V7X)
                   + 4 * n_chains * TQ * LANES_V7X)
    smem = pl.BlockSpec(memory_space=pltpu.SMEM)
    return pl.pallas_call(
        _fox_kernel,
        out_shape=jax.ShapeDtypeStruct((batch, seq, D_MODEL), BF16),
        grid=(batch, N_HEAD_PAIRS, seq // rows),
        in_specs=[
            smem, smem, smem,
            pl.BlockSpec((None, HEADS_PER_LANE_TILE, rows, LANES_V7X), lambda b, p, i: (b, p, i, 0)),
            pl.BlockSpec((None, HEADS_PER_LANE_TILE, seq, LANES_V7X), lambda b, p, i: (b, p, 0, 0)),
            pl.BlockSpec((None, seq, LANES_V7X), lambda b, p, i: (b, 0, p)),
        ],
        out_specs=pl.BlockSpec((None, rows, LANES_V7X), lambda b, p, i: (b, i, p)),
        scratch_shapes=[pltpu.VMEM((n_chains, TQ, LANES_V7X), F32)],
        compiler_params=pltpu.CompilerParams(
            dimension_semantics=("arbitrary", "arbitrary", "arbitrary"),
            vmem_limit_bytes=_vmem_limit(block_bytes)),
        name="fox_attn",
    )(walk_bound, fk_block_start, max_reach, q_aug, k_aug, v)


def kernel(x, c, ada_w, ada_b, norm_attn_g, norm_ffn_g, w_ffn_in, w_ffn_down, sb_w_qkv, sb_w_o,
           kv_ada_w, kv_ada_b, kv_norm_g, w_kvf, b_f, k_norm_g, fox_w_q, q_norm_g, fox_w_o):
    batch, seq, d = x.shape
    assert d == D_MODEL and seq % TM == 0
    assert seq % (FOX_SUB * TQ) == 0 and seq % (SB_SUB * SB_T) == 0
    assert ada_w.shape[0] == 2 and sb_w_qkv.shape[0] == 1 and fox_w_q.shape[0] == 1
    m = batch * seq

    c_pad = jnp.zeros((SUBLANES_V7X, d), F32).at[:batch].set(c.astype(F32))
    mod = _ada_mod(c_pad, ada_w, ada_b)[:, :batch]
    kv_mod = _ada_mod(c_pad, kv_ada_w[None], kv_ada_b[None])[0, :batch]

    def vecs(t, n):
        return [t[:, None, j * d:(j + 1) * d] for j in range(n)]

    row = lambda t: t.reshape(1, -1).astype(F32)
    bf = lambda t: t.astype(BF16)

    x2d = x.reshape(m, d)

    sh_a, sc_a, g_a, sh_f, sc_f, g_f = vecs(mod[0], 6)
    qkv_scale = jnp.concatenate([jnp.full((1, d), LOG2_E * HEAD_DIM ** -0.5, F32),
                                 jnp.ones((1, 2 * d), F32)], axis=1)
    qkv = _norm_matmul(x2d, row(norm_attn_g[0]), sh_a, sc_a, bf(sb_w_qkv[0]), qkv_scale, seq)
    o = _sb_attn(qkv.reshape(batch, seq, 3 * d), batch, seq)
    x2d, act = _out_ffn_in(o.reshape(m, d), bf(sb_w_o[0]), x2d, g_a,
                           row(norm_ffn_g[0]), sh_f, sc_f, bf(w_ffn_in[0]), seq)
    g_f0 = g_f

    sh_a, sc_a, g_a, sh_f, sc_f, g_f = vecs(mod[1], 6)
    kv_shift, kv_scale = vecs(kv_mod, 2)
    w_f_pad = jnp.zeros((d, LANES_V7X), F32).at[:, :N_HEADS].set(w_kvf[:, 2 * d:])
    b_f_pad = jnp.zeros((1, LANES_V7X), F32).at[0, :N_HEADS].set(b_f)
    x2d, k, v, q, log_f = _l1_proj(
        act, bf(w_ffn_down[0]), x2d, g_f0,
        row(kv_norm_g), kv_shift, kv_scale, row(norm_attn_g[1]), sh_a, sc_a,
        bf(w_kvf[:, :d]), bf(w_kvf[:, d:2 * d]), bf(w_f_pad), bf(fox_w_q[0]),
        row(jnp.tile(k_norm_g, N_HEADS)), row(jnp.tile(q_norm_g[0], N_HEADS)), b_f_pad, seq)
    key_norm_bound = (1.01 * HEAD_DIM ** 0.5 * jnp.max(jnp.abs(k_norm_g))).reshape(1, 1).astype(F32)
    q_aug, k_aug, stats = _fox_prep(log_f, q, k, key_norm_bound, batch, seq)
    tables = stats[:, :, :3, :N_HEADS].transpose(2, 0, 3, 1).reshape(3, batch * N_HEADS, seq // TK)
    o = _fox_attn(q_aug, k_aug, v.reshape(batch, seq, d), tables[0], tables[1], tables[2], batch, seq)
    x2d, act = _out_ffn_in(o.reshape(m, d), bf(fox_w_o[0]), x2d, g_a,
                           row(norm_ffn_g[1]), sh_f, sc_f, bf(w_ffn_in[1]), seq)
    x2d = _proj_residual(act, bf(w_ffn_down[1]), x2d, g_f, seq)
    return x2d.reshape(batch, seq, d)
```

```python
import functools

import jax
import jax.numpy as jnp
from jax import lax
from jax.experimental import pallas as pl
from jax.experimental.pallas import tpu as pltpu

D_MODEL = 1024
N_HEADS = 16
HEAD_DIM = D_MODEL // N_HEADS
EPS = 1e-6

LANES_V7X = 128
SUBLANES_V7X = 8
VMEM_BYTES_V7X = 64 * 1024 * 1024

HEADS_PER_LANE_TILE = LANES_V7X // HEAD_DIM
N_HEAD_PAIRS = N_HEADS // HEADS_PER_LANE_TILE

MXU_WIDTH_V7X = 256

TM = 512
COL_CHUNK = 2 * MXU_WIDTH_V7X
TQ = 512
TK = 512
FOX_SUB = 4
SB_T = 128
SB_LEFT = 256
SB_SUB = 16

ZERO_EXP = 104.5
SAFE_EXP_RANGE = 80.0
NEG_BIG = -1e30
LOG2_E = 1.4426950408889634
SOFTPLUS_LINEAR = 64.0

F32 = jnp.float32
BF16 = jnp.bfloat16


def _vmem_limit(block_bytes):
    want = 2 * block_bytes + 16 * 1024 * 1024
    return int(min(want, VMEM_BYTES_V7X - 8 * 1024 * 1024))


def _split2(x):
    hi = x.astype(BF16)
    lo = (x - hi.astype(F32)).astype(BF16)
    return hi, lo


def _dot(a, b):
    return jnp.dot(a, b, preferred_element_type=F32)


def _dot_nt(a, b):
    return lax.dot_general(a, b, (((1,), (1,)), ((), ())), preferred_element_type=F32)


def _ada_kernel(c_ref, w_ref, b_ref, o_ref):
    c = c_ref[...]
    c_act = c * jax.nn.sigmoid(c)
    o_ref[...] = jnp.dot(c_act, w_ref[...], precision=lax.Precision.HIGHEST,
                         preferred_element_type=F32) + b_ref[...]


def _ada_mod(c_pad, w, b):
    n_layers, d, n = w.shape
    tn = 1024
    rows = c_pad.shape[0]
    block_bytes = 4 * (d * tn + rows * d + tn + rows * tn)
    return pl.pallas_call(
        _ada_kernel,
        out_shape=jax.ShapeDtypeStruct((n_layers, rows, n), F32),
        grid=(n_layers, n // tn),
        in_specs=[
            pl.BlockSpec((rows, d), lambda l, j: (0, 0)),
            pl.BlockSpec((None, d, tn), lambda l, j: (l, 0, j)),
            pl.BlockSpec((None, 1, tn), lambda l, j: (l, 0, j)),
        ],
        out_specs=pl.BlockSpec((None, rows, tn), lambda l, j: (l, 0, j)),
        compiler_params=pltpu.CompilerParams(
            dimension_semantics=("arbitrary", "arbitrary"),
            vmem_limit_bytes=_vmem_limit(block_bytes)),
        name="ada_mod",
    )(c_pad, w, b.reshape(n_layers, 1, n))


def _norm_modulate(x, g, shift, scale):
    ms = jnp.mean(x * x, axis=-1, keepdims=True)
    y = x * lax.rsqrt(ms + EPS)
    return (y * g) * (1.0 + scale) + shift


def _row_spec(d):
    return pl.BlockSpec((1, d), lambda *_: (0, 0))


def _batch_vec_spec(d, tiles_per_batch):
    return pl.BlockSpec((None, 1, d), lambda i, *_: (i // tiles_per_batch, 0, 0))


def _column_chunks(n):
    return [(c, min(c + COL_CHUNK, n)) for c in range(0, n, COL_CHUNK)]


def _norm_matmul_kernel(x_ref, g_ref, sh_ref, sc_ref, w_ref, cs_ref, o_ref):
    h = _norm_modulate(x_ref[...], g_ref[...], sh_ref[...], sc_ref[...]).astype(BF16)
    for c0, c1 in _column_chunks(o_ref.shape[1]):
        o_ref[:, c0:c1] = (_dot(h, w_ref[:, c0:c1]) * cs_ref[:, c0:c1]).astype(o_ref.dtype)


def _norm_matmul(x2d, g, shift, scale, w, col_scale, seq):
    m, d = x2d.shape
    n = w.shape[1]
    block_bytes = 4 * TM * d + 2 * d * n + 2 * TM * n + 4 * n
    return pl.pallas_call(
        _norm_matmul_kernel,
        out_shape=jax.ShapeDtypeStruct((m, n), BF16),
        grid=(m // TM,),
        in_specs=[
            pl.BlockSpec((TM, d), lambda i: (i, 0)),
            _row_spec(d),
            _batch_vec_spec(d, seq // TM),
            _batch_vec_spec(d, seq // TM),
            pl.BlockSpec((d, n), lambda i: (0, 0)),
            pl.BlockSpec((1, n), lambda i: (0, 0)),
        ],
        out_specs=pl.BlockSpec((TM, n), lambda i: (i, 0)),
        compiler_params=pltpu.CompilerParams(
            dimension_semantics=("arbitrary",),
            vmem_limit_bytes=_vmem_limit(block_bytes)),
        name="norm_matmul",
    )(x2d, g, shift, scale, w, col_scale)


def _out_ffn_in_kernel(a_ref, wo_ref, x_ref, gate_ref, g_ref, sh_ref, sc_ref, wg_ref, wu_ref,
                       x1_ref, act_ref):
    x1 = x_ref[...] + gate_ref[...] * _dot(a_ref[...], wo_ref[...])
    x1_ref[...] = x1
    h = _norm_modulate(x1, g_ref[...], sh_ref[...], sc_ref[...]).astype(BF16)
    for c0, c1 in _column_chunks(act_ref.shape[1]):
        gate = _dot(h, wg_ref[:, c0:c1])
        up = _dot(h, wu_ref[:, c0:c1])
        act_ref[:, c0:c1] = (gate * jax.nn.sigmoid(gate) * up).astype(act_ref.dtype)


def _out_ffn_in(a, w_o, x2d, gate, g, shift, scale, w_in, seq):
    m, d = x2d.shape
    f = w_in.shape[1] // 2
    tpb = seq // TM
    block_bytes = 2 * TM * d + 2 * d * d + 2 * 4 * TM * d + 2 * 2 * d * f + 2 * TM * f
    return pl.pallas_call(
        _out_ffn_in_kernel,
        out_shape=(jax.ShapeDtypeStruct((m, d), F32), jax.ShapeDtypeStruct((m, f), BF16)),
        grid=(m // TM,),
        in_specs=[
            pl.BlockSpec((TM, d), lambda i: (i, 0)),
            pl.BlockSpec((d, d), lambda i: (0, 0)),
            pl.BlockSpec((TM, d), lambda i: (i, 0)),
            _batch_vec_spec(d, tpb),
            _row_spec(d),
            _batch_vec_spec(d, tpb),
            _batch_vec_spec(d, tpb),
            pl.BlockSpec((d, f), lambda i: (0, 0)),
            pl.BlockSpec((d, f), lambda i: (0, 1)),
        ],
        out_specs=(pl.BlockSpec((TM, d), lambda i: (i, 0)), pl.BlockSpec((TM, f), lambda i: (i, 0))),
        compiler_params=pltpu.CompilerParams(
            dimension_semantics=("arbitrary",),
            vmem_limit_bytes=_vmem_limit(block_bytes)),
        name="out_ffn_in",
    )(a, w_o, x2d, gate, g, shift, scale, w_in, w_in)


def _proj_residual_kernel(a_ref, w_ref, x_ref, gate_ref, o_ref):
    o_ref[...] = x_ref[...] + gate_ref[...] * _dot(a_ref[...], w_ref[...])


def _proj_residual(a, w, x2d, gate, seq):
    m, k = a.shape
    d = w.shape[1]
    block_bytes = 2 * TM * k + 2 * k * d + 2 * 4 * TM * d
    return pl.pallas_call(
        _proj_residual_kernel,
        out_shape=jax.ShapeDtypeStruct((m, d), F32),
        grid=(m // TM,),
        in_specs=[
            pl.BlockSpec((TM, k), lambda i: (i, 0)),
            pl.BlockSpec((k, d), lambda i: (0, 0)),
            pl.BlockSpec((TM, d), lambda i: (i, 0)),
            _batch_vec_spec(d, seq // TM),
        ],
        out_specs=pl.BlockSpec((TM, d), lambda i: (i, 0)),
        compiler_params=pltpu.CompilerParams(
            dimension_semantics=("arbitrary",),
            vmem_limit_bytes=_vmem_limit(block_bytes)),
        name="proj_residual",
    )(a, w, x2d, gate)


def _head_rms_norm(t, gain, group_mean):
    ms = _dot((t * t).astype(BF16), group_mean)
    return t * lax.rsqrt(ms + EPS) * gain


def _l1_proj_kernel(act_ref, wd_ref, x_ref, gate_ref,
                    gkv_ref, shkv_ref, sckv_ref, gq_ref, shq_ref, scq_ref,
                    wk_ref, wv_ref, wf_ref, wq_ref, kg_ref, qg_ref, bf_ref, gm_ref,
                    x2_ref, k_ref, v_ref, q_ref, lf_ref):
    x = x_ref[...] + gate_ref[...] * _dot(act_ref[...], wd_ref[...])
    x2_ref[...] = x
    ms = jnp.mean(x * x, axis=-1, keepdims=True)
    y = x * lax.rsqrt(ms + EPS)
    h_kv = ((y * gkv_ref[...]) * (1.0 + sckv_ref[...]) + shkv_ref[...]).astype(BF16)
    h_q = ((y * gq_ref[...]) * (1.0 + scq_ref[...]) + shq_ref[...]).astype(BF16)
    gm = gm_ref[...]
    k_all = _dot(h_kv, wk_ref[...])
    q_all = _dot(h_q, wq_ref[...])
    v_ref[...] = _dot(h_kv, wv_ref[...]).astype(BF16)
    lf_ref[...] = jax.nn.log_sigmoid(_dot(h_kv, wf_ref[...]) + bf_ref[...])
    for c0 in range(0, k_ref.shape[1], MXU_WIDTH_V7X):
        cols = slice(c0, c0 + MXU_WIDTH_V7X)
        k_ref[:, cols] = _head_rms_norm(k_all[:, cols], kg_ref[:, cols], gm).astype(BF16)
        q = _head_rms_norm(q_all[:, cols], qg_ref[:, cols], gm)
        q_ref[:, cols] = (q * (HEAD_DIM ** -0.5)).astype(BF16)


def _l1_proj(act, w_down, x2d, gate, g_kv, sh_kv, sc_kv, g_q, sh_q, sc_q, wk, wv, wf, wq,
             k_gain, q_gain, b_f, seq):
    m, d = x2d.shape
    f = act.shape[1]
    lane = jnp.arange(MXU_WIDTH_V7X)
    group_mean = jnp.where(lane[:, None] // HEAD_DIM == lane[None, :] // HEAD_DIM,
                           1.0 / HEAD_DIM, 0.0).astype(BF16)
    tpb = seq // TM
    block_bytes = (2 * TM * f + 2 * f * d + 2 * 4 * TM * d + 3 * 2 * d * d + 2 * d * LANES_V7X
                   + 3 * 2 * TM * d + 4 * TM * LANES_V7X)
    full = lambda r, c: pl.BlockSpec((r, c), lambda i: (0, 0))
    rows = lambda c: pl.BlockSpec((TM, c), lambda i: (i, 0))
    return pl.pallas_call(
        _l1_proj_kernel,
        out_shape=(jax.ShapeDtypeStruct((m, d), F32), jax.ShapeDtypeStruct((m, d), BF16),
                   jax.ShapeDtypeStruct((m, d), BF16), jax.ShapeDtypeStruct((m, d), BF16),
                   jax.ShapeDtypeStruct((m, LANES_V7X), F32)),
        grid=(m // TM,),
        in_specs=[
            rows(f), full(f, d), rows(d), _batch_vec_spec(d, tpb),
            _row_spec(d), _batch_vec_spec(d, tpb), _batch_vec_spec(d, tpb),
            _row_spec(d), _batch_vec_spec(d, tpb), _batch_vec_spec(d, tpb),
            full(d, d), full(d, d), full(d, LANES_V7X), full(d, d),
            _row_spec(d), _row_spec(d), _row_spec(LANES_V7X), full(MXU_WIDTH_V7X, MXU_WIDTH_V7X),
        ],
        out_specs=(rows(d), rows(d), rows(d), rows(d), rows(LANES_V7X)),
        compiler_params=pltpu.CompilerParams(
            dimension_semantics=("arbitrary",),
            vmem_limit_bytes=_vmem_limit(block_bytes)),
        name="l1_proj",
    )(act, w_down, x2d, gate, g_kv, sh_kv, sc_kv, g_q, sh_q, sc_q, wk, wv, wf, wq,
      k_gain, q_gain, b_f, group_mean)


N_SPLIT = 3


def _split3(x):
    x0 = x.astype(BF16)
    r1 = x - x0.astype(F32)
    x1 = r1.astype(BF16)
    x2 = (r1 - x1.astype(F32)).astype(BF16)
    return x0, x1, x2


def _fox_prep_kernel(kb_ref, lf_ref, q_ref, k_ref, qa_ref, ka_ref, stats_ref, carry_scr):
    @pl.when(pl.program_id(1) == 0)
    def _():
        carry_scr[...] = jnp.zeros_like(carry_scr)

    r = lax.broadcasted_iota(jnp.int32, (TK, TK), 0)
    c = lax.broadcasted_iota(jnp.int32, (TK, TK), 1)
    tri = jnp.where(c <= r, 1.0, 0.0).astype(BF16)
    x0, x1, x2 = _split3(lf_ref[...])
    cs = _dot(tri, x0) + _dot(tri, x1) + _dot(tri, x2) + carry_scr[0:1, :]
    carry_scr[...] = jnp.broadcast_to(cs[TK - 1:TK, :], carry_scr.shape)

    d_idx = lax.broadcasted_iota(jnp.int32, (D_MODEL, LANES_V7X), 0)
    h_idx = lax.broadcasted_iota(jnp.int32, (D_MODEL, LANES_V7X), 1)
    head_of = jnp.where(d_idx // HEAD_DIM == h_idx, 1.0, 0.0).astype(BF16)
    qf = q_ref[...].astype(F32)
    reach = jnp.sqrt(_dot((qf * qf).astype(BF16), head_of)) * kb_ref[0, 0]
    own_logit = _dot((qf * k_ref[...].astype(F32)).astype(BF16), head_of)

    stats_ref[...] = jnp.concatenate([
        jnp.max(reach * (1.0 + 2.0 ** -8) - own_logit + cs, axis=0, keepdims=True),
        cs[0:1, :],
        jnp.max(reach, axis=0, keepdims=True),
        jnp.zeros((SUBLANES_V7X - 3, LANES_V7X), F32)], axis=0)

    lane = lax.broadcasted_iota(jnp.int32, (1, LANES_V7X), 1)
    packed = jnp.zeros((TK, LANES_V7X), F32)
    for i, term in enumerate(_split3(cs - reach) + _split3(-cs)):
        term = jnp.where(lane < N_HEADS, term.astype(F32), 0.0)
        packed = packed + (term if i == 0 else pltpu.roll(term, N_HEADS * i, axis=1))
    packed = packed.astype(BF16)

    src = lax.broadcasted_iota(jnp.int32, (LANES_V7X, 2 * LANES_V7X), 0)
    dst = lax.broadcasted_iota(jnp.int32, (LANES_V7X, 2 * LANES_V7X), 1)
    for h in range(N_HEADS):
        pair, parity = divmod(h, HEADS_PER_LANE_TILE)
        spare = HEAD_DIM * (1 - parity)
        place = jnp.where((src % N_HEADS == h) & (src < 2 * N_SPLIT * N_HEADS)
                          & (dst % LANES_V7X == spare + src // N_HEADS)
                          & ((dst >= LANES_V7X) == (src >= N_SPLIT * N_HEADS)), 1.0, 0.0)
        placed = _dot(packed, place.astype(BF16))
        ones_q = jnp.where((lane >= spare + N_SPLIT) & (lane < spare + 2 * N_SPLIT), 1.0, 0.0)
        ones_k = jnp.where((lane >= spare) & (lane < spare + N_SPLIT), 1.0, 0.0)
        aug_q = (placed[:, :LANES_V7X] + ones_q).astype(BF16)
        aug_k = (placed[:, LANES_V7X:] + ones_k).astype(BF16)
        own = (lane // HEAD_DIM) == parity
        cols = slice(pair * LANES_V7X, (pair + 1) * LANES_V7X)
        qa_ref[h] = jnp.where(own, q_ref[:, cols], aug_q)
        ka_ref[h] = jnp.where(own, k_ref[:, cols], aug_k)


def _fox_prep(log_f, q, k, key_norm_bound, batch, seq):
    assert TQ == TK
    d = q.shape[1]
    block_bytes = (4 * TK * LANES_V7X + 2 * 2 * TK * d + 2 * 2 * N_HEADS * TK * LANES_V7X
                   + 4 * SUBLANES_V7X * LANES_V7X)
    aug_shape = jax.ShapeDtypeStruct((batch, N_HEADS, seq, LANES_V7X), BF16)
    aug_spec = pl.BlockSpec((None, N_HEADS, TK, LANES_V7X), lambda b, t: (b, 0, t, 0))
    return pl.pallas_call(
        _fox_prep_kernel,
        out_shape=(aug_shape, aug_shape,
                   jax.ShapeDtypeStruct((batch, seq // TK, SUBLANES_V7X, LANES_V7X), F32)),
        grid=(batch, seq // TK),
        in_specs=[
            pl.BlockSpec(memory_space=pltpu.SMEM),
            pl.BlockSpec((None, TK, LANES_V7X), lambda b, t: (b, t, 0)),
            pl.BlockSpec((None, TK, d), lambda b, t: (b, t, 0)),
            pl.BlockSpec((None, TK, d), lambda b, t: (b, t, 0)),
        ],
        out_specs=(aug_spec, aug_spec,
                   pl.BlockSpec((None, None, SUBLANES_V7X, LANES_V7X), lambda b, t: (b, t, 0, 0))),
        scratch_shapes=[pltpu.VMEM((SUBLANES_V7X, LANES_V7X), F32)],
        compiler_params=pltpu.CompilerParams(
            dimension_semantics=("arbitrary", "arbitrary"),
            vmem_limit_bytes=_vmem_limit(block_bytes)),
        name="fox_prep",
    )(key_norm_bound, log_f.reshape(batch, seq, LANES_V7X), q.reshape(batch, seq, d),
      k.reshape(batch, seq, d))


def _head_select(head):
    lane = lax.broadcasted_iota(jnp.int32, (1, LANES_V7X), 1)
    return (lane // HEAD_DIM) == head


def _sb_kernel(q_ref, k_ref, v_ref, o_ref):
    i = pl.program_id(2)

    def strictly_lower(n):
        return (lax.broadcasted_iota(jnp.int32, (n, n), 0)
                > lax.broadcasted_iota(jnp.int32, (n, n), 1))

    diag_bias = jnp.where(strictly_lower(SB_T), 0.0, NEG_BIG)
    after = jnp.where(strictly_lower(SB_T), 1.0, 0.0).astype(BF16)
    after_left = jnp.where(strictly_lower(SB_LEFT), 1.0, 0.0).astype(BF16)
    ones = jnp.ones((SB_T, LANES_V7X), BF16)
    ones_left = jnp.ones((SB_T, SB_LEFT), BF16)

    def softplus(z):
        return jnp.maximum(jnp.log2(1.0 + jnp.exp2(jnp.minimum(z, SOFTPLUS_LINEAR))), z)

    def keys(start, size):
        rows = pl.ds(pl.multiple_of(start, SB_T), size)
        return k_ref[rows, :], v_ref[rows, :]

    def left_start(g):
        return jnp.maximum(g * SB_T - SB_LEFT, 0)

    def band_logits(qh, g, may_clip):
        k_d, _ = keys(g * SB_T, SB_T)
        k_p, _ = keys(left_start(g), SB_LEFT)
        z_p = _dot_nt(qh, k_p)
        if may_clip:
            key = left_start(g) + lax.broadcasted_iota(jnp.int32, (SB_T, SB_LEFT), 1)
            z_p = jnp.where(key < g * SB_T, z_p, NEG_BIG)
        return _dot_nt(qh, k_d) + diag_bias, z_p

    def band_softplus(z_d, z_p):
        return z_d, z_p, softplus(z_d), softplus(z_p)

    def band_sums(z_d, z_p, sp_d, sp_p):
        sp_d16 = sp_d.astype(BF16)
        within_d = _dot(sp_d16, after)
        within_p = _dot(sp_p.astype(BF16), after_left) + _dot(sp_d16, ones_left)
        total = (within_p + sp_p)[:, 0:1]
        return (z_d - sp_d) - within_d, (z_p - sp_p) - within_p, total

    def band_weights(e_d, e_p, total):
        return jnp.exp2(e_d).astype(BF16), jnp.exp2(e_p).astype(BF16), total

    def band_out(g, a_d, a_p, total):
        _, v_d = keys(g * SB_T, SB_T)
        _, v_p = keys(left_start(g), SB_LEFT)
        return jnp.broadcast_to(-total, (SB_T, LANES_V7X)), _dot(a_d, v_d) + _dot(a_p, v_p)

    def block(qh, start, carry, acc):
        kj, vj = keys(start, SB_T)
        z = _dot_nt(qh, kj)
        sp = softplus(z)
        sp16 = sp.astype(BF16)
        within = _dot(sp16, after)
        a = jnp.exp2((z - sp) - within + carry)
        acc = acc + _dot(a.astype(BF16), vj)
        carry = carry - _dot(sp16, ones)
        return carry, acc

    heads = []
    for u in range(SB_SUB):
        g = i * SB_SUB + u
        q = q_ref[u * SB_T:(u + 1) * SB_T, :]
        for head in range(HEADS_PER_LANE_TILE):
            heads.append((g, jnp.where(_head_select(head), q, 0.0).astype(BF16)))
    logits = [band_logits(qh, g, may_clip=c // HEADS_PER_LANE_TILE < SB_LEFT // SB_T)
              for c, (g, qh) in enumerate(heads)]
    softplus_terms = [band_softplus(*z) for z in logits]
    sums = [band_sums(*t) for t in softplus_terms]
    weights = [band_weights(*s) for s in sums]
    chains = [(g, qh) + band_out(g, *w) for (g, qh), w in zip(heads, weights)]

    def is_open(carry):
        return jnp.max(carry) > -ZERO_EXP * LOG2_E

    chain_open = [is_open(carry) for _, _, carry, _ in chains]

    def walk_on():
        outs = []
        for (g, qh, carry, acc), this_open in zip(chains, chain_open):
            def cond(state):
                start, carry, _ = state
                return jnp.logical_and(start >= 0, is_open(carry))

            def body(state, qh=qh):
                start, carry, acc = state
                carry, acc = block(qh, start, carry, acc)
                return start - SB_T, carry, acc

            def walk(cond=cond, body=body, first=(left_start(g) - SB_T, carry, acc)):
                return lax.while_loop(cond, body, first)[2]

            outs.append(lax.cond(this_open, walk, lambda acc=acc: acc))
        return outs

    outs = lax.cond(functools.reduce(jnp.logical_or, chain_open), walk_on,
                    lambda: [acc for _, _, _, acc in chains])
    for u in range(SB_SUB):
        pair = outs[u * HEADS_PER_LANE_TILE:(u + 1) * HEADS_PER_LANE_TILE]
        o_ref[u * SB_T:(u + 1) * SB_T, :] = jnp.where(_head_select(0), pair[0], pair[1]).astype(o_ref.dtype)


def _sb_attn(qkv, batch, seq):
    assert SB_T == LANES_V7X and SB_LEFT % SB_T == 0
    rows = SB_SUB * SB_T
    stage_bytes = SB_SUB * HEADS_PER_LANE_TILE * 2 * 4 * SB_T * (SB_T + SB_LEFT)
    block_bytes = 2 * (2 * rows * LANES_V7X + 2 * seq * LANES_V7X) + stage_bytes // 2
    return pl.pallas_call(
        _sb_kernel,
        out_shape=jax.ShapeDtypeStruct((batch, seq, D_MODEL), BF16),
        grid=(batch, N_HEAD_PAIRS, seq // rows),
        in_specs=[
            pl.BlockSpec((None, rows, LANES_V7X), lambda b, p, i: (b, i, p)),
            pl.BlockSpec((None, seq, LANES_V7X), lambda b, p, i: (b, 0, N_HEAD_PAIRS + p)),
            pl.BlockSpec((None, seq, LANES_V7X), lambda b, p, i: (b, 0, 2 * N_HEAD_PAIRS + p)),
        ],
        out_specs=pl.BlockSpec((None, rows, LANES_V7X), lambda b, p, i: (b, i, p)),
        compiler_params=pltpu.CompilerParams(
            dimension_semantics=("arbitrary", "arbitrary", "arbitrary"),
            vmem_limit_bytes=_vmem_limit(block_bytes)),
        name="sb_attn",
    )(qkv, qkv, qkv)


def _fox_kernel(a_ref, fk0_ref, g_ref, qa_ref, ka_ref, v_ref, o_ref, acc_scr):
    b = pl.program_id(0)
    p_id = pl.program_id(1)
    i = pl.program_id(2)
    n_blocks = fk0_ref.shape[1]
    row = lax.broadcasted_iota(jnp.int32, (TQ, TK), 0)
    col = lax.broadcasted_iota(jnp.int32, (TQ, TK), 1)
    diag_mask = col <= row

    acc0 = jnp.zeros((TQ, LANES_V7X), F32)
    chains = range(FOX_SUB * HEADS_PER_LANE_TILE)
    bh0 = (b * N_HEAD_PAIRS + p_id) * HEADS_PER_LANE_TILE

    def scores(chain, j):
        sub, head = chain // HEADS_PER_LANE_TILE, chain % HEADS_PER_LANE_TILE
        q = qa_ref[head, pl.ds(pl.multiple_of(sub * TQ, TQ), TQ), :]
        return _dot_nt(q, ka_ref[head, pl.ds(pl.multiple_of(j * TK, TK), TK), :])

    def values(chain, j):
        vj = v_ref[pl.ds(pl.multiple_of(j * TK, TK), TK), :]
        own = _head_select(chain % HEADS_PER_LANE_TILE)
        return jnp.where(own, vj, jnp.ones_like(vj))

    def diag_block(chain):
        return i * FOX_SUB + chain // HEADS_PER_LANE_TILE

    def count_left(chain):
        bh, qi = bh0 + chain % HEADS_PER_LANE_TILE, diag_block(chain)
        bound = a_ref[bh, qi]

        def more(n):
            j = qi - 1 - n
            right = fk0_ref[bh, jnp.minimum(j + 1, n_blocks - 1)]
            return jnp.logical_and(j >= 0, bound - right > -ZERO_EXP)

        return lax.while_loop(more, lambda n: n + 1, jnp.int32(0))

    def fast_walks():
        counts = [count_left(chain) for chain in chains]
        firsts = [sum(n + 1 for n in counts[:chain]) for chain in chains]
        n_items = firsts[-1] + counts[-1] + 1
        diag_bias = jnp.where(diag_mask, 0.0, NEG_BIG)
        acc_scr[...] = jnp.zeros_like(acc_scr)

        def step(t, p, p_chain, p_block):
            chain = sum(jnp.where(t >= first, 1, 0) for first in firsts[1:])
            first = functools.reduce(lambda f, nxt: jnp.where(t >= nxt, nxt, f), firsts[1:], 0)
            back = t - first
            block = jnp.maximum(diag_block(chain) - back, 0)
            s = scores(chain, block) + jnp.where(back == 0, diag_bias,
                                                 jnp.where(t >= n_items, NEG_BIG, 0.0))
            acc_scr[p_chain] += _dot(p, values(p_chain, p_block))
            return jnp.exp(s).astype(BF16), chain, block

        def steps(first, count, carried):
            for t in range(count):
                carried = step(first + t, *carried)
            return carried

        pairs = (n_items + 2) // 2
        start = (jnp.zeros((TQ, TK), BF16), jnp.int32(0), jnp.int32(0))
        carried = lax.fori_loop(0, pairs // 2, lambda u, c: steps(4 * u, 4, c), start)
        lax.fori_loop(0, pairs % 2, lambda _, c: steps(4 * (pairs // 2), 2, c), carried)
        return [acc_scr[chain] for chain in chains]

    def online_walks():
        accs = []
        for chain in chains:
            def body(t, carried, chain=chain):
                acc, m = carried
                block = diag_block(chain) - t
                s = jnp.where(jnp.logical_or(t > 0, diag_mask), scores(chain, block), NEG_BIG)
                m_new = jnp.maximum(m, jnp.max(s, axis=-1, keepdims=True))
                p = jnp.exp(s - m_new).astype(BF16)
                return jnp.exp(m - m_new) * acc + _dot(p, values(chain, block)), m_new

            m0 = jnp.full((TQ, 1), NEG_BIG, F32)
            accs.append(lax.fori_loop(0, count_left(chain) + 1, body, (acc0, m0))[0])
        return accs

    largest_reach = functools.reduce(jnp.maximum, [
        g_ref[bh0 + chain % HEADS_PER_LANE_TILE, diag_block(chain)] for chain in chains])
    accs = lax.cond(2.0 * largest_reach < SAFE_EXP_RANGE, fast_walks, online_walks)
    outs = [acc / pltpu.roll(acc, HEAD_DIM, axis=1) for acc in accs]
    for sub in range(FOX_SUB):
        pair = outs[sub * HEADS_PER_LANE_TILE:(sub + 1) * HEADS_PER_LANE_TILE]
        o_ref[sub * TQ:(sub + 1) * TQ, :] = jnp.where(_head_select(0), pair[0], pair[1]).astype(o_ref.dtype)


def _fox_attn(q_aug, k_aug, v, walk_bound, fk_block_start, max_reach, batch, seq):
    assert TQ == TK
    rows = FOX_SUB * TQ
    n_chains = FOX_SUB * HEADS_PER_LANE_TILE
    block_bytes = (2 * (HEADS_PER_LANE_TILE * (rows + seq) * LANES_V7X + seq * LANES_V7X + rows * LANES_V7X)
                   + 4 * n_chains * TQ * LANES_V7X)
    smem = pl.BlockSpec(memory_space=pltpu.SMEM)
    return pl.pallas_call(
        _fox_kernel,
        out_shape=jax.ShapeDtypeStruct((batch, seq, D_MODEL), BF16),
        grid=(batch, N_HEAD_PAIRS, seq // rows),
        in_specs=[
            smem, smem, smem,
            pl.BlockSpec((None, HEADS_PER_LANE_TILE, rows, LANES_V7X), lambda b, p, i: (b, p, i, 0)),
            pl.BlockSpec((None, HEADS_PER_LANE_TILE, seq, LANES_V7X), lambda b, p, i: (b, p, 0, 0)),
            pl.BlockSpec((None, seq, LANES_V7X), lambda b, p, i: (b, 0, p)),
        ],
        out_specs=pl.BlockSpec((None, rows, LANES_V7X), lambda b, p, i: (b, i, p)),
        scratch_shapes=[pltpu.VMEM((n_chains, TQ, LANES_V7X), F32)],
        compiler_params=pltpu.CompilerParams(
            dimension_semantics=("arbitrary", "arbitrary", "arbitrary"),
            vmem_limit_bytes=_vmem_limit(block_bytes)),
        name="fox_attn",
    )(walk_bound, fk_block_start, max_reach, q_aug, k_aug, v)


def kernel(x, c, ada_w, ada_b, norm_attn_g, norm_ffn_g, w_ffn_in, w_ffn_down, sb_w_qkv, sb_w_o,
           kv_ada_w, kv_ada_b, kv_norm_g, w_kvf, b_f, k_norm_g, fox_w_q, q_norm_g, fox_w_o):
    batch, seq, d = x.shape
    assert d == D_MODEL and seq % TM == 0
    assert seq % (FOX_SUB * TQ) == 0 and seq % (SB_SUB * SB_T) == 0
    assert ada_w.shape[0] == 2 and sb_w_qkv.shape[0] == 1 and fox_w_q.shape[0] == 1
    m = batch * seq

    c_pad = jnp.zeros((SUBLANES_V7X, d), F32).at[:batch].set(c.astype(F32))
    mod = _ada_mod(c_pad, ada_w, ada_b)[:, :batch]
    kv_mod = _ada_mod(c_pad, kv_ada_w[None], kv_ada_b[None])[0, :batch]

    def vecs(t, n):
        return [t[:, None, j * d:(j + 1) * d] for j in range(n)]

    row = lambda t: t.reshape(1, -1).astype(F32)
    bf = lambda t: t.astype(BF16)

    x2d = x.reshape(m, d)

    sh_a, sc_a, g_a, sh_f, sc_f, g_f = vecs(mod[0], 6)
    qkv_scale = jnp.concatenate([jnp.full((1, d), LOG2_E * HEAD_DIM ** -0.5, F32),
                                 jnp.ones((1, 2 * d), F32)], axis=1)
    qkv = _norm_matmul(x2d, row(norm_attn_g[0]), sh_a, sc_a, bf(sb_w_qkv[0]), qkv_scale, seq)
    o = _sb_attn(qkv.reshape(batch, seq, 3 * d), batch, seq)
    x2d, act = _out_ffn_in(o.reshape(m, d), bf(sb_w_o[0]), x2d, g_a,
                           row(norm_ffn_g[0]), sh_f, sc_f, bf(w_ffn_in[0]), seq)
    g_f0 = g_f

    sh_a, sc_a, g_a, sh_f, sc_f, g_f = vecs(mod[1], 6)
    kv_shift, kv_scale = vecs(kv_mod, 2)
    w_f_pad = jnp.zeros((d, LANES_V7X), F32).at[:, :N_HEADS].set(w_kvf[:, 2 * d:])
    b_f_pad = jnp.zeros((1, LANES_V7X), F32).at[0, :N_HEADS].set(b_f)
    x2d, k, v, q, log_f = _l1_proj(
        act, bf(w_ffn_down[0]), x2d, g_f0,
        row(kv_norm_g), kv_shift, kv_scale, row(norm_attn_g[1]), sh_a, sc_a,
        bf(w_kvf[:, :d]), bf(w_kvf[:, d:2 * d]), bf(w_f_pad), bf(fox_w_q[0]),
        row(jnp.tile(k_norm_g, N_HEADS)), row(jnp.tile(q_norm_g[0], N_HEADS)), b_f_pad, seq)
    key_norm_bound = (1.01 * HEAD_DIM ** 0.5 * jnp.max(jnp.abs(k_norm_g))).reshape(1, 1).astype(F32)
    q_aug, k_aug, stats = _fox_prep(log_f, q, k, key_norm_bound, batch, seq)
    tables = stats[:, :, :3, :N_HEADS].transpose(2, 0, 3, 1).reshape(3, batch * N_HEADS, seq // TK)
    o = _fox_attn(q_aug, k_aug, v.reshape(batch, seq, d), tables[0], tables[1], tables[2], batch, seq)
    x2d, act = _out_ffn_in(o.reshape(m, d), bf(fox_w_o[0]), x2d, g_a,
                           row(norm_ffn_g[1]), sh_f, sc_f, bf(w_ffn_in[1]), seq)
    x2d = _proj_residual(act, bf(w_ffn_down[1]), x2d, g_f, seq)
    return x2d.reshape(batch, seq, d)
```

```python
import functools

import jax
import jax.numpy as jnp
from jax import lax
from jax.experimental import pallas as pl
from jax.experimental.pallas import tpu as pltpu

D_MODEL = 1024
N_HEADS = 16
HEAD_DIM = D_MODEL // N_HEADS
EPS = 1e-6

LANES_V7X = 128
SUBLANES_V7X = 8
VMEM_BYTES_V7X = 64 * 1024 * 1024

HEADS_PER_LANE_TILE = LANES_V7X // HEAD_DIM
N_HEAD_PAIRS = N_HEADS // HEADS_PER_LANE_TILE

MXU_WIDTH_V7X = 256

TM = 512
COL_CHUNK = 2 * MXU_WIDTH_V7X
TQ = 512
TK = 512
FOX_SUB = 4
SB_T = 128
SB_LEFT = 256
SB_SUB = 16

ZERO_EXP = 104.5
SAFE_EXP_RANGE = 80.0
NEG_BIG = -1e30
LOG2_E = 1.4426950408889634
SOFTPLUS_LINEAR = 64.0

F32 = jnp.float32
BF16 = jnp.bfloat16


def _vmem_limit(block_bytes):
    want = 2 * block_bytes + 16 * 1024 * 1024
    return int(min(want, VMEM_BYTES_V7X - 8 * 1024 * 1024))


def _split2(x):
    hi = x.astype(BF16)
    lo = (x - hi.astype(F32)).astype(BF16)
    return hi, lo


def _dot(a, b):
    return jnp.dot(a, b, preferred_element_type=F32)


def _dot_nt(a, b):
    return lax.dot_general(a, b, (((1,), (1,)), ((), ())), preferred_element_type=F32)


def _ada_kernel(ct_ref, w_ref, b_ref, o_ref, *, batch):
    c = ct_ref[...]
    c_act = c * jax.nn.sigmoid(c)
    w = w_ref[...]
    o_ref[...] = jnp.zeros_like(o_ref)
    for b in range(batch):
        o_ref[b:b + 1, :] = jnp.sum(w * c_act[:, b:b + 1], axis=0, keepdims=True) + b_ref[...]


def _ada_mod(c, w, b):
    n_layers, d, n = w.shape
    batch = c.shape[0]
    rows = SUBLANES_V7X
    assert batch <= rows
    tn = 1024
    c_t = jnp.zeros((d, rows), F32).at[:, :batch].set(c.astype(F32).T)
    block_bytes = 4 * (d * tn + d * LANES_V7X + tn + rows * tn)
    return pl.pallas_call(
        functools.partial(_ada_kernel, batch=batch),
        out_shape=jax.ShapeDtypeStruct((n_layers, rows, n), F32),
        grid=(n_layers, n // tn),
        in_specs=[
            pl.BlockSpec((d, rows), lambda l, j: (0, 0)),
            pl.BlockSpec((None, d, tn), lambda l, j: (l, 0, j)),
            pl.BlockSpec((None, 1, tn), lambda l, j: (l, 0, j)),
        ],
        out_specs=pl.BlockSpec((None, rows, tn), lambda l, j: (l, 0, j)),
        compiler_params=pltpu.CompilerParams(
            dimension_semantics=("arbitrary", "arbitrary"),
            vmem_limit_bytes=_vmem_limit(block_bytes)),
        name="ada_mod",
    )(c_t, w, b.reshape(n_layers, 1, n))


def _norm_modulate(x, g, shift, scale):
    ms = jnp.mean(x * x, axis=-1, keepdims=True)
    y = x * lax.rsqrt(ms + EPS)
    return (y * g) * (1.0 + scale) + shift


def _row_spec(d):
    return pl.BlockSpec((1, d), lambda *_: (0, 0))


def _resident_spec(shape, block_index=(0, 0)):
    return pl.BlockSpec(shape, lambda *_: block_index)


def _batch_vec_spec(d, tiles_per_batch):
    return pl.BlockSpec((None, 1, d), lambda i, *_: (i // tiles_per_batch, 0, 0))


def _column_chunks(n):
    return [(c, min(c + COL_CHUNK, n)) for c in range(0, n, COL_CHUNK)]


def _norm_matmul_kernel(x_ref, g_ref, sh_ref, sc_ref, w_ref, cs_ref, o_ref):
    h = _norm_modulate(x_ref[...], g_ref[...], sh_ref[...], sc_ref[...]).astype(BF16)
    for c0, c1 in _column_chunks(o_ref.shape[1]):
        o_ref[:, c0:c1] = (_dot(h, w_ref[:, c0:c1]) * cs_ref[:, c0:c1]).astype(o_ref.dtype)


def _norm_matmul(x2d, g, shift, scale, w, col_scale, seq):
    m, d = x2d.shape
    n = w.shape[1]
    block_bytes = 4 * TM * d + 2 * d * n + 2 * TM * n + 4 * n
    return pl.pallas_call(
        _norm_matmul_kernel,
        out_shape=jax.ShapeDtypeStruct((m, n), BF16),
        grid=(m // TM,),
        in_specs=[
            pl.BlockSpec((TM, d), lambda i: (i, 0)),
            _row_spec(d),
            _batch_vec_spec(d, seq // TM),
            _batch_vec_spec(d, seq // TM),
            _resident_spec((d, n)),
            pl.BlockSpec((1, n), lambda i: (0, 0)),
        ],
        out_specs=pl.BlockSpec((TM, n), lambda i: (i, 0)),
        compiler_params=pltpu.CompilerParams(
            dimension_semantics=("arbitrary",),
            vmem_limit_bytes=_vmem_limit(block_bytes)),
        name="norm_matmul",
    )(x2d, g, shift, scale, w, col_scale)


def _out_ffn_in_kernel(a_ref, wo_ref, x_ref, gate_ref, g_ref, sh_ref, sc_ref, wg_ref, wu_ref,
                       x1_ref, act_ref):
    x1 = x_ref[...] + gate_ref[...] * _dot(a_ref[...], wo_ref[...])
    x1_ref[...] = x1
    h = _norm_modulate(x1, g_ref[...], sh_ref[...], sc_ref[...]).astype(BF16)
    for c0, c1 in _column_chunks(act_ref.shape[1]):
        gate = _dot(h, wg_ref[:, c0:c1])
        up = _dot(h, wu_ref[:, c0:c1])
        act_ref[:, c0:c1] = (gate * jax.nn.sigmoid(gate) * up).astype(act_ref.dtype)


def _out_ffn_in(a, w_o, x2d, gate, g, shift, scale, w_in, seq):
    m, d = x2d.shape
    f = w_in.shape[1] // 2
    tpb = seq // TM
    block_bytes = 2 * TM * d + 2 * d * d + 2 * 4 * TM * d + 2 * 2 * d * f + 2 * TM * f
    return pl.pallas_call(
        _out_ffn_in_kernel,
        out_shape=(jax.ShapeDtypeStruct((m, d), F32), jax.ShapeDtypeStruct((m, f), BF16)),
        grid=(m // TM,),
        in_specs=[
            pl.BlockSpec((TM, d), lambda i: (i, 0)),
            _resident_spec((d, d)),
            pl.BlockSpec((TM, d), lambda i: (i, 0)),
            _batch_vec_spec(d, tpb),
            _row_spec(d),
            _batch_vec_spec(d, tpb),
            _batch_vec_spec(d, tpb),
            _resident_spec((d, f)),
            _resident_spec((d, f), (0, 1)),
        ],
        out_specs=(pl.BlockSpec((TM, d), lambda i: (i, 0)), pl.BlockSpec((TM, f), lambda i: (i, 0))),
        compiler_params=pltpu.CompilerParams(
            dimension_semantics=("arbitrary",),
            vmem_limit_bytes=_vmem_limit(block_bytes)),
        name="out_ffn_in",
    )(a, w_o, x2d, gate, g, shift, scale, w_in, w_in)


def _proj_residual_kernel(a_ref, w_ref, x_ref, gate_ref, o_ref):
    o_ref[...] = x_ref[...] + gate_ref[...] * _dot(a_ref[...], w_ref[...])


def _proj_residual(a, w, x2d, gate, seq):
    m, k = a.shape
    d = w.shape[1]
    block_bytes = 2 * TM * k + 2 * k * d + 2 * 4 * TM * d
    return pl.pallas_call(
        _proj_residual_kernel,
        out_shape=jax.ShapeDtypeStruct((m, d), F32),
        grid=(m // TM,),
        in_specs=[
            pl.BlockSpec((TM, k), lambda i: (i, 0)),
            _resident_spec((k, d)),
            pl.BlockSpec((TM, d), lambda i: (i, 0)),
            _batch_vec_spec(d, seq // TM),
        ],
        out_specs=pl.BlockSpec((TM, d), lambda i: (i, 0)),
        compiler_params=pltpu.CompilerParams(
            dimension_semantics=("arbitrary",),
            vmem_limit_bytes=_vmem_limit(block_bytes)),
        name="proj_residual",
    )(a, w, x2d, gate)


def _head_rms_norm(t, gain, group_mean):
    ms = _dot((t * t).astype(BF16), group_mean)
    return t * lax.rsqrt(ms + EPS) * gain


def _l1_proj_kernel(act_ref, wd_ref, x_ref, gate_ref,
                    gkv_ref, shkv_ref, sckv_ref, gq_ref, shq_ref, scq_ref,
                    wk_ref, wv_ref, wf_ref, wq_ref, kg_ref, qg_ref, bf_ref, gm_ref,
                    x2_ref, k_ref, v_ref, q_ref, lf_ref):
    x = x_ref[...] + gate_ref[...] * _dot(act_ref[...], wd_ref[...])
    x2_ref[...] = x
    ms = jnp.mean(x * x, axis=-1, keepdims=True)
    y = x * lax.rsqrt(ms + EPS)
    h_kv = ((y * gkv_ref[...]) * (1.0 + sckv_ref[...]) + shkv_ref[...]).astype(BF16)
    h_q = ((y * gq_ref[...]) * (1.0 + scq_ref[...]) + shq_ref[...]).astype(BF16)
    gm = gm_ref[...]
    k_all = _dot(h_kv, wk_ref[...])
    q_all = _dot(h_q, wq_ref[...])
    v_ref[...] = _dot(h_kv, wv_ref[...]).astype(BF16)
    lf_ref[...] = jax.nn.log_sigmoid(_dot(h_kv, wf_ref[...]) + bf_ref[...])
    for c0 in range(0, k_ref.shape[1], MXU_WIDTH_V7X):
        cols = slice(c0, c0 + MXU_WIDTH_V7X)
        k_ref[:, cols] = _head_rms_norm(k_all[:, cols], kg_ref[:, cols], gm).astype(BF16)
        q = _head_rms_norm(q_all[:, cols], qg_ref[:, cols], gm)
        q_ref[:, cols] = (q * (HEAD_DIM ** -0.5)).astype(BF16)


def _l1_proj(act, w_down, x2d, gate, g_kv, sh_kv, sc_kv, g_q, sh_q, sc_q, wk, wv, wf, wq,
             k_gain, q_gain, b_f, seq):
    m, d = x2d.shape
    f = act.shape[1]
    lane = jnp.arange(MXU_WIDTH_V7X)
    group_mean = jnp.where(lane[:, None] // HEAD_DIM == lane[None, :] // HEAD_DIM,
                           1.0 / HEAD_DIM, 0.0).astype(BF16)
    tpb = seq // TM
    block_bytes = (2 * TM * f + 2 * f * d + 2 * 4 * TM * d + 3 * 2 * d * d + 2 * d * LANES_V7X
                   + 3 * 2 * TM * d + 4 * TM * LANES_V7X)
    full = lambda r, c: _resident_spec((r, c))
    rows = lambda c: pl.BlockSpec((TM, c), lambda i: (i, 0))
    return pl.pallas_call(
        _l1_proj_kernel,
        out_shape=(jax.ShapeDtypeStruct((m, d), F32), jax.ShapeDtypeStruct((m, d), BF16),
                   jax.ShapeDtypeStruct((m, d), BF16), jax.ShapeDtypeStruct((m, d), BF16),
                   jax.ShapeDtypeStruct((m, LANES_V7X), F32)),
        grid=(m // TM,),
        in_specs=[
            rows(f), full(f, d), rows(d), _batch_vec_spec(d, tpb),
            _row_spec(d), _batch_vec_spec(d, tpb), _batch_vec_spec(d, tpb),
            _row_spec(d), _batch_vec_spec(d, tpb), _batch_vec_spec(d, tpb),
            full(d, d), full(d, d), full(d, LANES_V7X), full(d, d),
            _row_spec(d), _row_spec(d), _row_spec(LANES_V7X), full(MXU_WIDTH_V7X, MXU_WIDTH_V7X),
        ],
        out_specs=(rows(d), rows(d), rows(d), rows(d), rows(LANES_V7X)),
        compiler_params=pltpu.CompilerParams(
            dimension_semantics=("arbitrary",),
            vmem_limit_bytes=_vmem_limit(block_bytes)),
        name="l1_proj",
    )(act, w_down, x2d, gate, g_kv, sh_kv, sc_kv, g_q, sh_q, sc_q, wk, wv, wf, wq,
      k_gain, q_gain, b_f, group_mean)


N_SPLIT = 3


def _split3(x):
    x0 = x.astype(BF16)
    r1 = x - x0.astype(F32)
    x1 = r1.astype(BF16)
    x2 = (r1 - x1.astype(F32)).astype(BF16)
    return x0, x1, x2


def _fox_prep_kernel(kb_ref, lf_ref, q_ref, k_ref, qa_ref, ka_ref, stats_ref, carry_scr):
    @pl.when(pl.program_id(1) == 0)
    def _():
        carry_scr[...] = jnp.zeros_like(carry_scr)

    r = lax.broadcasted_iota(jnp.int32, (TK, TK), 0)
    c = lax.broadcasted_iota(jnp.int32, (TK, TK), 1)
    tri = jnp.where(c <= r, 1.0, 0.0).astype(BF16)
    x0, x1, x2 = _split3(lf_ref[...])
    cs = _dot(tri, x0) + _dot(tri, x1) + _dot(tri, x2) + carry_scr[0:1, :]
    carry_scr[...] = jnp.broadcast_to(cs[TK - 1:TK, :], carry_scr.shape)

    d_idx = lax.broadcasted_iota(jnp.int32, (D_MODEL, LANES_V7X), 0)
    h_idx = lax.broadcasted_iota(jnp.int32, (D_MODEL, LANES_V7X), 1)
    head_of = jnp.where(d_idx // HEAD_DIM == h_idx, 1.0, 0.0).astype(BF16)
    qf = q_ref[...].astype(F32)
    reach = jnp.sqrt(_dot((qf * qf).astype(BF16), head_of)) * kb_ref[0, 0]
    own_logit = _dot((qf * k_ref[...].astype(F32)).astype(BF16), head_of)

    stats_ref[...] = jnp.concatenate([
        jnp.max(reach * (1.0 + 2.0 ** -8) - own_logit + cs, axis=0, keepdims=True),
        cs[0:1, :],
        jnp.max(reach, axis=0, keepdims=True),
        jnp.zeros((SUBLANES_V7X - 3, LANES_V7X), F32)], axis=0)

    lane = lax.broadcasted_iota(jnp.int32, (1, LANES_V7X), 1)
    packed = jnp.zeros((TK, LANES_V7X), F32)
    for i, term in enumerate(_split3(cs - reach) + _split3(-cs)):
        term = jnp.where(lane < N_HEADS, term.astype(F32), 0.0)
        packed = packed + (term if i == 0 else pltpu.roll(term, N_HEADS * i, axis=1))
    packed = packed.astype(BF16)

    src = lax.broadcasted_iota(jnp.int32, (LANES_V7X, 2 * LANES_V7X), 0)
    dst = lax.broadcasted_iota(jnp.int32, (LANES_V7X, 2 * LANES_V7X), 1)

    for h in range(N_HEADS):
        pair, parity = divmod(h, HEADS_PER_LANE_TILE)
        spare = HEAD_DIM * (1 - parity)
        place = jnp.where((src % N_HEADS == h) & (src < 2 * N_SPLIT * N_HEADS)
                          & (dst % LANES_V7X == spare + src // N_HEADS)
                          & ((dst >= LANES_V7X) == (src >= N_SPLIT * N_HEADS)), 1.0, 0.0)
        placed = _dot(packed, place.astype(BF16))
        ones_q = jnp.where((lane >= spare + N_SPLIT) & (lane < spare + 2 * N_SPLIT), 1.0, 0.0)
        ones_k = jnp.where((lane >= spare) & (lane < spare + N_SPLIT), 1.0, 0.0)
        aug_q = (placed[:, :LANES_V7X] + ones_q).astype(BF16)
        aug_k = (placed[:, LANES_V7X:] + ones_k).astype(BF16)
        own = (lane // HEAD_DIM) == parity
        cols = slice(pair * LANES_V7X, (pair + 1) * LANES_V7X)
        qa_ref[h] = jnp.where(own, q_ref[:, cols], aug_q)
        ka_ref[h] = jnp.where(own, k_ref[:, cols], aug_k)


def _fox_prep(log_f, q, k, key_norm_bound, batch, seq):
    assert TQ == TK
    d = q.shape[1]
    block_bytes = (4 * TK * LANES_V7X + 2 * 2 * TK * d + 2 * 2 * N_HEADS * TK * LANES_V7X
                   + 4 * SUBLANES_V7X * LANES_V7X)
    aug_shape = jax.ShapeDtypeStruct((batch, N_HEADS, seq, LANES_V7X), BF16)
    aug_spec = pl.BlockSpec((None, N_HEADS, TK, LANES_V7X), lambda b, t: (b, 0, t, 0))
    return pl.pallas_call(
        _fox_prep_kernel,
        out_shape=(aug_shape, aug_shape,
                   jax.ShapeDtypeStruct((batch, seq // TK, SUBLANES_V7X, LANES_V7X), F32)),
        grid=(batch, seq // TK),
        in_specs=[
            pl.BlockSpec(memory_space=pltpu.SMEM),
            pl.BlockSpec((None, TK, LANES_V7X), lambda b, t: (b, t, 0)),
            pl.BlockSpec((None, TK, d), lambda b, t: (b, t, 0)),
            pl.BlockSpec((None, TK, d), lambda b, t: (b, t, 0)),
        ],
        out_specs=(aug_spec, aug_spec,
                   pl.BlockSpec((None, None, SUBLANES_V7X, LANES_V7X), lambda b, t: (b, t, 0, 0))),
        scratch_shapes=[pltpu.VMEM((SUBLANES_V7X, LANES_V7X), F32)],
        compiler_params=pltpu.CompilerParams(
            dimension_semantics=("arbitrary", "arbitrary"),
            vmem_limit_bytes=_vmem_limit(block_bytes)),
        name="fox_prep",
    )(key_norm_bound, log_f.reshape(batch, seq, LANES_V7X), q.reshape(batch, seq, d),
      k.reshape(batch, seq, d))


def _head_select(head):
    lane = lax.broadcasted_iota(jnp.int32, (1, LANES_V7X), 1)
    return (lane // HEAD_DIM) == head


def _sb_kernel(q_ref, k_ref, v_ref, o_ref):
    i = pl.program_id(2)

    def strictly_lower(n):
        return (lax.broadcasted_iota(jnp.int32, (n, n), 0)
                > lax.broadcasted_iota(jnp.int32, (n, n), 1))

    diag_bias = jnp.where(strictly_lower(SB_T), 0.0, NEG_BIG)
    after = jnp.where(strictly_lower(SB_T), 1.0, 0.0).astype(BF16)
    after_left = jnp.where(strictly_lower(SB_LEFT), 1.0, 0.0).astype(BF16)
    ones = jnp.ones((SB_T, LANES_V7X), BF16)
    ones_left = jnp.ones((SB_T, SB_LEFT), BF16)

    def softplus(z):
        return jnp.maximum(jnp.log2(1.0 + jnp.exp2(jnp.minimum(z, SOFTPLUS_LINEAR))), z)

    def keys(start, size):
        rows = pl.ds(pl.multiple_of(start, SB_T), size)
        return k_ref[rows, :], v_ref[rows, :]

    def left_start(g):
        return jnp.maximum(g * SB_T - SB_LEFT, 0)

    def band_logits(qh, g, may_clip):
        k_d, _ = keys(g * SB_T, SB_T)
        k_p, _ = keys(left_start(g), SB_LEFT)
        z_p = _dot_nt(qh, k_p)
        if may_clip:
            key = left_start(g) + lax.broadcasted_iota(jnp.int32, (SB_T, SB_LEFT), 1)
            z_p = jnp.where(key < g * SB_T, z_p, NEG_BIG)
        return _dot_nt(qh, k_d) + diag_bias, z_p

    def band_softplus(z_d, z_p):
        return z_d, z_p, softplus(z_d), softplus(z_p)

    def band_sums(z_d, z_p, sp_d, sp_p):
        sp_d16 = sp_d.astype(BF16)
        within_d = _dot(sp_d16, after)
        within_p = _dot(sp_p.astype(BF16), after_left) + _dot(sp_d16, ones_left)
        total = (within_p + sp_p)[:, 0:1]
        return (z_d - sp_d) - within_d, (z_p - sp_p) - within_p, total

    def band_weights(e_d, e_p, total):
        return jnp.exp2(e_d).astype(BF16), jnp.exp2(e_p).astype(BF16), total

    def band_out(g, a_d, a_p, total):
        _, v_d = keys(g * SB_T, SB_T)
        _, v_p = keys(left_start(g), SB_LEFT)
        return jnp.broadcast_to(-total, (SB_T, LANES_V7X)), _dot(a_d, v_d) + _dot(a_p, v_p)

    def block(qh, start, carry, acc):
        kj, vj = keys(start, SB_T)
        z = _dot_nt(qh, kj)
        sp = softplus(z)
        sp16 = sp.astype(BF16)
        within = _dot(sp16, after)
        a = jnp.exp2((z - sp) - within + carry)
        acc = acc + _dot(a.astype(BF16), vj)
        carry = carry - _dot(sp16, ones)
        return carry, acc

    heads = []
    for u in range(SB_SUB):
        g = i * SB_SUB + u
        q = q_ref[u * SB_T:(u + 1) * SB_T, :]
        for head in range(HEADS_PER_LANE_TILE):
            heads.append((g, jnp.where(_head_select(head), q, 0.0).astype(BF16)))
    logits = [band_logits(qh, g, may_clip=c // HEADS_PER_LANE_TILE < SB_LEFT // SB_T)
              for c, (g, qh) in enumerate(heads)]
    softplus_terms = [band_softplus(*z) for z in logits]
    sums = [band_sums(*t) for t in softplus_terms]
    weights = [band_weights(*s) for s in sums]
    chains = [(g, qh) + band_out(g, *w) for (g, qh), w in zip(heads, weights)]

    def is_open(carry):
        return jnp.max(carry) > -ZERO_EXP * LOG2_E

    chain_open = [is_open(carry) for _, _, carry, _ in chains]

    def walk_on():
        outs = []
        for (g, qh, carry, acc), this_open in zip(chains, chain_open):
            def cond(state):
                start, carry, _ = state
                return jnp.logical_and(start >= 0, is_open(carry))

            def body(state, qh=qh):
                start, carry, acc = state
                carry, acc = block(qh, start, carry, acc)
                return start - SB_T, carry, acc

            def walk(cond=cond, body=body, first=(left_start(g) - SB_T, carry, acc)):
                return lax.while_loop(cond, body, first)[2]

            outs.append(lax.cond(this_open, walk, lambda acc=acc: acc))
        return outs

    outs = lax.cond(functools.reduce(jnp.logical_or, chain_open), walk_on,
                    lambda: [acc for _, _, _, acc in chains])
    for u in range(SB_SUB):
        pair = outs[u * HEADS_PER_LANE_TILE:(u + 1) * HEADS_PER_LANE_TILE]
        o_ref[u * SB_T:(u + 1) * SB_T, :] = jnp.where(_head_select(0), pair[0], pair[1]).astype(o_ref.dtype)


def _sb_attn(qkv, batch, seq):
    assert SB_T == LANES_V7X and SB_LEFT % SB_T == 0
    rows = SB_SUB * SB_T
    stage_bytes = SB_SUB * HEADS_PER_LANE_TILE * 2 * 4 * SB_T * (SB_T + SB_LEFT)
    block_bytes = 2 * (2 * rows * LANES_V7X + 2 * seq * LANES_V7X) + stage_bytes // 2
    return pl.pallas_call(
        _sb_kernel,
        out_shape=jax.ShapeDtypeStruct((batch, seq, D_MODEL), BF16),
        grid=(batch, N_HEAD_PAIRS, seq // rows),
        in_specs=[
            pl.BlockSpec((None, rows, LANES_V7X), lambda b, p, i: (b, i, p)),
            pl.BlockSpec((None, seq, LANES_V7X), lambda b, p, i: (b, 0, N_HEAD_PAIRS + p)),
            pl.BlockSpec((None, seq, LANES_V7X), lambda b, p, i: (b, 0, 2 * N_HEAD_PAIRS + p)),
        ],
        out_specs=pl.BlockSpec((None, rows, LANES_V7X), lambda b, p, i: (b, i, p)),
        compiler_params=pltpu.CompilerParams(
            dimension_semantics=("arbitrary", "arbitrary", "arbitrary"),
            vmem_limit_bytes=_vmem_limit(block_bytes)),
        name="sb_attn",
    )(qkv, qkv, qkv)


def _fox_kernel(a_ref, fk0_ref, g_ref, qa_ref, ka_ref, v_ref, o_ref, acc_scr):
    b = pl.program_id(0)
    p_id = pl.program_id(1)
    i = pl.program_id(2)
    n_blocks = fk0_ref.shape[1]
    row = lax.broadcasted_iota(jnp.int32, (TQ, TK), 0)
    col = lax.broadcasted_iota(jnp.int32, (TQ, TK), 1)
    diag_mask = col <= row

    acc0 = jnp.zeros((TQ, LANES_V7X), F32)
    chains = range(FOX_SUB * HEADS_PER_LANE_TILE)
    bh0 = (b * N_HEAD_PAIRS + p_id) * HEADS_PER_LANE_TILE

    def scores(chain, j):
        sub, head = chain // HEADS_PER_LANE_TILE, chain % HEADS_PER_LANE_TILE
        q = qa_ref[head, pl.ds(pl.multiple_of(sub * TQ, TQ), TQ), :]
        return _dot_nt(q, ka_ref[head, pl.ds(pl.multiple_of(j * TK, TK), TK), :])

    def values(chain, j):
        vj = v_ref[pl.ds(pl.multiple_of(j * TK, TK), TK), :]
        own = _head_select(chain % HEADS_PER_LANE_TILE)
        return jnp.where(own, vj, jnp.ones_like(vj))

    def diag_block(chain):
        return i * FOX_SUB + chain // HEADS_PER_LANE_TILE

    def count_left(chain):
        bh, qi = bh0 + chain % HEADS_PER_LANE_TILE, diag_block(chain)
        bound = a_ref[bh, qi]

        def more(n):
            j = qi - 1 - n
            right = fk0_ref[bh, jnp.minimum(j + 1, n_blocks - 1)]
            return jnp.logical_and(j >= 0, bound - right > -ZERO_EXP)

        return lax.while_loop(more, lambda n: n + 1, jnp.int32(0))

    def fast_walks():
        counts = [count_left(chain) for chain in chains]
        firsts = [sum(n + 1 for n in counts[:chain]) for chain in chains]
        n_items = firsts[-1] + counts[-1] + 1
        diag_bias = jnp.where(diag_mask, 0.0, NEG_BIG)
        acc_scr[...] = jnp.zeros_like(acc_scr)

        def step(t, p, p_chain, p_block):
            chain = sum(jnp.where(t >= first, 1, 0) for first in firsts[1:])
            first = functools.reduce(lambda f, nxt: jnp.where(t >= nxt, nxt, f), firsts[1:], 0)
            back = t - first
            block = jnp.maximum(diag_block(chain) - back, 0)
            s = scores(chain, block) + jnp.where(back == 0, diag_bias,
                                                 jnp.where(t >= n_items, NEG_BIG, 0.0))
            acc_scr[p_chain] += _dot(p, values(p_chain, p_block))
            return jnp.exp(s).astype(BF16), chain, block

        def steps(first, count, carried):
            for t in range(count):
                carried = step(first + t, *carried)
            return carried

        pairs = (n_items + 2) // 2
        start = (jnp.zeros((TQ, TK), BF16), jnp.int32(0), jnp.int32(0))
        carried = lax.fori_loop(0, pairs // 2, lambda u, c: steps(4 * u, 4, c), start)
        lax.fori_loop(0, pairs % 2, lambda _, c: steps(4 * (pairs // 2), 2, c), carried)
        return [acc_scr[chain] for chain in chains]

    def online_walks():
        accs = []
        for chain in chains:
            def body(t, carried, chain=chain):
                acc, m = carried
                block = diag_block(chain) - t
                s = jnp.where(jnp.logical_or(t > 0, diag_mask), scores(chain, block), NEG_BIG)
                m_new = jnp.maximum(m, jnp.max(s, axis=-1, keepdims=True))
                p = jnp.exp(s - m_new).astype(BF16)
                return jnp.exp(m - m_new) * acc + _dot(p, values(chain, block)), m_new

            m0 = jnp.full((TQ, 1), NEG_BIG, F32)
            accs.append(lax.fori_loop(0, count_left(chain) + 1, body, (acc0, m0))[0])
        return accs

    largest_reach = functools.reduce(jnp.maximum, [
        g_ref[bh0 + chain % HEADS_PER_LANE_TILE, diag_block(chain)] for chain in chains])
    accs = lax.cond(2.0 * largest_reach < SAFE_EXP_RANGE, fast_walks, online_walks)
    outs = [acc / pltpu.roll(acc, HEAD_DIM, axis=1) for acc in accs]
    for sub in range(FOX_SUB):
        pair = outs[sub * HEADS_PER_LANE_TILE:(sub + 1) * HEADS_PER_LANE_TILE]
        o_ref[sub * TQ:(sub + 1) * TQ, :] = jnp.where(_head_select(0), pair[0], pair[1]).astype(o_ref.dtype)


def _fox_attn(q_aug, k_aug, v, walk_bound, fk_block_start, max_reach, batch, seq):
    assert TQ == TK
    rows = FOX_SUB * TQ
    n_chains = FOX_SUB * HEADS_PER_LANE_TILE
    block_bytes = (2 * (HEADS_PER_LANE_TILE * (rows + seq) * LANES_V7X + seq * LANES_V7X + rows * LANES_V7X)
                   + 4 * n_chains * TQ * LANES_V7X)
    smem = pl.BlockSpec(memory_space=pltpu.SMEM)
    return pl.pallas_call(
        _fox_kernel,
        out_shape=jax.ShapeDtypeStruct((batch, seq, D_MODEL), BF16),
        grid=(batch, N_HEAD_PAIRS, seq // rows),
        in_specs=[
            smem, smem, smem,
            pl.BlockSpec((None, HEADS_PER_LANE_TILE, rows, LANES_V7X), lambda b, p, i: (b, p, i, 0)),
            pl.BlockSpec((None, HEADS_PER_LANE_TILE, seq, LANES_V7X), lambda b, p, i: (b, p, 0, 0)),
            pl.BlockSpec((None, seq, LANES_V7X), lambda b, p, i: (b, 0, p)),
        ],
        out_specs=pl.BlockSpec((None, rows, LANES_V7X), lambda b, p, i: (b, i, p)),
        scratch_shapes=[pltpu.VMEM((n_chains, TQ, LANES_V7X), F32)],
        compiler_params=pltpu.CompilerParams(
            dimension_semantics=("arbitrary", "arbitrary", "arbitrary"),
            vmem_limit_bytes=_vmem_limit(block_bytes)),
        name="fox_attn",
    )(walk_bound, fk_block_start, max_reach, q_aug, k_aug, v)


def kernel(x, c, ada_w, ada_b, norm_attn_g, norm_ffn_g, w_ffn_in, w_ffn_down, sb_w_qkv, sb_w_o,
           kv_ada_w, kv_ada_b, kv_norm_g, w_kvf, b_f, k_norm_g, fox_w_q, q_norm_g, fox_w_o):
    batch, seq, d = x.shape
    assert d == D_MODEL and seq % TM == 0
    assert seq % (FOX_SUB * TQ) == 0 and seq % (SB_SUB * SB_T) == 0
    assert ada_w.shape[0] == 2 and sb_w_qkv.shape[0] == 1 and fox_w_q.shape[0] == 1
    m = batch * seq

    mod = _ada_mod(c, ada_w, ada_b)[:, :batch]
    kv_mod = _ada_mod(c, kv_ada_w[None], kv_ada_b[None])[0, :batch]

    def vecs(t, n):
        return [t[:, None, j * d:(j + 1) * d] for j in range(n)]

    row = lambda t: t.reshape(1, -1).astype(F32)
    bf = lambda t: t.astype(BF16)

    x2d = x.reshape(m, d)

    sh_a, sc_a, g_a, sh_f, sc_f, g_f = vecs(mod[0], 6)
    qkv_scale = jnp.concatenate([jnp.full((1, d), LOG2_E * HEAD_DIM ** -0.5, F32),
                                 jnp.ones((1, 2 * d), F32)], axis=1)
    qkv = _norm_matmul(x2d, row(norm_attn_g[0]), sh_a, sc_a, bf(sb_w_qkv[0]), qkv_scale, seq)
    o = _sb_attn(qkv.reshape(batch, seq, 3 * d), batch, seq)
    x2d, act = _out_ffn_in(o.reshape(m, d), bf(sb_w_o[0]), x2d, g_a,
                           row(norm_ffn_g[0]), sh_f, sc_f, bf(w_ffn_in[0]), seq)
    g_f0 = g_f

    sh_a, sc_a, g_a, sh_f, sc_f, g_f = vecs(mod[1], 6)
    kv_shift, kv_scale = vecs(kv_mod, 2)
    w_f_pad = jnp.zeros((d, LANES_V7X), F32).at[:, :N_HEADS].set(w_kvf[:, 2 * d:])
    b_f_pad = jnp.zeros((1, LANES_V7X), F32).at[0, :N_HEADS].set(b_f)
    x2d, k, v, q, log_f = _l1_proj(
        act, bf(w_ffn_down[0]), x2d, g_f0,
        row(kv_norm_g), kv_shift, kv_scale, row(norm_attn_g[1]), sh_a, sc_a,
        bf(w_kvf[:, :d]), bf(w_kvf[:, d:2 * d]), bf(w_f_pad), bf(fox_w_q[0]),
        row(jnp.tile(k_norm_g, N_HEADS)), row(jnp.tile(q_norm_g[0], N_HEADS)), b_f_pad, seq)
    key_norm_bound = (1.01 * HEAD_DIM ** 0.5 * jnp.max(jnp.abs(k_norm_g))).reshape(1, 1).astype(F32)
    q_aug, k_aug, stats = _fox_prep(log_f, q, k, key_norm_bound, batch, seq)
    tables = stats[:, :, :3, :N_HEADS].transpose(2, 0, 3, 1).reshape(3, batch * N_HEADS, seq // TK)
    o = _fox_attn(q_aug, k_aug, v.reshape(batch, seq, d), tables[0], tables[1], tables[2], batch, seq)
    x2d, act = _out_ffn_in(o.reshape(m, d), bf(fox_w_o[0]), x2d, g_a,
                           row(norm_ffn_g[1]), sh_f, sc_f, bf(w_ffn_in[1]), seq)
    x2d = _proj_residual(act, bf(w_ffn_down[1]), x2d, g_f, seq)
    return x2d.reshape(batch, seq, d)
```

```python
import functools

import jax
import jax.numpy as jnp
from jax import lax
from jax.experimental import pallas as pl
from jax.experimental.pallas import tpu as pltpu

D_MODEL = 1024
N_HEADS = 16
HEAD_DIM = D_MODEL // N_HEADS
EPS = 1e-6

LANES_V7X = 128
SUBLANES_V7X = 8
VMEM_BYTES_V7X = 64 * 1024 * 1024

HEADS_PER_LANE_TILE = LANES_V7X // HEAD_DIM
N_HEAD_PAIRS = N_HEADS // HEADS_PER_LANE_TILE

MXU_WIDTH_V7X = 256

TM = 512
COL_CHUNK = 2 * MXU_WIDTH_V7X
TQ = 512
TK = 512
FOX_SUB = 4
SB_T = 128
SB_LEFT = 256
SB_SUB = 16

ZERO_EXP = 104.5
SAFE_EXP_RANGE = 80.0
NEG_BIG = -1e30
LOG2_E = 1.4426950408889634
SOFTPLUS_LINEAR = 64.0

F32 = jnp.float32
BF16 = jnp.bfloat16


def _vmem_limit(block_bytes):
    want = 2 * block_bytes + 16 * 1024 * 1024
    return int(min(want, VMEM_BYTES_V7X - 8 * 1024 * 1024))


def _split2(x):
    hi = x.astype(BF16)
    lo = (x - hi.astype(F32)).astype(BF16)
    return hi, lo


def _dot(a, b):
    return jnp.dot(a, b, preferred_element_type=F32)


def _dot_nt(a, b):
    return lax.dot_general(a, b, (((1,), (1,)), ((), ())), preferred_element_type=F32)


def _ada_kernel(ct_ref, w_ref, b_ref, o_ref, *, batch):
    c = ct_ref[...]
    c_act = c * jax.nn.sigmoid(c)
    w = w_ref[...]
    o_ref[...] = jnp.zeros_like(o_ref)
    for b in range(batch):
        o_ref[b:b + 1, :] = jnp.sum(w * c_act[:, b:b + 1], axis=0, keepdims=True) + b_ref[...]


def _ada_mod(c, w, b):
    n_layers, d, n = w.shape
    batch = c.shape[0]
    rows = SUBLANES_V7X
    assert batch <= rows
    tn = 1024
    c_t = jnp.zeros((d, rows), F32).at[:, :batch].set(c.astype(F32).T)
    block_bytes = 4 * (d * tn + d * LANES_V7X + tn + rows * tn)
    return pl.pallas_call(
        functools.partial(_ada_kernel, batch=batch),
        out_shape=jax.ShapeDtypeStruct((n_layers, rows, n), F32),
        grid=(n_layers, n // tn),
        in_specs=[
            pl.BlockSpec((d, rows), lambda l, j: (0, 0)),
            pl.BlockSpec((None, d, tn), lambda l, j: (l, 0, j)),
            pl.BlockSpec((None, 1, tn), lambda l, j: (l, 0, j)),
        ],
        out_specs=pl.BlockSpec((None, rows, tn), lambda l, j: (l, 0, j)),
        compiler_params=pltpu.CompilerParams(
            dimension_semantics=("arbitrary", "arbitrary"),
            vmem_limit_bytes=_vmem_limit(block_bytes)),
        name="ada_mod",
    )(c_t, w, b.reshape(n_layers, 1, n))


def _norm_modulate(x, g, shift, scale):
    ms = jnp.mean(x * x, axis=-1, keepdims=True)
    y = x * lax.rsqrt(ms + EPS)
    return (y * g) * (1.0 + scale) + shift


def _row_spec(d):
    return pl.BlockSpec((1, d), lambda *_: (0, 0))


def _resident_spec(shape, block_index=(0, 0)):
    return pl.BlockSpec(shape, lambda *_: block_index)


def _batch_vec_spec(d, tiles_per_batch):
    return pl.BlockSpec((None, 1, d), lambda i, *_: (i // tiles_per_batch, 0, 0))


def _column_chunks(n):
    return [(c, min(c + COL_CHUNK, n)) for c in range(0, n, COL_CHUNK)]


def _norm_matmul_kernel(x_ref, g_ref, sh_ref, sc_ref, w_ref, cs_ref, o_ref):
    h = _norm_modulate(x_ref[...], g_ref[...], sh_ref[...], sc_ref[...]).astype(BF16)
    for c0, c1 in _column_chunks(o_ref.shape[1]):
        o_ref[:, c0:c1] = (_dot(h, w_ref[:, c0:c1]) * cs_ref[:, c0:c1]).astype(o_ref.dtype)


def _norm_matmul(x2d, g, shift, scale, w, col_scale, seq):
    m, d = x2d.shape
    n = w.shape[1]
    block_bytes = 4 * TM * d + 2 * d * n + 2 * TM * n + 4 * n
    return pl.pallas_call(
        _norm_matmul_kernel,
        out_shape=jax.ShapeDtypeStruct((m, n), BF16),
        grid=(m // TM,),
        in_specs=[
            pl.BlockSpec((TM, d), lambda i: (i, 0)),
            _row_spec(d),
            _batch_vec_spec(d, seq // TM),
            _batch_vec_spec(d, seq // TM),
            _resident_spec((d, n)),
            pl.BlockSpec((1, n), lambda i: (0, 0)),
        ],
        out_specs=pl.BlockSpec((TM, n), lambda i: (i, 0)),
        compiler_params=pltpu.CompilerParams(
            dimension_semantics=("arbitrary",),
            vmem_limit_bytes=_vmem_limit(block_bytes)),
        name="norm_matmul",
    )(x2d, g, shift, scale, w, col_scale)


def _out_ffn_in_kernel(a_ref, wo_ref, x_ref, gate_ref, g_ref, sh_ref, sc_ref, wg_ref, wu_ref,
                       x1_ref, act_ref):
    x1 = x_ref[...] + gate_ref[...] * _dot(a_ref[...], wo_ref[...])
    x1_ref[...] = x1
    h = _norm_modulate(x1, g_ref[...], sh_ref[...], sc_ref[...]).astype(BF16)
    for c0, c1 in _column_chunks(act_ref.shape[1]):
        gate = _dot(h, wg_ref[:, c0:c1])
        up = _dot(h, wu_ref[:, c0:c1])
        act_ref[:, c0:c1] = (gate * jax.nn.sigmoid(gate) * up).astype(act_ref.dtype)


def _out_ffn_in(a, w_o, x2d, gate, g, shift, scale, w_in, seq):
    m, d = x2d.shape
    f = w_in.shape[1] // 2
    tpb = seq // TM
    block_bytes = 2 * TM * d + 2 * d * d + 2 * 4 * TM * d + 2 * 2 * d * f + 2 * TM * f
    return pl.pallas_call(
        _out_ffn_in_kernel,
        out_shape=(jax.ShapeDtypeStruct((m, d), F32), jax.ShapeDtypeStruct((m, f), BF16)),
        grid=(m // TM,),
        in_specs=[
            pl.BlockSpec((TM, d), lambda i: (i, 0)),
            _resident_spec((d, d)),
            pl.BlockSpec((TM, d), lambda i: (i, 0)),
            _batch_vec_spec(d, tpb),
            _row_spec(d),
            _batch_vec_spec(d, tpb),
            _batch_vec_spec(d, tpb),
            _resident_spec((d, f)),
            _resident_spec((d, f), (0, 1)),
        ],
        out_specs=(pl.BlockSpec((TM, d), lambda i: (i, 0)), pl.BlockSpec((TM, f), lambda i: (i, 0))),
        compiler_params=pltpu.CompilerParams(
            dimension_semantics=("arbitrary",),
            vmem_limit_bytes=_vmem_limit(block_bytes)),
        name="out_ffn_in",
    )(a, w_o, x2d, gate, g, shift, scale, w_in, w_in)


def _proj_residual_kernel(a_ref, w_ref, x_ref, gate_ref, o_ref):
    o_ref[...] = x_ref[...] + gate_ref[...] * _dot(a_ref[...], w_ref[...])


def _proj_residual(a, w, x2d, gate, seq):
    m, k = a.shape
    d = w.shape[1]
    block_bytes = 2 * TM * k + 2 * k * d + 2 * 4 * TM * d
    return pl.pallas_call(
        _proj_residual_kernel,
        out_shape=jax.ShapeDtypeStruct((m, d), F32),
        grid=(m // TM,),
        in_specs=[
            pl.BlockSpec((TM, k), lambda i: (i, 0)),
            _resident_spec((k, d)),
            pl.BlockSpec((TM, d), lambda i: (i, 0)),
            _batch_vec_spec(d, seq // TM),
        ],
        out_specs=pl.BlockSpec((TM, d), lambda i: (i, 0)),
        compiler_params=pltpu.CompilerParams(
            dimension_semantics=("arbitrary",),
            vmem_limit_bytes=_vmem_limit(block_bytes)),
        name="proj_residual",
    )(a, w, x2d, gate)


def _head_rms_norm(t, gain, group_mean):
    ms = _dot((t * t).astype(BF16), group_mean)
    return t * lax.rsqrt(ms + EPS) * gain


def _l1_proj_kernel(act_ref, wd_ref, x_ref, gate_ref,
                    gkv_ref, shkv_ref, sckv_ref, gq_ref, shq_ref, scq_ref,
                    wk_ref, wv_ref, wf_ref, wq_ref, kg_ref, qg_ref, bf_ref, gm_ref,
                    x2_ref, k_ref, v_ref, q_ref, lf_ref):
    x = x_ref[...] + gate_ref[...] * _dot(act_ref[...], wd_ref[...])
    x2_ref[...] = x
    ms = jnp.mean(x * x, axis=-1, keepdims=True)
    y = x * lax.rsqrt(ms + EPS)
    h_kv = ((y * gkv_ref[...]) * (1.0 + sckv_ref[...]) + shkv_ref[...]).astype(BF16)
    h_q = ((y * gq_ref[...]) * (1.0 + scq_ref[...]) + shq_ref[...]).astype(BF16)
    gm = gm_ref[...]
    k_all = _dot(h_kv, wk_ref[...])
    q_all = _dot(h_q, wq_ref[...])
    v_ref[...] = _dot(h_kv, wv_ref[...]).astype(BF16)
    lf_ref[...] = jax.nn.log_sigmoid(_dot(h_kv, wf_ref[...]) + bf_ref[...])
    for c0 in range(0, k_ref.shape[1], MXU_WIDTH_V7X):
        cols = slice(c0, c0 + MXU_WIDTH_V7X)
        k_ref[:, cols] = _head_rms_norm(k_all[:, cols], kg_ref[:, cols], gm).astype(BF16)
        q = _head_rms_norm(q_all[:, cols], qg_ref[:, cols], gm)
        q_ref[:, cols] = (q * (HEAD_DIM ** -0.5)).astype(BF16)


def _l1_proj(act, w_down, x2d, gate, g_kv, sh_kv, sc_kv, g_q, sh_q, sc_q, wk, wv, wf, wq,
             k_gain, q_gain, b_f, seq):
    m, d = x2d.shape
    f = act.shape[1]
    lane = jnp.arange(MXU_WIDTH_V7X)
    group_mean = jnp.where(lane[:, None] // HEAD_DIM == lane[None, :] // HEAD_DIM,
                           1.0 / HEAD_DIM, 0.0).astype(BF16)
    tpb = seq // TM
    block_bytes = (2 * TM * f + 2 * f * d + 2 * 4 * TM * d + 3 * 2 * d * d + 2 * d * LANES_V7X
                   + 3 * 2 * TM * d + 4 * TM * LANES_V7X)
    full = lambda r, c: _resident_spec((r, c))
    rows = lambda c: pl.BlockSpec((TM, c), lambda i: (i, 0))
    return pl.pallas_call(
        _l1_proj_kernel,
        out_shape=(jax.ShapeDtypeStruct((m, d), F32), jax.ShapeDtypeStruct((m, d), BF16),
                   jax.ShapeDtypeStruct((m, d), BF16), jax.ShapeDtypeStruct((m, d), BF16),
                   jax.ShapeDtypeStruct((m, LANES_V7X), F32)),
        grid=(m // TM,),
        in_specs=[
            rows(f), full(f, d), rows(d), _batch_vec_spec(d, tpb),
            _row_spec(d), _batch_vec_spec(d, tpb), _batch_vec_spec(d, tpb),
            _row_spec(d), _batch_vec_spec(d, tpb), _batch_vec_spec(d, tpb),
            full(d, d), full(d, d), full(d, LANES_V7X), full(d, d),
            _row_spec(d), _row_spec(d), _row_spec(LANES_V7X), full(MXU_WIDTH_V7X, MXU_WIDTH_V7X),
        ],
        out_specs=(rows(d), rows(d), rows(d), rows(d), rows(LANES_V7X)),
        compiler_params=pltpu.CompilerParams(
            dimension_semantics=("arbitrary",),
            vmem_limit_bytes=_vmem_limit(block_bytes)),
        name="l1_proj",
    )(act, w_down, x2d, gate, g_kv, sh_kv, sc_kv, g_q, sh_q, sc_q, wk, wv, wf, wq,
      k_gain, q_gain, b_f, group_mean)


N_SPLIT = 3


def _split3(x):
    x0 = x.astype(BF16)
    r1 = x - x0.astype(F32)
    x1 = r1.astype(BF16)
    x2 = (r1 - x1.astype(F32)).astype(BF16)
    return x0, x1, x2


def _fox_prep_kernel(kb_ref, lf_ref, q_ref, k_ref, qa_ref, ka_ref, stats_ref, carry_scr):
    @pl.when(pl.program_id(1) == 0)
    def _():
        carry_scr[...] = jnp.zeros_like(carry_scr)

    r = lax.broadcasted_iota(jnp.int32, (TK, TK), 0)
    c = lax.broadcasted_iota(jnp.int32, (TK, TK), 1)
    tri = jnp.where(c <= r, 1.0, 0.0).astype(BF16)
    x0, x1, x2 = _split3(lf_ref[...])
    cs = _dot(tri, x0) + _dot(tri, x1) + _dot(tri, x2) + carry_scr[0:1, :]
    carry_scr[...] = jnp.broadcast_to(cs[TK - 1:TK, :], carry_scr.shape)

    d_idx = lax.broadcasted_iota(jnp.int32, (D_MODEL, LANES_V7X), 0)
    h_idx = lax.broadcasted_iota(jnp.int32, (D_MODEL, LANES_V7X), 1)
    head_of = jnp.where(d_idx // HEAD_DIM == h_idx, 1.0, 0.0).astype(BF16)
    qf = q_ref[...].astype(F32)
    reach = jnp.sqrt(_dot((qf * qf).astype(BF16), head_of)) * kb_ref[0, 0]
    own_logit = _dot((qf * k_ref[...].astype(F32)).astype(BF16), head_of)

    stats_ref[...] = jnp.concatenate([
        jnp.max(reach * (1.0 + 2.0 ** -8) - own_logit + cs, axis=0, keepdims=True),
        cs[0:1, :],
        jnp.max(reach, axis=0, keepdims=True),
        jnp.zeros((SUBLANES_V7X - 3, LANES_V7X), F32)], axis=0)

    lane = lax.broadcasted_iota(jnp.int32, (1, LANES_V7X), 1)
    packed = jnp.zeros((TK, LANES_V7X), F32)
    for i, term in enumerate(_split3(cs - reach) + _split3(-cs)):
        term = jnp.where(lane < N_HEADS, term.astype(F32), 0.0)
        packed = packed + (term if i == 0 else pltpu.roll(term, N_HEADS * i, axis=1))
    packed = packed.astype(BF16)

    src = lax.broadcasted_iota(jnp.int32, (LANES_V7X, 2 * LANES_V7X), 0)
    dst = lax.broadcasted_iota(jnp.int32, (LANES_V7X, 2 * LANES_V7X), 1)

    for h in range(N_HEADS):
        pair, parity = divmod(h, HEADS_PER_LANE_TILE)
        spare = HEAD_DIM * (1 - parity)
        place = jnp.where((src % N_HEADS == h) & (src < 2 * N_SPLIT * N_HEADS)
                          & (dst % LANES_V7X == spare + src // N_HEADS)
                          & ((dst >= LANES_V7X) == (src >= N_SPLIT * N_HEADS)), 1.0, 0.0)
        placed = _dot(packed, place.astype(BF16))
        ones_q = jnp.where((lane >= spare + N_SPLIT) & (lane < spare + 2 * N_SPLIT), 1.0, 0.0)
        ones_k = jnp.where((lane >= spare) & (lane < spare + N_SPLIT), 1.0, 0.0)
        aug_q = (placed[:, :LANES_V7X] + ones_q).astype(BF16)
        aug_k = (placed[:, LANES_V7X:] + ones_k).astype(BF16)
        own = (lane // HEAD_DIM) == parity
        cols = slice(pair * LANES_V7X, (pair + 1) * LANES_V7X)
        qa_ref[h] = jnp.where(own, q_ref[:, cols], aug_q)
        ka_ref[h] = jnp.where(own, k_ref[:, cols], aug_k)


def _fox_prep(log_f, q, k, key_norm_bound, batch, seq):
    assert TQ == TK
    d = q.shape[1]
    block_bytes = (4 * TK * LANES_V7X + 2 * 2 * TK * d + 2 * 2 * N_HEADS * TK * LANES_V7X
                   + 4 * SUBLANES_V7X * LANES_V7X)
    aug_shape = jax.ShapeDtypeStruct((batch, N_HEADS, seq, LANES_V7X), BF16)
    aug_spec = pl.BlockSpec((None, N_HEADS, TK, LANES_V7X), lambda b, t: (b, 0, t, 0))
    return pl.pallas_call(
        _fox_prep_kernel,
        out_shape=(aug_shape, aug_shape,
                   jax.ShapeDtypeStruct((batch, seq // TK, SUBLANES_V7X, LANES_V7X), F32)),
        grid=(batch, seq // TK),
        in_specs=[
            pl.BlockSpec(memory_space=pltpu.SMEM),
            pl.BlockSpec((None, TK, LANES_V7X), lambda b, t: (b, t, 0)),
            pl.BlockSpec((None, TK, d), lambda b, t: (b, t, 0)),
            pl.BlockSpec((None, TK, d), lambda b, t: (b, t, 0)),
        ],
        out_specs=(aug_spec, aug_spec,
                   pl.BlockSpec((None, None, SUBLANES_V7X, LANES_V7X), lambda b, t: (b, t, 0, 0))),
        scratch_shapes=[pltpu.VMEM((SUBLANES_V7X, LANES_V7X), F32)],
        compiler_params=pltpu.CompilerParams(
            dimension_semantics=("arbitrary", "arbitrary"),
            vmem_limit_bytes=_vmem_limit(block_bytes)),
        name="fox_prep",
    )(key_norm_bound, log_f.reshape(batch, seq, LANES_V7X), q.reshape(batch, seq, d),
      k.reshape(batch, seq, d))


def _head_select(head):
    lane = lax.broadcasted_iota(jnp.int32, (1, LANES_V7X), 1)
    return (lane // HEAD_DIM) == head


def _sb_kernel(q_ref, k_ref, v_ref, o_ref):
    i = pl.program_id(2)

    def strictly_lower(n):
        return (lax.broadcasted_iota(jnp.int32, (n, n), 0)
                > lax.broadcasted_iota(jnp.int32, (n, n), 1))

    diag_bias = jnp.where(strictly_lower(SB_T), 0.0, NEG_BIG)
    after = jnp.where(strictly_lower(SB_T), 1.0, 0.0).astype(BF16)
    after_left = jnp.where(strictly_lower(SB_LEFT), 1.0, 0.0).astype(BF16)
    ones = jnp.ones((SB_T, LANES_V7X), BF16)
    ones_left = jnp.ones((SB_T, SB_LEFT), BF16)

    def softplus(z):
        return jnp.maximum(jnp.log2(1.0 + jnp.exp2(jnp.minimum(z, SOFTPLUS_LINEAR))), z)

    def keys(start, size):
        rows = pl.ds(pl.multiple_of(start, SB_T), size)
        return k_ref[rows, :], v_ref[rows, :]

    def left_start(g):
        return jnp.maximum(g * SB_T - SB_LEFT, 0)

    def band_logits(qh, g, may_clip):
        k_d, _ = keys(g * SB_T, SB_T)
        k_p, _ = keys(left_start(g), SB_LEFT)
        z_p = _dot_nt(qh, k_p)
        if may_clip:
            key = left_start(g) + lax.broadcasted_iota(jnp.int32, (SB_T, SB_LEFT), 1)
            z_p = jnp.where(key < g * SB_T, z_p, NEG_BIG)
        return _dot_nt(qh, k_d) + diag_bias, z_p

    def band_softplus(z_d, z_p):
        return z_d, z_p, softplus(z_d), softplus(z_p)

    def band_sums(z_d, z_p, sp_d, sp_p):
        sp_d16 = sp_d.astype(BF16)
        within_d = _dot(sp_d16, after)
        within_p = _dot(sp_p.astype(BF16), after_left) + _dot(sp_d16, ones_left)
        total = (within_p + sp_p)[:, 0:1]
        return (z_d - sp_d) - within_d, (z_p - sp_p) - within_p, total

    def band_weights(e_d, e_p, total):
        return jnp.exp2(e_d).astype(BF16), jnp.exp2(e_p).astype(BF16), total

    def band_out(g, a_d, a_p, total):
        _, v_d = keys(g * SB_T, SB_T)
        _, v_p = keys(left_start(g), SB_LEFT)
        return jnp.broadcast_to(-total, (SB_T, LANES_V7X)), _dot(a_d, v_d) + _dot(a_p, v_p)

    def block(qh, start, carry, acc):
        kj, vj = keys(start, SB_T)
        z = _dot_nt(qh, kj)
        sp = softplus(z)
        sp16 = sp.astype(BF16)
        within = _dot(sp16, after)
        a = jnp.exp2((z - sp) - within + carry)
        acc = acc + _dot(a.astype(BF16), vj)
        carry = carry - _dot(sp16, ones)
        return carry, acc

    heads = []
    for u in range(SB_SUB):
        g = i * SB_SUB + u
        q = q_ref[u * SB_T:(u + 1) * SB_T, :]
        for head in range(HEADS_PER_LANE_TILE):
            heads.append((g, jnp.where(_head_select(head), q, 0.0).astype(BF16)))
    logits = [band_logits(qh, g, may_clip=c // HEADS_PER_LANE_TILE < SB_LEFT // SB_T)
              for c, (g, qh) in enumerate(heads)]
    softplus_terms = [band_softplus(*z) for z in logits]
    sums = [band_sums(*t) for t in softplus_terms]
    weights = [band_weights(*s) for s in sums]
    chains = [(g, qh) + band_out(g, *w) for (g, qh), w in zip(heads, weights)]

    def is_open(carry):
        return jnp.max(carry) > -ZERO_EXP * LOG2_E

    chain_open = [is_open(carry) for _, _, carry, _ in chains]

    def walk_on():
        outs = []
        for (g, qh, carry, acc), this_open in zip(chains, chain_open):
            def cond(state):
                start, carry, _ = state
                return jnp.logical_and(start >= 0, is_open(carry))

            def body(state, qh=qh):
                start, carry, acc = state
                carry, acc = block(qh, start, carry, acc)
                return start - SB_T, carry, acc

            def walk(cond=cond, body=body, first=(left_start(g) - SB_T, carry, acc)):
                return lax.while_loop(cond, body, first)[2]

            outs.append(lax.cond(this_open, walk, lambda acc=acc: acc))
        return outs

    outs = lax.cond(functools.reduce(jnp.logical_or, chain_open), walk_on,
                    lambda: [acc for _, _, _, acc in chains])
    for u in range(SB_SUB):
        pair = outs[u * HEADS_PER_LANE_TILE:(u + 1) * HEADS_PER_LANE_TILE]
        o_ref[u * SB_T:(u + 1) * SB_T, :] = jnp.where(_head_select(0), pair[0], pair[1]).astype(o_ref.dtype)


def _sb_attn(qkv, batch, seq):
    assert SB_T == LANES_V7X and SB_LEFT % SB_T == 0
    rows = SB_SUB * SB_T
    stage_bytes = SB_SUB * HEADS_PER_LANE_TILE * 2 * 4 * SB_T * (SB_T + SB_LEFT)
    block_bytes = 2 * (2 * rows * LANES_V7X + 2 * seq * LANES_V7X) + stage_bytes // 2
    return pl.pallas_call(
        _sb_kernel,
        out_shape=jax.ShapeDtypeStruct((batch, seq, D_MODEL), BF16),
        grid=(batch, N_HEAD_PAIRS, seq // rows),
        in_specs=[
            pl.BlockSpec((None, rows, LANES_V7X), lambda b, p, i: (b, i, p)),
            pl.BlockSpec((None, seq, LANES_V7X), lambda b, p, i: (b, 0, N_HEAD_PAIRS + p)),
            pl.BlockSpec((None, seq, LANES_V7X), lambda b, p, i: (b, 0, 2 * N_HEAD_PAIRS + p)),
        ],
        out_specs=pl.BlockSpec((None, rows, LANES_V7X), lambda b, p, i: (b, i, p)),
        compiler_params=pltpu.CompilerParams(
            dimension_semantics=("arbitrary", "arbitrary", "arbitrary"),
            vmem_limit_bytes=_vmem_limit(block_bytes)),
        name="sb_attn",
    )(qkv, qkv, qkv)


def _fox_kernel(a_ref, fk0_ref, g_ref, qa_ref, ka_ref, v_ref, o_ref, acc_scr, vt_scr):
    b = pl.program_id(0)
    p_id = pl.program_id(1)
    i = pl.program_id(2)
    n_blocks = fk0_ref.shape[1]
    row = lax.broadcasted_iota(jnp.int32, (TQ, TK), 0)
    col = lax.broadcasted_iota(jnp.int32, (TQ, TK), 1)
    diag_mask = col <= row

    acc0 = jnp.zeros((TQ, LANES_V7X), F32)
    chains = range(FOX_SUB * HEADS_PER_LANE_TILE)
    bh0 = (b * N_HEAD_PAIRS + p_id) * HEADS_PER_LANE_TILE

    def scores(chain, j):
        sub, head = chain // HEADS_PER_LANE_TILE, chain % HEADS_PER_LANE_TILE
        q = qa_ref[head, pl.ds(pl.multiple_of(sub * TQ, TQ), TQ), :]
        return _dot_nt(q, ka_ref[head, pl.ds(pl.multiple_of(j * TK, TK), TK), :])

    def values(chain, j):
        vj = v_ref[pl.ds(pl.multiple_of(j * TK, TK), TK), :]
        own = _head_select(chain % HEADS_PER_LANE_TILE)
        return jnp.where(own, vj, jnp.ones_like(vj))

    def diag_block(chain):
        return i * FOX_SUB + chain // HEADS_PER_LANE_TILE

    def count_left(chain):
        bh, qi = bh0 + chain % HEADS_PER_LANE_TILE, diag_block(chain)
        bound = a_ref[bh, qi]

        def more(n):
            j = qi - 1 - n
            right = fk0_ref[bh, jnp.minimum(j + 1, n_blocks - 1)]
            return jnp.logical_and(j >= 0, bound - right > -ZERO_EXP)

        return lax.while_loop(more, lambda n: n + 1, jnp.int32(0))

    def fast_walks():
        counts = [count_left(chain) for chain in chains]
        firsts = [sum(n + 1 for n in counts[:chain]) for chain in chains]
        n_items = firsts[-1] + counts[-1] + 1
        diag_bias_t = jnp.where(row <= col, 0.0, NEG_BIG)
        feature = lax.broadcasted_iota(jnp.int32, (LANES_V7X, 1), 0)
        acc_scr[...] = jnp.zeros_like(acc_scr)

        def scores_t(chain, j):
            sub, head = chain // HEADS_PER_LANE_TILE, chain % HEADS_PER_LANE_TILE
            q = qa_ref[head, pl.ds(pl.multiple_of(sub * TQ, TQ), TQ), :]
            return _dot_nt(ka_ref[head, pl.ds(pl.multiple_of(j * TK, TK), TK), :], q)

        def values_t(chain, j):
            vt = vt_scr[:, pl.ds(pl.multiple_of(j * TK, TK), TK)]
            own = (feature // HEAD_DIM) == chain % HEADS_PER_LANE_TILE
            return jnp.where(own, vt, jnp.ones_like(vt))

        def step(t, p_t, p_chain, p_block):
            chain = sum(jnp.where(t >= first, 1, 0) for first in firsts[1:])
            first = functools.reduce(lambda f, nxt: jnp.where(t >= nxt, nxt, f), firsts[1:], 0)
            back = t - first
            block = jnp.maximum(diag_block(chain) - back, 0)
            s_t = scores_t(chain, block) + jnp.where(back == 0, diag_bias_t,
                                                     jnp.where(t >= n_items, NEG_BIG, 0.0))
            acc_scr[p_chain] += _dot(values_t(p_chain, p_block), p_t)
            return jnp.exp(s_t).astype(BF16), chain, block

        def steps(first, count, carried):
            for t in range(count):
                carried = step(first + t, *carried)
            return carried

        pairs = (n_items + 2) // 2
        start = (jnp.zeros((TK, TQ), BF16), jnp.int32(0), jnp.int32(0))
        carried = lax.fori_loop(0, pairs // 2, lambda u, c: steps(4 * u, 4, c), start)
        lax.fori_loop(0, pairs % 2, lambda _, c: steps(4 * (pairs // 2), 2, c), carried)

        outs = []
        for chain in chains:
            acc_t = acc_scr[chain]
            spare = HEAD_DIM * (1 - chain % HEADS_PER_LANE_TILE)
            outs.append((acc_t / acc_t[spare:spare + 1, :]).T)
        return outs

    def online_walks():
        accs = []
        for chain in chains:
            def body(t, carried, chain=chain):
                acc, m = carried
                block = diag_block(chain) - t
                s = jnp.where(jnp.logical_or(t > 0, diag_mask), scores(chain, block), NEG_BIG)
                m_new = jnp.maximum(m, jnp.max(s, axis=-1, keepdims=True))
                p = jnp.exp(s - m_new).astype(BF16)
                return jnp.exp(m - m_new) * acc + _dot(p, values(chain, block)), m_new

            m0 = jnp.full((TQ, 1), NEG_BIG, F32)
            acc = lax.fori_loop(0, count_left(chain) + 1, body, (acc0, m0))[0]
            accs.append(acc / pltpu.roll(acc, HEAD_DIM, axis=1))
        return accs

    @pl.when(i == 0)
    def _():
        for c in range(0, vt_scr.shape[1], TK):
            vt_scr[:, c:c + TK] = v_ref[c:c + TK, :].astype(F32).T.astype(BF16)

    largest_reach = functools.reduce(jnp.maximum, [
        g_ref[bh0 + chain % HEADS_PER_LANE_TILE, diag_block(chain)] for chain in chains])
    outs = lax.cond(2.0 * largest_reach < SAFE_EXP_RANGE, fast_walks, online_walks)
    for sub in range(FOX_SUB):
        pair = outs[sub * HEADS_PER_LANE_TILE:(sub + 1) * HEADS_PER_LANE_TILE]
        o_ref[sub * TQ:(sub + 1) * TQ, :] = jnp.where(_head_select(0), pair[0], pair[1]).astype(o_ref.dtype)


def _fox_attn(q_aug, k_aug, v, walk_bound, fk_block_start, max_reach, batch, seq):
    assert TQ == TK
    rows = FOX_SUB * TQ
    n_chains = FOX_SUB * HEADS_PER_LANE_TILE
    block_bytes = (2 * (HEADS_PER_LANE_TILE * (rows + seq) * LANES_V7X + seq * LANES_V7X + rows * LANES_V7X)
                   + 4 * n_chains * TQ * LANES_V7X + 2 * seq * LANES_V7X)
    smem = pl.BlockSpec(memory_space=pltpu.SMEM)
    return pl.pallas_call(
        _fox_kernel,
        out_shape=jax.ShapeDtypeStruct((batch, seq, D_MODEL), BF16),
        grid=(batch, N_HEAD_PAIRS, seq // rows),
        in_specs=[
            smem, smem, smem,
            pl.BlockSpec((None, HEADS_PER_LANE_TILE, rows, LANES_V7X), lambda b, p, i: (b, p, i, 0)),
            pl.BlockSpec((None, HEADS_PER_LANE_TILE, seq, LANES_V7X), lambda b, p, i: (b, p, 0, 0)),
            pl.BlockSpec((None, seq, LANES_V7X), lambda b, p, i: (b, 0, p)),
        ],
        out_specs=pl.BlockSpec((None, rows, LANES_V7X), lambda b, p, i: (b, i, p)),
        scratch_shapes=[pltpu.VMEM((n_chains, LANES_V7X, TQ), F32),
                        pltpu.VMEM((LANES_V7X, seq), BF16)],
        compiler_params=pltpu.CompilerParams(
            dimension_semantics=("arbitrary", "arbitrary", "arbitrary"),
            vmem_limit_bytes=_vmem_limit(block_bytes)),
        name="fox_attn",
    )(walk_bound, fk_block_start, max_reach, q_aug, k_aug, v)


def kernel(x, c, ada_w, ada_b, norm_attn_g, norm_ffn_g, w_ffn_in, w_ffn_down, sb_w_qkv, sb_w_o,
           kv_ada_w, kv_ada_b, kv_norm_g, w_kvf, b_f, k_norm_g, fox_w_q, q_norm_g, fox_w_o):
    batch, seq, d = x.shape
    assert d == D_MODEL and seq % TM == 0
    assert seq % (FOX_SUB * TQ) == 0 and seq % (SB_SUB * SB_T) == 0
    assert ada_w.shape[0] == 2 and sb_w_qkv.shape[0] == 1 and fox_w_q.shape[0] == 1
    m = batch * seq

    mod = _ada_mod(c, ada_w, ada_b)[:, :batch]
    kv_mod = _ada_mod(c, kv_ada_w[None], kv_ada_b[None])[0, :batch]

    def vecs(t, n):
        return [t[:, None, j * d:(j + 1) * d] for j in range(n)]

    row = lambda t: t.reshape(1, -1).astype(F32)
    bf = lambda t: t.astype(BF16)

    x2d = x.reshape(m, d)

    sh_a, sc_a, g_a, sh_f, sc_f, g_f = vecs(mod[0], 6)
    qkv_scale = jnp.concatenate([jnp.full((1, d), LOG2_E * HEAD_DIM ** -0.5, F32),
                                 jnp.ones((1, 2 * d), F32)], axis=1)
    qkv = _norm_matmul(x2d, row(norm_attn_g[0]), sh_a, sc_a, bf(sb_w_qkv[0]), qkv_scale, seq)
    o = _sb_attn(qkv.reshape(batch, seq, 3 * d), batch, seq)
    x2d, act = _out_ffn_in(o.reshape(m, d), bf(sb_w_o[0]), x2d, g_a,
                           row(norm_ffn_g[0]), sh_f, sc_f, bf(w_ffn_in[0]), seq)
    g_f0 = g_f

    sh_a, sc_a, g_a, sh_f, sc_f, g_f = vecs(mod[1], 6)
    kv_shift, kv_scale = vecs(kv_mod, 2)
    w_f_pad = jnp.zeros((d, LANES_V7X), F32).at[:, :N_HEADS].set(w_kvf[:, 2 * d:])
    b_f_pad = jnp.zeros((1, LANES_V7X), F32).at[0, :N_HEADS].set(b_f)
    x2d, k, v, q, log_f = _l1_proj(
        act, bf(w_ffn_down[0]), x2d, g_f0,
        row(kv_norm_g), kv_shift, kv_scale, row(norm_attn_g[1]), sh_a, sc_a,
        bf(w_kvf[:, :d]), bf(w_kvf[:, d:2 * d]), bf(w_f_pad), bf(fox_w_q[0]),
        row(jnp.tile(k_norm_g, N_HEADS)), row(jnp.tile(q_norm_g[0], N_HEADS)), b_f_pad, seq)
    key_norm_bound = (1.01 * HEAD_DIM ** 0.5 * jnp.max(jnp.abs(k_norm_g))).reshape(1, 1).astype(F32)
    q_aug, k_aug, stats = _fox_prep(log_f, q, k, key_norm_bound, batch, seq)
    tables = stats[:, :, :3, :N_HEADS].transpose(2, 0, 3, 1).reshape(3, batch * N_HEADS, seq // TK)
    o = _fox_attn(q_aug, k_aug, v.reshape(batch, seq, d), tables[0], tables[1], tables[2], batch, seq)
    x2d, act = _out_ffn_in(o.reshape(m, d), bf(fox_w_o[0]), x2d, g_a,
                           row(norm_ffn_g[1]), sh_f, sc_f, bf(w_ffn_in[1]), seq)
    x2d = _proj_residual(act, bf(w_ffn_down[1]), x2d, g_f, seq)
    return x2d.reshape(batch, seq, d)
```

```python
import functools

import jax
import jax.numpy as jnp
from jax import lax
from jax.experimental import pallas as pl
from jax.experimental.pallas import tpu as pltpu

D_MODEL = 1024
N_HEADS = 16
HEAD_DIM = D_MODEL // N_HEADS
EPS = 1e-6

LANES_V7X = 128
SUBLANES_V7X = 8
VMEM_BYTES_V7X = 64 * 1024 * 1024

HEADS_PER_LANE_TILE = LANES_V7X // HEAD_DIM
N_HEAD_PAIRS = N_HEADS // HEADS_PER_LANE_TILE

MXU_WIDTH_V7X = 256

TM = 512
COL_CHUNK = 2 * MXU_WIDTH_V7X
TQ = 512
TK = 512
FOX_SUB = 4
SB_T = 128
SB_LEFT = 256
SB_SUB = 16

ZERO_EXP = 104.5
SAFE_EXP_RANGE = 80.0
NEG_BIG = -1e30
LOG2_E = 1.4426950408889634
SOFTPLUS_LINEAR = 64.0

F32 = jnp.float32
BF16 = jnp.bfloat16


def _vmem_limit(block_bytes):
    want = 2 * block_bytes + 16 * 1024 * 1024
    return int(min(want, VMEM_BYTES_V7X - 8 * 1024 * 1024))


def _split2(x):
    hi = x.astype(BF16)
    lo = (x - hi.astype(F32)).astype(BF16)
    return hi, lo


def _dot(a, b):
    return jnp.dot(a, b, preferred_element_type=F32)


def _dot_nt(a, b):
    return lax.dot_general(a, b, (((1,), (1,)), ((), ())), preferred_element_type=F32)


def _ada_kernel(ct_ref, w_ref, b_ref, o_ref, *, batch):
    c = ct_ref[...]
    c_act = c * jax.nn.sigmoid(c)
    w = w_ref[...]
    o_ref[...] = jnp.zeros_like(o_ref)
    for b in range(batch):
        o_ref[b:b + 1, :] = jnp.sum(w * c_act[:, b:b + 1], axis=0, keepdims=True) + b_ref[...]


def _ada_mod(c, w, b):
    n_layers, d, n = w.shape
    batch = c.shape[0]
    rows = SUBLANES_V7X
    assert batch <= rows
    tn = 1024
    c_t = jnp.zeros((d, rows), F32).at[:, :batch].set(c.astype(F32).T)
    block_bytes = 4 * (d * tn + d * LANES_V7X + tn + rows * tn)
    return pl.pallas_call(
        functools.partial(_ada_kernel, batch=batch),
        out_shape=jax.ShapeDtypeStruct((n_layers, rows, n), F32),
        grid=(n_layers, n // tn),
        in_specs=[
            pl.BlockSpec((d, rows), lambda l, j: (0, 0)),
            pl.BlockSpec((None, d, tn), lambda l, j: (l, 0, j)),
            pl.BlockSpec((None, 1, tn), lambda l, j: (l, 0, j)),
        ],
        out_specs=pl.BlockSpec((None, rows, tn), lambda l, j: (l, 0, j)),
        compiler_params=pltpu.CompilerParams(
            dimension_semantics=("arbitrary", "arbitrary"),
            vmem_limit_bytes=_vmem_limit(block_bytes)),
        name="ada_mod",
    )(c_t, w, b.reshape(n_layers, 1, n))


def _norm_modulate(x, g, shift, scale):
    ms = jnp.mean(x * x, axis=-1, keepdims=True)
    y = x * lax.rsqrt(ms + EPS)
    return (y * g) * (1.0 + scale) + shift


def _row_spec(d):
    return pl.BlockSpec((1, d), lambda *_: (0, 0))


def _resident_spec(shape, block_index=(0, 0)):
    return pl.BlockSpec(shape, lambda *_: block_index)


def _batch_vec_spec(d, tiles_per_batch):
    return pl.BlockSpec((None, 1, d), lambda i, *_: (i // tiles_per_batch, 0, 0))


def _column_chunks(n):
    return [(c, min(c + COL_CHUNK, n)) for c in range(0, n, COL_CHUNK)]


def _norm_matmul_kernel(x_ref, g_ref, sh_ref, sc_ref, w_ref, cs_ref, o_ref):
    h = _norm_modulate(x_ref[...], g_ref[...], sh_ref[...], sc_ref[...]).astype(BF16)
    for c0, c1 in _column_chunks(o_ref.shape[1]):
        o_ref[:, c0:c1] = (_dot(h, w_ref[:, c0:c1]) * cs_ref[:, c0:c1]).astype(o_ref.dtype)


def _norm_matmul(x2d, g, shift, scale, w, col_scale, seq):
    m, d = x2d.shape
    n = w.shape[1]
    block_bytes = 4 * TM * d + 2 * d * n + 2 * TM * n + 4 * n
    return pl.pallas_call(
        _norm_matmul_kernel,
        out_shape=jax.ShapeDtypeStruct((m, n), BF16),
        grid=(m // TM,),
        in_specs=[
            pl.BlockSpec((TM, d), lambda i: (i, 0)),
            _row_spec(d),
            _batch_vec_spec(d, seq // TM),
            _batch_vec_spec(d, seq // TM),
            _resident_spec((d, n)),
            pl.BlockSpec((1, n), lambda i: (0, 0)),
        ],
        out_specs=pl.BlockSpec((TM, n), lambda i: (i, 0)),
        compiler_params=pltpu.CompilerParams(
            dimension_semantics=("arbitrary",),
            vmem_limit_bytes=_vmem_limit(block_bytes)),
        name="norm_matmul",
    )(x2d, g, shift, scale, w, col_scale)


def _out_ffn_in_kernel(a_ref, wo_ref, x_ref, gate_ref, g_ref, sh_ref, sc_ref, wg_ref, wu_ref,
                       x1_ref, act_ref):
    x1 = x_ref[...] + gate_ref[...] * _dot(a_ref[...], wo_ref[...])
    x1_ref[...] = x1
    h = _norm_modulate(x1, g_ref[...], sh_ref[...], sc_ref[...]).astype(BF16)
    for c0, c1 in _column_chunks(act_ref.shape[1]):
        gate = _dot(h, wg_ref[:, c0:c1])
        up = _dot(h, wu_ref[:, c0:c1])
        act_ref[:, c0:c1] = (gate * jax.nn.sigmoid(gate) * up).astype(act_ref.dtype)


def _out_ffn_in(a, w_o, x2d, gate, g, shift, scale, w_in, seq):
    m, d = x2d.shape
    f = w_in.shape[1] // 2
    tpb = seq // TM
    block_bytes = 2 * TM * d + 2 * d * d + 2 * 4 * TM * d + 2 * 2 * d * f + 2 * TM * f
    return pl.pallas_call(
        _out_ffn_in_kernel,
        out_shape=(jax.ShapeDtypeStruct((m, d), F32), jax.ShapeDtypeStruct((m, f), BF16)),
        grid=(m // TM,),
        in_specs=[
            pl.BlockSpec((TM, d), lambda i: (i, 0)),
            _resident_spec((d, d)),
            pl.BlockSpec((TM, d), lambda i: (i, 0)),
            _batch_vec_spec(d, tpb),
            _row_spec(d),
            _batch_vec_spec(d, tpb),
            _batch_vec_spec(d, tpb),
            _resident_spec((d, f)),
            _resident_spec((d, f), (0, 1)),
        ],
        out_specs=(pl.BlockSpec((TM, d), lambda i: (i, 0)), pl.BlockSpec((TM, f), lambda i: (i, 0))),
        compiler_params=pltpu.CompilerParams(
            dimension_semantics=("arbitrary",),
            vmem_limit_bytes=_vmem_limit(block_bytes)),
        name="out_ffn_in",
    )(a, w_o, x2d, gate, g, shift, scale, w_in, w_in)


def _proj_residual_kernel(a_ref, w_ref, x_ref, gate_ref, o_ref):
    o_ref[...] = x_ref[...] + gate_ref[...] * _dot(a_ref[...], w_ref[...])


def _proj_residual(a, w, x2d, gate, seq):
    m, k = a.shape
    d = w.shape[1]
    block_bytes = 2 * TM * k + 2 * k * d + 2 * 4 * TM * d
    return pl.pallas_call(
        _proj_residual_kernel,
        out_shape=jax.ShapeDtypeStruct((m, d), F32),
        grid=(m // TM,),
        in_specs=[
            pl.BlockSpec((TM, k), lambda i: (i, 0)),
            _resident_spec((k, d)),
            pl.BlockSpec((TM, d), lambda i: (i, 0)),
            _batch_vec_spec(d, seq // TM),
        ],
        out_specs=pl.BlockSpec((TM, d), lambda i: (i, 0)),
        compiler_params=pltpu.CompilerParams(
            dimension_semantics=("arbitrary",),
            vmem_limit_bytes=_vmem_limit(block_bytes)),
        name="proj_residual",
    )(a, w, x2d, gate)


def _head_rms_norm(t, gain, group_mean):
    ms = _dot((t * t).astype(BF16), group_mean)
    return t * lax.rsqrt(ms + EPS) * gain


def _l1_proj_kernel(act_ref, wd_ref, x_ref, gate_ref,
                    gkv_ref, shkv_ref, sckv_ref, gq_ref, shq_ref, scq_ref,
                    wk_ref, wv_ref, wf_ref, wq_ref, kg_ref, qg_ref, bf_ref, gm_ref,
                    x2_ref, k_ref, v_ref, q_ref, lf_ref):
    x = x_ref[...] + gate_ref[...] * _dot(act_ref[...], wd_ref[...])
    x2_ref[...] = x
    ms = jnp.mean(x * x, axis=-1, keepdims=True)
    y = x * lax.rsqrt(ms + EPS)
    h_kv = ((y * gkv_ref[...]) * (1.0 + sckv_ref[...]) + shkv_ref[...]).astype(BF16)
    h_q = ((y * gq_ref[...]) * (1.0 + scq_ref[...]) + shq_ref[...]).astype(BF16)
    gm = gm_ref[...]
    k_all = _dot(h_kv, wk_ref[...])
    q_all = _dot(h_q, wq_ref[...])
    v_ref[...] = _dot(h_kv, wv_ref[...]).astype(BF16)
    lf_ref[...] = jax.nn.log_sigmoid(_dot(h_kv, wf_ref[...]) + bf_ref[...])
    for c0 in range(0, k_ref.shape[1], MXU_WIDTH_V7X):
        cols = slice(c0, c0 + MXU_WIDTH_V7X)
        k_ref[:, cols] = _head_rms_norm(k_all[:, cols], kg_ref[:, cols], gm).astype(BF16)
        q = _head_rms_norm(q_all[:, cols], qg_ref[:, cols], gm)
        q_ref[:, cols] = (q * (HEAD_DIM ** -0.5)).astype(BF16)


def _l1_proj(act, w_down, x2d, gate, g_kv, sh_kv, sc_kv, g_q, sh_q, sc_q, wk, wv, wf, wq,
             k_gain, q_gain, b_f, seq):
    m, d = x2d.shape
    f = act.shape[1]
    lane = jnp.arange(MXU_WIDTH_V7X)
    group_mean = jnp.where(lane[:, None] // HEAD_DIM == lane[None, :] // HEAD_DIM,
                           1.0 / HEAD_DIM, 0.0).astype(BF16)
    tpb = seq // TM
    block_bytes = (2 * TM * f + 2 * f * d + 2 * 4 * TM * d + 3 * 2 * d * d + 2 * d * LANES_V7X
                   + 3 * 2 * TM * d + 4 * TM * LANES_V7X)
    full = lambda r, c: _resident_spec((r, c))
    rows = lambda c: pl.BlockSpec((TM, c), lambda i: (i, 0))
    return pl.pallas_call(
        _l1_proj_kernel,
        out_shape=(jax.ShapeDtypeStruct((m, d), F32), jax.ShapeDtypeStruct((m, d), BF16),
                   jax.ShapeDtypeStruct((m, d), BF16), jax.ShapeDtypeStruct((m, d), BF16),
                   jax.ShapeDtypeStruct((m, LANES_V7X), F32)),
        grid=(m // TM,),
        in_specs=[
            rows(f), full(f, d), rows(d), _batch_vec_spec(d, tpb),
            _row_spec(d), _batch_vec_spec(d, tpb), _batch_vec_spec(d, tpb),
            _row_spec(d), _batch_vec_spec(d, tpb), _batch_vec_spec(d, tpb),
            full(d, d), full(d, d), full(d, LANES_V7X), full(d, d),
            _row_spec(d), _row_spec(d), _row_spec(LANES_V7X), full(MXU_WIDTH_V7X, MXU_WIDTH_V7X),
        ],
        out_specs=(rows(d), rows(d), rows(d), rows(d), rows(LANES_V7X)),
        compiler_params=pltpu.CompilerParams(
            dimension_semantics=("arbitrary",),
            vmem_limit_bytes=_vmem_limit(block_bytes)),
        name="l1_proj",
    )(act, w_down, x2d, gate, g_kv, sh_kv, sc_kv, g_q, sh_q, sc_q, wk, wv, wf, wq,
      k_gain, q_gain, b_f, group_mean)


N_SPLIT = 3


def _split3(x):
    x0 = x.astype(BF16)
    r1 = x - x0.astype(F32)
    x1 = r1.astype(BF16)
    x2 = (r1 - x1.astype(F32)).astype(BF16)
    return x0, x1, x2


def _fox_prep_kernel(kb_ref, lf_ref, q_ref, k_ref, qa_ref, ka_ref, stats_ref, carry_scr):
    @pl.when(pl.program_id(1) == 0)
    def _():
        carry_scr[...] = jnp.zeros_like(carry_scr)

    r = lax.broadcasted_iota(jnp.int32, (TK, TK), 0)
    c = lax.broadcasted_iota(jnp.int32, (TK, TK), 1)
    tri = jnp.where(c <= r, 1.0, 0.0).astype(BF16)
    x0, x1, x2 = _split3(lf_ref[...])
    cs = _dot(tri, x0) + _dot(tri, x1) + _dot(tri, x2) + carry_scr[0:1, :]
    carry_scr[...] = jnp.broadcast_to(cs[TK - 1:TK, :], carry_scr.shape)

    d_idx = lax.broadcasted_iota(jnp.int32, (D_MODEL, LANES_V7X), 0)
    h_idx = lax.broadcasted_iota(jnp.int32, (D_MODEL, LANES_V7X), 1)
    head_of = jnp.where(d_idx // HEAD_DIM == h_idx, 1.0, 0.0).astype(BF16)
    qf = q_ref[...].astype(F32)
    reach = jnp.sqrt(_dot((qf * qf).astype(BF16), head_of)) * kb_ref[0, 0]
    own_logit = _dot((qf * k_ref[...].astype(F32)).astype(BF16), head_of)

    stats_ref[...] = jnp.concatenate([
        jnp.max(reach * (1.0 + 2.0 ** -8) - own_logit + cs, axis=0, keepdims=True),
        cs[0:1, :],
        jnp.max(reach, axis=0, keepdims=True),
        jnp.zeros((SUBLANES_V7X - 3, LANES_V7X), F32)], axis=0)

    lane = lax.broadcasted_iota(jnp.int32, (1, LANES_V7X), 1)
    packed = jnp.zeros((TK, LANES_V7X), F32)
    for i, term in enumerate(_split3(cs - reach) + _split3(-cs)):
        term = jnp.where(lane < N_HEADS, term.astype(F32), 0.0)
        packed = packed + (term if i == 0 else pltpu.roll(term, N_HEADS * i, axis=1))
    packed = packed.astype(BF16)

    src = lax.broadcasted_iota(jnp.int32, (LANES_V7X, 2 * LANES_V7X), 0)
    dst = lax.broadcasted_iota(jnp.int32, (LANES_V7X, 2 * LANES_V7X), 1)

    for h in range(N_HEADS):
        pair, parity = divmod(h, HEADS_PER_LANE_TILE)
        spare = HEAD_DIM * (1 - parity)
        place = jnp.where((src % N_HEADS == h) & (src < 2 * N_SPLIT * N_HEADS)
                          & (dst % LANES_V7X == spare + src // N_HEADS)
                          & ((dst >= LANES_V7X) == (src >= N_SPLIT * N_HEADS)), 1.0, 0.0)
        placed = _dot(packed, place.astype(BF16))
        ones_q = jnp.where((lane >= spare + N_SPLIT) & (lane < spare + 2 * N_SPLIT), 1.0, 0.0)
        ones_k = jnp.where((lane >= spare) & (lane < spare + N_SPLIT), 1.0, 0.0)
        aug_q = (placed[:, :LANES_V7X] + ones_q).astype(BF16)
        aug_k = (placed[:, LANES_V7X:] + ones_k).astype(BF16)
        own = (lane // HEAD_DIM) == parity
        cols = slice(pair * LANES_V7X, (pair + 1) * LANES_V7X)
        qa_ref[h] = jnp.where(own, q_ref[:, cols], aug_q)
        ka_ref[h] = jnp.where(own, k_ref[:, cols], aug_k)


def _fox_prep(log_f, q, k, key_norm_bound, batch, seq):
    assert TQ == TK
    d = q.shape[1]
    block_bytes = (4 * TK * LANES_V7X + 2 * 2 * TK * d + 2 * 2 * N_HEADS * TK * LANES_V7X
                   + 4 * SUBLANES_V7X * LANES_V7X)
    aug_shape = jax.ShapeDtypeStruct((batch, N_HEADS, seq, LANES_V7X), BF16)
    aug_spec = pl.BlockSpec((None, N_HEADS, TK, LANES_V7X), lambda b, t: (b, 0, t, 0))
    return pl.pallas_call(
        _fox_prep_kernel,
        out_shape=(aug_shape, aug_shape,
                   jax.ShapeDtypeStruct((batch, seq // TK, SUBLANES_V7X, LANES_V7X), F32)),
        grid=(batch, seq // TK),
        in_specs=[
            pl.BlockSpec(memory_space=pltpu.SMEM),
            pl.BlockSpec((None, TK, LANES_V7X), lambda b, t: (b, t, 0)),
            pl.BlockSpec((None, TK, d), lambda b, t: (b, t, 0)),
            pl.BlockSpec((None, TK, d), lambda b, t: (b, t, 0)),
        ],
        out_specs=(aug_spec, aug_spec,
                   pl.BlockSpec((None, None, SUBLANES_V7X, LANES_V7X), lambda b, t: (b, t, 0, 0))),
        scratch_shapes=[pltpu.VMEM((SUBLANES_V7X, LANES_V7X), F32)],
        compiler_params=pltpu.CompilerParams(
            dimension_semantics=("arbitrary", "arbitrary"),
            vmem_limit_bytes=_vmem_limit(block_bytes)),
        name="fox_prep",
    )(key_norm_bound, log_f.reshape(batch, seq, LANES_V7X), q.reshape(batch, seq, d),
      k.reshape(batch, seq, d))


def _head_select(head):
    lane = lax.broadcasted_iota(jnp.int32, (1, LANES_V7X), 1)
    return (lane // HEAD_DIM) == head


def _sb_kernel(q_ref, k_ref, v_ref, o_ref):
    i = pl.program_id(2)

    def strictly_lower(n):
        return (lax.broadcasted_iota(jnp.int32, (n, n), 0)
                > lax.broadcasted_iota(jnp.int32, (n, n), 1))

    diag_bias = jnp.where(strictly_lower(SB_T), 0.0, NEG_BIG)
    after = jnp.where(strictly_lower(SB_T), 1.0, 0.0).astype(BF16)
    after_left = jnp.where(strictly_lower(SB_LEFT), 1.0, 0.0).astype(BF16)
    ones = jnp.ones((SB_T, LANES_V7X), BF16)
    ones_left = jnp.ones((SB_T, SB_LEFT), BF16)

    def softplus(z):
        return jnp.maximum(jnp.log2(1.0 + jnp.exp2(jnp.minimum(z, SOFTPLUS_LINEAR))), z)

    def keys(start, size):
        rows = pl.ds(pl.multiple_of(start, SB_T), size)
        return k_ref[rows, :], v_ref[rows, :]

    def left_start(g):
        return jnp.maximum(g * SB_T - SB_LEFT, 0)

    def band_logits(qh, g, may_clip):
        k_d, _ = keys(g * SB_T, SB_T)
        k_p, _ = keys(left_start(g), SB_LEFT)
        z_p = _dot_nt(qh, k_p)
        if may_clip:
            key = left_start(g) + lax.broadcasted_iota(jnp.int32, (SB_T, SB_LEFT), 1)
            z_p = jnp.where(key < g * SB_T, z_p, NEG_BIG)
        return _dot_nt(qh, k_d) + diag_bias, z_p

    def band_softplus(z_d, z_p):
        return z_d, z_p, softplus(z_d), softplus(z_p)

    def band_sums(z_d, z_p, sp_d, sp_p):
        sp_d16 = sp_d.astype(BF16)
        within_d = _dot(sp_d16, after)
        within_p = _dot(sp_p.astype(BF16), after_left) + _dot(sp_d16, ones_left)
        total = (within_p + sp_p)[:, 0:1]
        return (z_d - sp_d) - within_d, (z_p - sp_p) - within_p, total

    def band_weights(e_d, e_p, total):
        return jnp.exp2(e_d).astype(BF16), jnp.exp2(e_p).astype(BF16), total

    def band_out(g, a_d, a_p, total):
        _, v_d = keys(g * SB_T, SB_T)
        _, v_p = keys(left_start(g), SB_LEFT)
        return jnp.broadcast_to(-total, (SB_T, LANES_V7X)), _dot(a_d, v_d) + _dot(a_p, v_p)

    def block(qh, start, carry, acc):
        kj, vj = keys(start, SB_T)
        z = _dot_nt(qh, kj)
        sp = softplus(z)
        sp16 = sp.astype(BF16)
        within = _dot(sp16, after)
        a = jnp.exp2((z - sp) - within + carry)
        acc = acc + _dot(a.astype(BF16), vj)
        carry = carry - _dot(sp16, ones)
        return carry, acc

    heads = []
    for u in range(SB_SUB):
        g = i * SB_SUB + u
        q = q_ref[u * SB_T:(u + 1) * SB_T, :]
        for head in range(HEADS_PER_LANE_TILE):
            heads.append((g, jnp.where(_head_select(head), q, 0.0).astype(BF16)))
    logits = [band_logits(qh, g, may_clip=c // HEADS_PER_LANE_TILE < SB_LEFT // SB_T)
              for c, (g, qh) in enumerate(heads)]
    softplus_terms = [band_softplus(*z) for z in logits]
    sums = [band_sums(*t) for t in softplus_terms]
    weights = [band_weights(*s) for s in sums]
    chains = [(g, qh) + band_out(g, *w) for (g, qh), w in zip(heads, weights)]

    def is_open(carry):
        return jnp.max(carry) > -ZERO_EXP * LOG2_E

    chain_open = [is_open(carry) for _, _, carry, _ in chains]

    def walk_on():
        outs = []
        for (g, qh, carry, acc), this_open in zip(chains, chain_open):
            def cond(state):
                start, carry, _ = state
                return jnp.logical_and(start >= 0, is_open(carry))

            def body(state, qh=qh):
                start, carry, acc = state
                carry, acc = block(qh, start, carry, acc)
                return start - SB_T, carry, acc

            def walk(cond=cond, body=body, first=(left_start(g) - SB_T, carry, acc)):
                return lax.while_loop(cond, body, first)[2]

            outs.append(lax.cond(this_open, walk, lambda acc=acc: acc))
        return outs

    outs = lax.cond(functools.reduce(jnp.logical_or, chain_open), walk_on,
                    lambda: [acc for _, _, _, acc in chains])
    for u in range(SB_SUB):
        pair = outs[u * HEADS_PER_LANE_TILE:(u + 1) * HEADS_PER_LANE_TILE]
        o_ref[u * SB_T:(u + 1) * SB_T, :] = jnp.where(_head_select(0), pair[0], pair[1]).astype(o_ref.dtype)


def _sb_attn(qkv, batch, seq):
    assert SB_T == LANES_V7X and SB_LEFT % SB_T == 0
    rows = SB_SUB * SB_T
    stage_bytes = SB_SUB * HEADS_PER_LANE_TILE * 2 * 4 * SB_T * (SB_T + SB_LEFT)
    block_bytes = 2 * (2 * rows * LANES_V7X + 2 * seq * LANES_V7X) + stage_bytes // 2
    return pl.pallas_call(
        _sb_kernel,
        out_shape=jax.ShapeDtypeStruct((batch, seq, D_MODEL), BF16),
        grid=(batch, N_HEAD_PAIRS, seq // rows),
        in_specs=[
            pl.BlockSpec((None, rows, LANES_V7X), lambda b, p, i: (b, i, p)),
            pl.BlockSpec((None, seq, LANES_V7X), lambda b, p, i: (b, 0, N_HEAD_PAIRS + p)),
            pl.BlockSpec((None, seq, LANES_V7X), lambda b, p, i: (b, 0, 2 * N_HEAD_PAIRS + p)),
        ],
        out_specs=pl.BlockSpec((None, rows, LANES_V7X), lambda b, p, i: (b, i, p)),
        compiler_params=pltpu.CompilerParams(
            dimension_semantics=("arbitrary", "arbitrary", "arbitrary"),
            vmem_limit_bytes=_vmem_limit(block_bytes)),
        name="sb_attn",
    )(qkv, qkv, qkv)


def _fox_kernel(a_ref, fk0_ref, g_ref, qa_ref, ka_ref, v_ref, o_ref, acc_scr, vt_scr):
    b = pl.program_id(0)
    p_id = pl.program_id(1)
    i = pl.program_id(2)
    n_blocks = fk0_ref.shape[1]
    row = lax.broadcasted_iota(jnp.int32, (TQ, TK), 0)
    col = lax.broadcasted_iota(jnp.int32, (TQ, TK), 1)
    diag_mask = col <= row

    acc0 = jnp.zeros((TQ, LANES_V7X), F32)
    chains = range(FOX_SUB * HEADS_PER_LANE_TILE)
    bh0 = (b * N_HEAD_PAIRS + p_id) * HEADS_PER_LANE_TILE

    def scores(chain, j):
        sub, head = chain // HEADS_PER_LANE_TILE, chain % HEADS_PER_LANE_TILE
        q = qa_ref[head, pl.ds(pl.multiple_of(sub * TQ, TQ), TQ), :]
        return _dot_nt(q, ka_ref[head, pl.ds(pl.multiple_of(j * TK, TK), TK), :])

    def values(chain, j):
        vj = v_ref[pl.ds(pl.multiple_of(j * TK, TK), TK), :]
        own = _head_select(chain % HEADS_PER_LANE_TILE)
        return jnp.where(own, vj, jnp.ones_like(vj))

    def diag_block(chain):
        return i * FOX_SUB + chain // HEADS_PER_LANE_TILE

    def count_left(chain):
        bh, qi = bh0 + chain % HEADS_PER_LANE_TILE, diag_block(chain)
        bound = a_ref[bh, qi]

        def more(n):
            j = qi - 1 - n
            right = fk0_ref[bh, jnp.minimum(j + 1, n_blocks - 1)]
            return jnp.logical_and(j >= 0, bound - right > -ZERO_EXP)

        return lax.while_loop(more, lambda n: n + 1, jnp.int32(0))

    def fast_walks():
        counts = [count_left(chain) for chain in chains]
        firsts = [sum(n + 1 for n in counts[:chain]) for chain in chains]
        n_items = firsts[-1] + counts[-1] + 1
        diag_bias_t = jnp.where(row <= col, 0.0, NEG_BIG)
        feature = lax.broadcasted_iota(jnp.int32, (LANES_V7X, 1), 0)
        acc_scr[...] = jnp.zeros_like(acc_scr)

        def scores_t(chain, j):
            sub, head = chain // HEADS_PER_LANE_TILE, chain % HEADS_PER_LANE_TILE
            q = qa_ref[head, pl.ds(pl.multiple_of(sub * TQ, TQ), TQ), :]
            return _dot_nt(ka_ref[head, pl.ds(pl.multiple_of(j * TK, TK), TK), :], q)

        def values_t(chain, j):
            vt = vt_scr[:, pl.ds(pl.multiple_of(j * TK, TK), TK)]
            own = (feature // HEAD_DIM) == chain % HEADS_PER_LANE_TILE
            return jnp.where(own, vt, jnp.ones_like(vt))

        def step(t, p_t, p_chain, p_block):
            chain = sum(jnp.where(t >= first, 1, 0) for first in firsts[1:])
            first = functools.reduce(lambda f, nxt: jnp.where(t >= nxt, nxt, f), firsts[1:], 0)
            back = t - first
            block = jnp.maximum(diag_block(chain) - back, 0)
            s_t = scores_t(chain, block) + jnp.where(back == 0, diag_bias_t,
                                                     jnp.where(t >= n_items, NEG_BIG, 0.0))
            acc_scr[p_chain] += _dot(values_t(p_chain, p_block), p_t)
            return jnp.exp(s_t).astype(BF16), chain, block

        def steps(first, count, carried):
            for t in range(count):
                carried = step(first + t, *carried)
            return carried

        pairs = (n_items + 2) // 2
        start = (jnp.zeros((TK, TQ), BF16), jnp.int32(0), jnp.int32(0))
        carried = lax.fori_loop(0, pairs // 2, lambda u, c: steps(4 * u, 4, c), start)
        lax.fori_loop(0, pairs % 2, lambda _, c: steps(4 * (pairs // 2), 2, c), carried)

        outs = []
        for sub in range(FOX_SUB):
            normed = []
            for head in range(HEADS_PER_LANE_TILE):
                acc_t = acc_scr[sub * HEADS_PER_LANE_TILE + head]
                spare = HEAD_DIM * (1 - head)
                normed.append(acc_t / acc_t[spare:spare + 1, :])
            outs.append(jnp.where(feature // HEAD_DIM == 0, normed[0], normed[1]).T)
        return outs

    def online_walks():
        accs = []
        for chain in chains:
            def body(t, carried, chain=chain):
                acc, m = carried
                block = diag_block(chain) - t
                s = jnp.where(jnp.logical_or(t > 0, diag_mask), scores(chain, block), NEG_BIG)
                m_new = jnp.maximum(m, jnp.max(s, axis=-1, keepdims=True))
                p = jnp.exp(s - m_new).astype(BF16)
                return jnp.exp(m - m_new) * acc + _dot(p, values(chain, block)), m_new

            m0 = jnp.full((TQ, 1), NEG_BIG, F32)
            acc = lax.fori_loop(0, count_left(chain) + 1, body, (acc0, m0))[0]
            accs.append(acc / pltpu.roll(acc, HEAD_DIM, axis=1))
        return [jnp.where(_head_select(0), accs[sub * HEADS_PER_LANE_TILE],
                          accs[sub * HEADS_PER_LANE_TILE + 1]) for sub in range(FOX_SUB)]

    @pl.when(i == 0)
    def _():
        for c in range(0, vt_scr.shape[1], TK):
            vt_scr[:, c:c + TK] = v_ref[c:c + TK, :].T

    largest_reach = functools.reduce(jnp.maximum, [
        g_ref[bh0 + chain % HEADS_PER_LANE_TILE, diag_block(chain)] for chain in chains])
    outs = lax.cond(2.0 * largest_reach < SAFE_EXP_RANGE, fast_walks, online_walks)
    for sub in range(FOX_SUB):
        o_ref[sub * TQ:(sub + 1) * TQ, :] = outs[sub].astype(o_ref.dtype)


def _fox_attn(q_aug, k_aug, v, walk_bound, fk_block_start, max_reach, batch, seq):
    assert TQ == TK
    rows = FOX_SUB * TQ
    n_chains = FOX_SUB * HEADS_PER_LANE_TILE
    block_bytes = (2 * (HEADS_PER_LANE_TILE * (rows + seq) * LANES_V7X + seq * LANES_V7X + rows * LANES_V7X)
                   + 4 * n_chains * TQ * LANES_V7X + 2 * seq * LANES_V7X)
    smem = pl.BlockSpec(memory_space=pltpu.SMEM)
    return pl.pallas_call(
        _fox_kernel,
        out_shape=jax.ShapeDtypeStruct((batch, seq, D_MODEL), BF16),
        grid=(batch, N_HEAD_PAIRS, seq // rows),
        in_specs=[
            smem, smem, smem,
            pl.BlockSpec((None, HEADS_PER_LANE_TILE, rows, LANES_V7X), lambda b, p, i: (b, p, i, 0)),
            pl.BlockSpec((None, HEADS_PER_LANE_TILE, seq, LANES_V7X), lambda b, p, i: (b, p, 0, 0)),
            pl.BlockSpec((None, seq, LANES_V7X), lambda b, p, i: (b, 0, p)),
        ],
        out_specs=pl.BlockSpec((None, rows, LANES_V7X), lambda b, p, i: (b, i, p)),
        scratch_shapes=[pltpu.VMEM((n_chains, LANES_V7X, TQ), F32),
                        pltpu.VMEM((LANES_V7X, seq), BF16)],
        compiler_params=pltpu.CompilerParams(
            dimension_semantics=("arbitrary", "arbitrary", "arbitrary"),
            vmem_limit_bytes=_vmem_limit(block_bytes)),
        name="fox_attn",
    )(walk_bound, fk_block_start, max_reach, q_aug, k_aug, v)


def kernel(x, c, ada_w, ada_b, norm_attn_g, norm_ffn_g, w_ffn_in, w_ffn_down, sb_w_qkv, sb_w_o,
           kv_ada_w, kv_ada_b, kv_norm_g, w_kvf, b_f, k_norm_g, fox_w_q, q_norm_g, fox_w_o):
    batch, seq, d = x.shape
    assert d == D_MODEL and seq % TM == 0
    assert seq % (FOX_SUB * TQ) == 0 and seq % (SB_SUB * SB_T) == 0
    assert ada_w.shape[0] == 2 and sb_w_qkv.shape[0] == 1 and fox_w_q.shape[0] == 1
    m = batch * seq

    mod = _ada_mod(c, ada_w, ada_b)[:, :batch]
    kv_mod = _ada_mod(c, kv_ada_w[None], kv_ada_b[None])[0, :batch]

    def vecs(t, n):
        return [t[:, None, j * d:(j + 1) * d] for j in range(n)]

    row = lambda t: t.reshape(1, -1).astype(F32)
    bf = lambda t: t.astype(BF16)

    x2d = x.reshape(m, d)

    sh_a, sc_a, g_a, sh_f, sc_f, g_f = vecs(mod[0], 6)
    qkv_scale = jnp.concatenate([jnp.full((1, d), LOG2_E * HEAD_DIM ** -0.5, F32),
                                 jnp.ones((1, 2 * d), F32)], axis=1)
    qkv = _norm_matmul(x2d, row(norm_attn_g[0]), sh_a, sc_a, bf(sb_w_qkv[0]), qkv_scale, seq)
    o = _sb_attn(qkv.reshape(batch, seq, 3 * d), batch, seq)
    x2d, act = _out_ffn_in(o.reshape(m, d), bf(sb_w_o[0]), x2d, g_a,
                           row(norm_ffn_g[0]), sh_f, sc_f, bf(w_ffn_in[0]), seq)
    g_f0 = g_f

    sh_a, sc_a, g_a, sh_f, sc_f, g_f = vecs(mod[1], 6)
    kv_shift, kv_scale = vecs(kv_mod, 2)
    w_f_pad = jnp.zeros((d, LANES_V7X), F32).at[:, :N_HEADS].set(w_kvf[:, 2 * d:])
    b_f_pad = jnp.zeros((1, LANES_V7X), F32).at[0, :N_HEADS].set(b_f)
    x2d, k, v, q, log_f = _l1_proj(
        act, bf(w_ffn_down[0]), x2d, g_f0,
        row(kv_norm_g), kv_shift, kv_scale, row(norm_attn_g[1]), sh_a, sc_a,
        bf(w_kvf[:, :d]), bf(w_kvf[:, d:2 * d]), bf(w_f_pad), bf(fox_w_q[0]),
        row(jnp.tile(k_norm_g, N_HEADS)), row(jnp.tile(q_norm_g[0], N_HEADS)), b_f_pad, seq)
    key_norm_bound = (1.01 * HEAD_DIM ** 0.5 * jnp.max(jnp.abs(k_norm_g))).reshape(1, 1).astype(F32)
    q_aug, k_aug, stats = _fox_prep(log_f, q, k, key_norm_bound, batch, seq)
    tables = stats[:, :, :3, :N_HEADS].transpose(2, 0, 3, 1).reshape(3, batch * N_HEADS, seq // TK)
    o = _fox_attn(q_aug, k_aug, v.reshape(batch, seq, d), tables[0], tables[1], tables[2], batch, seq)
    x2d, act = _out_ffn_in(o.reshape(m, d), bf(fox_w_o[0]), x2d, g_a,
                           row(norm_ffn_g[1]), sh_f, sc_f, bf(w_ffn_in[1]), seq)
    x2d = _proj_residual(act, bf(w_ffn_down[1]), x2d, g_f, seq)
    return x2d.reshape(batch, seq, d)
```

```python
import functools

import jax
import jax.numpy as jnp
from jax import lax
from jax.experimental import pallas as pl
from jax.experimental.pallas import tpu as pltpu

D_MODEL = 1024
N_HEADS = 16
HEAD_DIM = D_MODEL // N_HEADS
EPS = 1e-6

LANES_V7X = 128
SUBLANES_V7X = 8
VMEM_BYTES_V7X = 64 * 1024 * 1024

HEADS_PER_LANE_TILE = LANES_V7X // HEAD_DIM
N_HEAD_PAIRS = N_HEADS // HEADS_PER_LANE_TILE

MXU_WIDTH_V7X = 256

TM = 512
COL_CHUNK = 2 * MXU_WIDTH_V7X
TQ = 512
TK = 512
FOX_SUB = 4
SB_T = 128
SB_LEFT = 256
SB_SUB = 16

ZERO_EXP = 104.5
SAFE_EXP_RANGE = 80.0
NEG_BIG = -1e30
LOG2_E = 1.4426950408889634
SOFTPLUS_LINEAR = 64.0

F32 = jnp.float32
BF16 = jnp.bfloat16


def _vmem_limit(block_bytes):
    want = 2 * block_bytes + 16 * 1024 * 1024
    return int(min(want, VMEM_BYTES_V7X - 8 * 1024 * 1024))


def _dot(a, b):
    return jnp.dot(a, b, preferred_element_type=F32)


def _dot_nt(a, b):
    return lax.dot_general(a, b, (((1,), (1,)), ((), ())), preferred_element_type=F32)


def _ada_kernel(ct_ref, w_ref, b_ref, o_ref, *, batch):
    c = ct_ref[...]
    c_act = c * jax.nn.sigmoid(c)
    w = w_ref[...]
    o_ref[...] = jnp.zeros_like(o_ref)
    for b in range(batch):
        o_ref[b:b + 1, :] = jnp.sum(w * c_act[:, b:b + 1], axis=0, keepdims=True) + b_ref[...]


def _ada_mod(c, w, b):
    n_layers, d, n = w.shape
    batch = c.shape[0]
    rows = SUBLANES_V7X
    assert batch <= rows
    tn = 1024
    c_t = jnp.zeros((d, rows), F32).at[:, :batch].set(c.astype(F32).T)
    block_bytes = 4 * (d * tn + d * LANES_V7X + tn + rows * tn)
    return pl.pallas_call(
        functools.partial(_ada_kernel, batch=batch),
        out_shape=jax.ShapeDtypeStruct((n_layers, rows, n), F32),
        grid=(n_layers, n // tn),
        in_specs=[
            pl.BlockSpec((d, rows), lambda l, j: (0, 0)),
            pl.BlockSpec((None, d, tn), lambda l, j: (l, 0, j)),
            pl.BlockSpec((None, 1, tn), lambda l, j: (l, 0, j)),
        ],
        out_specs=pl.BlockSpec((None, rows, tn), lambda l, j: (l, 0, j)),
        compiler_params=pltpu.CompilerParams(
            dimension_semantics=("arbitrary", "arbitrary"),
            vmem_limit_bytes=_vmem_limit(block_bytes)),
        name="ada_mod",
    )(c_t, w, b.reshape(n_layers, 1, n))


def _norm_modulate(x, g, shift, scale):
    ms = jnp.mean(x * x, axis=-1, keepdims=True)
    y = x * lax.rsqrt(ms + EPS)
    return (y * g) * (1.0 + scale) + shift


def _row_spec(d):
    return pl.BlockSpec((1, d), lambda *_: (0, 0))


def _resident_spec(shape, block_index=(0, 0)):
    return pl.BlockSpec(shape, lambda *_: block_index)


def _batch_vec_spec(d, tiles_per_batch):
    return pl.BlockSpec((None, 1, d), lambda i, *_: (i // tiles_per_batch, 0, 0))


def _column_chunks(n):
    return [(c, min(c + COL_CHUNK, n)) for c in range(0, n, COL_CHUNK)]


def _norm_matmul_kernel(x_ref, g_ref, sh_ref, sc_ref, w_ref, cs_ref, o_ref):
    h = _norm_modulate(x_ref[...], g_ref[...], sh_ref[...], sc_ref[...]).astype(BF16)
    for c0, c1 in _column_chunks(o_ref.shape[1]):
        o_ref[:, c0:c1] = (_dot(h, w_ref[:, c0:c1]) * cs_ref[:, c0:c1]).astype(o_ref.dtype)


def _norm_matmul(x2d, g, shift, scale, w, col_scale, seq):
    m, d = x2d.shape
    n = w.shape[1]
    block_bytes = 4 * TM * d + 2 * d * n + 2 * TM * n + 4 * n
    return pl.pallas_call(
        _norm_matmul_kernel,
        out_shape=jax.ShapeDtypeStruct((m, n), BF16),
        grid=(m // TM,),
        in_specs=[
            pl.BlockSpec((TM, d), lambda i: (i, 0)),
            _row_spec(d),
            _batch_vec_spec(d, seq // TM),
            _batch_vec_spec(d, seq // TM),
            _resident_spec((d, n)),
            pl.BlockSpec((1, n), lambda i: (0, 0)),
        ],
        out_specs=pl.BlockSpec((TM, n), lambda i: (i, 0)),
        compiler_params=pltpu.CompilerParams(
            dimension_semantics=("arbitrary",),
            vmem_limit_bytes=_vmem_limit(block_bytes)),
        name="norm_matmul",
    )(x2d, g, shift, scale, w, col_scale)


def _out_ffn_in_kernel(a_ref, wo_ref, x_ref, gate_ref, g_ref, sh_ref, sc_ref, wg_ref, wu_ref,
                       x1_ref, act_ref):
    x1 = x_ref[...] + gate_ref[...] * _dot(a_ref[...], wo_ref[...])
    x1_ref[...] = x1
    h = _norm_modulate(x1, g_ref[...], sh_ref[...], sc_ref[...]).astype(BF16)
    for c0, c1 in _column_chunks(act_ref.shape[1]):
        gate = _dot(h, wg_ref[:, c0:c1])
        up = _dot(h, wu_ref[:, c0:c1])
        act_ref[:, c0:c1] = (gate * jax.nn.sigmoid(gate) * up).astype(act_ref.dtype)


def _out_ffn_in(a, w_o, x2d, gate, g, shift, scale, w_in, seq):
    m, d = x2d.shape
    f = w_in.shape[1] // 2
    tpb = seq // TM
    block_bytes = 2 * TM * d + 2 * d * d + 2 * 4 * TM * d + 2 * 2 * d * f + 2 * TM * f
    return pl.pallas_call(
        _out_ffn_in_kernel,
        out_shape=(jax.ShapeDtypeStruct((m, d), F32), jax.ShapeDtypeStruct((m, f), BF16)),
        grid=(m // TM,),
        in_specs=[
            pl.BlockSpec((TM, d), lambda i: (i, 0)),
            _resident_spec((d, d)),
            pl.BlockSpec((TM, d), lambda i: (i, 0)),
            _batch_vec_spec(d, tpb),
            _row_spec(d),
            _batch_vec_spec(d, tpb),
            _batch_vec_spec(d, tpb),
            _resident_spec((d, f)),
            _resident_spec((d, f), (0, 1)),
        ],
        out_specs=(pl.BlockSpec((TM, d), lambda i: (i, 0)), pl.BlockSpec((TM, f), lambda i: (i, 0))),
        compiler_params=pltpu.CompilerParams(
            dimension_semantics=("arbitrary",),
            vmem_limit_bytes=_vmem_limit(block_bytes)),
        name="out_ffn_in",
    )(a, w_o, x2d, gate, g, shift, scale, w_in, w_in)


def _proj_residual_kernel(a_ref, w_ref, x_ref, gate_ref, o_ref):
    o_ref[...] = x_ref[...] + gate_ref[...] * _dot(a_ref[...], w_ref[...])


def _proj_residual(a, w, x2d, gate, seq):
    m, k = a.shape
    d = w.shape[1]
    block_bytes = 2 * TM * k + 2 * k * d + 2 * 4 * TM * d
    return pl.pallas_call(
        _proj_residual_kernel,
        out_shape=jax.ShapeDtypeStruct((m, d), F32),
        grid=(m // TM,),
        in_specs=[
            pl.BlockSpec((TM, k), lambda i: (i, 0)),
            _resident_spec((k, d)),
            pl.BlockSpec((TM, d), lambda i: (i, 0)),
            _batch_vec_spec(d, seq // TM),
        ],
        out_specs=pl.BlockSpec((TM, d), lambda i: (i, 0)),
        compiler_params=pltpu.CompilerParams(
            dimension_semantics=("arbitrary",),
            vmem_limit_bytes=_vmem_limit(block_bytes)),
        name="proj_residual",
    )(a, w, x2d, gate)


def _head_rms_norm(t, gain, group_mean):
    ms = _dot((t * t).astype(BF16), group_mean)
    return t * lax.rsqrt(ms + EPS) * gain


def _l1_proj_kernel(kb_ref, act_ref, wd_ref, x_ref, gate_ref,
                    gkv_ref, shkv_ref, sckv_ref, gq_ref, shq_ref, scq_ref,
                    wk_ref, wv_ref, wf_ref, wq_ref, kg_ref, qg_ref, bf_ref, gm_ref,
                    x2_ref, v_ref, qa_ref, ka_ref, stats_ref,
                    q_ref, k_ref, carry_scr, *, tiles_per_sequence):
    x = x_ref[...] + gate_ref[...] * _dot(act_ref[...], wd_ref[...])
    x2_ref[...] = x
    ms = jnp.mean(x * x, axis=-1, keepdims=True)
    y = x * lax.rsqrt(ms + EPS)
    h_kv = ((y * gkv_ref[...]) * (1.0 + sckv_ref[...]) + shkv_ref[...]).astype(BF16)
    h_q = ((y * gq_ref[...]) * (1.0 + scq_ref[...]) + shq_ref[...]).astype(BF16)
    gm = gm_ref[...]
    k_all = _dot(h_kv, wk_ref[...])
    q_all = _dot(h_q, wq_ref[...])
    v_ref[...] = _dot(h_kv, wv_ref[...]).astype(BF16)
    log_f = jax.nn.log_sigmoid(_dot(h_kv, wf_ref[...]) + bf_ref[...])
    for c0 in range(0, k_ref.shape[1], MXU_WIDTH_V7X):
        cols = slice(c0, c0 + MXU_WIDTH_V7X)
        k_ref[:, cols] = _head_rms_norm(k_all[:, cols], kg_ref[:, cols], gm).astype(BF16)
        q = _head_rms_norm(q_all[:, cols], qg_ref[:, cols], gm)
        q_ref[:, cols] = (q * (HEAD_DIM ** -0.5)).astype(BF16)
    _fox_augment(pl.program_id(0) % tiles_per_sequence == 0, kb_ref[0, 0], log_f,
                 q_ref, k_ref, qa_ref, ka_ref, stats_ref, carry_scr)


def _l1_proj(act, w_down, x2d, gate, g_kv, sh_kv, sc_kv, g_q, sh_q, sc_q, wk, wv, wf, wq,
             k_gain, q_gain, b_f, key_norm_bound, batch, seq):
    assert TM == TK == TQ
    m, d = x2d.shape
    f = act.shape[1]
    lane = jnp.arange(MXU_WIDTH_V7X)
    group_mean = jnp.where(lane[:, None] // HEAD_DIM == lane[None, :] // HEAD_DIM,
                           1.0 / HEAD_DIM, 0.0).astype(BF16)
    tpb = seq // TM
    block_bytes = (2 * TM * f + 2 * f * d + 2 * 4 * TM * d + 3 * 2 * d * d + 2 * d * LANES_V7X
                   + 2 * TM * d + 2 * 2 * N_HEADS * TM * LANES_V7X + 2 * 2 * TM * d)
    full = lambda r, c: _resident_spec((r, c))
    rows = lambda c: pl.BlockSpec((TM, c), lambda i: (i, 0))
    aug_shape = jax.ShapeDtypeStruct((batch, N_HEADS, seq, LANES_V7X), BF16)
    aug_spec = pl.BlockSpec((None, N_HEADS, TM, LANES_V7X), lambda i: (i // tpb, 0, i % tpb, 0))
    return pl.pallas_call(
        functools.partial(_l1_proj_kernel, tiles_per_sequence=tpb),
        out_shape=(jax.ShapeDtypeStruct((m, d), F32), jax.ShapeDtypeStruct((m, d), BF16),
                   aug_shape, aug_shape,
                   jax.ShapeDtypeStruct((batch, tpb, SUBLANES_V7X, LANES_V7X), F32)),
        grid=(m // TM,),
        in_specs=[
            pl.BlockSpec(memory_space=pltpu.SMEM),
            rows(f), full(f, d), rows(d), _batch_vec_spec(d, tpb),
            _row_spec(d), _batch_vec_spec(d, tpb), _batch_vec_spec(d, tpb),
            _row_spec(d), _batch_vec_spec(d, tpb), _batch_vec_spec(d, tpb),
            full(d, d), full(d, d), full(d, LANES_V7X), full(d, d),
            _row_spec(d), _row_spec(d), _row_spec(LANES_V7X), full(MXU_WIDTH_V7X, MXU_WIDTH_V7X),
        ],
        out_specs=(rows(d), rows(d), aug_spec, aug_spec,
                   pl.BlockSpec((None, None, SUBLANES_V7X, LANES_V7X),
                                lambda i: (i // tpb, i % tpb, 0, 0))),
        scratch_shapes=[pltpu.VMEM((TM, d), BF16), pltpu.VMEM((TM, d), BF16),
                        pltpu.VMEM((SUBLANES_V7X, LANES_V7X), F32)],
        compiler_params=pltpu.CompilerParams(
            dimension_semantics=("arbitrary",),
            vmem_limit_bytes=_vmem_limit(block_bytes)),
        name="l1_proj",
    )(key_norm_bound, act, w_down, x2d, gate, g_kv, sh_kv, sc_kv, g_q, sh_q, sc_q, wk, wv, wf, wq,
      k_gain, q_gain, b_f, group_mean)


N_SPLIT = 3


def _split3(x):
    x0 = x.astype(BF16)
    r1 = x - x0.astype(F32)
    x1 = r1.astype(BF16)
    x2 = (r1 - x1.astype(F32)).astype(BF16)
    return x0, x1, x2


def _fox_augment(first_tile, kb, log_f, q_ref, k_ref, qa_ref, ka_ref, stats_ref, carry_scr):
    @pl.when(first_tile)
    def _():
        carry_scr[...] = jnp.zeros_like(carry_scr)

    r = lax.broadcasted_iota(jnp.int32, (TK, TK), 0)
    c = lax.broadcasted_iota(jnp.int32, (TK, TK), 1)
    tri = jnp.where(c <= r, 1.0, 0.0).astype(BF16)
    x0, x1, x2 = _split3(log_f)
    cs = _dot(tri, x0) + _dot(tri, x1) + _dot(tri, x2) + carry_scr[0:1, :]
    carry_scr[...] = jnp.broadcast_to(cs[TK - 1:TK, :], carry_scr.shape)

    d_idx = lax.broadcasted_iota(jnp.int32, (D_MODEL, LANES_V7X), 0)
    h_idx = lax.broadcasted_iota(jnp.int32, (D_MODEL, LANES_V7X), 1)
    head_of = jnp.where(d_idx // HEAD_DIM == h_idx, 1.0, 0.0).astype(BF16)
    qf = q_ref[...].astype(F32)
    reach = jnp.sqrt(_dot((qf * qf).astype(BF16), head_of)) * kb
    own_logit = _dot((qf * k_ref[...].astype(F32)).astype(BF16), head_of)

    stats_ref[...] = jnp.concatenate([
        jnp.max(reach * (1.0 + 2.0 ** -8) - own_logit + cs, axis=0, keepdims=True),
        cs[0:1, :],
        jnp.max(reach, axis=0, keepdims=True),
        jnp.zeros((SUBLANES_V7X - 3, LANES_V7X), F32)], axis=0)

    lane = lax.broadcasted_iota(jnp.int32, (1, LANES_V7X), 1)
    packed = jnp.zeros((TK, LANES_V7X), F32)
    for i, term in enumerate(_split3(cs - reach) + _split3(-cs)):
        term = jnp.where(lane < N_HEADS, term.astype(F32), 0.0)
        packed = packed + (term if i == 0 else pltpu.roll(term, N_HEADS * i, axis=1))
    packed = packed.astype(BF16)

    src = lax.broadcasted_iota(jnp.int32, (LANES_V7X, 2 * LANES_V7X), 0)
    dst = lax.broadcasted_iota(jnp.int32, (LANES_V7X, 2 * LANES_V7X), 1)

    for h in range(N_HEADS):
        pair, parity = divmod(h, HEADS_PER_LANE_TILE)
        spare = HEAD_DIM * (1 - parity)
        place = jnp.where((src % N_HEADS == h) & (src < 2 * N_SPLIT * N_HEADS)
                          & (dst % LANES_V7X == spare + src // N_HEADS)
                          & ((dst >= LANES_V7X) == (src >= N_SPLIT * N_HEADS)), 1.0, 0.0)
        placed = _dot(packed, place.astype(BF16))
        ones_q = jnp.where((lane >= spare + N_SPLIT) & (lane < spare + 2 * N_SPLIT), 1.0, 0.0)
        ones_k = jnp.where((lane >= spare) & (lane < spare + N_SPLIT), 1.0, 0.0)
        aug_q = (placed[:, :LANES_V7X] + ones_q).astype(BF16)
        aug_k = (placed[:, LANES_V7X:] + ones_k).astype(BF16)
        own = (lane // HEAD_DIM) == parity
        cols = slice(pair * LANES_V7X, (pair + 1) * LANES_V7X)
        qa_ref[h] = jnp.where(own, q_ref[:, cols], aug_q)
        ka_ref[h] = jnp.where(own, k_ref[:, cols], aug_k)


def _head_select(head):
    lane = lax.broadcasted_iota(jnp.int32, (1, LANES_V7X), 1)
    return (lane // HEAD_DIM) == head


def _sb_kernel(q_ref, k_ref, v_ref, o_ref):
    i = pl.program_id(2)

    def strictly_lower(n):
        return (lax.broadcasted_iota(jnp.int32, (n, n), 0)
                > lax.broadcasted_iota(jnp.int32, (n, n), 1))

    diag_bias = jnp.where(strictly_lower(SB_T), 0.0, NEG_BIG)
    after = jnp.where(strictly_lower(SB_T), 1.0, 0.0).astype(BF16)
    after_left = jnp.where(strictly_lower(SB_LEFT), 1.0, 0.0).astype(BF16)
    ones = jnp.ones((SB_T, LANES_V7X), BF16)
    ones_left = jnp.ones((SB_T, SB_LEFT), BF16)

    def softplus(z):
        return jnp.maximum(jnp.log2(1.0 + jnp.exp2(jnp.minimum(z, SOFTPLUS_LINEAR))), z)

    def keys(start, size):
        rows = pl.ds(pl.multiple_of(start, SB_T), size)
        return k_ref[rows, :], v_ref[rows, :]

    def left_start(g):
        return jnp.maximum(g * SB_T - SB_LEFT, 0)

    def band_logits(qh, g, may_clip):
        k_d, _ = keys(g * SB_T, SB_T)
        k_p, _ = keys(left_start(g), SB_LEFT)
        z_p = _dot_nt(qh, k_p)
        if may_clip:
            key = left_start(g) + lax.broadcasted_iota(jnp.int32, (SB_T, SB_LEFT), 1)
            z_p = jnp.where(key < g * SB_T, z_p, NEG_BIG)
        return _dot_nt(qh, k_d) + diag_bias, z_p

    def band_softplus(z_d, z_p):
        return z_d, z_p, softplus(z_d), softplus(z_p)

    def band_sums(z_d, z_p, sp_d, sp_p):
        sp_d16 = sp_d.astype(BF16)
        within_d = _dot(sp_d16, after)
        within_p = _dot(sp_p.astype(BF16), after_left) + _dot(sp_d16, ones_left)
        total = (within_p + sp_p)[:, 0:1]
        return (z_d - sp_d) - within_d, (z_p - sp_p) - within_p, total

    def band_weights(e_d, e_p, total):
        return jnp.exp2(e_d).astype(BF16), jnp.exp2(e_p).astype(BF16), total

    def band_out(g, a_d, a_p, total):
        _, v_d = keys(g * SB_T, SB_T)
        _, v_p = keys(left_start(g), SB_LEFT)
        return jnp.broadcast_to(-total, (SB_T, LANES_V7X)), _dot(a_d, v_d) + _dot(a_p, v_p)

    def block(qh, start, carry, acc):
        kj, vj = keys(start, SB_T)
        z = _dot_nt(qh, kj)
        sp = softplus(z)
        sp16 = sp.astype(BF16)
        within = _dot(sp16, after)
        a = jnp.exp2((z - sp) - within + carry)
        acc = acc + _dot(a.astype(BF16), vj)
        carry = carry - _dot(sp16, ones)
        return carry, acc

    heads = []
    for u in range(SB_SUB):
        g = i * SB_SUB + u
        q = q_ref[u * SB_T:(u + 1) * SB_T, :]
        for head in range(HEADS_PER_LANE_TILE):
            heads.append((g, jnp.where(_head_select(head), q, 0.0).astype(BF16)))
    logits = [band_logits(qh, g, may_clip=c // HEADS_PER_LANE_TILE < SB_LEFT // SB_T)
              for c, (g, qh) in enumerate(heads)]
    softplus_terms = [band_softplus(*z) for z in logits]
    sums = [band_sums(*t) for t in softplus_terms]
    weights = [band_weights(*s) for s in sums]
    chains = [(g, qh) + band_out(g, *w) for (g, qh), w in zip(heads, weights)]

    def is_open(carry):
        return jnp.max(carry) > -ZERO_EXP * LOG2_E

    chain_open = [is_open(carry) for _, _, carry, _ in chains]

    def walk_on():
        outs = []
        for (g, qh, carry, acc), this_open in zip(chains, chain_open):
            def cond(state):
                start, carry, _ = state
                return jnp.logical_and(start >= 0, is_open(carry))

            def body(state, qh=qh):
                start, carry, acc = state
                carry, acc = block(qh, start, carry, acc)
                return start - SB_T, carry, acc

            def walk(cond=cond, body=body, first=(left_start(g) - SB_T, carry, acc)):
                return lax.while_loop(cond, body, first)[2]

            outs.append(lax.cond(this_open, walk, lambda acc=acc: acc))
        return outs

    outs = lax.cond(functools.reduce(jnp.logical_or, chain_open), walk_on,
                    lambda: [acc for _, _, _, acc in chains])
    for u in range(SB_SUB):
        pair = outs[u * HEADS_PER_LANE_TILE:(u + 1) * HEADS_PER_LANE_TILE]
        o_ref[u * SB_T:(u + 1) * SB_T, :] = jnp.where(_head_select(0), pair[0], pair[1]).astype(o_ref.dtype)


def _sb_attn(qkv, batch, seq):
    assert SB_T == LANES_V7X and SB_LEFT % SB_T == 0
    rows = SB_SUB * SB_T
    stage_bytes = SB_SUB * HEADS_PER_LANE_TILE * 2 * 4 * SB_T * (SB_T + SB_LEFT)
    block_bytes = 2 * (2 * rows * LANES_V7X + 2 * seq * LANES_V7X) + stage_bytes // 2
    return pl.pallas_call(
        _sb_kernel,
        out_shape=jax.ShapeDtypeStruct((batch, seq, D_MODEL), BF16),
        grid=(batch, N_HEAD_PAIRS, seq // rows),
        in_specs=[
            pl.BlockSpec((None, rows, LANES_V7X), lambda b, p, i: (b, i, p)),
            pl.BlockSpec((None, seq, LANES_V7X), lambda b, p, i: (b, 0, N_HEAD_PAIRS + p)),
            pl.BlockSpec((None, seq, LANES_V7X), lambda b, p, i: (b, 0, 2 * N_HEAD_PAIRS + p)),
        ],
        out_specs=pl.BlockSpec((None, rows, LANES_V7X), lambda b, p, i: (b, i, p)),
        compiler_params=pltpu.CompilerParams(
            dimension_semantics=("arbitrary", "arbitrary", "arbitrary"),
            vmem_limit_bytes=_vmem_limit(block_bytes)),
        name="sb_attn",
    )(qkv, qkv, qkv)


def _fox_kernel(a_ref, fk0_ref, g_ref, qa_ref, ka_ref, v_ref, o_ref, acc_scr, vt_scr):
    b = pl.program_id(0)
    p_id = pl.program_id(1)
    i = pl.program_id(2)
    n_blocks = fk0_ref.shape[1]
    row = lax.broadcasted_iota(jnp.int32, (TQ, TK), 0)
    col = lax.broadcasted_iota(jnp.int32, (TQ, TK), 1)
    diag_mask = col <= row

    acc0 = jnp.zeros((TQ, LANES_V7X), F32)
    chains = range(FOX_SUB * HEADS_PER_LANE_TILE)
    bh0 = (b * N_HEAD_PAIRS + p_id) * HEADS_PER_LANE_TILE

    def scores(chain, j):
        sub, head = chain // HEADS_PER_LANE_TILE, chain % HEADS_PER_LANE_TILE
        q = qa_ref[head, pl.ds(pl.multiple_of(sub * TQ, TQ), TQ), :]
        return _dot_nt(q, ka_ref[head, pl.ds(pl.multiple_of(j * TK, TK), TK), :])

    def values(chain, j):
        vj = v_ref[pl.ds(pl.multiple_of(j * TK, TK), TK), :]
        own = _head_select(chain % HEADS_PER_LANE_TILE)
        return jnp.where(own, vj, jnp.ones_like(vj))

    def diag_block(chain):
        return i * FOX_SUB + chain // HEADS_PER_LANE_TILE

    def count_left(chain):
        bh, qi = bh0 + chain % HEADS_PER_LANE_TILE, diag_block(chain)
        bound = a_ref[bh, qi]

        def more(n):
            j = qi - 1 - n
            right = fk0_ref[bh, jnp.minimum(j + 1, n_blocks - 1)]
            return jnp.logical_and(j >= 0, bound - right > -ZERO_EXP)

        return lax.while_loop(more, lambda n: n + 1, jnp.int32(0))

    def fast_walks():
        counts = [count_left(chain) for chain in chains]
        firsts = [sum(n + 1 for n in counts[:chain]) for chain in chains]
        n_items = firsts[-1] + counts[-1] + 1
        diag_bias_t = jnp.where(row <= col, 0.0, NEG_BIG)
        feature = lax.broadcasted_iota(jnp.int32, (LANES_V7X, 1), 0)
        acc_scr[...] = jnp.zeros_like(acc_scr)

        def scores_t(chain, j):
            sub, head = chain // HEADS_PER_LANE_TILE, chain % HEADS_PER_LANE_TILE
            q = qa_ref[head, pl.ds(pl.multiple_of(sub * TQ, TQ), TQ), :]
            return _dot_nt(ka_ref[head, pl.ds(pl.multiple_of(j * TK, TK), TK), :], q)

        def values_t(chain, j):
            vt = vt_scr[:, pl.ds(pl.multiple_of(j * TK, TK), TK)]
            own = (feature // HEAD_DIM) == chain % HEADS_PER_LANE_TILE
            return jnp.where(own, vt, jnp.ones_like(vt))

        def step(t, p_t, p_chain, p_block):
            chain = sum(jnp.where(t >= first, 1, 0) for first in firsts[1:])
            first = functools.reduce(lambda f, nxt: jnp.where(t >= nxt, nxt, f), firsts[1:], 0)
            back = t - first
            block = jnp.maximum(diag_block(chain) - back, 0)
            s_t = scores_t(chain, block) + jnp.where(back == 0, diag_bias_t,
                                                     jnp.where(t >= n_items, NEG_BIG, 0.0))
            acc_scr[p_chain] += _dot(values_t(p_chain, p_block), p_t)
            return jnp.exp(s_t).astype(BF16), chain, block

        def steps(first, count, carried):
            for t in range(count):
                carried = step(first + t, *carried)
            return carried

        pairs = (n_items + 2) // 2
        start = (jnp.zeros((TK, TQ), BF16), jnp.int32(0), jnp.int32(0))
        carried = lax.fori_loop(0, pairs // 2, lambda u, c: steps(4 * u, 4, c), start)
        lax.fori_loop(0, pairs % 2, lambda _, c: steps(4 * (pairs // 2), 2, c), carried)

        outs = []
        for sub in range(FOX_SUB):
            normed = []
            for head in range(HEADS_PER_LANE_TILE):
                acc_t = acc_scr[sub * HEADS_PER_LANE_TILE + head]
                spare = HEAD_DIM * (1 - head)
                normed.append(acc_t / acc_t[spare:spare + 1, :])
            outs.append(jnp.where(feature // HEAD_DIM == 0, normed[0], normed[1]).T)
        return outs

    def online_walks():
        accs = []
        for chain in chains:
            def body(t, carried, chain=chain):
                acc, m = carried
                block = diag_block(chain) - t
                s = jnp.where(jnp.logical_or(t > 0, diag_mask), scores(chain, block), NEG_BIG)
                m_new = jnp.maximum(m, jnp.max(s, axis=-1, keepdims=True))
                p = jnp.exp(s - m_new).astype(BF16)
                return jnp.exp(m - m_new) * acc + _dot(p, values(chain, block)), m_new

            m0 = jnp.full((TQ, 1), NEG_BIG, F32)
            acc = lax.fori_loop(0, count_left(chain) + 1, body, (acc0, m0))[0]
            accs.append(acc / pltpu.roll(acc, HEAD_DIM, axis=1))
        return [jnp.where(_head_select(0), accs[sub * HEADS_PER_LANE_TILE],
                          accs[sub * HEADS_PER_LANE_TILE + 1]) for sub in range(FOX_SUB)]

    @pl.when(i == 0)
    def _():
        for c in range(0, vt_scr.shape[1], TK):
            vt_scr[:, c:c + TK] = v_ref[c:c + TK, :].T

    largest_reach = functools.reduce(jnp.maximum, [
        g_ref[bh0 + chain % HEADS_PER_LANE_TILE, diag_block(chain)] for chain in chains])
    outs = lax.cond(2.0 * largest_reach < SAFE_EXP_RANGE, fast_walks, online_walks)
    for sub in range(FOX_SUB):
        o_ref[sub * TQ:(sub + 1) * TQ, :] = outs[sub].astype(o_ref.dtype)


def _fox_attn(q_aug, k_aug, v, walk_bound, fk_block_start, max_reach, batch, seq):
    assert TQ == TK
    rows = FOX_SUB * TQ
    n_chains = FOX_SUB * HEADS_PER_LANE_TILE
    block_bytes = (2 * (HEADS_PER_LANE_TILE * (rows + seq) * LANES_V7X + seq * LANES_V7X + rows * LANES_V7X)
                   + 4 * n_chains * TQ * LANES_V7X + 2 * seq * LANES_V7X)
    smem = pl.BlockSpec(memory_space=pltpu.SMEM)
    return pl.pallas_call(
        _fox_kernel,
        out_shape=jax.ShapeDtypeStruct((batch, seq, D_MODEL), BF16),
        grid=(batch, N_HEAD_PAIRS, seq // rows),
        in_specs=[
            smem, smem, smem,
            pl.BlockSpec((None, HEADS_PER_LANE_TILE, rows, LANES_V7X), lambda b, p, i: (b, p, i, 0)),
            pl.BlockSpec((None, HEADS_PER_LANE_TILE, seq, LANES_V7X), lambda b, p, i: (b, p, 0, 0)),
            pl.BlockSpec((None, seq, LANES_V7X), lambda b, p, i: (b, 0, p)),
        ],
        out_specs=pl.BlockSpec((None, rows, LANES_V7X), lambda b, p, i: (b, i, p)),
        scratch_shapes=[pltpu.VMEM((n_chains, LANES_V7X, TQ), F32),
                        pltpu.VMEM((LANES_V7X, seq), BF16)],
        compiler_params=pltpu.CompilerParams(
            dimension_semantics=("arbitrary", "arbitrary", "arbitrary"),
            vmem_limit_bytes=_vmem_limit(block_bytes)),
        name="fox_attn",
    )(walk_bound, fk_block_start, max_reach, q_aug, k_aug, v)


def kernel(x, c, ada_w, ada_b, norm_attn_g, norm_ffn_g, w_ffn_in, w_ffn_down, sb_w_qkv, sb_w_o,
           kv_ada_w, kv_ada_b, kv_norm_g, w_kvf, b_f, k_norm_g, fox_w_q, q_norm_g, fox_w_o):
    batch, seq, d = x.shape
    assert d == D_MODEL and seq % TM == 0
    assert seq % (FOX_SUB * TQ) == 0 and seq % (SB_SUB * SB_T) == 0
    assert ada_w.shape[0] == 2 and sb_w_qkv.shape[0] == 1 and fox_w_q.shape[0] == 1
    m = batch * seq

    mod = _ada_mod(c, ada_w, ada_b)[:, :batch]
    kv_mod = _ada_mod(c, kv_ada_w[None], kv_ada_b[None])[0, :batch]

    def vecs(t, n):
        return [t[:, None, j * d:(j + 1) * d] for j in range(n)]

    row = lambda t: t.reshape(1, -1).astype(F32)
    bf = lambda t: t.astype(BF16)

    x2d = x.reshape(m, d)

    sh_a, sc_a, g_a, sh_f, sc_f, g_f = vecs(mod[0], 6)
    qkv_scale = jnp.concatenate([jnp.full((1, d), LOG2_E * HEAD_DIM ** -0.5, F32),
                                 jnp.ones((1, 2 * d), F32)], axis=1)
    qkv = _norm_matmul(x2d, row(norm_attn_g[0]), sh_a, sc_a, bf(sb_w_qkv[0]), qkv_scale, seq)
    o = _sb_attn(qkv.reshape(batch, seq, 3 * d), batch, seq)
    x2d, act = _out_ffn_in(o.reshape(m, d), bf(sb_w_o[0]), x2d, g_a,
                           row(norm_ffn_g[0]), sh_f, sc_f, bf(w_ffn_in[0]), seq)
    g_f0 = g_f

    sh_a, sc_a, g_a, sh_f, sc_f, g_f = vecs(mod[1], 6)
    kv_shift, kv_scale = vecs(kv_mod, 2)
    w_f_pad = jnp.zeros((d, LANES_V7X), F32).at[:, :N_HEADS].set(w_kvf[:, 2 * d:])
    b_f_pad = jnp.zeros((1, LANES_V7X), F32).at[0, :N_HEADS].set(b_f)
    key_norm_bound = (1.01 * HEAD_DIM ** 0.5 * jnp.max(jnp.abs(k_norm_g))).reshape(1, 1).astype(F32)
    x2d, v, q_aug, k_aug, stats = _l1_proj(
        act, bf(w_ffn_down[0]), x2d, g_f0,
        row(kv_norm_g), kv_shift, kv_scale, row(norm_attn_g[1]), sh_a, sc_a,
        bf(w_kvf[:, :d]), bf(w_kvf[:, d:2 * d]), bf(w_f_pad), bf(fox_w_q[0]),
        row(jnp.tile(k_norm_g, N_HEADS)), row(jnp.tile(q_norm_g[0], N_HEADS)), b_f_pad,
        key_norm_bound, batch, seq)
    tables = stats[:, :, :3, :N_HEADS].transpose(2, 0, 3, 1).reshape(3, batch * N_HEADS, seq // TK)
    o = _fox_attn(q_aug, k_aug, v.reshape(batch, seq, d), tables[0], tables[1], tables[2], batch, seq)
    x2d, act = _out_ffn_in(o.reshape(m, d), bf(fox_w_o[0]), x2d, g_a,
                           row(norm_ffn_g[1]), sh_f, sc_f, bf(w_ffn_in[1]), seq)
    x2d = _proj_residual(act, bf(w_ffn_down[1]), x2d, g_f, seq)
    return x2d.reshape(batch, seq, d)
```

```python
import functools

import jax
import jax.numpy as jnp
from jax import lax
from jax.experimental import pallas as pl
from jax.experimental.pallas import tpu as pltpu

D_MODEL = 1024
N_HEADS = 16
HEAD_DIM = D_MODEL // N_HEADS
EPS = 1e-6

LANES_V7X = 128
SUBLANES_V7X = 8
VMEM_BYTES_V7X = 64 * 1024 * 1024

HEADS_PER_LANE_TILE = LANES_V7X // HEAD_DIM
N_HEAD_PAIRS = N_HEADS // HEADS_PER_LANE_TILE

MXU_WIDTH_V7X = 256

TM = 512
TM_WIDE = 1024
COL_CHUNK = 2 * MXU_WIDTH_V7X
TQ = 512
TK = 512
FOX_SUB = 4
SB_T = 128
SB_LEFT = 256
SB_SUB = 16

ZERO_EXP = 104.5
SAFE_EXP_RANGE = 80.0
NEG_BIG = -1e30
LOG2_E = 1.4426950408889634
SOFTPLUS_LINEAR = 64.0

F32 = jnp.float32
BF16 = jnp.bfloat16


def _vmem_limit(block_bytes):
    want = 2 * block_bytes + 16 * 1024 * 1024
    return int(min(want, VMEM_BYTES_V7X - 8 * 1024 * 1024))


def _dot(a, b):
    return jnp.dot(a, b, preferred_element_type=F32)


def _dot_nt(a, b):
    return lax.dot_general(a, b, (((1,), (1,)), ((), ())), preferred_element_type=F32)


def _ada_kernel(ct_ref, w_ref, b_ref, o_ref, *, batch):
    c = ct_ref[...]
    c_act = c * jax.nn.sigmoid(c)
    w = w_ref[...]
    o_ref[...] = jnp.zeros_like(o_ref)
    for b in range(batch):
        o_ref[b:b + 1, :] = jnp.sum(w * c_act[:, b:b + 1], axis=0, keepdims=True) + b_ref[...]


def _ada_mod(c, w, b):
    n_layers, d, n = w.shape
    batch = c.shape[0]
    rows = SUBLANES_V7X
    assert batch <= rows
    tn = 1024
    c_t = jnp.zeros((d, rows), F32).at[:, :batch].set(c.astype(F32).T)
    block_bytes = 4 * (d * tn + d * LANES_V7X + tn + rows * tn)
    return pl.pallas_call(
        functools.partial(_ada_kernel, batch=batch),
        out_shape=jax.ShapeDtypeStruct((n_layers, rows, n), F32),
        grid=(n_layers, n // tn),
        in_specs=[
            pl.BlockSpec((d, rows), lambda l, j: (0, 0)),
            pl.BlockSpec((None, d, tn), lambda l, j: (l, 0, j)),
            pl.BlockSpec((None, 1, tn), lambda l, j: (l, 0, j)),
        ],
        out_specs=pl.BlockSpec((None, rows, tn), lambda l, j: (l, 0, j)),
        compiler_params=pltpu.CompilerParams(
            dimension_semantics=("arbitrary", "arbitrary"),
            vmem_limit_bytes=_vmem_limit(block_bytes)),
        name="ada_mod",
    )(c_t, w, b.reshape(n_layers, 1, n))


def _norm_modulate(x, g, shift, scale):
    ms = jnp.mean(x * x, axis=-1, keepdims=True)
    y = x * lax.rsqrt(ms + EPS)
    return (y * g) * (1.0 + scale) + shift


def _row_spec(d):
    return pl.BlockSpec((1, d), lambda *_: (0, 0))


def _resident_spec(shape, block_index=(0, 0)):
    return pl.BlockSpec(shape, lambda *_: block_index)


def _batch_vec_spec(d, tiles_per_batch):
    return pl.BlockSpec((None, 1, d), lambda i, *_: (i // tiles_per_batch, 0, 0))


def _column_chunks(n):
    return [(c, min(c + COL_CHUNK, n)) for c in range(0, n, COL_CHUNK)]


def _norm_matmul_kernel(x_ref, g_ref, sh_ref, sc_ref, w_ref, cs_ref, o_ref):
    h = _norm_modulate(x_ref[...], g_ref[...], sh_ref[...], sc_ref[...]).astype(BF16)
    for c0, c1 in _column_chunks(o_ref.shape[1]):
        o_ref[:, c0:c1] = (_dot(h, w_ref[:, c0:c1]) * cs_ref[:, c0:c1]).astype(o_ref.dtype)


def _norm_matmul(x2d, g, shift, scale, w, col_scale, seq):
    m, d = x2d.shape
    n = w.shape[1]
    tm = TM_WIDE
    block_bytes = 4 * tm * d + 2 * d * n + 2 * tm * n + 4 * n
    return pl.pallas_call(
        _norm_matmul_kernel,
        out_shape=jax.ShapeDtypeStruct((m, n), BF16),
        grid=(m // tm,),
        in_specs=[
            pl.BlockSpec((tm, d), lambda i: (i, 0)),
            _row_spec(d),
            _batch_vec_spec(d, seq // tm),
            _batch_vec_spec(d, seq // tm),
            _resident_spec((d, n)),
            pl.BlockSpec((1, n), lambda i: (0, 0)),
        ],
        out_specs=pl.BlockSpec((tm, n), lambda i: (i, 0)),
        compiler_params=pltpu.CompilerParams(
            dimension_semantics=("arbitrary",),
            vmem_limit_bytes=_vmem_limit(block_bytes)),
        name="norm_matmul",
    )(x2d, g, shift, scale, w, col_scale)


def _out_ffn_in_kernel(a_ref, wo_ref, x_ref, gate_ref, g_ref, sh_ref, sc_ref, wg_ref, wu_ref,
                       x1_ref, act_ref):
    x1 = x_ref[...] + gate_ref[...] * _dot(a_ref[...], wo_ref[...])
    x1_ref[...] = x1
    h = _norm_modulate(x1, g_ref[...], sh_ref[...], sc_ref[...]).astype(BF16)
    for c0, c1 in _column_chunks(act_ref.shape[1]):
        gate = _dot(h, wg_ref[:, c0:c1])
        up = _dot(h, wu_ref[:, c0:c1])
        act_ref[:, c0:c1] = (gate * jax.nn.sigmoid(gate) * up).astype(act_ref.dtype)


def _out_ffn_in(a, w_o, x2d, gate, g, shift, scale, w_in, seq):
    m, d = x2d.shape
    f = w_in.shape[1] // 2
    tpb = seq // TM
    block_bytes = 2 * TM * d + 2 * d * d + 2 * 4 * TM * d + 2 * 2 * d * f + 2 * TM * f
    return pl.pallas_call(
        _out_ffn_in_kernel,
        out_shape=(jax.ShapeDtypeStruct((m, d), F32), jax.ShapeDtypeStruct((m, f), BF16)),
        grid=(m // TM,),
        in_specs=[
            pl.BlockSpec((TM, d), lambda i: (i, 0)),
            _resident_spec((d, d)),
            pl.BlockSpec((TM, d), lambda i: (i, 0)),
            _batch_vec_spec(d, tpb),
            _row_spec(d),
            _batch_vec_spec(d, tpb),
            _batch_vec_spec(d, tpb),
            _resident_spec((d, f)),
            _resident_spec((d, f), (0, 1)),
        ],
        out_specs=(pl.BlockSpec((TM, d), lambda i: (i, 0)), pl.BlockSpec((TM, f), lambda i: (i, 0))),
        compiler_params=pltpu.CompilerParams(
            dimension_semantics=("arbitrary",),
            vmem_limit_bytes=_vmem_limit(block_bytes)),
        name="out_ffn_in",
    )(a, w_o, x2d, gate, g, shift, scale, w_in, w_in)


def _proj_residual_kernel(a_ref, w_ref, x_ref, gate_ref, o_ref):
    o_ref[...] = x_ref[...] + gate_ref[...] * _dot(a_ref[...], w_ref[...])


def _proj_residual(a, w, x2d, gate, seq):
    m, k = a.shape
    d = w.shape[1]
    tm = TM_WIDE
    block_bytes = 2 * tm * k + 2 * k * d + 2 * 4 * tm * d
    return pl.pallas_call(
        _proj_residual_kernel,
        out_shape=jax.ShapeDtypeStruct((m, d), F32),
        grid=(m // tm,),
        in_specs=[
            pl.BlockSpec((tm, k), lambda i: (i, 0)),
            _resident_spec((k, d)),
            pl.BlockSpec((tm, d), lambda i: (i, 0)),
            _batch_vec_spec(d, seq // tm),
        ],
        out_specs=pl.BlockSpec((tm, d), lambda i: (i, 0)),
        compiler_params=pltpu.CompilerParams(
            dimension_semantics=("arbitrary",),
            vmem_limit_bytes=_vmem_limit(block_bytes)),
        name="proj_residual",
    )(a, w, x2d, gate)


def _head_rms_norm(t, gain, group_mean):
    ms = _dot((t * t).astype(BF16), group_mean)
    return t * lax.rsqrt(ms + EPS) * gain


def _l1_proj_kernel(kb_ref, act_ref, wd_ref, x_ref, gate_ref,
                    gkv_ref, shkv_ref, sckv_ref, gq_ref, shq_ref, scq_ref,
                    wk_ref, wv_ref, wf_ref, wq_ref, kg_ref, qg_ref, bf_ref, gm_ref,
                    x2_ref, v_ref, qa_ref, ka_ref, stats_ref,
                    q_ref, k_ref, carry_scr, *, tiles_per_sequence):
    x = x_ref[...] + gate_ref[...] * _dot(act_ref[...], wd_ref[...])
    x2_ref[...] = x
    ms = jnp.mean(x * x, axis=-1, keepdims=True)
    y = x * lax.rsqrt(ms + EPS)
    h_kv = ((y * gkv_ref[...]) * (1.0 + sckv_ref[...]) + shkv_ref[...]).astype(BF16)
    h_q = ((y * gq_ref[...]) * (1.0 + scq_ref[...]) + shq_ref[...]).astype(BF16)
    gm = gm_ref[...]
    k_all = _dot(h_kv, wk_ref[...])
    q_all = _dot(h_q, wq_ref[...])
    v_ref[...] = _dot(h_kv, wv_ref[...]).astype(BF16)
    log_f = jax.nn.log_sigmoid(_dot(h_kv, wf_ref[...]) + bf_ref[...])
    for c0 in range(0, k_ref.shape[1], MXU_WIDTH_V7X):
        cols = slice(c0, c0 + MXU_WIDTH_V7X)
        k_ref[:, cols] = _head_rms_norm(k_all[:, cols], kg_ref[:, cols], gm).astype(BF16)
        q = _head_rms_norm(q_all[:, cols], qg_ref[:, cols], gm)
        q_ref[:, cols] = (q * (HEAD_DIM ** -0.5)).astype(BF16)
    _fox_augment(pl.program_id(0) % tiles_per_sequence == 0, kb_ref[0, 0], log_f,
                 q_ref, k_ref, qa_ref, ka_ref, stats_ref, carry_scr)


def _l1_proj(act, w_down, x2d, gate, g_kv, sh_kv, sc_kv, g_q, sh_q, sc_q, wk, wv, wf, wq,
             k_gain, q_gain, b_f, key_norm_bound, batch, seq):
    assert TM == TK == TQ
    m, d = x2d.shape
    f = act.shape[1]
    lane = jnp.arange(MXU_WIDTH_V7X)
    group_mean = jnp.where(lane[:, None] // HEAD_DIM == lane[None, :] // HEAD_DIM,
                           1.0 / HEAD_DIM, 0.0).astype(BF16)
    tpb = seq // TM
    block_bytes = (2 * TM * f + 2 * f * d + 2 * 4 * TM * d + 3 * 2 * d * d + 2 * d * LANES_V7X
                   + 2 * TM * d + 2 * 2 * N_HEADS * TM * LANES_V7X + 2 * 2 * TM * d)
    full = lambda r, c: _resident_spec((r, c))
    rows = lambda c: pl.BlockSpec((TM, c), lambda i: (i, 0))
    aug_shape = jax.ShapeDtypeStruct((batch, N_HEADS, seq, LANES_V7X), BF16)
    aug_spec = pl.BlockSpec((None, N_HEADS, TM, LANES_V7X), lambda i: (i // tpb, 0, i % tpb, 0))
    return pl.pallas_call(
        functools.partial(_l1_proj_kernel, tiles_per_sequence=tpb),
        out_shape=(jax.ShapeDtypeStruct((m, d), F32), jax.ShapeDtypeStruct((m, d), BF16),
                   aug_shape, aug_shape,
                   jax.ShapeDtypeStruct((batch, tpb, SUBLANES_V7X, LANES_V7X), F32)),
        grid=(m // TM,),
        in_specs=[
            pl.BlockSpec(memory_space=pltpu.SMEM),
            rows(f), full(f, d), rows(d), _batch_vec_spec(d, tpb),
            _row_spec(d), _batch_vec_spec(d, tpb), _batch_vec_spec(d, tpb),
            _row_spec(d), _batch_vec_spec(d, tpb), _batch_vec_spec(d, tpb),
            full(d, d), full(d, d), full(d, LANES_V7X), full(d, d),
            _row_spec(d), _row_spec(d), _row_spec(LANES_V7X), full(MXU_WIDTH_V7X, MXU_WIDTH_V7X),
        ],
        out_specs=(rows(d), rows(d), aug_spec, aug_spec,
                   pl.BlockSpec((None, None, SUBLANES_V7X, LANES_V7X),
                                lambda i: (i // tpb, i % tpb, 0, 0))),
        scratch_shapes=[pltpu.VMEM((TM, d), BF16), pltpu.VMEM((TM, d), BF16),
                        pltpu.VMEM((SUBLANES_V7X, LANES_V7X), F32)],
        compiler_params=pltpu.CompilerParams(
            dimension_semantics=("arbitrary",),
            vmem_limit_bytes=_vmem_limit(block_bytes)),
        name="l1_proj",
    )(key_norm_bound, act, w_down, x2d, gate, g_kv, sh_kv, sc_kv, g_q, sh_q, sc_q, wk, wv, wf, wq,
      k_gain, q_gain, b_f, group_mean)


N_SPLIT = 3


def _split3(x):
    x0 = x.astype(BF16)
    r1 = x - x0.astype(F32)
    x1 = r1.astype(BF16)
    x2 = (r1 - x1.astype(F32)).astype(BF16)
    return x0, x1, x2


def _fox_augment(first_tile, kb, log_f, q_ref, k_ref, qa_ref, ka_ref, stats_ref, carry_scr):
    @pl.when(first_tile)
    def _():
        carry_scr[...] = jnp.zeros_like(carry_scr)

    r = lax.broadcasted_iota(jnp.int32, (TK, TK), 0)
    c = lax.broadcasted_iota(jnp.int32, (TK, TK), 1)
    tri = jnp.where(c <= r, 1.0, 0.0).astype(BF16)
    x0, x1, x2 = _split3(log_f)
    cs = _dot(tri, x0) + _dot(tri, x1) + _dot(tri, x2) + carry_scr[0:1, :]
    carry_scr[...] = jnp.broadcast_to(cs[TK - 1:TK, :], carry_scr.shape)

    d_idx = lax.broadcasted_iota(jnp.int32, (D_MODEL, LANES_V7X), 0)
    h_idx = lax.broadcasted_iota(jnp.int32, (D_MODEL, LANES_V7X), 1)
    head_of = jnp.where(d_idx // HEAD_DIM == h_idx, 1.0, 0.0).astype(BF16)
    qf = q_ref[...].astype(F32)
    reach = jnp.sqrt(_dot((qf * qf).astype(BF16), head_of)) * kb
    own_logit = _dot((qf * k_ref[...].astype(F32)).astype(BF16), head_of)

    stats_ref[...] = jnp.concatenate([
        jnp.max(reach * (1.0 + 2.0 ** -8) - own_logit + cs, axis=0, keepdims=True),
        cs[0:1, :],
        jnp.max(reach, axis=0, keepdims=True),
        jnp.zeros((SUBLANES_V7X - 3, LANES_V7X), F32)], axis=0)

    lane = lax.broadcasted_iota(jnp.int32, (1, LANES_V7X), 1)
    packed = jnp.zeros((TK, LANES_V7X), F32)
    for i, term in enumerate(_split3(cs - reach) + _split3(-cs)):
        term = jnp.where(lane < N_HEADS, term.astype(F32), 0.0)
        packed = packed + (term if i == 0 else pltpu.roll(term, N_HEADS * i, axis=1))
    packed = packed.astype(BF16)

    src = lax.broadcasted_iota(jnp.int32, (LANES_V7X, 2 * LANES_V7X), 0)
    dst = lax.broadcasted_iota(jnp.int32, (LANES_V7X, 2 * LANES_V7X), 1)

    for h in range(N_HEADS):
        pair, parity = divmod(h, HEADS_PER_LANE_TILE)
        spare = HEAD_DIM * (1 - parity)
        place = jnp.where((src % N_HEADS == h) & (src < 2 * N_SPLIT * N_HEADS)
                          & (dst % LANES_V7X == spare + src // N_HEADS)
                          & ((dst >= LANES_V7X) == (src >= N_SPLIT * N_HEADS)), 1.0, 0.0)
        placed = _dot(packed, place.astype(BF16))
        ones_q = jnp.where((lane >= spare + N_SPLIT) & (lane < spare + 2 * N_SPLIT), 1.0, 0.0)
        ones_k = jnp.where((lane >= spare) & (lane < spare + N_SPLIT), 1.0, 0.0)
        aug_q = (placed[:, :LANES_V7X] + ones_q).astype(BF16)
        aug_k = (placed[:, LANES_V7X:] + ones_k).astype(BF16)
        own = (lane // HEAD_DIM) == parity
        cols = slice(pair * LANES_V7X, (pair + 1) * LANES_V7X)
        qa_ref[h] = jnp.where(own, q_ref[:, cols], aug_q)
        ka_ref[h] = jnp.where(own, k_ref[:, cols], aug_k)


def _head_select(head):
    lane = lax.broadcasted_iota(jnp.int32, (1, LANES_V7X), 1)
    return (lane // HEAD_DIM) == head


def _sb_kernel(q_ref, k_ref, v_ref, o_ref):
    i = pl.program_id(2)

    def strictly_lower(n):
        return (lax.broadcasted_iota(jnp.int32, (n, n), 0)
                > lax.broadcasted_iota(jnp.int32, (n, n), 1))

    diag_bias = jnp.where(strictly_lower(SB_T), 0.0, NEG_BIG)
    after = jnp.where(strictly_lower(SB_T), 1.0, 0.0).astype(BF16)
    after_left = jnp.where(strictly_lower(SB_LEFT), 1.0, 0.0).astype(BF16)
    ones = jnp.ones((SB_T, LANES_V7X), BF16)
    ones_left = jnp.ones((SB_T, SB_LEFT), BF16)

    def softplus(z):
        return jnp.maximum(jnp.log2(1.0 + jnp.exp2(jnp.minimum(z, SOFTPLUS_LINEAR))), z)

    def keys(start, size):
        rows = pl.ds(pl.multiple_of(start, SB_T), size)
        return k_ref[rows, :], v_ref[rows, :]

    def left_start(g):
        return jnp.maximum(g * SB_T - SB_LEFT, 0)

    def band_logits(qh, g, may_clip):
        k_d, _ = keys(g * SB_T, SB_T)
        k_p, _ = keys(left_start(g), SB_LEFT)
        z_p = _dot_nt(qh, k_p)
        if may_clip:
            key = left_start(g) + lax.broadcasted_iota(jnp.int32, (SB_T, SB_LEFT), 1)
            z_p = jnp.where(key < g * SB_T, z_p, NEG_BIG)
        return _dot_nt(qh, k_d) + diag_bias, z_p

    def band_softplus(z_d, z_p):
        return z_d, z_p, softplus(z_d), softplus(z_p)

    def band_sums(z_d, z_p, sp_d, sp_p):
        sp_d16 = sp_d.astype(BF16)
        within_d = _dot(sp_d16, after)
        within_p = _dot(sp_p.astype(BF16), after_left) + _dot(sp_d16, ones_left)
        total = (within_p + sp_p)[:, 0:1]
        return (z_d - sp_d) - within_d, (z_p - sp_p) - within_p, total

    def band_weights(e_d, e_p, total):
        return jnp.exp2(e_d).astype(BF16), jnp.exp2(e_p).astype(BF16), total

    def band_out(g, a_d, a_p, total):
        _, v_d = keys(g * SB_T, SB_T)
        _, v_p = keys(left_start(g), SB_LEFT)
        return jnp.broadcast_to(-total, (SB_T, LANES_V7X)), _dot(a_d, v_d) + _dot(a_p, v_p)

    def block(qh, start, carry, acc):
        kj, vj = keys(start, SB_T)
        z = _dot_nt(qh, kj)
        sp = softplus(z)
        sp16 = sp.astype(BF16)
        within = _dot(sp16, after)
        a = jnp.exp2((z - sp) - within + carry)
        acc = acc + _dot(a.astype(BF16), vj)
        carry = carry - _dot(sp16, ones)
        return carry, acc

    heads = []
    for u in range(SB_SUB):
        g = i * SB_SUB + u
        q = q_ref[u * SB_T:(u + 1) * SB_T, :]
        for head in range(HEADS_PER_LANE_TILE):
            heads.append((g, jnp.where(_head_select(head), q, 0.0).astype(BF16)))
    logits = [band_logits(qh, g, may_clip=c // HEADS_PER_LANE_TILE < SB_LEFT // SB_T)
              for c, (g, qh) in enumerate(heads)]
    softplus_terms = [band_softplus(*z) for z in logits]
    sums = [band_sums(*t) for t in softplus_terms]
    weights = [band_weights(*s) for s in sums]
    chains = [(g, qh) + band_out(g, *w) for (g, qh), w in zip(heads, weights)]

    def is_open(carry):
        return jnp.max(carry) > -ZERO_EXP * LOG2_E

    chain_open = [is_open(carry) for _, _, carry, _ in chains]

    def walk_on():
        outs = []
        for (g, qh, carry, acc), this_open in zip(chains, chain_open):
            def cond(state):
                start, carry, _ = state
                return jnp.logical_and(start >= 0, is_open(carry))

            def body(state, qh=qh):
                start, carry, acc = state
                carry, acc = block(qh, start, carry, acc)
                return start - SB_T, carry, acc

            def walk(cond=cond, body=body, first=(left_start(g) - SB_T, carry, acc)):
                return lax.while_loop(cond, body, first)[2]

            outs.append(lax.cond(this_open, walk, lambda acc=acc: acc))
        return outs

    outs = lax.cond(functools.reduce(jnp.logical_or, chain_open), walk_on,
                    lambda: [acc for _, _, _, acc in chains])
    for u in range(SB_SUB):
        pair = outs[u * HEADS_PER_LANE_TILE:(u + 1) * HEADS_PER_LANE_TILE]
        o_ref[u * SB_T:(u + 1) * SB_T, :] = jnp.where(_head_select(0), pair[0], pair[1]).astype(o_ref.dtype)


def _sb_attn(qkv, batch, seq):
    assert SB_T == LANES_V7X and SB_LEFT % SB_T == 0
    rows = SB_SUB * SB_T
    stage_bytes = SB_SUB * HEADS_PER_LANE_TILE * 2 * 4 * SB_T * (SB_T + SB_LEFT)
    block_bytes = 2 * (2 * rows * LANES_V7X + 2 * seq * LANES_V7X) + stage_bytes // 2
    return pl.pallas_call(
        _sb_kernel,
        out_shape=jax.ShapeDtypeStruct((batch, seq, D_MODEL), BF16),
        grid=(batch, N_HEAD_PAIRS, seq // rows),
        in_specs=[
            pl.BlockSpec((None, rows, LANES_V7X), lambda b, p, i: (b, i, p)),
            pl.BlockSpec((None, seq, LANES_V7X), lambda b, p, i: (b, 0, N_HEAD_PAIRS + p)),
            pl.BlockSpec((None, seq, LANES_V7X), lambda b, p, i: (b, 0, 2 * N_HEAD_PAIRS + p)),
        ],
        out_specs=pl.BlockSpec((None, rows, LANES_V7X), lambda b, p, i: (b, i, p)),
        compiler_params=pltpu.CompilerParams(
            dimension_semantics=("arbitrary", "arbitrary", "arbitrary"),
            vmem_limit_bytes=_vmem_limit(block_bytes)),
        name="sb_attn",
    )(qkv, qkv, qkv)


def _fox_kernel(a_ref, fk0_ref, g_ref, qa_ref, ka_ref, v_ref, o_ref, acc_scr, vt_scr):
    b = pl.program_id(0)
    p_id = pl.program_id(1)
    i = pl.program_id(2)
    n_blocks = fk0_ref.shape[1]
    row = lax.broadcasted_iota(jnp.int32, (TQ, TK), 0)
    col = lax.broadcasted_iota(jnp.int32, (TQ, TK), 1)
    diag_mask = col <= row

    acc0 = jnp.zeros((TQ, LANES_V7X), F32)
    chains = range(FOX_SUB * HEADS_PER_LANE_TILE)
    bh0 = (b * N_HEAD_PAIRS + p_id) * HEADS_PER_LANE_TILE

    def scores(chain, j):
        sub, head = chain // HEADS_PER_LANE_TILE, chain % HEADS_PER_LANE_TILE
        q = qa_ref[head, pl.ds(pl.multiple_of(sub * TQ, TQ), TQ), :]
        return _dot_nt(q, ka_ref[head, pl.ds(pl.multiple_of(j * TK, TK), TK), :])

    def values(chain, j):
        vj = v_ref[pl.ds(pl.multiple_of(j * TK, TK), TK), :]
        own = _head_select(chain % HEADS_PER_LANE_TILE)
        return jnp.where(own, vj, jnp.ones_like(vj))

    def diag_block(chain):
        return i * FOX_SUB + chain // HEADS_PER_LANE_TILE

    def count_left(chain):
        bh, qi = bh0 + chain % HEADS_PER_LANE_TILE, diag_block(chain)
        bound = a_ref[bh, qi]

        def more(n):
            j = qi - 1 - n
            right = fk0_ref[bh, jnp.minimum(j + 1, n_blocks - 1)]
            return jnp.logical_and(j >= 0, bound - right > -ZERO_EXP)

        return lax.while_loop(more, lambda n: n + 1, jnp.int32(0))

    def fast_walks():
        counts = [count_left(chain) for chain in chains]
        firsts = [sum(n + 1 for n in counts[:chain]) for chain in chains]
        n_items = firsts[-1] + counts[-1] + 1
        diag_bias_t = jnp.where(row <= col, 0.0, NEG_BIG)
        feature = lax.broadcasted_iota(jnp.int32, (LANES_V7X, 1), 0)
        acc_scr[...] = jnp.zeros_like(acc_scr)

        def scores_t(chain, j):
            sub, head = chain // HEADS_PER_LANE_TILE, chain % HEADS_PER_LANE_TILE
            q = qa_ref[head, pl.ds(pl.multiple_of(sub * TQ, TQ), TQ), :]
            return _dot_nt(ka_ref[head, pl.ds(pl.multiple_of(j * TK, TK), TK), :], q)

        def values_t(chain, j):
            vt = vt_scr[:, pl.ds(pl.multiple_of(j * TK, TK), TK)]
            own = (feature // HEAD_DIM) == chain % HEADS_PER_LANE_TILE
            return jnp.where(own, vt, jnp.ones_like(vt))

        def step(t, p_t, p_chain, p_block):
            chain = sum(jnp.where(t >= first, 1, 0) for first in firsts[1:])
            first = functools.reduce(lambda f, nxt: jnp.where(t >= nxt, nxt, f), firsts[1:], 0)
            back = t - first
            block = jnp.maximum(diag_block(chain) - back, 0)
            s_t = scores_t(chain, block) + jnp.where(back == 0, diag_bias_t,
                                                     jnp.where(t >= n_items, NEG_BIG, 0.0))
            acc_scr[p_chain] += _dot(values_t(p_chain, p_block), p_t)
            return jnp.exp(s_t).astype(BF16), chain, block

        def steps(first, count, carried):
            for t in range(count):
                carried = step(first + t, *carried)
            return carried

        pairs = (n_items + 2) // 2
        start = (jnp.zeros((TK, TQ), BF16), jnp.int32(0), jnp.int32(0))
        carried = lax.fori_loop(0, pairs // 2, lambda u, c: steps(4 * u, 4, c), start)
        lax.fori_loop(0, pairs % 2, lambda _, c: steps(4 * (pairs // 2), 2, c), carried)

        outs = []
        for sub in range(FOX_SUB):
            normed = []
            for head in range(HEADS_PER_LANE_TILE):
                acc_t = acc_scr[sub * HEADS_PER_LANE_TILE + head]
                spare = HEAD_DIM * (1 - head)
                normed.append(acc_t / acc_t[spare:spare + 1, :])
            outs.append(jnp.where(feature // HEAD_DIM == 0, normed[0], normed[1]).T)
        return outs

    def online_walks():
        accs = []
        for chain in chains:
            def body(t, carried, chain=chain):
                acc, m = carried
                block = diag_block(chain) - t
                s = jnp.where(jnp.logical_or(t > 0, diag_mask), scores(chain, block), NEG_BIG)
                m_new = jnp.maximum(m, jnp.max(s, axis=-1, keepdims=True))
                p = jnp.exp(s - m_new).astype(BF16)
                return jnp.exp(m - m_new) * acc + _dot(p, values(chain, block)), m_new

            m0 = jnp.full((TQ, 1), NEG_BIG, F32)
            acc = lax.fori_loop(0, count_left(chain) + 1, body, (acc0, m0))[0]
            accs.append(acc / pltpu.roll(acc, HEAD_DIM, axis=1))
        return [jnp.where(_head_select(0), accs[sub * HEADS_PER_LANE_TILE],
                          accs[sub * HEADS_PER_LANE_TILE + 1]) for sub in range(FOX_SUB)]

    @pl.when(i == 0)
    def _():
        for c in range(0, vt_scr.shape[1], TK):
            vt_scr[:, c:c + TK] = v_ref[c:c + TK, :].T

    largest_reach = functools.reduce(jnp.maximum, [
        g_ref[bh0 + chain % HEADS_PER_LANE_TILE, diag_block(chain)] for chain in chains])
    outs = lax.cond(2.0 * largest_reach < SAFE_EXP_RANGE, fast_walks, online_walks)
    for sub in range(FOX_SUB):
        o_ref[sub * TQ:(sub + 1) * TQ, :] = outs[sub].astype(o_ref.dtype)


def _fox_attn(q_aug, k_aug, v, walk_bound, fk_block_start, max_reach, batch, seq):
    assert TQ == TK
    rows = FOX_SUB * TQ
    n_chains = FOX_SUB * HEADS_PER_LANE_TILE
    block_bytes = (2 * (HEADS_PER_LANE_TILE * (rows + seq) * LANES_V7X + seq * LANES_V7X + rows * LANES_V7X)
                   + 4 * n_chains * TQ * LANES_V7X + 2 * seq * LANES_V7X)
    smem = pl.BlockSpec(memory_space=pltpu.SMEM)
    return pl.pallas_call(
        _fox_kernel,
        out_shape=jax.ShapeDtypeStruct((batch, seq, D_MODEL), BF16),
        grid=(batch, N_HEAD_PAIRS, seq // rows),
        in_specs=[
            smem, smem, smem,
            pl.BlockSpec((None, HEADS_PER_LANE_TILE, rows, LANES_V7X), lambda b, p, i: (b, p, i, 0)),
            pl.BlockSpec((None, HEADS_PER_LANE_TILE, seq, LANES_V7X), lambda b, p, i: (b, p, 0, 0)),
            pl.BlockSpec((None, seq, LANES_V7X), lambda b, p, i: (b, 0, p)),
        ],
        out_specs=pl.BlockSpec((None, rows, LANES_V7X), lambda b, p, i: (b, i, p)),
        scratch_shapes=[pltpu.VMEM((n_chains, LANES_V7X, TQ), F32),
                        pltpu.VMEM((LANES_V7X, seq), BF16)],
        compiler_params=pltpu.CompilerParams(
            dimension_semantics=("arbitrary", "arbitrary", "arbitrary"),
            vmem_limit_bytes=_vmem_limit(block_bytes)),
        name="fox_attn",
    )(walk_bound, fk_block_start, max_reach, q_aug, k_aug, v)


def kernel(x, c, ada_w, ada_b, norm_attn_g, norm_ffn_g, w_ffn_in, w_ffn_down, sb_w_qkv, sb_w_o,
           kv_ada_w, kv_ada_b, kv_norm_g, w_kvf, b_f, k_norm_g, fox_w_q, q_norm_g, fox_w_o):
    batch, seq, d = x.shape
    assert d == D_MODEL and seq % TM == 0 and seq % TM_WIDE == 0
    assert seq % (FOX_SUB * TQ) == 0 and seq % (SB_SUB * SB_T) == 0
    assert ada_w.shape[0] == 2 and sb_w_qkv.shape[0] == 1 and fox_w_q.shape[0] == 1
    m = batch * seq

    mod = _ada_mod(c, ada_w, ada_b)[:, :batch]
    kv_mod = _ada_mod(c, kv_ada_w[None], kv_ada_b[None])[0, :batch]

    def vecs(t, n):
        return [t[:, None, j * d:(j + 1) * d] for j in range(n)]

    row = lambda t: t.reshape(1, -1).astype(F32)
    bf = lambda t: t.astype(BF16)

    x2d = x.reshape(m, d)

    sh_a, sc_a, g_a, sh_f, sc_f, g_f = vecs(mod[0], 6)
    qkv_scale = jnp.concatenate([jnp.full((1, d), LOG2_E * HEAD_DIM ** -0.5, F32),
                                 jnp.ones((1, 2 * d), F32)], axis=1)
    qkv = _norm_matmul(x2d, row(norm_attn_g[0]), sh_a, sc_a, bf(sb_w_qkv[0]), qkv_scale, seq)
    o = _sb_attn(qkv.reshape(batch, seq, 3 * d), batch, seq)
    x2d, act = _out_ffn_in(o.reshape(m, d), bf(sb_w_o[0]), x2d, g_a,
                           row(norm_ffn_g[0]), sh_f, sc_f, bf(w_ffn_in[0]), seq)
    g_f0 = g_f

    sh_a, sc_a, g_a, sh_f, sc_f, g_f = vecs(mod[1], 6)
    kv_shift, kv_scale = vecs(kv_mod, 2)
    w_f_pad = jnp.zeros((d, LANES_V7X), F32).at[:, :N_HEADS].set(w_kvf[:, 2 * d:])
    b_f_pad = jnp.zeros((1, LANES_V7X), F32).at[0, :N_HEADS].set(b_f)
    key_norm_bound = (1.01 * HEAD_DIM ** 0.5 * jnp.max(jnp.abs(k_norm_g))).reshape(1, 1).astype(F32)
    x2d, v, q_aug, k_aug, stats = _l1_proj(
        act, bf(w_ffn_down[0]), x2d, g_f0,
        row(kv_norm_g), kv_shift, kv_scale, row(norm_attn_g[1]), sh_a, sc_a,
        bf(w_kvf[:, :d]), bf(w_kvf[:, d:2 * d]), bf(w_f_pad), bf(fox_w_q[0]),
        row(jnp.tile(k_norm_g, N_HEADS)), row(jnp.tile(q_norm_g[0], N_HEADS)), b_f_pad,
        key_norm_bound, batch, seq)
    tables = stats[:, :, :3, :N_HEADS].transpose(2, 0, 3, 1).reshape(3, batch * N_HEADS, seq // TK)
    o = _fox_attn(q_aug, k_aug, v.reshape(batch, seq, d), tables[0], tables[1], tables[2], batch, seq)
    x2d, act = _out_ffn_in(o.reshape(m, d), bf(fox_w_o[0]), x2d, g_a,
                           row(norm_ffn_g[1]), sh_f, sc_f, bf(w_ffn_in[1]), seq)
    x2d = _proj_residual(act, bf(w_ffn_down[1]), x2d, g_f, seq)
    return x2d.reshape(batch, seq, d)
```

```python
import functools

import jax
import jax.numpy as jnp
from jax import lax
from jax.experimental import pallas as pl
from jax.experimental.pallas import tpu as pltpu

D_MODEL = 1024
N_HEADS = 16
HEAD_DIM = D_MODEL // N_HEADS
EPS = 1e-6

LANES_V7X = 128
SUBLANES_V7X = 8
VMEM_BYTES_V7X = 64 * 1024 * 1024

HEADS_PER_LANE_TILE = LANES_V7X // HEAD_DIM
N_HEAD_PAIRS = N_HEADS // HEADS_PER_LANE_TILE

MXU_WIDTH_V7X = 256

TM = 512
COL_CHUNK = 2 * MXU_WIDTH_V7X
TQ = 512
TK = 512
FOX_SUB = 4
SB_T = 128
SB_LEFT = 256
SB_SUB = 16

ZERO_EXP = 104.5
SAFE_EXP_RANGE = 80.0
NEG_BIG = -1e30
LOG2_E = 1.4426950408889634
SOFTPLUS_LINEAR = 64.0

F32 = jnp.float32
BF16 = jnp.bfloat16


def _vmem_limit(block_bytes):
    want = 2 * block_bytes + 16 * 1024 * 1024
    return int(min(want, VMEM_BYTES_V7X - 8 * 1024 * 1024))


def _dot(a, b):
    return jnp.dot(a, b, preferred_element_type=F32)


def _dot_nt(a, b):
    return lax.dot_general(a, b, (((1,), (1,)), ((), ())), preferred_element_type=F32)


def _ada_kernel(ct_ref, w_ref, b_ref, o_ref, *, batch):
    c = ct_ref[...]
    c_act = c * jax.nn.sigmoid(c)
    w = w_ref[...]
    o_ref[...] = jnp.zeros_like(o_ref)
    for b in range(batch):
        o_ref[b:b + 1, :] = jnp.sum(w * c_act[:, b:b + 1], axis=0, keepdims=True) + b_ref[...]


def _ada_mod(c, w, b):
    n_layers, d, n = w.shape
    batch = c.shape[0]
    rows = SUBLANES_V7X
    assert batch <= rows
    tn = 1024
    c_t = jnp.zeros((d, rows), F32).at[:, :batch].set(c.astype(F32).T)
    block_bytes = 4 * (d * tn + d * LANES_V7X + tn + rows * tn)
    return pl.pallas_call(
        functools.partial(_ada_kernel, batch=batch),
        out_shape=jax.ShapeDtypeStruct((n_layers, rows, n), F32),
        grid=(n_layers, n // tn),
        in_specs=[
            pl.BlockSpec((d, rows), lambda l, j: (0, 0)),
            pl.BlockSpec((None, d, tn), lambda l, j: (l, 0, j)),
            pl.BlockSpec((None, 1, tn), lambda l, j: (l, 0, j)),
        ],
        out_specs=pl.BlockSpec((None, rows, tn), lambda l, j: (l, 0, j)),
        compiler_params=pltpu.CompilerParams(
            dimension_semantics=("arbitrary", "arbitrary"),
            vmem_limit_bytes=_vmem_limit(block_bytes)),
        name="ada_mod",
    )(c_t, w, b.reshape(n_layers, 1, n))


def _norm_modulate(x, g, shift, scale):
    ms = jnp.mean(x * x, axis=-1, keepdims=True)
    y = x * lax.rsqrt(ms + EPS)
    return (y * g) * (1.0 + scale) + shift


def _row_spec(d):
    return pl.BlockSpec((1, d), lambda *_: (0, 0))


def _resident_spec(shape, block_index=(0, 0)):
    return pl.BlockSpec(shape, lambda *_: block_index)


def _batch_vec_spec(d, tiles_per_batch):
    return pl.BlockSpec((None, 1, d), lambda i, *_: (i // tiles_per_batch, 0, 0))


def _column_chunks(n):
    return [(c, min(c + COL_CHUNK, n)) for c in range(0, n, COL_CHUNK)]


def _norm_matmul_kernel(x_ref, g_ref, sh_ref, sc_ref, w_ref, cs_ref, o_ref):
    h = _norm_modulate(x_ref[...], g_ref[...], sh_ref[...], sc_ref[...]).astype(BF16)
    for c0, c1 in _column_chunks(o_ref.shape[1]):
        o_ref[:, c0:c1] = (_dot(h, w_ref[:, c0:c1]) * cs_ref[:, c0:c1]).astype(o_ref.dtype)


def _norm_matmul(x2d, g, shift, scale, w, col_scale, seq):
    m, d = x2d.shape
    n = w.shape[1]
    block_bytes = 4 * TM * d + 2 * d * n + 2 * TM * n + 4 * n
    return pl.pallas_call(
        _norm_matmul_kernel,
        out_shape=jax.ShapeDtypeStruct((m, n), BF16),
        grid=(m // TM,),
        in_specs=[
            pl.BlockSpec((TM, d), lambda i: (i, 0)),
            _row_spec(d),
            _batch_vec_spec(d, seq // TM),
            _batch_vec_spec(d, seq // TM),
            _resident_spec((d, n)),
            pl.BlockSpec((1, n), lambda i: (0, 0)),
        ],
        out_specs=pl.BlockSpec((TM, n), lambda i: (i, 0)),
        compiler_params=pltpu.CompilerParams(
            dimension_semantics=("arbitrary",),
            vmem_limit_bytes=_vmem_limit(block_bytes)),
        name="norm_matmul",
    )(x2d, g, shift, scale, w, col_scale)


def _out_ffn_in_kernel(a_ref, wo_ref, x_ref, gate_ref, g_ref, sh_ref, sc_ref, wg_ref, wu_ref,
                       x1_ref, act_ref):
    x1 = x_ref[...] + gate_ref[...] * _dot(a_ref[...], wo_ref[...])
    x1_ref[...] = x1
    h = _norm_modulate(x1, g_ref[...], sh_ref[...], sc_ref[...]).astype(BF16)
    for c0, c1 in _column_chunks(act_ref.shape[1]):
        gate = _dot(h, wg_ref[:, c0:c1])
        up = _dot(h, wu_ref[:, c0:c1])
        act_ref[:, c0:c1] = (gate * jax.nn.sigmoid(gate) * up).astype(act_ref.dtype)


def _out_ffn_in(a, w_o, x2d, gate, g, shift, scale, w_in, seq):
    m, d = x2d.shape
    f = w_in.shape[1] // 2
    tpb = seq // TM
    block_bytes = 2 * TM * d + 2 * d * d + 2 * 4 * TM * d + 2 * 2 * d * f + 2 * TM * f
    return pl.pallas_call(
        _out_ffn_in_kernel,
        out_shape=(jax.ShapeDtypeStruct((m, d), F32), jax.ShapeDtypeStruct((m, f), BF16)),
        grid=(m // TM,),
        in_specs=[
            pl.BlockSpec((TM, d), lambda i: (i, 0)),
            _resident_spec((d, d)),
            pl.BlockSpec((TM, d), lambda i: (i, 0)),
            _batch_vec_spec(d, tpb),
            _row_spec(d),
            _batch_vec_spec(d, tpb),
            _batch_vec_spec(d, tpb),
            _resident_spec((d, f)),
            _resident_spec((d, f), (0, 1)),
        ],
        out_specs=(pl.BlockSpec((TM, d), lambda i: (i, 0)), pl.BlockSpec((TM, f), lambda i: (i, 0))),
        compiler_params=pltpu.CompilerParams(
            dimension_semantics=("arbitrary",),
            vmem_limit_bytes=_vmem_limit(block_bytes)),
        name="out_ffn_in",
    )(a, w_o, x2d, gate, g, shift, scale, w_in, w_in)


def _proj_residual_kernel(a_ref, w_ref, x_ref, gate_ref, o_ref):
    o_ref[...] = x_ref[...] + gate_ref[...] * _dot(a_ref[...], w_ref[...])


def _proj_residual(a, w, x2d, gate, seq):
    m, k = a.shape
    d = w.shape[1]
    block_bytes = 2 * TM * k + 2 * k * d + 2 * 4 * TM * d
    return pl.pallas_call(
        _proj_residual_kernel,
        out_shape=jax.ShapeDtypeStruct((m, d), F32),
        grid=(m // TM,),
        in_specs=[
            pl.BlockSpec((TM, k), lambda i: (i, 0)),
            _resident_spec((k, d)),
            pl.BlockSpec((TM, d), lambda i: (i, 0)),
            _batch_vec_spec(d, seq // TM),
        ],
        out_specs=pl.BlockSpec((TM, d), lambda i: (i, 0)),
        compiler_params=pltpu.CompilerParams(
            dimension_semantics=("arbitrary",),
            vmem_limit_bytes=_vmem_limit(block_bytes)),
        name="proj_residual",
    )(a, w, x2d, gate)


def _head_rms_norm(t, gain, group_mean):
    ms = _dot((t * t).astype(BF16), group_mean)
    return t * lax.rsqrt(ms + EPS) * gain


def _l1_proj_kernel(kb_ref, act_ref, wd_ref, x_ref, gate_ref,
                    gkv_ref, shkv_ref, sckv_ref, gq_ref, shq_ref, scq_ref,
                    wk_ref, wv_ref, wf_ref, wq_ref, kg_ref, qg_ref, bf_ref, gm_ref,
                    x2_ref, v_ref, qa_ref, ka_ref, stats_ref,
                    q_ref, k_ref, carry_scr, *, tiles_per_sequence):
    x = x_ref[...] + gate_ref[...] * _dot(act_ref[...], wd_ref[...])
    x2_ref[...] = x
    ms = jnp.mean(x * x, axis=-1, keepdims=True)
    y = x * lax.rsqrt(ms + EPS)
    h_kv = ((y * gkv_ref[...]) * (1.0 + sckv_ref[...]) + shkv_ref[...]).astype(BF16)
    h_q = ((y * gq_ref[...]) * (1.0 + scq_ref[...]) + shq_ref[...]).astype(BF16)
    gm = gm_ref[...]
    k_all = _dot(h_kv, wk_ref[...])
    q_all = _dot(h_q, wq_ref[...])
    v_ref[...] = _dot(h_kv, wv_ref[...]).astype(BF16)
    log_f = jax.nn.log_sigmoid(_dot(h_kv, wf_ref[...]) + bf_ref[...])
    for c0 in range(0, k_ref.shape[1], MXU_WIDTH_V7X):
        cols = slice(c0, c0 + MXU_WIDTH_V7X)
        k_ref[:, cols] = _head_rms_norm(k_all[:, cols], kg_ref[:, cols], gm).astype(BF16)
        q = _head_rms_norm(q_all[:, cols], qg_ref[:, cols], gm)
        q_ref[:, cols] = (q * (HEAD_DIM ** -0.5)).astype(BF16)
    _fox_augment(pl.program_id(0) % tiles_per_sequence == 0, kb_ref[0, 0], log_f,
                 q_ref, k_ref, qa_ref, ka_ref, stats_ref, carry_scr)


def _l1_proj(act, w_down, x2d, gate, g_kv, sh_kv, sc_kv, g_q, sh_q, sc_q, wk, wv, wf, wq,
             k_gain, q_gain, b_f, key_norm_bound, batch, seq):
    assert TM == TK == TQ
    m, d = x2d.shape
    f = act.shape[1]
    lane = jnp.arange(MXU_WIDTH_V7X)
    group_mean = jnp.where(lane[:, None] // HEAD_DIM == lane[None, :] // HEAD_DIM,
                           1.0 / HEAD_DIM, 0.0).astype(BF16)
    tpb = seq // TM
    block_bytes = (2 * TM * f + 2 * f * d + 2 * 4 * TM * d + 3 * 2 * d * d + 2 * d * LANES_V7X
                   + 2 * TM * d + 2 * 2 * N_HEADS * TM * LANES_V7X + 2 * 2 * TM * d)
    full = lambda r, c: _resident_spec((r, c))
    rows = lambda c: pl.BlockSpec((TM, c), lambda i: (i, 0))
    aug_shape = jax.ShapeDtypeStruct((batch, N_HEADS, seq, LANES_V7X), BF16)
    aug_spec = pl.BlockSpec((None, N_HEADS, TM, LANES_V7X), lambda i: (i // tpb, 0, i % tpb, 0))
    return pl.pallas_call(
        functools.partial(_l1_proj_kernel, tiles_per_sequence=tpb),
        out_shape=(jax.ShapeDtypeStruct((m, d), F32), jax.ShapeDtypeStruct((m, d), BF16),
                   aug_shape, aug_shape,
                   jax.ShapeDtypeStruct((batch, tpb, SUBLANES_V7X, LANES_V7X), F32)),
        grid=(m // TM,),
        in_specs=[
            pl.BlockSpec(memory_space=pltpu.SMEM),
            rows(f), full(f, d), rows(d), _batch_vec_spec(d, tpb),
            _row_spec(d), _batch_vec_spec(d, tpb), _batch_vec_spec(d, tpb),
            _row_spec(d), _batch_vec_spec(d, tpb), _batch_vec_spec(d, tpb),
            full(d, d), full(d, d), full(d, LANES_V7X), full(d, d),
            _row_spec(d), _row_spec(d), _row_spec(LANES_V7X), full(MXU_WIDTH_V7X, MXU_WIDTH_V7X),
        ],
        out_specs=(rows(d), rows(d), aug_spec, aug_spec,
                   pl.BlockSpec((None, None, SUBLANES_V7X, LANES_V7X),
                                lambda i: (i // tpb, i % tpb, 0, 0))),
        scratch_shapes=[pltpu.VMEM((TM, d), BF16), pltpu.VMEM((TM, d), BF16),
                        pltpu.VMEM((SUBLANES_V7X, LANES_V7X), F32)],
        compiler_params=pltpu.CompilerParams(
            dimension_semantics=("arbitrary",),
            vmem_limit_bytes=_vmem_limit(block_bytes)),
        name="l1_proj",
    )(key_norm_bound, act, w_down, x2d, gate, g_kv, sh_kv, sc_kv, g_q, sh_q, sc_q, wk, wv, wf, wq,
      k_gain, q_gain, b_f, group_mean)


N_SPLIT = 3


def _split3(x):
    x0 = x.astype(BF16)
    r1 = x - x0.astype(F32)
    x1 = r1.astype(BF16)
    x2 = (r1 - x1.astype(F32)).astype(BF16)
    return x0, x1, x2


def _fox_augment(first_tile, kb, log_f, q_ref, k_ref, qa_ref, ka_ref, stats_ref, carry_scr):
    @pl.when(first_tile)
    def _():
        carry_scr[...] = jnp.zeros_like(carry_scr)

    r = lax.broadcasted_iota(jnp.int32, (TK, TK), 0)
    c = lax.broadcasted_iota(jnp.int32, (TK, TK), 1)
    tri = jnp.where(c <= r, 1.0, 0.0).astype(BF16)
    x0, x1, x2 = _split3(log_f)
    cs = _dot(tri, x0) + _dot(tri, x1) + _dot(tri, x2) + carry_scr[0:1, :]
    carry_scr[...] = jnp.broadcast_to(cs[TK - 1:TK, :], carry_scr.shape)

    d_idx = lax.broadcasted_iota(jnp.int32, (D_MODEL, LANES_V7X), 0)
    h_idx = lax.broadcasted_iota(jnp.int32, (D_MODEL, LANES_V7X), 1)
    head_of = jnp.where(d_idx // HEAD_DIM == h_idx, 1.0, 0.0).astype(BF16)
    qf = q_ref[...].astype(F32)
    reach = jnp.sqrt(_dot((qf * qf).astype(BF16), head_of)) * kb
    own_logit = _dot((qf * k_ref[...].astype(F32)).astype(BF16), head_of)

    stats_ref[...] = jnp.concatenate([
        jnp.max(reach * (1.0 + 2.0 ** -8) - own_logit + cs, axis=0, keepdims=True),
        cs[0:1, :],
        jnp.max(reach, axis=0, keepdims=True),
        jnp.zeros((SUBLANES_V7X - 3, LANES_V7X), F32)], axis=0)

    lane = lax.broadcasted_iota(jnp.int32, (1, LANES_V7X), 1)
    packed = jnp.zeros((TK, LANES_V7X), F32)
    for i, term in enumerate(_split3(cs - reach) + _split3(-cs)):
        term = jnp.where(lane < N_HEADS, term.astype(F32), 0.0)
        packed = packed + (term if i == 0 else pltpu.roll(term, N_HEADS * i, axis=1))
    packed = packed.astype(BF16)

    src = lax.broadcasted_iota(jnp.int32, (LANES_V7X, 2 * LANES_V7X), 0)
    dst = lax.broadcasted_iota(jnp.int32, (LANES_V7X, 2 * LANES_V7X), 1)

    for h in range(N_HEADS):
        pair, parity = divmod(h, HEADS_PER_LANE_TILE)
        spare = HEAD_DIM * (1 - parity)
        place = jnp.where((src % N_HEADS == h) & (src < 2 * N_SPLIT * N_HEADS)
                          & (dst % LANES_V7X == spare + src // N_HEADS)
                          & ((dst >= LANES_V7X) == (src >= N_SPLIT * N_HEADS)), 1.0, 0.0)
        placed = _dot(packed, place.astype(BF16))
        ones_q = jnp.where((lane >= spare + N_SPLIT) & (lane < spare + 2 * N_SPLIT), 1.0, 0.0)
        ones_k = jnp.where((lane >= spare) & (lane < spare + N_SPLIT), 1.0, 0.0)
        aug_q = (placed[:, :LANES_V7X] + ones_q).astype(BF16)
        aug_k = (placed[:, LANES_V7X:] + ones_k).astype(BF16)
        own = (lane // HEAD_DIM) == parity
        cols = slice(pair * LANES_V7X, (pair + 1) * LANES_V7X)
        qa_ref[h] = jnp.where(own, q_ref[:, cols], aug_q)
        ka_ref[h] = jnp.where(own, k_ref[:, cols], aug_k)


def _head_select(head):
    lane = lax.broadcasted_iota(jnp.int32, (1, LANES_V7X), 1)
    return (lane // HEAD_DIM) == head


def _sb_kernel(q_ref, k_ref, v_ref, o_ref):
    i = pl.program_id(2)

    def strictly_lower(n):
        return (lax.broadcasted_iota(jnp.int32, (n, n), 0)
                > lax.broadcasted_iota(jnp.int32, (n, n), 1))

    diag_bias = jnp.where(strictly_lower(SB_T), 0.0, NEG_BIG)
    after = jnp.where(strictly_lower(SB_T), 1.0, 0.0).astype(BF16)
    after_left = jnp.where(strictly_lower(SB_LEFT), 1.0, 0.0).astype(BF16)
    ones = jnp.ones((SB_T, LANES_V7X), BF16)
    ones_left = jnp.ones((SB_T, SB_LEFT), BF16)

    def softplus(z):
        return jnp.maximum(jnp.log2(1.0 + jnp.exp2(jnp.minimum(z, SOFTPLUS_LINEAR))), z)

    def keys(start, size):
        rows = pl.ds(pl.multiple_of(start, SB_T), size)
        return k_ref[rows, :], v_ref[rows, :]

    def left_start(g):
        return jnp.maximum(g * SB_T - SB_LEFT, 0)

    def band_logits(qh, g, may_clip):
        k_d, _ = keys(g * SB_T, SB_T)
        k_p, _ = keys(left_start(g), SB_LEFT)
        z_p = _dot_nt(qh, k_p)
        if may_clip:
            key = left_start(g) + lax.broadcasted_iota(jnp.int32, (SB_T, SB_LEFT), 1)
            z_p = jnp.where(key < g * SB_T, z_p, NEG_BIG)
        return _dot_nt(qh, k_d) + diag_bias, z_p

    def band_softplus(z_d, z_p):
        return z_d, z_p, softplus(z_d), softplus(z_p)

    def band_sums(z_d, z_p, sp_d, sp_p):
        sp_d16 = sp_d.astype(BF16)
        within_d = _dot(sp_d16, after)
        within_p = _dot(sp_p.astype(BF16), after_left) + _dot(sp_d16, ones_left)
        total = (within_p + sp_p)[:, 0:1]
        return (z_d - sp_d) - within_d, (z_p - sp_p) - within_p, total

    def band_weights(e_d, e_p, total):
        return jnp.exp2(e_d).astype(BF16), jnp.exp2(e_p).astype(BF16), total

    def band_out(g, a_d, a_p, total):
        _, v_d = keys(g * SB_T, SB_T)
        _, v_p = keys(left_start(g), SB_LEFT)
        return jnp.broadcast_to(-total, (SB_T, LANES_V7X)), _dot(a_d, v_d) + _dot(a_p, v_p)

    def block(qh, start, carry, acc):
        kj, vj = keys(start, SB_T)
        z = _dot_nt(qh, kj)
        sp = softplus(z)
        sp16 = sp.astype(BF16)
        within = _dot(sp16, after)
        a = jnp.exp2((z - sp) - within + carry)
        acc = acc + _dot(a.astype(BF16), vj)
        carry = carry - _dot(sp16, ones)
        return carry, acc

    heads = []
    for u in range(SB_SUB):
        g = i * SB_SUB + u
        q = q_ref[u * SB_T:(u + 1) * SB_T, :]
        for head in range(HEADS_PER_LANE_TILE):
            heads.append((g, jnp.where(_head_select(head), q, 0.0).astype(BF16)))
    logits = [band_logits(qh, g, may_clip=c // HEADS_PER_LANE_TILE < SB_LEFT // SB_T)
              for c, (g, qh) in enumerate(heads)]
    softplus_terms = [band_softplus(*z) for z in logits]
    sums = [band_sums(*t) for t in softplus_terms]
    weights = [band_weights(*s) for s in sums]
    chains = [(g, qh) + band_out(g, *w) for (g, qh), w in zip(heads, weights)]

    def is_open(carry):
        return jnp.max(carry) > -ZERO_EXP * LOG2_E

    chain_open = [is_open(carry) for _, _, carry, _ in chains]

    def walk_on():
        outs = []
        for (g, qh, carry, acc), this_open in zip(chains, chain_open):
            def cond(state):
                start, carry, _ = state
                return jnp.logical_and(start >= 0, is_open(carry))

            def body(state, qh=qh):
                start, carry, acc = state
                carry, acc = block(qh, start, carry, acc)
                return start - SB_T, carry, acc

            def walk(cond=cond, body=body, first=(left_start(g) - SB_T, carry, acc)):
                return lax.while_loop(cond, body, first)[2]

            outs.append(lax.cond(this_open, walk, lambda acc=acc: acc))
        return outs

    outs = lax.cond(functools.reduce(jnp.logical_or, chain_open), walk_on,
                    lambda: [acc for _, _, _, acc in chains])
    for u in range(SB_SUB):
        pair = outs[u * HEADS_PER_LANE_TILE:(u + 1) * HEADS_PER_LANE_TILE]
        o_ref[u * SB_T:(u + 1) * SB_T, :] = jnp.where(_head_select(0), pair[0], pair[1]).astype(o_ref.dtype)


def _sb_attn(qkv, batch, seq):
    assert SB_T == LANES_V7X and SB_LEFT % SB_T == 0
    rows = SB_SUB * SB_T
    stage_bytes = SB_SUB * HEADS_PER_LANE_TILE * 2 * 4 * SB_T * (SB_T + SB_LEFT)
    block_bytes = 2 * (2 * rows * LANES_V7X + 2 * seq * LANES_V7X) + stage_bytes // 2
    return pl.pallas_call(
        _sb_kernel,
        out_shape=jax.ShapeDtypeStruct((batch, seq, D_MODEL), BF16),
        grid=(batch, N_HEAD_PAIRS, seq // rows),
        in_specs=[
            pl.BlockSpec((None, rows, LANES_V7X), lambda b, p, i: (b, i, p)),
            pl.BlockSpec((None, seq, LANES_V7X), lambda b, p, i: (b, 0, N_HEAD_PAIRS + p)),
            pl.BlockSpec((None, seq, LANES_V7X), lambda b, p, i: (b, 0, 2 * N_HEAD_PAIRS + p)),
        ],
        out_specs=pl.BlockSpec((None, rows, LANES_V7X), lambda b, p, i: (b, i, p)),
        compiler_params=pltpu.CompilerParams(
            dimension_semantics=("arbitrary", "arbitrary", "arbitrary"),
            vmem_limit_bytes=_vmem_limit(block_bytes)),
        name="sb_attn",
    )(qkv, qkv, qkv)


def _fox_kernel(a_ref, fk0_ref, g_ref, qa_ref, ka_ref, v_ref, o_ref, acc_scr, vt_scr):
    b = pl.program_id(0)
    p_id = pl.program_id(1)
    i = pl.program_id(2)
    n_blocks = fk0_ref.shape[1]
    row = lax.broadcasted_iota(jnp.int32, (TQ, TK), 0)
    col = lax.broadcasted_iota(jnp.int32, (TQ, TK), 1)
    diag_mask = col <= row

    acc0 = jnp.zeros((TQ, LANES_V7X), F32)
    chains = range(FOX_SUB * HEADS_PER_LANE_TILE)
    bh0 = (b * N_HEAD_PAIRS + p_id) * HEADS_PER_LANE_TILE

    def scores(chain, j):
        sub, head = chain // HEADS_PER_LANE_TILE, chain % HEADS_PER_LANE_TILE
        q = qa_ref[head, pl.ds(pl.multiple_of(sub * TQ, TQ), TQ), :]
        return _dot_nt(q, ka_ref[head, pl.ds(pl.multiple_of(j * TK, TK), TK), :])

    def values(chain, j):
        vj = v_ref[pl.ds(pl.multiple_of(j * TK, TK), TK), :]
        own = _head_select(chain % HEADS_PER_LANE_TILE)
        return jnp.where(own, vj, jnp.ones_like(vj))

    def diag_block(chain):
        return i * FOX_SUB + chain // HEADS_PER_LANE_TILE

    def count_left(chain):
        bh, qi = bh0 + chain % HEADS_PER_LANE_TILE, diag_block(chain)
        bound = a_ref[bh, qi]

        def more(n):
            j = qi - 1 - n
            right = fk0_ref[bh, jnp.minimum(j + 1, n_blocks - 1)]
            return jnp.logical_and(j >= 0, bound - right > -ZERO_EXP)

        return lax.while_loop(more, lambda n: n + 1, jnp.int32(0))

    def fast_walks():
        counts = [count_left(chain) for chain in chains]
        firsts = [sum(n + 1 for n in counts[:chain]) for chain in chains]
        n_items = firsts[-1] + counts[-1] + 1
        diag_bias_t = jnp.where(row <= col, 0.0, NEG_BIG)
        feature = lax.broadcasted_iota(jnp.int32, (LANES_V7X, 1), 0)
        acc_scr[...] = jnp.zeros_like(acc_scr)

        def scores_t(chain, j):
            sub, head = chain // HEADS_PER_LANE_TILE, chain % HEADS_PER_LANE_TILE
            q = qa_ref[head, pl.ds(pl.multiple_of(sub * TQ, TQ), TQ), :]
            return _dot_nt(ka_ref[head, pl.ds(pl.multiple_of(j * TK, TK), TK), :], q)

        def values_t(chain, j):
            vt = vt_scr[:, pl.ds(pl.multiple_of(j * TK, TK), TK)]
            own = (feature // HEAD_DIM) == chain % HEADS_PER_LANE_TILE
            return jnp.where(own, vt, jnp.ones_like(vt))

        def step(t, p_t, p_chain, p_block):
            chain = sum(jnp.where(t >= first, 1, 0) for first in firsts[1:])
            first = functools.reduce(lambda f, nxt: jnp.where(t >= nxt, nxt, f), firsts[1:], 0)
            back = t - first
            block = jnp.maximum(diag_block(chain) - back, 0)
            s_t = scores_t(chain, block) + jnp.where(back == 0, diag_bias_t,
                                                     jnp.where(t >= n_items, NEG_BIG, 0.0))
            acc_scr[p_chain] += _dot(values_t(p_chain, p_block), p_t)
            return jnp.exp(s_t).astype(BF16), chain, block

        def steps(first, count, carried):
            for t in range(count):
                carried = step(first + t, *carried)
            return carried

        pairs = (n_items + 2) // 2
        start = (jnp.zeros((TK, TQ), BF16), jnp.int32(0), jnp.int32(0))
        carried = lax.fori_loop(0, pairs // 4, lambda u, c: steps(8 * u, 8, c), start)
        done = 8 * (pairs // 4)
        carried = lax.fori_loop(0, (pairs % 4) // 2, lambda _, c: steps(done, 4, c), carried)
        done = done + 4 * ((pairs % 4) // 2)
        lax.fori_loop(0, pairs % 2, lambda _, c: steps(done, 2, c), carried)

        outs = []
        for sub in range(FOX_SUB):
            normed = []
            for head in range(HEADS_PER_LANE_TILE):
                acc_t = acc_scr[sub * HEADS_PER_LANE_TILE + head]
                spare = HEAD_DIM * (1 - head)
                normed.append(acc_t / acc_t[spare:spare + 1, :])
            outs.append(jnp.where(feature // HEAD_DIM == 0, normed[0], normed[1]).T)
        return outs

    def online_walks():
        accs = []
        for chain in chains:
            def body(t, carried, chain=chain):
                acc, m = carried
                block = diag_block(chain) - t
                s = jnp.where(jnp.logical_or(t > 0, diag_mask), scores(chain, block), NEG_BIG)
                m_new = jnp.maximum(m, jnp.max(s, axis=-1, keepdims=True))
                p = jnp.exp(s - m_new).astype(BF16)
                return jnp.exp(m - m_new) * acc + _dot(p, values(chain, block)), m_new

            m0 = jnp.full((TQ, 1), NEG_BIG, F32)
            acc = lax.fori_loop(0, count_left(chain) + 1, body, (acc0, m0))[0]
            accs.append(acc / pltpu.roll(acc, HEAD_DIM, axis=1))
        return [jnp.where(_head_select(0), accs[sub * HEADS_PER_LANE_TILE],
                          accs[sub * HEADS_PER_LANE_TILE + 1]) for sub in range(FOX_SUB)]

    @pl.when(i == 0)
    def _():
        for c in range(0, vt_scr.shape[1], TK):
            vt_scr[:, c:c + TK] = v_ref[c:c + TK, :].T

    largest_reach = functools.reduce(jnp.maximum, [
        g_ref[bh0 + chain % HEADS_PER_LANE_TILE, diag_block(chain)] for chain in chains])
    outs = lax.cond(2.0 * largest_reach < SAFE_EXP_RANGE, fast_walks, online_walks)
    for sub in range(FOX_SUB):
        o_ref[sub * TQ:(sub + 1) * TQ, :] = outs[sub].astype(o_ref.dtype)


def _fox_attn(q_aug, k_aug, v, walk_bound, fk_block_start, max_reach, batch, seq):
    assert TQ == TK
    rows = FOX_SUB * TQ
    n_chains = FOX_SUB * HEADS_PER_LANE_TILE
    block_bytes = (2 * (HEADS_PER_LANE_TILE * (rows + seq) * LANES_V7X + seq * LANES_V7X + rows * LANES_V7X)
                   + 4 * n_chains * TQ * LANES_V7X + 2 * seq * LANES_V7X)
    smem = pl.BlockSpec(memory_space=pltpu.SMEM)
    return pl.pallas_call(
        _fox_kernel,
        out_shape=jax.ShapeDtypeStruct((batch, seq, D_MODEL), BF16),
        grid=(batch, N_HEAD_PAIRS, seq // rows),
        in_specs=[
            smem, smem, smem,
            pl.BlockSpec((None, HEADS_PER_LANE_TILE, rows, LANES_V7X), lambda b, p, i: (b, p, i, 0)),
            pl.BlockSpec((None, HEADS_PER_LANE_TILE, seq, LANES_V7X), lambda b, p, i: (b, p, 0, 0)),
            pl.BlockSpec((None, seq, LANES_V7X), lambda b, p, i: (b, 0, p)),
        ],
        out_specs=pl.BlockSpec((None, rows, LANES_V7X), lambda b, p, i: (b, i, p)),
        scratch_shapes=[pltpu.VMEM((n_chains, LANES_V7X, TQ), F32),
                        pltpu.VMEM((LANES_V7X, seq), BF16)],
        compiler_params=pltpu.CompilerParams(
            dimension_semantics=("arbitrary", "arbitrary", "arbitrary"),
            vmem_limit_bytes=_vmem_limit(block_bytes)),
        name="fox_attn",
    )(walk_bound, fk_block_start, max_reach, q_aug, k_aug, v)


def kernel(x, c, ada_w, ada_b, norm_attn_g, norm_ffn_g, w_ffn_in, w_ffn_down, sb_w_qkv, sb_w_o,
           kv_ada_w, kv_ada_b, kv_norm_g, w_kvf, b_f, k_norm_g, fox_w_q, q_norm_g, fox_w_o):
    batch, seq, d = x.shape
    assert d == D_MODEL and seq % TM == 0
    assert seq % (FOX_SUB * TQ) == 0 and seq % (SB_SUB * SB_T) == 0
    assert ada_w.shape[0] == 2 and sb_w_qkv.shape[0] == 1 and fox_w_q.shape[0] == 1
    m = batch * seq

    mod = _ada_mod(c, ada_w, ada_b)[:, :batch]
    kv_mod = _ada_mod(c, kv_ada_w[None], kv_ada_b[None])[0, :batch]

    def vecs(t, n):
        return [t[:, None, j * d:(j + 1) * d] for j in range(n)]

    row = lambda t: t.reshape(1, -1).astype(F32)
    bf = lambda t: t.astype(BF16)

    x2d = x.reshape(m, d)

    sh_a, sc_a, g_a, sh_f, sc_f, g_f = vecs(mod[0], 6)
    qkv_scale = jnp.concatenate([jnp.full((1, d), LOG2_E * HEAD_DIM ** -0.5, F32),
                                 jnp.ones((1, 2 * d), F32)], axis=1)
    qkv = _norm_matmul(x2d, row(norm_attn_g[0]), sh_a, sc_a, bf(sb_w_qkv[0]), qkv_scale, seq)
    o = _sb_attn(qkv.reshape(batch, seq, 3 * d), batch, seq)
    x2d, act = _out_ffn_in(o.reshape(m, d), bf(sb_w_o[0]), x2d, g_a,
                           row(norm_ffn_g[0]), sh_f, sc_f, bf(w_ffn_in[0]), seq)
    g_f0 = g_f

    sh_a, sc_a, g_a, sh_f, sc_f, g_f = vecs(mod[1], 6)
    kv_shift, kv_scale = vecs(kv_mod, 2)
    w_f_pad = jnp.zeros((d, LANES_V7X), F32).at[:, :N_HEADS].set(w_kvf[:, 2 * d:])
    b_f_pad = jnp.zeros((1, LANES_V7X), F32).at[0, :N_HEADS].set(b_f)
    key_norm_bound = (1.01 * HEAD_DIM ** 0.5 * jnp.max(jnp.abs(k_norm_g))).reshape(1, 1).astype(F32)
    x2d, v, q_aug, k_aug, stats = _l1_proj(
        act, bf(w_ffn_down[0]), x2d, g_f0,
        row(kv_norm_g), kv_shift, kv_scale, row(norm_attn_g[1]), sh_a, sc_a,
        bf(w_kvf[:, :d]), bf(w_kvf[:, d:2 * d]), bf(w_f_pad), bf(fox_w_q[0]),
        row(jnp.tile(k_norm_g, N_HEADS)), row(jnp.tile(q_norm_g[0], N_HEADS)), b_f_pad,
        key_norm_bound, batch, seq)
    tables = stats[:, :, :3, :N_HEADS].transpose(2, 0, 3, 1).reshape(3, batch * N_HEADS, seq // TK)
    o = _fox_attn(q_aug, k_aug, v.reshape(batch, seq, d), tables[0], tables[1], tables[2], batch, seq)
    x2d, act = _out_ffn_in(o.reshape(m, d), bf(fox_w_o[0]), x2d, g_a,
                           row(norm_ffn_g[1]), sh_f, sc_f, bf(w_ffn_in[1]), seq)
    x2d = _proj_residual(act, bf(w_ffn_down[1]), x2d, g_f, seq)
    return x2d.reshape(batch, seq, d)
```

```python
import functools

import jax
import jax.numpy as jnp
from jax import lax
from jax.experimental import pallas as pl
from jax.experimental.pallas import tpu as pltpu

D_MODEL = 1024
N_HEADS = 16
HEAD_DIM = D_MODEL // N_HEADS
EPS = 1e-6

LANES_V7X = 128
SUBLANES_V7X = 8
VMEM_BYTES_V7X = 64 * 1024 * 1024

HEADS_PER_LANE_TILE = LANES_V7X // HEAD_DIM
N_HEAD_PAIRS = N_HEADS // HEADS_PER_LANE_TILE

MXU_WIDTH_V7X = 256

TM = 512
COL_CHUNK = 2 * MXU_WIDTH_V7X
TQ = 512
TK = 512
FOX_SUB = 4
FOX_STEP_GROUPS = (16, 8, 4, 2)
SB_T = 128
SB_LEFT = 256
SB_SUB = 16

ZERO_EXP = 104.5
SAFE_EXP_RANGE = 80.0
NEG_BIG = -1e30
LOG2_E = 1.4426950408889634
SOFTPLUS_LINEAR = 64.0

F32 = jnp.float32
BF16 = jnp.bfloat16


def _vmem_limit(block_bytes):
    want = 2 * block_bytes + 16 * 1024 * 1024
    return int(min(want, VMEM_BYTES_V7X - 8 * 1024 * 1024))


def _dot(a, b):
    return jnp.dot(a, b, preferred_element_type=F32)


def _dot_nt(a, b):
    return lax.dot_general(a, b, (((1,), (1,)), ((), ())), preferred_element_type=F32)


def _ada_kernel(ct_ref, w_ref, b_ref, o_ref, *, batch):
    c = ct_ref[...]
    c_act = c * jax.nn.sigmoid(c)
    w = w_ref[...]
    o_ref[...] = jnp.zeros_like(o_ref)
    for b in range(batch):
        o_ref[b:b + 1, :] = jnp.sum(w * c_act[:, b:b + 1], axis=0, keepdims=True) + b_ref[...]


def _ada_mod(c, w, b):
    n_layers, d, n = w.shape
    batch = c.shape[0]
    rows = SUBLANES_V7X
    assert batch <= rows
    tn = 1024
    c_t = jnp.zeros((d, rows), F32).at[:, :batch].set(c.astype(F32).T)
    block_bytes = 4 * (d * tn + d * LANES_V7X + tn + rows * tn)
    return pl.pallas_call(
        functools.partial(_ada_kernel, batch=batch),
        out_shape=jax.ShapeDtypeStruct((n_layers, rows, n), F32),
        grid=(n_layers, n // tn),
        in_specs=[
            pl.BlockSpec((d, rows), lambda l, j: (0, 0)),
            pl.BlockSpec((None, d, tn), lambda l, j: (l, 0, j)),
            pl.BlockSpec((None, 1, tn), lambda l, j: (l, 0, j)),
        ],
        out_specs=pl.BlockSpec((None, rows, tn), lambda l, j: (l, 0, j)),
        compiler_params=pltpu.CompilerParams(
            dimension_semantics=("arbitrary", "arbitrary"),
            vmem_limit_bytes=_vmem_limit(block_bytes)),
        name="ada_mod",
    )(c_t, w, b.reshape(n_layers, 1, n))


def _norm_modulate(x, g, shift, scale):
    ms = jnp.mean(x * x, axis=-1, keepdims=True)
    y = x * lax.rsqrt(ms + EPS)
    return (y * g) * (1.0 + scale) + shift


def _row_spec(d):
    return pl.BlockSpec((1, d), lambda *_: (0, 0))


def _resident_spec(shape, block_index=(0, 0)):
    return pl.BlockSpec(shape, lambda *_: block_index)


def _batch_vec_spec(d, tiles_per_batch):
    return pl.BlockSpec((None, 1, d), lambda i, *_: (i // tiles_per_batch, 0, 0))


def _column_chunks(n):
    return [(c, min(c + COL_CHUNK, n)) for c in range(0, n, COL_CHUNK)]


def _norm_matmul_kernel(x_ref, g_ref, sh_ref, sc_ref, w_ref, cs_ref, o_ref):
    h = _norm_modulate(x_ref[...], g_ref[...], sh_ref[...], sc_ref[...]).astype(BF16)
    for c0, c1 in _column_chunks(o_ref.shape[1]):
        o_ref[:, c0:c1] = (_dot(h, w_ref[:, c0:c1]) * cs_ref[:, c0:c1]).astype(o_ref.dtype)


def _norm_matmul(x2d, g, shift, scale, w, col_scale, seq):
    m, d = x2d.shape
    n = w.shape[1]
    block_bytes = 4 * TM * d + 2 * d * n + 2 * TM * n + 4 * n
    return pl.pallas_call(
        _norm_matmul_kernel,
        out_shape=jax.ShapeDtypeStruct((m, n), BF16),
        grid=(m // TM,),
        in_specs=[
            pl.BlockSpec((TM, d), lambda i: (i, 0)),
            _row_spec(d),
            _batch_vec_spec(d, seq // TM),
            _batch_vec_spec(d, seq // TM),
            _resident_spec((d, n)),
            pl.BlockSpec((1, n), lambda i: (0, 0)),
        ],
        out_specs=pl.BlockSpec((TM, n), lambda i: (i, 0)),
        compiler_params=pltpu.CompilerParams(
            dimension_semantics=("arbitrary",),
            vmem_limit_bytes=_vmem_limit(block_bytes)),
        name="norm_matmul",
    )(x2d, g, shift, scale, w, col_scale)


def _out_ffn_in_kernel(a_ref, wo_ref, x_ref, gate_ref, g_ref, sh_ref, sc_ref, wg_ref, wu_ref,
                       x1_ref, act_ref):
    x1 = x_ref[...] + gate_ref[...] * _dot(a_ref[...], wo_ref[...])
    x1_ref[...] = x1
    h = _norm_modulate(x1, g_ref[...], sh_ref[...], sc_ref[...]).astype(BF16)
    for c0, c1 in _column_chunks(act_ref.shape[1]):
        gate = _dot(h, wg_ref[:, c0:c1])
        up = _dot(h, wu_ref[:, c0:c1])
        act_ref[:, c0:c1] = (gate * jax.nn.sigmoid(gate) * up).astype(act_ref.dtype)


def _out_ffn_in(a, w_o, x2d, gate, g, shift, scale, w_in, seq):
    m, d = x2d.shape
    f = w_in.shape[1] // 2
    tpb = seq // TM
    block_bytes = 2 * TM * d + 2 * d * d + 2 * 4 * TM * d + 2 * 2 * d * f + 2 * TM * f
    return pl.pallas_call(
        _out_ffn_in_kernel,
        out_shape=(jax.ShapeDtypeStruct((m, d), F32), jax.ShapeDtypeStruct((m, f), BF16)),
        grid=(m // TM,),
        in_specs=[
            pl.BlockSpec((TM, d), lambda i: (i, 0)),
            _resident_spec((d, d)),
            pl.BlockSpec((TM, d), lambda i: (i, 0)),
            _batch_vec_spec(d, tpb),
            _row_spec(d),
            _batch_vec_spec(d, tpb),
            _batch_vec_spec(d, tpb),
            _resident_spec((d, f)),
            _resident_spec((d, f), (0, 1)),
        ],
        out_specs=(pl.BlockSpec((TM, d), lambda i: (i, 0)), pl.BlockSpec((TM, f), lambda i: (i, 0))),
        compiler_params=pltpu.CompilerParams(
            dimension_semantics=("arbitrary",),
            vmem_limit_bytes=_vmem_limit(block_bytes)),
        name="out_ffn_in",
    )(a, w_o, x2d, gate, g, shift, scale, w_in, w_in)


def _proj_residual_kernel(a_ref, w_ref, x_ref, gate_ref, o_ref):
    o_ref[...] = x_ref[...] + gate_ref[...] * _dot(a_ref[...], w_ref[...])


def _proj_residual(a, w, x2d, gate, seq):
    m, k = a.shape
    d = w.shape[1]
    block_bytes = 2 * TM * k + 2 * k * d + 2 * 4 * TM * d
    return pl.pallas_call(
        _proj_residual_kernel,
        out_shape=jax.ShapeDtypeStruct((m, d), F32),
        grid=(m // TM,),
        in_specs=[
            pl.BlockSpec((TM, k), lambda i: (i, 0)),
            _resident_spec((k, d)),
            pl.BlockSpec((TM, d), lambda i: (i, 0)),
            _batch_vec_spec(d, seq // TM),
        ],
        out_specs=pl.BlockSpec((TM, d), lambda i: (i, 0)),
        compiler_params=pltpu.CompilerParams(
            dimension_semantics=("arbitrary",),
            vmem_limit_bytes=_vmem_limit(block_bytes)),
        name="proj_residual",
    )(a, w, x2d, gate)


def _head_rms_norm(t, gain, group_mean):
    ms = _dot((t * t).astype(BF16), group_mean)
    return t * lax.rsqrt(ms + EPS) * gain


def _l1_proj_kernel(kb_ref, act_ref, wd_ref, x_ref, gate_ref,
                    gkv_ref, shkv_ref, sckv_ref, gq_ref, shq_ref, scq_ref,
                    wk_ref, wv_ref, wf_ref, wq_ref, kg_ref, qg_ref, bf_ref, gm_ref,
                    x2_ref, v_ref, qa_ref, ka_ref, stats_ref,
                    q_ref, k_ref, carry_scr, *, tiles_per_sequence):
    x = x_ref[...] + gate_ref[...] * _dot(act_ref[...], wd_ref[...])
    x2_ref[...] = x
    ms = jnp.mean(x * x, axis=-1, keepdims=True)
    y = x * lax.rsqrt(ms + EPS)
    h_kv = ((y * gkv_ref[...]) * (1.0 + sckv_ref[...]) + shkv_ref[...]).astype(BF16)
    h_q = ((y * gq_ref[...]) * (1.0 + scq_ref[...]) + shq_ref[...]).astype(BF16)
    gm = gm_ref[...]
    k_all = _dot(h_kv, wk_ref[...])
    q_all = _dot(h_q, wq_ref[...])
    v_ref[...] = _dot(h_kv, wv_ref[...]).astype(BF16)
    log_f = jax.nn.log_sigmoid(_dot(h_kv, wf_ref[...]) + bf_ref[...])
    for c0 in range(0, k_ref.shape[1], MXU_WIDTH_V7X):
        cols = slice(c0, c0 + MXU_WIDTH_V7X)
        k_ref[:, cols] = _head_rms_norm(k_all[:, cols], kg_ref[:, cols], gm).astype(BF16)
        q = _head_rms_norm(q_all[:, cols], qg_ref[:, cols], gm)
        q_ref[:, cols] = (q * (HEAD_DIM ** -0.5)).astype(BF16)
    _fox_augment(pl.program_id(0) % tiles_per_sequence == 0, kb_ref[0, 0], log_f,
                 q_ref, k_ref, qa_ref, ka_ref, stats_ref, carry_scr)


def _l1_proj(act, w_down, x2d, gate, g_kv, sh_kv, sc_kv, g_q, sh_q, sc_q, wk, wv, wf, wq,
             k_gain, q_gain, b_f, key_norm_bound, batch, seq):
    assert TM == TK == TQ
    m, d = x2d.shape
    f = act.shape[1]
    lane = jnp.arange(MXU_WIDTH_V7X)
    group_mean = jnp.where(lane[:, None] // HEAD_DIM == lane[None, :] // HEAD_DIM,
                           1.0 / HEAD_DIM, 0.0).astype(BF16)
    tpb = seq // TM
    block_bytes = (2 * TM * f + 2 * f * d + 2 * 4 * TM * d + 3 * 2 * d * d + 2 * d * LANES_V7X
                   + 2 * TM * d + 2 * 2 * N_HEADS * TM * LANES_V7X + 2 * 2 * TM * d)
    full = lambda r, c: _resident_spec((r, c))
    rows = lambda c: pl.BlockSpec((TM, c), lambda i: (i, 0))
    aug_shape = jax.ShapeDtypeStruct((batch, N_HEADS, seq, LANES_V7X), BF16)
    aug_spec = pl.BlockSpec((None, N_HEADS, TM, LANES_V7X), lambda i: (i // tpb, 0, i % tpb, 0))
    return pl.pallas_call(
        functools.partial(_l1_proj_kernel, tiles_per_sequence=tpb),
        out_shape=(jax.ShapeDtypeStruct((m, d), F32), jax.ShapeDtypeStruct((m, d), BF16),
                   aug_shape, aug_shape,
                   jax.ShapeDtypeStruct((batch, tpb, SUBLANES_V7X, LANES_V7X), F32)),
        grid=(m // TM,),
        in_specs=[
            pl.BlockSpec(memory_space=pltpu.SMEM),
            rows(f), full(f, d), rows(d), _batch_vec_spec(d, tpb),
            _row_spec(d), _batch_vec_spec(d, tpb), _batch_vec_spec(d, tpb),
            _row_spec(d), _batch_vec_spec(d, tpb), _batch_vec_spec(d, tpb),
            full(d, d), full(d, d), full(d, LANES_V7X), full(d, d),
            _row_spec(d), _row_spec(d), _row_spec(LANES_V7X), full(MXU_WIDTH_V7X, MXU_WIDTH_V7X),
        ],
        out_specs=(rows(d), rows(d), aug_spec, aug_spec,
                   pl.BlockSpec((None, None, SUBLANES_V7X, LANES_V7X),
                                lambda i: (i // tpb, i % tpb, 0, 0))),
        scratch_shapes=[pltpu.VMEM((TM, d), BF16), pltpu.VMEM((TM, d), BF16),
                        pltpu.VMEM((SUBLANES_V7X, LANES_V7X), F32)],
        compiler_params=pltpu.CompilerParams(
            dimension_semantics=("arbitrary",),
            vmem_limit_bytes=_vmem_limit(block_bytes)),
        name="l1_proj",
    )(key_norm_bound, act, w_down, x2d, gate, g_kv, sh_kv, sc_kv, g_q, sh_q, sc_q, wk, wv, wf, wq,
      k_gain, q_gain, b_f, group_mean)


N_SPLIT = 3


def _split3(x):
    x0 = x.astype(BF16)
    r1 = x - x0.astype(F32)
    x1 = r1.astype(BF16)
    x2 = (r1 - x1.astype(F32)).astype(BF16)
    return x0, x1, x2


def _fox_augment(first_tile, kb, log_f, q_ref, k_ref, qa_ref, ka_ref, stats_ref, carry_scr):
    @pl.when(first_tile)
    def _():
        carry_scr[...] = jnp.zeros_like(carry_scr)

    r = lax.broadcasted_iota(jnp.int32, (TK, TK), 0)
    c = lax.broadcasted_iota(jnp.int32, (TK, TK), 1)
    tri = jnp.where(c <= r, 1.0, 0.0).astype(BF16)
    x0, x1, x2 = _split3(log_f)
    cs = _dot(tri, x0) + _dot(tri, x1) + _dot(tri, x2) + carry_scr[0:1, :]
    carry_scr[...] = jnp.broadcast_to(cs[TK - 1:TK, :], carry_scr.shape)

    d_idx = lax.broadcasted_iota(jnp.int32, (D_MODEL, LANES_V7X), 0)
    h_idx = lax.broadcasted_iota(jnp.int32, (D_MODEL, LANES_V7X), 1)
    head_of = jnp.where(d_idx // HEAD_DIM == h_idx, 1.0, 0.0).astype(BF16)
    qf = q_ref[...].astype(F32)
    reach = jnp.sqrt(_dot((qf * qf).astype(BF16), head_of)) * kb
    own_logit = _dot((qf * k_ref[...].astype(F32)).astype(BF16), head_of)

    stats_ref[...] = jnp.concatenate([
        jnp.max(reach * (1.0 + 2.0 ** -8) - own_logit + cs, axis=0, keepdims=True),
        cs[0:1, :],
        jnp.max(reach, axis=0, keepdims=True),
        jnp.zeros((SUBLANES_V7X - 3, LANES_V7X), F32)], axis=0)

    lane = lax.broadcasted_iota(jnp.int32, (1, LANES_V7X), 1)
    packed = jnp.zeros((TK, LANES_V7X), F32)
    for i, term in enumerate(_split3(cs - reach) + _split3(-cs)):
        term = jnp.where(lane < N_HEADS, term.astype(F32), 0.0)
        packed = packed + (term if i == 0 else pltpu.roll(term, N_HEADS * i, axis=1))
    packed = packed.astype(BF16)

    src = lax.broadcasted_iota(jnp.int32, (LANES_V7X, 2 * LANES_V7X), 0)
    dst = lax.broadcasted_iota(jnp.int32, (LANES_V7X, 2 * LANES_V7X), 1)

    for h in range(N_HEADS):
        pair, parity = divmod(h, HEADS_PER_LANE_TILE)
        spare = HEAD_DIM * (1 - parity)
        place = jnp.where((src % N_HEADS == h) & (src < 2 * N_SPLIT * N_HEADS)
                          & (dst % LANES_V7X == spare + src // N_HEADS)
                          & ((dst >= LANES_V7X) == (src >= N_SPLIT * N_HEADS)), 1.0, 0.0)
        placed = _dot(packed, place.astype(BF16))
        ones_q = jnp.where((lane >= spare + N_SPLIT) & (lane < spare + 2 * N_SPLIT), 1.0, 0.0)
        ones_k = jnp.where((lane >= spare) & (lane < spare + N_SPLIT), 1.0, 0.0)
        aug_q = (placed[:, :LANES_V7X] + ones_q).astype(BF16)
        aug_k = (placed[:, LANES_V7X:] + ones_k).astype(BF16)
        own = (lane // HEAD_DIM) == parity
        cols = slice(pair * LANES_V7X, (pair + 1) * LANES_V7X)
        qa_ref[h] = jnp.where(own, q_ref[:, cols], aug_q)
        ka_ref[h] = jnp.where(own, k_ref[:, cols], aug_k)


def _head_select(head):
    lane = lax.broadcasted_iota(jnp.int32, (1, LANES_V7X), 1)
    return (lane // HEAD_DIM) == head


def _sb_kernel(q_ref, k_ref, v_ref, o_ref):
    i = pl.program_id(2)

    def strictly_lower(n):
        return (lax.broadcasted_iota(jnp.int32, (n, n), 0)
                > lax.broadcasted_iota(jnp.int32, (n, n), 1))

    diag_bias = jnp.where(strictly_lower(SB_T), 0.0, NEG_BIG)
    after = jnp.where(strictly_lower(SB_T), 1.0, 0.0).astype(BF16)
    after_left = jnp.where(strictly_lower(SB_LEFT), 1.0, 0.0).astype(BF16)
    ones = jnp.ones((SB_T, LANES_V7X), BF16)
    ones_left = jnp.ones((SB_T, SB_LEFT), BF16)

    def softplus(z):
        return jnp.maximum(jnp.log2(1.0 + jnp.exp2(jnp.minimum(z, SOFTPLUS_LINEAR))), z)

    def keys(start, size):
        rows = pl.ds(pl.multiple_of(start, SB_T), size)
        return k_ref[rows, :], v_ref[rows, :]

    def left_start(g):
        return jnp.maximum(g * SB_T - SB_LEFT, 0)

    def band_logits(qh, g, may_clip):
        k_d, _ = keys(g * SB_T, SB_T)
        k_p, _ = keys(left_start(g), SB_LEFT)
        z_p = _dot_nt(qh, k_p)
        if may_clip:
            key = left_start(g) + lax.broadcasted_iota(jnp.int32, (SB_T, SB_LEFT), 1)
            z_p = jnp.where(key < g * SB_T, z_p, NEG_BIG)
        return _dot_nt(qh, k_d) + diag_bias, z_p

    def band_softplus(z_d, z_p):
        return z_d, z_p, softplus(z_d), softplus(z_p)

    def band_sums(z_d, z_p, sp_d, sp_p):
        sp_d16 = sp_d.astype(BF16)
        within_d = _dot(sp_d16, after)
        within_p = _dot(sp_p.astype(BF16), after_left) + _dot(sp_d16, ones_left)
        total = (within_p + sp_p)[:, 0:1]
        return (z_d - sp_d) - within_d, (z_p - sp_p) - within_p, total

    def band_weights(e_d, e_p, total):
        return jnp.exp2(e_d).astype(BF16), jnp.exp2(e_p).astype(BF16), total

    def band_out(g, a_d, a_p, total):
        _, v_d = keys(g * SB_T, SB_T)
        _, v_p = keys(left_start(g), SB_LEFT)
        return jnp.broadcast_to(-total, (SB_T, LANES_V7X)), _dot(a_d, v_d) + _dot(a_p, v_p)

    def block(qh, start, carry, acc):
        kj, vj = keys(start, SB_T)
        z = _dot_nt(qh, kj)
        sp = softplus(z)
        sp16 = sp.astype(BF16)
        within = _dot(sp16, after)
        a = jnp.exp2((z - sp) - within + carry)
        acc = acc + _dot(a.astype(BF16), vj)
        carry = carry - _dot(sp16, ones)
        return carry, acc

    heads = []
    for u in range(SB_SUB):
        g = i * SB_SUB + u
        q = q_ref[u * SB_T:(u + 1) * SB_T, :]
        for head in range(HEADS_PER_LANE_TILE):
            heads.append((g, jnp.where(_head_select(head), q, 0.0).astype(BF16)))
    logits = [band_logits(qh, g, may_clip=c // HEADS_PER_LANE_TILE < SB_LEFT // SB_T)
              for c, (g, qh) in enumerate(heads)]
    softplus_terms = [band_softplus(*z) for z in logits]
    sums = [band_sums(*t) for t in softplus_terms]
    weights = [band_weights(*s) for s in sums]
    chains = [(g, qh) + band_out(g, *w) for (g, qh), w in zip(heads, weights)]

    def is_open(carry):
        return jnp.max(carry) > -ZERO_EXP * LOG2_E

    chain_open = [is_open(carry) for _, _, carry, _ in chains]

    def walk_on():
        outs = []
        for (g, qh, carry, acc), this_open in zip(chains, chain_open):
            def cond(state):
                start, carry, _ = state
                return jnp.logical_and(start >= 0, is_open(carry))

            def body(state, qh=qh):
                start, carry, acc = state
                carry, acc = block(qh, start, carry, acc)
                return start - SB_T, carry, acc

            def walk(cond=cond, body=body, first=(left_start(g) - SB_T, carry, acc)):
                return lax.while_loop(cond, body, first)[2]

            outs.append(lax.cond(this_open, walk, lambda acc=acc: acc))
        return outs

    outs = lax.cond(functools.reduce(jnp.logical_or, chain_open), walk_on,
                    lambda: [acc for _, _, _, acc in chains])
    for u in range(SB_SUB):
        pair = outs[u * HEADS_PER_LANE_TILE:(u + 1) * HEADS_PER_LANE_TILE]
        o_ref[u * SB_T:(u + 1) * SB_T, :] = jnp.where(_head_select(0), pair[0], pair[1]).astype(o_ref.dtype)


def _sb_attn(qkv, batch, seq):
    assert SB_T == LANES_V7X and SB_LEFT % SB_T == 0
    rows = SB_SUB * SB_T
    stage_bytes = SB_SUB * HEADS_PER_LANE_TILE * 2 * 4 * SB_T * (SB_T + SB_LEFT)
    block_bytes = 2 * (2 * rows * LANES_V7X + 2 * seq * LANES_V7X) + stage_bytes // 2
    return pl.pallas_call(
        _sb_kernel,
        out_shape=jax.ShapeDtypeStruct((batch, seq, D_MODEL), BF16),
        grid=(batch, N_HEAD_PAIRS, seq // rows),
        in_specs=[
            pl.BlockSpec((None, rows, LANES_V7X), lambda b, p, i: (b, i, p)),
            pl.BlockSpec((None, seq, LANES_V7X), lambda b, p, i: (b, 0, N_HEAD_PAIRS + p)),
            pl.BlockSpec((None, seq, LANES_V7X), lambda b, p, i: (b, 0, 2 * N_HEAD_PAIRS + p)),
        ],
        out_specs=pl.BlockSpec((None, rows, LANES_V7X), lambda b, p, i: (b, i, p)),
        compiler_params=pltpu.CompilerParams(
            dimension_semantics=("arbitrary", "arbitrary", "arbitrary"),
            vmem_limit_bytes=_vmem_limit(block_bytes)),
        name="sb_attn",
    )(qkv, qkv, qkv)


def _fox_kernel(a_ref, fk0_ref, g_ref, qa_ref, ka_ref, v_ref, o_ref, acc_scr, vt_scr):
    b = pl.program_id(0)
    p_id = pl.program_id(1)
    i = pl.program_id(2)
    n_blocks = fk0_ref.shape[1]
    row = lax.broadcasted_iota(jnp.int32, (TQ, TK), 0)
    col = lax.broadcasted_iota(jnp.int32, (TQ, TK), 1)
    diag_mask = col <= row

    acc0 = jnp.zeros((TQ, LANES_V7X), F32)
    chains = range(FOX_SUB * HEADS_PER_LANE_TILE)
    bh0 = (b * N_HEAD_PAIRS + p_id) * HEADS_PER_LANE_TILE

    def scores(chain, j):
        sub, head = chain // HEADS_PER_LANE_TILE, chain % HEADS_PER_LANE_TILE
        q = qa_ref[head, pl.ds(pl.multiple_of(sub * TQ, TQ), TQ), :]
        return _dot_nt(q, ka_ref[head, pl.ds(pl.multiple_of(j * TK, TK), TK), :])

    def values(chain, j):
        vj = v_ref[pl.ds(pl.multiple_of(j * TK, TK), TK), :]
        own = _head_select(chain % HEADS_PER_LANE_TILE)
        return jnp.where(own, vj, jnp.ones_like(vj))

    def diag_block(chain):
        return i * FOX_SUB + chain // HEADS_PER_LANE_TILE

    def count_left(chain):
        bh, qi = bh0 + chain % HEADS_PER_LANE_TILE, diag_block(chain)
        bound = a_ref[bh, qi]

        def more(n):
            j = qi - 1 - n
            right = fk0_ref[bh, jnp.minimum(j + 1, n_blocks - 1)]
            return jnp.logical_and(j >= 0, bound - right > -ZERO_EXP)

        return lax.while_loop(more, lambda n: n + 1, jnp.int32(0))

    def fast_walks():
        counts = [count_left(chain) for chain in chains]
        firsts = [sum(n + 1 for n in counts[:chain]) for chain in chains]
        n_items = firsts[-1] + counts[-1] + 1
        diag_bias_t = jnp.where(row <= col, 0.0, NEG_BIG)
        feature = lax.broadcasted_iota(jnp.int32, (LANES_V7X, 1), 0)
        acc_scr[...] = jnp.zeros_like(acc_scr)

        def scores_t(chain, j):
            sub, head = chain // HEADS_PER_LANE_TILE, chain % HEADS_PER_LANE_TILE
            q = qa_ref[head, pl.ds(pl.multiple_of(sub * TQ, TQ), TQ), :]
            return _dot_nt(ka_ref[head, pl.ds(pl.multiple_of(j * TK, TK), TK), :], q)

        def values_t(chain, j):
            vt = vt_scr[:, pl.ds(pl.multiple_of(j * TK, TK), TK)]
            own = (feature // HEAD_DIM) == chain % HEADS_PER_LANE_TILE
            return jnp.where(own, vt, jnp.ones_like(vt))

        def step(t, p_t, p_chain, p_block):
            chain = sum(jnp.where(t >= first, 1, 0) for first in firsts[1:])
            first = functools.reduce(lambda f, nxt: jnp.where(t >= nxt, nxt, f), firsts[1:], 0)
            back = t - first
            block = jnp.maximum(diag_block(chain) - back, 0)
            s_t = scores_t(chain, block) + jnp.where(back == 0, diag_bias_t,
                                                     jnp.where(t >= n_items, NEG_BIG, 0.0))
            acc_scr[p_chain] += _dot(values_t(p_chain, p_block), p_t)
            return jnp.exp(s_t).astype(BF16), chain, block

        def steps(first, count, carried):
            for t in range(count):
                carried = step(first + t, *carried)
            return carried

        carried = (jnp.zeros((TK, TQ), BF16), jnp.int32(0), jnp.int32(0))
        done, left = 0, 2 * ((n_items + 2) // 2)
        for size in FOX_STEP_GROUPS:
            groups = left // size
            carried = lax.fori_loop(
                0, groups, lambda u, c, size=size, done=done: steps(done + size * u, size, c), carried)
            done, left = done + size * groups, left - size * groups

        outs = []
        for sub in range(FOX_SUB):
            normed = []
            for head in range(HEADS_PER_LANE_TILE):
                acc_t = acc_scr[sub * HEADS_PER_LANE_TILE + head]
                spare = HEAD_DIM * (1 - head)
                normed.append(acc_t / acc_t[spare:spare + 1, :])
            outs.append(jnp.where(feature // HEAD_DIM == 0, normed[0], normed[1]).T)
        return outs

    def online_walks():
        accs = []
        for chain in chains:
            def body(t, carried, chain=chain):
                acc, m = carried
                block = diag_block(chain) - t
                s = jnp.where(jnp.logical_or(t > 0, diag_mask), scores(chain, block), NEG_BIG)
                m_new = jnp.maximum(m, jnp.max(s, axis=-1, keepdims=True))
                p = jnp.exp(s - m_new).astype(BF16)
                return jnp.exp(m - m_new) * acc + _dot(p, values(chain, block)), m_new

            m0 = jnp.full((TQ, 1), NEG_BIG, F32)
            acc = lax.fori_loop(0, count_left(chain) + 1, body, (acc0, m0))[0]
            accs.append(acc / pltpu.roll(acc, HEAD_DIM, axis=1))
        return [jnp.where(_head_select(0), accs[sub * HEADS_PER_LANE_TILE],
                          accs[sub * HEADS_PER_LANE_TILE + 1]) for sub in range(FOX_SUB)]

    @pl.when(i == 0)
    def _():
        for c in range(0, vt_scr.shape[1], TK):
            vt_scr[:, c:c + TK] = v_ref[c:c + TK, :].T

    largest_reach = functools.reduce(jnp.maximum, [
        g_ref[bh0 + chain % HEADS_PER_LANE_TILE, diag_block(chain)] for chain in chains])
    outs = lax.cond(2.0 * largest_reach < SAFE_EXP_RANGE, fast_walks, online_walks)
    for sub in range(FOX_SUB):
        o_ref[sub * TQ:(sub + 1) * TQ, :] = outs[sub].astype(o_ref.dtype)


def _fox_attn(q_aug, k_aug, v, walk_bound, fk_block_start, max_reach, batch, seq):
    assert TQ == TK
    rows = FOX_SUB * TQ
    n_chains = FOX_SUB * HEADS_PER_LANE_TILE
    block_bytes = (2 * (HEADS_PER_LANE_TILE * (rows + seq) * LANES_V7X + seq * LANES_V7X + rows * LANES_V7X)
                   + 4 * n_chains * TQ * LANES_V7X + 2 * seq * LANES_V7X)
    smem = pl.BlockSpec(memory_space=pltpu.SMEM)
    return pl.pallas_call(
        _fox_kernel,
        out_shape=jax.ShapeDtypeStruct((batch, seq, D_MODEL), BF16),
        grid=(batch, N_HEAD_PAIRS, seq // rows),
        in_specs=[
            smem, smem, smem,
            pl.BlockSpec((None, HEADS_PER_LANE_TILE, rows, LANES_V7X), lambda b, p, i: (b, p, i, 0)),
            pl.BlockSpec((None, HEADS_PER_LANE_TILE, seq, LANES_V7X), lambda b, p, i: (b, p, 0, 0)),
            pl.BlockSpec((None, seq, LANES_V7X), lambda b, p, i: (b, 0, p)),
        ],
        out_specs=pl.BlockSpec((None, rows, LANES_V7X), lambda b, p, i: (b, i, p)),
        scratch_shapes=[pltpu.VMEM((n_chains, LANES_V7X, TQ), F32),
                        pltpu.VMEM((LANES_V7X, seq), BF16)],
        compiler_params=pltpu.CompilerParams(
            dimension_semantics=("arbitrary", "arbitrary", "arbitrary"),
            vmem_limit_bytes=_vmem_limit(block_bytes)),
        name="fox_attn",
    )(walk_bound, fk_block_start, max_reach, q_aug, k_aug, v)


def kernel(x, c, ada_w, ada_b, norm_attn_g, norm_ffn_g, w_ffn_in, w_ffn_down, sb_w_qkv, sb_w_o,
           kv_ada_w, kv_ada_b, kv_norm_g, w_kvf, b_f, k_norm_g, fox_w_q, q_norm_g, fox_w_o):
    batch, seq, d = x.shape
    assert d == D_MODEL and seq % TM == 0
    assert seq % (FOX_SUB * TQ) == 0 and seq % (SB_SUB * SB_T) == 0
    assert ada_w.shape[0] == 2 and sb_w_qkv.shape[0] == 1 and fox_w_q.shape[0] == 1
    m = batch * seq

    mod = _ada_mod(c, ada_w, ada_b)[:, :batch]
    kv_mod = _ada_mod(c, kv_ada_w[None], kv_ada_b[None])[0, :batch]

    def vecs(t, n):
        return [t[:, None, j * d:(j + 1) * d] for j in range(n)]

    row = lambda t: t.reshape(1, -1).astype(F32)
    bf = lambda t: t.astype(BF16)

    x2d = x.reshape(m, d)

    sh_a, sc_a, g_a, sh_f, sc_f, g_f = vecs(mod[0], 6)
    qkv_scale = jnp.concatenate([jnp.full((1, d), LOG2_E * HEAD_DIM ** -0.5, F32),
                                 jnp.ones((1, 2 * d), F32)], axis=1)
    qkv = _norm_matmul(x2d, row(norm_attn_g[0]), sh_a, sc_a, bf(sb_w_qkv[0]), qkv_scale, seq)
    o = _sb_attn(qkv.reshape(batch, seq, 3 * d), batch, seq)
    x2d, act = _out_ffn_in(o.reshape(m, d), bf(sb_w_o[0]), x2d, g_a,
                           row(norm_ffn_g[0]), sh_f, sc_f, bf(w_ffn_in[0]), seq)
    g_f0 = g_f

    sh_a, sc_a, g_a, sh_f, sc_f, g_f = vecs(mod[1], 6)
    kv_shift, kv_scale = vecs(kv_mod, 2)
    w_f_pad = jnp.zeros((d, LANES_V7X), F32).at[:, :N_HEADS].set(w_kvf[:, 2 * d:])
    b_f_pad = jnp.zeros((1, LANES_V7X), F32).at[0, :N_HEADS].set(b_f)
    key_norm_bound = (1.01 * HEAD_DIM ** 0.5 * jnp.max(jnp.abs(k_norm_g))).reshape(1, 1).astype(F32)
    x2d, v, q_aug, k_aug, stats = _l1_proj(
        act, bf(w_ffn_down[0]), x2d, g_f0,
        row(kv_norm_g), kv_shift, kv_scale, row(norm_attn_g[1]), sh_a, sc_a,
        bf(w_kvf[:, :d]), bf(w_kvf[:, d:2 * d]), bf(w_f_pad), bf(fox_w_q[0]),
        row(jnp.tile(k_norm_g, N_HEADS)), row(jnp.tile(q_norm_g[0], N_HEADS)), b_f_pad,
        key_norm_bound, batch, seq)
    tables = stats[:, :, :3, :N_HEADS].transpose(2, 0, 3, 1).reshape(3, batch * N_HEADS, seq // TK)
    o = _fox_attn(q_aug, k_aug, v.reshape(batch, seq, d), tables[0], tables[1], tables[2], batch, seq)
    x2d, act = _out_ffn_in(o.reshape(m, d), bf(fox_w_o[0]), x2d, g_a,
                           row(norm_ffn_g[1]), sh_f, sc_f, bf(w_ffn_in[1]), seq)
    x2d = _proj_residual(act, bf(w_ffn_down[1]), x2d, g_f, seq)
    return x2d.reshape(batch, seq, d)
```

```python
import functools

import jax
import jax.numpy as jnp
from jax import lax
from jax.experimental import pallas as pl
from jax.experimental.pallas import tpu as pltpu

D_MODEL = 1024
N_HEADS = 16
HEAD_DIM = D_MODEL // N_HEADS
EPS = 1e-6

LANES_V7X = 128
SUBLANES_V7X = 8
VMEM_BYTES_V7X = 64 * 1024 * 1024

HEADS_PER_LANE_TILE = LANES_V7X // HEAD_DIM
N_HEAD_PAIRS = N_HEADS // HEADS_PER_LANE_TILE

MXU_WIDTH_V7X = 256

TM = 512
COL_CHUNK = 2 * MXU_WIDTH_V7X
TQ = 512
TK = 512
FOX_SUB = 4
FOX_STEP_GROUPS = (16, 8, 4, 2)
SB_T = 128
SB_LEFT = 256
SB_SUB = 16

ZERO_EXP = 104.5
SAFE_EXP_RANGE = 80.0
NEG_BIG = -1e30
LOG2_E = 1.4426950408889634
SOFTPLUS_LINEAR = 64.0

F32 = jnp.float32
BF16 = jnp.bfloat16


def _vmem_limit(block_bytes):
    want = 2 * block_bytes + 16 * 1024 * 1024
    return int(min(want, VMEM_BYTES_V7X - 8 * 1024 * 1024))


def _dot(a, b):
    return jnp.dot(a, b, preferred_element_type=F32)


def _dot_nt(a, b):
    return lax.dot_general(a, b, (((1,), (1,)), ((), ())), preferred_element_type=F32)


def _ada_kernel(ct_ref, w_ref, b_ref, o_ref, *, batch):
    c = ct_ref[...]
    c_act = c * jax.nn.sigmoid(c)
    w = w_ref[...]
    o_ref[...] = jnp.zeros_like(o_ref)
    for b in range(batch):
        o_ref[b:b + 1, :] = jnp.sum(w * c_act[:, b:b + 1], axis=0, keepdims=True) + b_ref[...]


def _ada_mod(c, w, b):
    n_layers, d, n = w.shape
    batch = c.shape[0]
    rows = SUBLANES_V7X
    assert batch <= rows
    tn = 1024
    c_t = jnp.zeros((d, rows), F32).at[:, :batch].set(c.astype(F32).T)
    block_bytes = 4 * (d * tn + d * LANES_V7X + tn + rows * tn)
    return pl.pallas_call(
        functools.partial(_ada_kernel, batch=batch),
        out_shape=jax.ShapeDtypeStruct((n_layers, rows, n), F32),
        grid=(n_layers, n // tn),
        in_specs=[
            pl.BlockSpec((d, rows), lambda l, j: (0, 0)),
            pl.BlockSpec((None, d, tn), lambda l, j: (l, 0, j)),
            pl.BlockSpec((None, 1, tn), lambda l, j: (l, 0, j)),
        ],
        out_specs=pl.BlockSpec((None, rows, tn), lambda l, j: (l, 0, j)),
        compiler_params=pltpu.CompilerParams(
            dimension_semantics=("arbitrary", "arbitrary"),
            vmem_limit_bytes=_vmem_limit(block_bytes)),
        name="ada_mod",
    )(c_t, w, b.reshape(n_layers, 1, n))


def _norm_modulate(x, g, shift, scale):
    ms = jnp.mean(x * x, axis=-1, keepdims=True)
    y = x * lax.rsqrt(ms + EPS)
    return (y * g) * (1.0 + scale) + shift


def _row_spec(d):
    return pl.BlockSpec((1, d), lambda *_: (0, 0))


def _resident_spec(shape, block_index=(0, 0)):
    return pl.BlockSpec(shape, lambda *_: block_index)


def _batch_vec_spec(d, tiles_per_batch):
    return pl.BlockSpec((None, 1, d), lambda i, *_: (i // tiles_per_batch, 0, 0))


def _column_chunks(n):
    return [(c, min(c + COL_CHUNK, n)) for c in range(0, n, COL_CHUNK)]


def _norm_matmul_kernel(x_ref, g_ref, sh_ref, sc_ref, w_ref, cs_ref, o_ref):
    h = _norm_modulate(x_ref[...], g_ref[...], sh_ref[...], sc_ref[...]).astype(BF16)
    for c0, c1 in _column_chunks(o_ref.shape[1]):
        o_ref[:, c0:c1] = (_dot(h, w_ref[:, c0:c1]) * cs_ref[:, c0:c1]).astype(o_ref.dtype)


def _norm_matmul(x2d, g, shift, scale, w, col_scale, seq):
    m, d = x2d.shape
    n = w.shape[1]
    block_bytes = 4 * TM * d + 2 * d * n + 2 * TM * n + 4 * n
    return pl.pallas_call(
        _norm_matmul_kernel,
        out_shape=jax.ShapeDtypeStruct((m, n), BF16),
        grid=(m // TM,),
        in_specs=[
            pl.BlockSpec((TM, d), lambda i: (i, 0)),
            _row_spec(d),
            _batch_vec_spec(d, seq // TM),
            _batch_vec_spec(d, seq // TM),
            _resident_spec((d, n)),
            pl.BlockSpec((1, n), lambda i: (0, 0)),
        ],
        out_specs=pl.BlockSpec((TM, n), lambda i: (i, 0)),
        compiler_params=pltpu.CompilerParams(
            dimension_semantics=("arbitrary",),
            vmem_limit_bytes=_vmem_limit(block_bytes)),
        name="norm_matmul",
    )(x2d, g, shift, scale, w, col_scale)


def _out_ffn_in_kernel(a_ref, wo_ref, x_ref, gate_ref, g_ref, sh_ref, sc_ref, wg_ref, wu_ref,
                       x1_ref, act_ref):
    x1 = x_ref[...] + gate_ref[...] * _dot(a_ref[...], wo_ref[...])
    x1_ref[...] = x1
    h = _norm_modulate(x1, g_ref[...], sh_ref[...], sc_ref[...]).astype(BF16)
    for c0, c1 in _column_chunks(act_ref.shape[1]):
        gate = _dot(h, wg_ref[:, c0:c1])
        up = _dot(h, wu_ref[:, c0:c1])
        act_ref[:, c0:c1] = (gate * jax.nn.sigmoid(gate) * up).astype(act_ref.dtype)


def _out_ffn_in(a, w_o, x2d, gate, g, shift, scale, w_in, seq):
    m, d = x2d.shape
    f = w_in.shape[1] // 2
    tpb = seq // TM
    block_bytes = 2 * TM * d + 2 * d * d + 2 * 4 * TM * d + 2 * 2 * d * f + 2 * TM * f
    return pl.pallas_call(
        _out_ffn_in_kernel,
        out_shape=(jax.ShapeDtypeStruct((m, d), F32), jax.ShapeDtypeStruct((m, f), BF16)),
        grid=(m // TM,),
        in_specs=[
            pl.BlockSpec((TM, d), lambda i: (i, 0)),
            _resident_spec((d, d)),
            pl.BlockSpec((TM, d), lambda i: (i, 0)),
            _batch_vec_spec(d, tpb),
            _row_spec(d),
            _batch_vec_spec(d, tpb),
            _batch_vec_spec(d, tpb),
            _resident_spec((d, f)),
            _resident_spec((d, f), (0, 1)),
        ],
        out_specs=(pl.BlockSpec((TM, d), lambda i: (i, 0)), pl.BlockSpec((TM, f), lambda i: (i, 0))),
        compiler_params=pltpu.CompilerParams(
            dimension_semantics=("arbitrary",),
            vmem_limit_bytes=_vmem_limit(block_bytes)),
        name="out_ffn_in",
    )(a, w_o, x2d, gate, g, shift, scale, w_in, w_in)


def _out_ffn_kernel(a_ref, wo_ref, x_ref, gate_a_ref, g_ref, sh_ref, sc_ref, wg_ref, wu_ref,
                    wd_ref, gate_f_ref, o_ref):
    x1 = x_ref[...] + gate_a_ref[...] * _dot(a_ref[...], wo_ref[...])
    h = _norm_modulate(x1, g_ref[...], sh_ref[...], sc_ref[...]).astype(BF16)
    down = jnp.zeros_like(x1)
    for c0, c1 in _column_chunks(wg_ref.shape[1]):
        gate = _dot(h, wg_ref[:, c0:c1])
        up = _dot(h, wu_ref[:, c0:c1])
        act = (gate * jax.nn.sigmoid(gate) * up).astype(BF16)
        down = down + _dot(act, wd_ref[c0:c1, :])
    o_ref[...] = x1 + gate_f_ref[...] * down


def _out_ffn(a, w_o, x2d, gate_a, g, shift, scale, w_in, w_down, gate_f, seq):
    m, d = x2d.shape
    f = w_in.shape[1] // 2
    tpb = seq // TM
    block_bytes = 2 * TM * d + 2 * d * d + 2 * 4 * TM * d + 2 * 2 * d * f + 2 * f * d
    rows = pl.BlockSpec((TM, d), lambda i: (i, 0))
    return pl.pallas_call(
        _out_ffn_kernel,
        out_shape=jax.ShapeDtypeStruct((m, d), F32),
        grid=(m // TM,),
        in_specs=[
            rows, _resident_spec((d, d)), rows, _batch_vec_spec(d, tpb),
            _row_spec(d), _batch_vec_spec(d, tpb), _batch_vec_spec(d, tpb),
            _resident_spec((d, f)), _resident_spec((d, f), (0, 1)),
            _resident_spec((f, d)), _batch_vec_spec(d, tpb),
        ],
        out_specs=rows,
        compiler_params=pltpu.CompilerParams(
            dimension_semantics=("arbitrary",),
            vmem_limit_bytes=_vmem_limit(block_bytes)),
        name="out_ffn",
    )(a, w_o, x2d, gate_a, g, shift, scale, w_in, w_in, w_down, gate_f)


def _head_rms_norm(t, gain, group_mean):
    ms = _dot((t * t).astype(BF16), group_mean)
    return t * lax.rsqrt(ms + EPS) * gain


def _l1_proj_kernel(kb_ref, act_ref, wd_ref, x_ref, gate_ref,
                    gkv_ref, shkv_ref, sckv_ref, gq_ref, shq_ref, scq_ref,
                    wk_ref, wv_ref, wf_ref, wq_ref, kg_ref, qg_ref, bf_ref, gm_ref,
                    x2_ref, v_ref, qa_ref, ka_ref, stats_ref,
                    q_ref, k_ref, carry_scr, *, tiles_per_sequence):
    x = x_ref[...] + gate_ref[...] * _dot(act_ref[...], wd_ref[...])
    x2_ref[...] = x
    ms = jnp.mean(x * x, axis=-1, keepdims=True)
    y = x * lax.rsqrt(ms + EPS)
    h_kv = ((y * gkv_ref[...]) * (1.0 + sckv_ref[...]) + shkv_ref[...]).astype(BF16)
    h_q = ((y * gq_ref[...]) * (1.0 + scq_ref[...]) + shq_ref[...]).astype(BF16)
    gm = gm_ref[...]
    k_all = _dot(h_kv, wk_ref[...])
    q_all = _dot(h_q, wq_ref[...])
    v_ref[...] = _dot(h_kv, wv_ref[...]).astype(BF16)
    log_f = jax.nn.log_sigmoid(_dot(h_kv, wf_ref[...]) + bf_ref[...])
    for c0 in range(0, k_ref.shape[1], MXU_WIDTH_V7X):
        cols = slice(c0, c0 + MXU_WIDTH_V7X)
        k_ref[:, cols] = _head_rms_norm(k_all[:, cols], kg_ref[:, cols], gm).astype(BF16)
        q = _head_rms_norm(q_all[:, cols], qg_ref[:, cols], gm)
        q_ref[:, cols] = (q * (HEAD_DIM ** -0.5)).astype(BF16)
    _fox_augment(pl.program_id(0) % tiles_per_sequence == 0, kb_ref[0, 0], log_f,
                 q_ref, k_ref, qa_ref, ka_ref, stats_ref, carry_scr)


def _l1_proj(act, w_down, x2d, gate, g_kv, sh_kv, sc_kv, g_q, sh_q, sc_q, wk, wv, wf, wq,
             k_gain, q_gain, b_f, key_norm_bound, batch, seq):
    assert TM == TK == TQ
    m, d = x2d.shape
    f = act.shape[1]
    lane = jnp.arange(MXU_WIDTH_V7X)
    group_mean = jnp.where(lane[:, None] // HEAD_DIM == lane[None, :] // HEAD_DIM,
                           1.0 / HEAD_DIM, 0.0).astype(BF16)
    tpb = seq // TM
    block_bytes = (2 * TM * f + 2 * f * d + 2 * 4 * TM * d + 3 * 2 * d * d + 2 * d * LANES_V7X
                   + 2 * TM * d + 2 * 2 * N_HEADS * TM * LANES_V7X + 2 * 2 * TM * d)
    full = lambda r, c: _resident_spec((r, c))
    rows = lambda c: pl.BlockSpec((TM, c), lambda i: (i, 0))
    aug_shape = jax.ShapeDtypeStruct((batch, N_HEADS, seq, LANES_V7X), BF16)
    aug_spec = pl.BlockSpec((None, N_HEADS, TM, LANES_V7X), lambda i: (i // tpb, 0, i % tpb, 0))
    return pl.pallas_call(
        functools.partial(_l1_proj_kernel, tiles_per_sequence=tpb),
        out_shape=(jax.ShapeDtypeStruct((m, d), F32), jax.ShapeDtypeStruct((m, d), BF16),
                   aug_shape, aug_shape,
                   jax.ShapeDtypeStruct((batch, tpb, SUBLANES_V7X, LANES_V7X), F32)),
        grid=(m // TM,),
        in_specs=[
            pl.BlockSpec(memory_space=pltpu.SMEM),
            rows(f), full(f, d), rows(d), _batch_vec_spec(d, tpb),
            _row_spec(d), _batch_vec_spec(d, tpb), _batch_vec_spec(d, tpb),
            _row_spec(d), _batch_vec_spec(d, tpb), _batch_vec_spec(d, tpb),
            full(d, d), full(d, d), full(d, LANES_V7X), full(d, d),
            _row_spec(d), _row_spec(d), _row_spec(LANES_V7X), full(MXU_WIDTH_V7X, MXU_WIDTH_V7X),
        ],
        out_specs=(rows(d), rows(d), aug_spec, aug_spec,
                   pl.BlockSpec((None, None, SUBLANES_V7X, LANES_V7X),
                                lambda i: (i // tpb, i % tpb, 0, 0))),
        scratch_shapes=[pltpu.VMEM((TM, d), BF16), pltpu.VMEM((TM, d), BF16),
                        pltpu.VMEM((SUBLANES_V7X, LANES_V7X), F32)],
        compiler_params=pltpu.CompilerParams(
            dimension_semantics=("arbitrary",),
            vmem_limit_bytes=_vmem_limit(block_bytes)),
        name="l1_proj",
    )(key_norm_bound, act, w_down, x2d, gate, g_kv, sh_kv, sc_kv, g_q, sh_q, sc_q, wk, wv, wf, wq,
      k_gain, q_gain, b_f, group_mean)


N_SPLIT = 3


def _split3(x):
    x0 = x.astype(BF16)
    r1 = x - x0.astype(F32)
    x1 = r1.astype(BF16)
    x2 = (r1 - x1.astype(F32)).astype(BF16)
    return x0, x1, x2


def _fox_augment(first_tile, kb, log_f, q_ref, k_ref, qa_ref, ka_ref, stats_ref, carry_scr):
    @pl.when(first_tile)
    def _():
        carry_scr[...] = jnp.zeros_like(carry_scr)

    r = lax.broadcasted_iota(jnp.int32, (TK, TK), 0)
    c = lax.broadcasted_iota(jnp.int32, (TK, TK), 1)
    tri = jnp.where(c <= r, 1.0, 0.0).astype(BF16)
    x0, x1, x2 = _split3(log_f)
    cs = _dot(tri, x0) + _dot(tri, x1) + _dot(tri, x2) + carry_scr[0:1, :]
    carry_scr[...] = jnp.broadcast_to(cs[TK - 1:TK, :], carry_scr.shape)

    d_idx = lax.broadcasted_iota(jnp.int32, (D_MODEL, LANES_V7X), 0)
    h_idx = lax.broadcasted_iota(jnp.int32, (D_MODEL, LANES_V7X), 1)
    head_of = jnp.where(d_idx // HEAD_DIM == h_idx, 1.0, 0.0).astype(BF16)
    qf = q_ref[...].astype(F32)
    reach = jnp.sqrt(_dot((qf * qf).astype(BF16), head_of)) * kb
    own_logit = _dot((qf * k_ref[...].astype(F32)).astype(BF16), head_of)

    stats_ref[...] = jnp.concatenate([
        jnp.max(reach * (1.0 + 2.0 ** -8) - own_logit + cs, axis=0, keepdims=True),
        cs[0:1, :],
        jnp.max(reach, axis=0, keepdims=True),
        jnp.zeros((SUBLANES_V7X - 3, LANES_V7X), F32)], axis=0)

    lane = lax.broadcasted_iota(jnp.int32, (1, LANES_V7X), 1)
    packed = jnp.zeros((TK, LANES_V7X), F32)
    for i, term in enumerate(_split3(cs - reach) + _split3(-cs)):
        term = jnp.where(lane < N_HEADS, term.astype(F32), 0.0)
        packed = packed + (term if i == 0 else pltpu.roll(term, N_HEADS * i, axis=1))
    packed = packed.astype(BF16)

    src = lax.broadcasted_iota(jnp.int32, (LANES_V7X, 2 * LANES_V7X), 0)
    dst = lax.broadcasted_iota(jnp.int32, (LANES_V7X, 2 * LANES_V7X), 1)

    for h in range(N_HEADS):
        pair, parity = divmod(h, HEADS_PER_LANE_TILE)
        spare = HEAD_DIM * (1 - parity)
        place = jnp.where((src % N_HEADS == h) & (src < 2 * N_SPLIT * N_HEADS)
                          & (dst % LANES_V7X == spare + src // N_HEADS)
                          & ((dst >= LANES_V7X) == (src >= N_SPLIT * N_HEADS)), 1.0, 0.0)
        placed = _dot(packed, place.astype(BF16))
        ones_q = jnp.where((lane >= spare + N_SPLIT) & (lane < spare + 2 * N_SPLIT), 1.0, 0.0)
        ones_k = jnp.where((lane >= spare) & (lane < spare + N_SPLIT), 1.0, 0.0)
        aug_q = (placed[:, :LANES_V7X] + ones_q).astype(BF16)
        aug_k = (placed[:, LANES_V7X:] + ones_k).astype(BF16)
        own = (lane // HEAD_DIM) == parity
        cols = slice(pair * LANES_V7X, (pair + 1) * LANES_V7X)
        qa_ref[h] = jnp.where(own, q_ref[:, cols], aug_q)
        ka_ref[h] = jnp.where(own, k_ref[:, cols], aug_k)


def _head_select(head):
    lane = lax.broadcasted_iota(jnp.int32, (1, LANES_V7X), 1)
    return (lane // HEAD_DIM) == head


def _sb_kernel(q_ref, k_ref, v_ref, o_ref):
    i = pl.program_id(2)

    def strictly_lower(n):
        return (lax.broadcasted_iota(jnp.int32, (n, n), 0)
                > lax.broadcasted_iota(jnp.int32, (n, n), 1))

    diag_bias = jnp.where(strictly_lower(SB_T), 0.0, NEG_BIG)
    after = jnp.where(strictly_lower(SB_T), 1.0, 0.0).astype(BF16)
    after_left = jnp.where(strictly_lower(SB_LEFT), 1.0, 0.0).astype(BF16)
    ones = jnp.ones((SB_T, LANES_V7X), BF16)
    ones_left = jnp.ones((SB_T, SB_LEFT), BF16)

    def softplus(z):
        return jnp.maximum(jnp.log2(1.0 + jnp.exp2(jnp.minimum(z, SOFTPLUS_LINEAR))), z)

    def keys(start, size):
        rows = pl.ds(pl.multiple_of(start, SB_T), size)
        return k_ref[rows, :], v_ref[rows, :]

    def left_start(g):
        return jnp.maximum(g * SB_T - SB_LEFT, 0)

    def band_logits(qh, g, may_clip):
        k_d, _ = keys(g * SB_T, SB_T)
        k_p, _ = keys(left_start(g), SB_LEFT)
        z_p = _dot_nt(qh, k_p)
        if may_clip:
            key = left_start(g) + lax.broadcasted_iota(jnp.int32, (SB_T, SB_LEFT), 1)
            z_p = jnp.where(key < g * SB_T, z_p, NEG_BIG)
        return _dot_nt(qh, k_d) + diag_bias, z_p

    def band_softplus(z_d, z_p):
        return z_d, z_p, softplus(z_d), softplus(z_p)

    def band_sums(z_d, z_p, sp_d, sp_p):
        sp_d16 = sp_d.astype(BF16)
        within_d = _dot(sp_d16, after)
        within_p = _dot(sp_p.astype(BF16), after_left) + _dot(sp_d16, ones_left)
        total = (within_p + sp_p)[:, 0:1]
        return (z_d - sp_d) - within_d, (z_p - sp_p) - within_p, total

    def band_weights(e_d, e_p, total):
        return jnp.exp2(e_d).astype(BF16), jnp.exp2(e_p).astype(BF16), total

    def band_out(g, a_d, a_p, total):
        _, v_d = keys(g * SB_T, SB_T)
        _, v_p = keys(left_start(g), SB_LEFT)
        return jnp.broadcast_to(-total, (SB_T, LANES_V7X)), _dot(a_d, v_d) + _dot(a_p, v_p)

    def block(qh, start, carry, acc):
        kj, vj = keys(start, SB_T)
        z = _dot_nt(qh, kj)
        sp = softplus(z)
        sp16 = sp.astype(BF16)
        within = _dot(sp16, after)
        a = jnp.exp2((z - sp) - within + carry)
        acc = acc + _dot(a.astype(BF16), vj)
        carry = carry - _dot(sp16, ones)
        return carry, acc

    heads = []
    for u in range(SB_SUB):
        g = i * SB_SUB + u
        q = q_ref[u * SB_T:(u + 1) * SB_T, :]
        for head in range(HEADS_PER_LANE_TILE):
            heads.append((g, jnp.where(_head_select(head), q, 0.0).astype(BF16)))
    logits = [band_logits(qh, g, may_clip=c // HEADS_PER_LANE_TILE < SB_LEFT // SB_T)
              for c, (g, qh) in enumerate(heads)]
    softplus_terms = [band_softplus(*z) for z in logits]
    sums = [band_sums(*t) for t in softplus_terms]
    weights = [band_weights(*s) for s in sums]
    chains = [(g, qh) + band_out(g, *w) for (g, qh), w in zip(heads, weights)]

    def is_open(carry):
        return jnp.max(carry) > -ZERO_EXP * LOG2_E

    chain_open = [is_open(carry) for _, _, carry, _ in chains]

    def walk_on():
        outs = []
        for (g, qh, carry, acc), this_open in zip(chains, chain_open):
            def cond(state):
                start, carry, _ = state
                return jnp.logical_and(start >= 0, is_open(carry))

            def body(state, qh=qh):
                start, carry, acc = state
                carry, acc = block(qh, start, carry, acc)
                return start - SB_T, carry, acc

            def walk(cond=cond, body=body, first=(left_start(g) - SB_T, carry, acc)):
                return lax.while_loop(cond, body, first)[2]

            outs.append(lax.cond(this_open, walk, lambda acc=acc: acc))
        return outs

    outs = lax.cond(functools.reduce(jnp.logical_or, chain_open), walk_on,
                    lambda: [acc for _, _, _, acc in chains])
    for u in range(SB_SUB):
        pair = outs[u * HEADS_PER_LANE_TILE:(u + 1) * HEADS_PER_LANE_TILE]
        o_ref[u * SB_T:(u + 1) * SB_T, :] = jnp.where(_head_select(0), pair[0], pair[1]).astype(o_ref.dtype)


def _sb_attn(qkv, batch, seq):
    assert SB_T == LANES_V7X and SB_LEFT % SB_T == 0
    rows = SB_SUB * SB_T
    stage_bytes = SB_SUB * HEADS_PER_LANE_TILE * 2 * 4 * SB_T * (SB_T + SB_LEFT)
    block_bytes = 2 * (2 * rows * LANES_V7X + 2 * seq * LANES_V7X) + stage_bytes // 2
    return pl.pallas_call(
        _sb_kernel,
        out_shape=jax.ShapeDtypeStruct((batch, seq, D_MODEL), BF16),
        grid=(batch, N_HEAD_PAIRS, seq // rows),
        in_specs=[
            pl.BlockSpec((None, rows, LANES_V7X), lambda b, p, i: (b, i, p)),
            pl.BlockSpec((None, seq, LANES_V7X), lambda b, p, i: (b, 0, N_HEAD_PAIRS + p)),
            pl.BlockSpec((None, seq, LANES_V7X), lambda b, p, i: (b, 0, 2 * N_HEAD_PAIRS + p)),
        ],
        out_specs=pl.BlockSpec((None, rows, LANES_V7X), lambda b, p, i: (b, i, p)),
        compiler_params=pltpu.CompilerParams(
            dimension_semantics=("arbitrary", "arbitrary", "arbitrary"),
            vmem_limit_bytes=_vmem_limit(block_bytes)),
        name="sb_attn",
    )(qkv, qkv, qkv)


def _fox_kernel(a_ref, fk0_ref, g_ref, qa_ref, ka_ref, v_ref, o_ref, acc_scr, vt_scr):
    b = pl.program_id(0)
    p_id = pl.program_id(1)
    i = pl.program_id(2)
    n_blocks = fk0_ref.shape[1]
    row = lax.broadcasted_iota(jnp.int32, (TQ, TK), 0)
    col = lax.broadcasted_iota(jnp.int32, (TQ, TK), 1)
    diag_mask = col <= row

    acc0 = jnp.zeros((TQ, LANES_V7X), F32)
    chains = range(FOX_SUB * HEADS_PER_LANE_TILE)
    bh0 = (b * N_HEAD_PAIRS + p_id) * HEADS_PER_LANE_TILE

    def scores(chain, j):
        sub, head = chain // HEADS_PER_LANE_TILE, chain % HEADS_PER_LANE_TILE
        q = qa_ref[head, pl.ds(pl.multiple_of(sub * TQ, TQ), TQ), :]
        return _dot_nt(q, ka_ref[head, pl.ds(pl.multiple_of(j * TK, TK), TK), :])

    def values(chain, j):
        vj = v_ref[pl.ds(pl.multiple_of(j * TK, TK), TK), :]
        own = _head_select(chain % HEADS_PER_LANE_TILE)
        return jnp.where(own, vj, jnp.ones_like(vj))

    def diag_block(chain):
        return i * FOX_SUB + chain // HEADS_PER_LANE_TILE

    def count_left(chain):
        bh, qi = bh0 + chain % HEADS_PER_LANE_TILE, diag_block(chain)
        bound = a_ref[bh, qi]

        def more(n):
            j = qi - 1 - n
            right = fk0_ref[bh, jnp.minimum(j + 1, n_blocks - 1)]
            return jnp.logical_and(j >= 0, bound - right > -ZERO_EXP)

        return lax.while_loop(more, lambda n: n + 1, jnp.int32(0))

    def fast_walks():
        counts = [count_left(chain) for chain in chains]
        firsts = [sum(n + 1 for n in counts[:chain]) for chain in chains]
        n_items = firsts[-1] + counts[-1] + 1
        diag_bias_t = jnp.where(row <= col, 0.0, NEG_BIG)
        feature = lax.broadcasted_iota(jnp.int32, (LANES_V7X, 1), 0)
        acc_scr[...] = jnp.zeros_like(acc_scr)

        def scores_t(chain, j):
            sub, head = chain // HEADS_PER_LANE_TILE, chain % HEADS_PER_LANE_TILE
            q = qa_ref[head, pl.ds(pl.multiple_of(sub * TQ, TQ), TQ), :]
            return _dot_nt(ka_ref[head, pl.ds(pl.multiple_of(j * TK, TK), TK), :], q)

        def values_t(chain, j):
            vt = vt_scr[:, pl.ds(pl.multiple_of(j * TK, TK), TK)]
            own = (feature // HEAD_DIM) == chain % HEADS_PER_LANE_TILE
            return jnp.where(own, vt, jnp.ones_like(vt))

        def step(t, p_t, p_chain, p_block):
            chain = sum(jnp.where(t >= first, 1, 0) for first in firsts[1:])
            first = functools.reduce(lambda f, nxt: jnp.where(t >= nxt, nxt, f), firsts[1:], 0)
            back = t - first
            block = jnp.maximum(diag_block(chain) - back, 0)
            s_t = scores_t(chain, block) + jnp.where(back == 0, diag_bias_t,
                                                     jnp.where(t >= n_items, NEG_BIG, 0.0))
            acc_scr[p_chain] += _dot(values_t(p_chain, p_block), p_t)
            return jnp.exp(s_t).astype(BF16), chain, block

        def steps(first, count, carried):
            for t in range(count):
                carried = step(first + t, *carried)
            return carried

        carried = (jnp.zeros((TK, TQ), BF16), jnp.int32(0), jnp.int32(0))
        done, left = 0, 2 * ((n_items + 2) // 2)
        for size in FOX_STEP_GROUPS:
            groups = left // size
            carried = lax.fori_loop(
                0, groups, lambda u, c, size=size, done=done: steps(done + size * u, size, c), carried)
            done, left = done + size * groups, left - size * groups

        outs = []
        for sub in range(FOX_SUB):
            normed = []
            for head in range(HEADS_PER_LANE_TILE):
                acc_t = acc_scr[sub * HEADS_PER_LANE_TILE + head]
                spare = HEAD_DIM * (1 - head)
                normed.append(acc_t / acc_t[spare:spare + 1, :])
            outs.append(jnp.where(feature // HEAD_DIM == 0, normed[0], normed[1]).T)
        return outs

    def online_walks():
        accs = []
        for chain in chains:
            def body(t, carried, chain=chain):
                acc, m = carried
                block = diag_block(chain) - t
                s = jnp.where(jnp.logical_or(t > 0, diag_mask), scores(chain, block), NEG_BIG)
                m_new = jnp.maximum(m, jnp.max(s, axis=-1, keepdims=True))
                p = jnp.exp(s - m_new).astype(BF16)
                return jnp.exp(m - m_new) * acc + _dot(p, values(chain, block)), m_new

            m0 = jnp.full((TQ, 1), NEG_BIG, F32)
            acc = lax.fori_loop(0, count_left(chain) + 1, body, (acc0, m0))[0]
            accs.append(acc / pltpu.roll(acc, HEAD_DIM, axis=1))
        return [jnp.where(_head_select(0), accs[sub * HEADS_PER_LANE_TILE],
                          accs[sub * HEADS_PER_LANE_TILE + 1]) for sub in range(FOX_SUB)]

    @pl.when(i == 0)
    def _():
        for c in range(0, vt_scr.shape[1], TK):
            vt_scr[:, c:c + TK] = v_ref[c:c + TK, :].T

    largest_reach = functools.reduce(jnp.maximum, [
        g_ref[bh0 + chain % HEADS_PER_LANE_TILE, diag_block(chain)] for chain in chains])
    outs = lax.cond(2.0 * largest_reach < SAFE_EXP_RANGE, fast_walks, online_walks)
    for sub in range(FOX_SUB):
        o_ref[sub * TQ:(sub + 1) * TQ, :] = outs[sub].astype(o_ref.dtype)


def _fox_attn(q_aug, k_aug, v, walk_bound, fk_block_start, max_reach, batch, seq):
    assert TQ == TK
    rows = FOX_SUB * TQ
    n_chains = FOX_SUB * HEADS_PER_LANE_TILE
    block_bytes = (2 * (HEADS_PER_LANE_TILE * (rows + seq) * LANES_V7X + seq * LANES_V7X + rows * LANES_V7X)
                   + 4 * n_chains * TQ * LANES_V7X + 2 * seq * LANES_V7X)
    smem = pl.BlockSpec(memory_space=pltpu.SMEM)
    return pl.pallas_call(
        _fox_kernel,
        out_shape=jax.ShapeDtypeStruct((batch, seq, D_MODEL), BF16),
        grid=(batch, N_HEAD_PAIRS, seq // rows),
        in_specs=[
            smem, smem, smem,
            pl.BlockSpec((None, HEADS_PER_LANE_TILE, rows, LANES_V7X), lambda b, p, i: (b, p, i, 0)),
            pl.BlockSpec((None, HEADS_PER_LANE_TILE, seq, LANES_V7X), lambda b, p, i: (b, p, 0, 0)),
            pl.BlockSpec((None, seq, LANES_V7X), lambda b, p, i: (b, 0, p)),
        ],
        out_specs=pl.BlockSpec((None, rows, LANES_V7X), lambda b, p, i: (b, i, p)),
        scratch_shapes=[pltpu.VMEM((n_chains, LANES_V7X, TQ), F32),
                        pltpu.VMEM((LANES_V7X, seq), BF16)],
        compiler_params=pltpu.CompilerParams(
            dimension_semantics=("arbitrary", "arbitrary", "arbitrary"),
            vmem_limit_bytes=_vmem_limit(block_bytes)),
        name="fox_attn",
    )(walk_bound, fk_block_start, max_reach, q_aug, k_aug, v)


def kernel(x, c, ada_w, ada_b, norm_attn_g, norm_ffn_g, w_ffn_in, w_ffn_down, sb_w_qkv, sb_w_o,
           kv_ada_w, kv_ada_b, kv_norm_g, w_kvf, b_f, k_norm_g, fox_w_q, q_norm_g, fox_w_o):
    batch, seq, d = x.shape
    assert d == D_MODEL and seq % TM == 0
    assert seq % (FOX_SUB * TQ) == 0 and seq % (SB_SUB * SB_T) == 0
    assert ada_w.shape[0] == 2 and sb_w_qkv.shape[0] == 1 and fox_w_q.shape[0] == 1
    m = batch * seq

    mod = _ada_mod(c, ada_w, ada_b)[:, :batch]
    kv_mod = _ada_mod(c, kv_ada_w[None], kv_ada_b[None])[0, :batch]

    def vecs(t, n):
        return [t[:, None, j * d:(j + 1) * d] for j in range(n)]

    row = lambda t: t.reshape(1, -1).astype(F32)
    bf = lambda t: t.astype(BF16)

    x2d = x.reshape(m, d)

    sh_a, sc_a, g_a, sh_f, sc_f, g_f = vecs(mod[0], 6)
    qkv_scale = jnp.concatenate([jnp.full((1, d), LOG2_E * HEAD_DIM ** -0.5, F32),
                                 jnp.ones((1, 2 * d), F32)], axis=1)
    qkv = _norm_matmul(x2d, row(norm_attn_g[0]), sh_a, sc_a, bf(sb_w_qkv[0]), qkv_scale, seq)
    o = _sb_attn(qkv.reshape(batch, seq, 3 * d), batch, seq)
    x2d, act = _out_ffn_in(o.reshape(m, d), bf(sb_w_o[0]), x2d, g_a,
                           row(norm_ffn_g[0]), sh_f, sc_f, bf(w_ffn_in[0]), seq)
    g_f0 = g_f

    sh_a, sc_a, g_a, sh_f, sc_f, g_f = vecs(mod[1], 6)
    kv_shift, kv_scale = vecs(kv_mod, 2)
    w_f_pad = jnp.zeros((d, LANES_V7X), F32).at[:, :N_HEADS].set(w_kvf[:, 2 * d:])
    b_f_pad = jnp.zeros((1, LANES_V7X), F32).at[0, :N_HEADS].set(b_f)
    key_norm_bound = (1.01 * HEAD_DIM ** 0.5 * jnp.max(jnp.abs(k_norm_g))).reshape(1, 1).astype(F32)
    x2d, v, q_aug, k_aug, stats = _l1_proj(
        act, bf(w_ffn_down[0]), x2d, g_f0,
        row(kv_norm_g), kv_shift, kv_scale, row(norm_attn_g[1]), sh_a, sc_a,
        bf(w_kvf[:, :d]), bf(w_kvf[:, d:2 * d]), bf(w_f_pad), bf(fox_w_q[0]),
        row(jnp.tile(k_norm_g, N_HEADS)), row(jnp.tile(q_norm_g[0], N_HEADS)), b_f_pad,
        key_norm_bound, batch, seq)
    tables = stats[:, :, :3, :N_HEADS].transpose(2, 0, 3, 1).reshape(3, batch * N_HEADS, seq // TK)
    o = _fox_attn(q_aug, k_aug, v.reshape(batch, seq, d), tables[0], tables[1], tables[2], batch, seq)
    x2d = _out_ffn(o.reshape(m, d), bf(fox_w_o[0]), x2d, g_a, row(norm_ffn_g[1]), sh_f, sc_f,
                   bf(w_ffn_in[1]), bf(w_ffn_down[1]), g_f, seq)
    return x2d.reshape(batch, seq, d)
```
